```python
import jax, jax.numpy as jnp
from jax import lax
import numpy as np

D_MODEL = 1024
BATCH = 8
SEQ = 2048
DEPTH = 1
DEC_BATCH = 128
DEC_SEQ = 1
PAST_LEN = 16384
PAGE_SIZE = 128

RET_HEADS = 4
RET_DK = 128
RET_DV = 128
GDN_HEADS = 4
GDN_DK = 128
GDN_DV = 128
CONV_W = 4
D_FF = 2816
CHUNK = 64
ROPE_BASE = 10000.0
EPS = 1e-6

RET_QK = RET_HEADS * RET_DK
RET_V = RET_HEADS * RET_DV
GDN_QK = GDN_HEADS * GDN_DK
GDN_V = GDN_HEADS * GDN_DV
CONV_DIM = 2 * GDN_QK + GDN_V
SPLITS = (RET_QK, RET_QK, RET_V, RET_V, CONV_DIM, GDN_V, GDN_HEADS, GDN_HEADS, D_MODEL, D_MODEL)
D_IN = sum(SPLITS)

kernel_name = "hybrid_retention_gdn_macaron_step"


def rmsnorm(x, g):
    xf = x.astype(jnp.float32)
    y = xf * lax.rsqrt(jnp.mean(xf * xf, axis=-1, keepdims=True) + EPS) * g.astype(jnp.float32)
    return y.astype(x.dtype)


def swiglu(x, w_gate, w_up, w_down):
    return (jax.nn.silu(x @ w_gate) * (x @ w_up)) @ w_down


def rope(x, pos):
    d = x.shape[-1]
    inv = ROPE_BASE ** (-jnp.arange(0, d, 2, dtype=jnp.float32) / d)
    ang = pos[:, None] * inv[None, :]
    cos = jnp.cos(ang)[None, :, None, :]
    sin = jnp.sin(ang)[None, :, None, :]
    x1, x2 = x[..., : d // 2], x[..., d // 2:]
    return jnp.concatenate([x1 * cos - x2 * sin, x1 * sin + x2 * cos], axis=-1)


def chunk_len(t):
    return CHUNK if t % CHUNK == 0 else t


def to_chunks(x, c):
    b, t, h = x.shape[:3]
    x = x.reshape((b, t // c, c, h) + x.shape[3:])
    return jnp.moveaxis(x, (1, 3), (0, 2))


def from_chunks(x):
    x = jnp.moveaxis(x, (0, 2), (1, 3))
    b, n, c, h, d = x.shape
    return x.reshape(b, n * c, h, d)


def decay_masks(G):
    c = G.shape[-1]
    causal = jnp.tril(jnp.ones((c, c), dtype=bool))
    diff = G[..., :, None] - G[..., None, :]
    return jnp.where(causal, jnp.exp(jnp.where(causal, diff, 0.0)), 0.0)


def retention_chunked(q, k, v, g, s0):
    c = chunk_len(q.shape[1])
    qc, kc, vc, gc = to_chunks(q, c), to_chunks(k, c), to_chunks(v, c), to_chunks(g, c)
    G = jnp.cumsum(gc, axis=-1)
    dm = decay_masks(G)
    scores = jnp.einsum('nbhid,nbhjd->nbhij', qc, kc) * dm
    o_intra = jnp.einsum('nbhij,nbhje->nbhie', scores, vc)

    def step(s, inp):
        q_, k_, v_, G_ = inp
        o = jnp.exp(G_)[..., None] * jnp.einsum('bhid,bhde->bhie', q_, s)
        gl = G_[..., -1:]
        s = jnp.exp(gl)[..., None] * s + jnp.einsum('bhjd,bhje->bhde', k_ * jnp.exp(gl - G_)[..., None], v_)
        return s, o

    s_final, o_cross = lax.scan(step, s0, (qc, kc, vc, G))
    return from_chunks(o_intra + o_cross), s_final


def gated_delta_chunked(q, k, v, g, beta, s0):
    c = chunk_len(q.shape[1])
    qc, kc, vc = to_chunks(q, c), to_chunks(k, c), to_chunks(v, c)
    gc, bc = to_chunks(g, c), to_chunks(beta, c)
    G = jnp.cumsum(gc, axis=-1)
    dm = decay_masks(G)
    strict = jnp.tril(jnp.ones((c, c), dtype=bool), k=-1)
    kk = jnp.einsum('nbhid,nbhjd->nbhij', kc, kc)
    a_mat = jnp.eye(c, dtype=jnp.float32) + jnp.where(strict, bc[..., :, None] * dm * kk, 0.0)
    u_v = lax.linalg.triangular_solve(a_mat, bc[..., None] * vc, left_side=True, lower=True, unit_diagonal=True)
    w_k = lax.linalg.triangular_solve(a_mat, (bc * jnp.exp(G))[..., None] * kc,
                                      left_side=True, lower=True, unit_diagonal=True)
    qk = jnp.einsum('nbhid,nbhjd->nbhij', qc, kc) * dm

    def step(s, inp):
        q_, k_, uv_, wk_, qk_, G_ = inp
        u = uv_ - jnp.einsum('bhid,bhde->bhie', wk_, s)
        o = jnp.exp(G_)[..., None] * jnp.einsum('bhid,bhde->bhie', q_, s) + jnp.einsum('bhij,bhje->bhie', qk_, u)
        gl = G_[..., -1:]
        s = jnp.exp(gl)[..., None] * s + jnp.einsum('bhjd,bhje->bhde', k_ * jnp.exp(gl - G_)[..., None], u)
        return s, o

    s_final, o = lax.scan(step, s0, (qc, kc, u_v, w_k, qk, G))
    return from_chunks(o), s_final


def short_conv(u, buf, w):
    t = u.shape[1]
    full = jnp.concatenate([buf, u], axis=1)
    out = full[:, 0:t] * w[0]
    for i in range(1, CONV_W):
        out = out + full[:, i:i + t] * w[i]
    return jax.nn.silu(out), full[:, -(CONV_W - 1):]


def token_mixer(h, pos, s_ret, s_gdn, s_conv, w_in, ret_norm_g, gdn_conv_w, gdn_a_log, gdn_dt_bias,
                gdn_norm_g, w_ret_branch, w_gdn_branch, w_out):
    f32 = jnp.float32
    b, t, _ = h.shape
    proj = (h @ w_in).astype(f32)
    idx = tuple(int(i) for i in np.cumsum(SPLITS)[:-1])
    rq, rk, rv, rg, conv_in, gz, ga, gb, gate_r, gate_g = jnp.split(proj, idx, axis=-1)

    q = rope(rq.reshape(b, t, RET_HEADS, RET_DK), pos)
    k = rope(rk.reshape(b, t, RET_HEADS, RET_DK), pos) * (RET_DK ** -0.5)
    v = rv.reshape(b, t, RET_HEADS, RET_DV)
    log_gamma = jnp.log(1.0 - 2.0 ** (-5.0 - jnp.arange(RET_HEADS, dtype=f32)))
    g_ret = jnp.broadcast_to(log_gamma, (b, t, RET_HEADS))
    o_r, s_ret_new = retention_chunked(q, k, v, g_ret, s_ret.astype(f32))
    mu = jnp.mean(o_r, axis=-1, keepdims=True)
    var = jnp.mean(jnp.square(o_r - mu), axis=-1, keepdims=True)
    o_r = (o_r - mu) * lax.rsqrt(var + EPS)
    o_r = o_r.reshape(b, t, RET_V) * ret_norm_g.astype(f32)
    o_r = jax.nn.silu(rg) * o_r

    cq, s_conv_new = short_conv(conv_in, s_conv.astype(f32), gdn_conv_w.astype(f32))
    gq, gk, gv = jnp.split(cq, (GDN_QK, 2 * GDN_QK), axis=-1)
    gq = gq.reshape(b, t, GDN_HEADS, GDN_DK)
    gk = gk.reshape(b, t, GDN_HEADS, GDN_DK)
    gv = gv.reshape(b, t, GDN_HEADS, GDN_DV)
    gq = gq * lax.rsqrt(jnp.sum(gq * gq, axis=-1, keepdims=True) + EPS) * (GDN_DK ** -0.5)
    gk = gk * lax.rsqrt(jnp.sum(gk * gk, axis=-1, keepdims=True) + EPS)
    beta = jax.nn.sigmoid(gb)
    g_gdn = -jnp.exp(gdn_a_log.astype(f32)) * jax.nn.softplus(ga + gdn_dt_bias.astype(f32))
    o_g, s_gdn_new = gated_delta_chunked(gq, gk, gv, g_gdn, beta, s_gdn.astype(f32))
    o_g = o_g * lax.rsqrt(jnp.mean(o_g * o_g, axis=-1, keepdims=True) + EPS) * gdn_norm_g.astype(f32)
    o_g = o_g.reshape(b, t, GDN_V) * jax.nn.silu(gz)

    y = (jax.nn.sigmoid(gate_r) * (o_r @ w_ret_branch) + jax.nn.sigmoid(gate_g) * (o_g @ w_gdn_branch))
    y = (y @ w_out).astype(h.dtype)
    return y, s_ret_new, s_gdn_new, s_conv_new


def layer(x, pos, s_ret, s_gdn, s_conv, ffn1_pre_g, ffn1_post_g, ffn1_w_gate, ffn1_w_up, ffn1_w_down,
          mix_pre_g, mix_post_g, w_in, ret_norm_g, gdn_conv_w, gdn_a_log, gdn_dt_bias, gdn_norm_g,
          w_ret_branch, w_gdn_branch, w_out, ffn2_pre_g, ffn2_post_g, ffn2_w_gate, ffn2_w_up, ffn2_w_down):
    x = x + 0.5 * rmsnorm(swiglu(rmsnorm(x, ffn1_pre_g), ffn1_w_gate, ffn1_w_up, ffn1_w_down), ffn1_post_g)
    m, s_ret_new, s_gdn_new, s_conv_new = token_mixer(
        rmsnorm(x, mix_pre_g), pos, s_ret, s_gdn, s_conv, w_in, ret_norm_g, gdn_conv_w, gdn_a_log,
        gdn_dt_bias, gdn_norm_g, w_ret_branch, w_gdn_branch, w_out)
    x = x + rmsnorm(m, mix_post_g)
    x = x + 0.5 * rmsnorm(swiglu(rmsnorm(x, ffn2_pre_g), ffn2_w_gate, ffn2_w_up, ffn2_w_down), ffn2_post_g)
    return x, s_ret_new, s_gdn_new, s_conv_new


def setup_inputs(seed: int = 0) -> dict:
    key = jax.random.key(seed)
    ks = iter(jax.random.split(key, 40))
    f32 = jnp.float32
    L = DEPTH

    def nrm(shape, scale):
        return scale * jax.random.normal(next(ks), shape, f32)

    def gain(n):
        return 1.0 + nrm((L, n), 0.02)

    return {
        "x_prompt": nrm((BATCH, SEQ, D_MODEL), 1.0),
        "x_sample": nrm((DEC_BATCH, DEC_SEQ, D_MODEL), 1.0),
        "state_ret": nrm((L, DEC_BATCH, RET_HEADS, RET_DK, RET_DV), 0.1),
        "state_gdn": nrm((L, DEC_BATCH, GDN_HEADS, GDN_DK, GDN_DV), 0.1),
        "state_conv": nrm((L, DEC_BATCH, CONV_W - 1, CONV_DIM), 1.0),
        "ffn1_pre_g": gain(D_MODEL),
        "ffn1_post_g": gain(D_MODEL),
        "ffn1_w_gate": nrm((L, D_MODEL, D_FF), D_MODEL ** -0.5),
        "ffn1_w_up": nrm((L, D_MODEL, D_FF), D_MODEL ** -0.5),
        "ffn1_w_down": nrm((L, D_FF, D_MODEL), D_FF ** -0.5),
        "mix_pre_g": gain(D_MODEL),
        "mix_post_g": gain(D_MODEL),
        "w_in": nrm((L, D_MODEL, D_IN), D_MODEL ** -0.5),
        "ret_norm_g": gain(RET_V),
        "gdn_conv_w": nrm((L, CONV_W, CONV_DIM), CONV_W ** -0.5),
        "gdn_a_log": jnp.log(jax.random.uniform(next(ks), (L, GDN_HEADS), f32, 1.0, 16.0)),
        "gdn_dt_bias": nrm((L, GDN_HEADS), 0.5),
        "gdn_norm_g": gain(GDN_DV),
        "w_ret_branch": nrm((L, RET_V, D_MODEL), RET_V ** -0.5),
        "w_gdn_branch": nrm((L, GDN_V, D_MODEL), GDN_V ** -0.5),
        "w_out": nrm((L, D_MODEL, D_MODEL), D_MODEL ** -0.5),
        "ffn2_pre_g": gain(D_MODEL),
        "ffn2_post_g": gain(D_MODEL),
        "ffn2_w_gate": nrm((L, D_MODEL, D_FF), D_MODEL ** -0.5),
        "ffn2_w_up": nrm((L, D_MODEL, D_FF), D_MODEL ** -0.5),
        "ffn2_w_down": nrm((L, D_FF, D_MODEL), D_FF ** -0.5),
    }


def reference(x_prompt, x_sample, state_ret, state_gdn, state_conv, ffn1_pre_g, ffn1_post_g, ffn1_w_gate,
              ffn1_w_up, ffn1_w_down, mix_pre_g, mix_post_g, w_in, ret_norm_g, gdn_conv_w, gdn_a_log,
              gdn_dt_bias, gdn_norm_g, w_ret_branch, w_gdn_branch, w_out, ffn2_pre_g, ffn2_post_g,
              ffn2_w_gate, ffn2_w_up, ffn2_w_down):
    f32 = jnp.float32
    pos_p = jnp.arange(SEQ, dtype=f32)
    pos_s = PAST_LEN + jnp.arange(DEC_SEQ, dtype=f32)
    yp, ys = x_prompt, x_sample
    rp, gp, cp, rs, gs, cs = [], [], [], [], [], []
    for l in range(DEPTH):
        w = (ffn1_pre_g[l], ffn1_post_g[l], ffn1_w_gate[l], ffn1_w_up[l], ffn1_w_down[l],
             mix_pre_g[l], mix_post_g[l], w_in[l], ret_norm_g[l], gdn_conv_w[l], gdn_a_log[l],
             gdn_dt_bias[l], gdn_norm_g[l], w_ret_branch[l], w_gdn_branch[l], w_out[l],
             ffn2_pre_g[l], ffn2_post_g[l], ffn2_w_gate[l], ffn2_w_up[l], ffn2_w_down[l])
        z_ret = jnp.zeros((BATCH, RET_HEADS, RET_DK, RET_DV), f32)
        z_gdn = jnp.zeros((BATCH, GDN_HEADS, GDN_DK, GDN_DV), f32)
        z_conv = jnp.zeros((BATCH, CONV_W - 1, CONV_DIM), f32)
        yp, r1, g1, c1 = layer(yp, pos_p, z_ret, z_gdn, z_conv, *w)
        ys, r2, g2, c2 = layer(ys, pos_s, state_ret[l], state_gdn[l], state_conv[l], *w)
        rp.append(r1); gp.append(g1); cp.append(c1)
        rs.append(r2); gs.append(g2); cs.append(c2)
    new_ret_p = jnp.stack(rp)
    new_gdn_p = jnp.stack(gp)
    new_conv_p = jnp.stack(cp)
    new_ret_s = jnp.stack(rs)
    new_gdn_s = jnp.stack(gs)
    new_conv_s = jnp.stack(cs)
    return (yp, ys, new_ret_p, new_gdn_p, new_conv_p, new_ret_s, new_gdn_s, new_conv_s)
```

```python
import functools

import numpy as np
import jax
import jax.numpy as jnp
from jax import lax
from jax.experimental import pallas as pl
from jax.experimental.pallas import tpu as pltpu

F32 = jnp.float32
BF16 = jnp.bfloat16

D_MODEL = 1024
D_FF = 2816
HEADS = 4
HD = 128
QK = HEADS * HD
CONV_W = 4
CONV_DIM = 3 * QK
CHUNK = 64
ROPE_BASE = 10000.0
EPS = 1e-6
PAST_LEN = 16384

OFF_RQ, OFF_RK, OFF_RV, OFF_RG = 0, QK, 2 * QK, 3 * QK
OFF_CONV = 4 * QK
OFF_GZ = OFF_CONV + CONV_DIM
OFF_AB = OFF_GZ + QK
OFF_GATES = OFF_AB + 2 * HEADS
D_IN = OFF_GATES + 2 * D_MODEL

TM = 256
TB = 8
FFN_TM = 256
CONV_PAD = 8

VMEM_LIMIT = 56 * 1024 * 1024


def _silu(x):
    return x * jax.nn.sigmoid(x)


def _rms(x, g):
    return x * lax.rsqrt(jnp.mean(x * x, axis=-1, keepdims=True) + EPS) * g


def _dot(a, b):
    return jnp.dot(a.astype(BF16), b.astype(BF16), preferred_element_type=F32)


def _dot_nt(a, b):
    return lax.dot_general(a.astype(BF16), b.astype(BF16), (((1,), (1,)), ((), ())),
                           preferred_element_type=F32)


def _dot_tn(a, b):
    return lax.dot_general(a.astype(BF16), b.astype(BF16), (((0,), (0,)), ((), ())),
                           preferred_element_type=F32)


def _split3(a):
    hi = a.astype(BF16)
    r = a - hi.astype(F32)
    mid = r.astype(BF16)
    lo = (r - mid.astype(F32)).astype(BF16)
    return hi, mid, lo


def _dot_x3(a, b):
    a_hi = a.astype(BF16)
    a_lo = (a - a_hi.astype(F32)).astype(BF16)
    b_hi = b.astype(BF16)
    b_lo = (b - b_hi.astype(F32)).astype(BF16)
    d = functools.partial(jnp.dot, preferred_element_type=F32)
    return d(a_hi, b_hi) + (d(a_hi, b_lo) + d(a_lo, b_hi))


def _ffn_body(x_ref, pre_ref, post_ref, wg_ref, wu_ref, wd_ref, o_ref):
    x = x_ref[...]
    h = _rms(x, pre_ref[...]).astype(BF16)
    g = jnp.dot(h, wg_ref[...], preferred_element_type=F32)
    u = jnp.dot(h, wu_ref[...], preferred_element_type=F32)
    a = (_silu(g) * u).astype(BF16)
    y = jnp.dot(a, wd_ref[...], preferred_element_type=F32)
    o_ref[...] = x + 0.5 * _rms(y, post_ref[...])


def _resident(shape):
    nd = len(shape)
    return pl.BlockSpec(shape, lambda *_: (0,) * nd, pipeline_mode=pl.Buffered(1))


def _ffn(x, pre_g, post_g, wg, wu, wd, tm):
    n = x.shape[0]
    assert n % tm == 0
    return pl.pallas_call(
        _ffn_body,
        grid=(n // tm,),
        in_specs=[
            pl.BlockSpec((tm, D_MODEL), lambda i: (i, 0)),
            _resident((1, D_MODEL)), _resident((1, D_MODEL)),
            _resident((D_MODEL, D_FF)), _resident((D_MODEL, D_FF)), _resident((D_FF, D_MODEL)),
        ],
        out_specs=pl.BlockSpec((tm, D_MODEL), lambda i: (i, 0)),
        out_shape=jax.ShapeDtypeStruct((n, D_MODEL), F32),
        compiler_params=pltpu.CompilerParams(dimension_semantics=("arbitrary",),
                                             vmem_limit_bytes=VMEM_LIMIT),
        name="ffn",
    )(x, pre_g, post_g, wg, wu, wd)


def _rope(x, cos, sin_signed):
    return x * cos + pltpu.roll(x, HD // 2, 1) * sin_signed


def _ret_out_norm(o, g_row, gate):
    mu = jnp.mean(o, axis=-1, keepdims=True)
    d = o - mu
    var = jnp.mean(d * d, axis=-1, keepdims=True)
    return _silu(gate) * (d * lax.rsqrt(var + EPS) * g_row)


def _gdn_out_norm(o, g_row, gate):
    return o * lax.rsqrt(jnp.mean(o * o, axis=-1, keepdims=True) + EPS) * g_row * _silu(gate)


def _l2norm(x, scale):
    return x * (lax.rsqrt(jnp.sum(x * x, axis=-1, keepdims=True) + EPS) * scale)


def _softplus(x):
    return jnp.maximum(x, 0.0) + jnp.log(1.0 + jnp.exp(-jnp.abs(x)))


def _merge(x, o_r, o_g, gates, wrb_ref, wgb_ref, wout_ref, post_ref):
    y = (jax.nn.sigmoid(gates[:, :D_MODEL]) * _dot(o_r, wrb_ref[...])
         + jax.nn.sigmoid(gates[:, D_MODEL:]) * _dot(o_g, wgb_ref[...]))
    m = _dot(y, wout_ref[...])
    return x + _rms(m, post_ref[...])


def _unit_lower_inverse(a, masks):
    eye, m16, off32, off64 = masks
    ad = a * m16
    x = eye - ad
    p = _dot_x3(ad, ad)
    x = x + _dot_x3(x, p)
    p = _dot_x3(p, p)
    x = x + _dot_x3(x, p)
    p = _dot_x3(p, p)
    x = x + _dot_x3(x, p)
    for m in (off32, off64):
        x = x - _dot_x3(_dot_x3(x, a * m), x)
    return x


def _inverse_masks():
    r = lax.broadcasted_iota(jnp.int32, (CHUNK, CHUNK), 0)
    c = lax.broadcasted_iota(jnp.int32, (CHUNK, CHUNK), 1)
    one, zero = jnp.float32(1.0), jnp.float32(0.0)
    eye = jnp.where(r == c, one, zero)
    same16 = (r >> 4) == (c >> 4)
    same32 = (r >> 5) == (c >> 5)
    m16 = jnp.where(same16, one, zero)
    m32 = jnp.where(same32, one, zero)
    return eye, m16, m32 - m16, 1.0 - m32


def _mixer_prompt_body(x_ref, cos_ref, sin_ref, rd_ref, ku_ref, ltri_ref,
                       pre_ref, post_ref, wmain_ref, wab_ref, wgates_ref, convw_ref, alog_ref, dtb_ref,
                       retg_ref, gdng_ref, wrb_ref, wgb_ref, wout_ref,
                       y_ref, sret_ref, sgdn_ref, conv_ref,
                       cbuf, *, ret_tile_decay):
    t = pl.program_id(1)

    @pl.when(t == 0)
    def _():
        sret_ref[...] = jnp.zeros_like(sret_ref)
        sgdn_ref[...] = jnp.zeros_like(sgdn_ref)
        cbuf[0:CONV_PAD, :] = jnp.zeros((CONV_PAD, CONV_DIM), F32)

    x = x_ref[0]
    h = _rms(x, pre_ref[...]).astype(BF16)
    pm = jnp.dot(h, wmain_ref[...], preferred_element_type=F32)
    ab = jnp.dot(h, wab_ref[...], preferred_element_type=F32)
    gates = jnp.dot(h, wgates_ref[...], preferred_element_type=F32)

    cos = cos_ref[...]
    sin = sin_ref[...]
    ri = lax.broadcasted_iota(jnp.int32, (TM, TM), 0)
    ci = lax.broadcasted_iota(jnp.int32, (TM, TM), 1)
    causal = ri >= ci
    o_r = []
    for hh in range(HEADS):
        sl = slice(hh * HD, (hh + 1) * HD)
        rq = pm[:, OFF_RQ + hh * HD:OFF_RQ + (hh + 1) * HD]
        rk = pm[:, OFF_RK + hh * HD:OFF_RK + (hh + 1) * HD]
        v = pm[:, OFF_RV + hh * HD:OFF_RV + (hh + 1) * HD]
        rg = pm[:, OFF_RG + hh * HD:OFF_RG + (hh + 1) * HD]
        qs = _rope(rq, cos, sin) * rd_ref[hh]
        ku = _rope(rk, cos, sin) * ku_ref[hh]
        s = sret_ref[0, hh]
        sc = jnp.where(causal, _dot_nt(qs, ku), 0.0)
        o = _dot(sc, v) + _dot(qs, s)
        sret_ref[0, hh] = ret_tile_decay[hh] * (s + _dot_tn(ku, v))
        o_r.append(_ret_out_norm(o, retg_ref[:, sl], rg))
    o_r = jnp.concatenate(o_r, axis=1)

    cbuf[CONV_PAD:CONV_PAD + TM, :] = pm[:, OFF_CONV:OFF_CONV + CONV_DIM]
    acc = cbuf[CONV_PAD - 3:CONV_PAD - 3 + TM, :] * convw_ref[0:1, :]
    for i in range(1, CONV_W):
        acc = acc + cbuf[CONV_PAD - 3 + i:CONV_PAD - 3 + i + TM, :] * convw_ref[i:i + 1, :]
    tail = cbuf[CONV_PAD + TM - 3:CONV_PAD + TM, :]
    conv_ref[0] = tail
    cbuf[CONV_PAD - 3:CONV_PAD, :] = tail
    cq = _silu(acc)

    g_all = -jnp.exp(alog_ref[...]) * _softplus(ab + dtb_ref[...])
    beta_all = jax.nn.sigmoid(ab)
    ltri = ltri_ref[...]
    g_hi, g_mid, g_lo = _split3(g_all)
    dd = functools.partial(jnp.dot, preferred_element_type=F32)
    gcum = dd(ltri, g_hi) + (dd(ltri, g_mid) + dd(ltri, g_lo))
    gcum_t = gcum.T

    r64 = lax.broadcasted_iota(jnp.int32, (CHUNK, CHUNK), 0)
    c64 = lax.broadcasted_iota(jnp.int32, (CHUNK, CHUNK), 1)
    causal64 = r64 >= c64
    strict64 = r64 > c64
    masks = _inverse_masks()

    o_g = []
    for hh in range(HEADS):
        gq = _l2norm(cq[:, hh * HD:(hh + 1) * HD], HD ** -0.5)
        gk = _l2norm(cq[:, QK + hh * HD:QK + (hh + 1) * HD], 1.0)
        gv = cq[:, 2 * QK + hh * HD:2 * QK + (hh + 1) * HD]
        s = sgdn_ref[0, hh]
        outs = []
        for c in range(TM // CHUNK):
            rows = slice(c * CHUNK, (c + 1) * CHUNK)
            gc = gcum[rows, hh:hh + 1]
            gr = gcum_t[hh:hh + 1, rows]
            bcol = beta_all[rows, HEADS + hh:HEADS + hh + 1]
            qc, kc, vc = gq[rows], gk[rows], gv[rows]
            dm = jnp.where(causal64, jnp.exp(jnp.minimum(gc - gr, 0.0)), 0.0)
            kk = _dot_nt(kc, kc)
            qk = _dot_nt(qc, kc) * dm
            a = jnp.where(strict64, bcol * dm * kk, 0.0)
            tinv = _unit_lower_inverse(a, masks)
            eg = jnp.exp(gc)
            rhs = jnp.concatenate([bcol * vc, (bcol * eg) * kc], axis=1)
            uw = _dot_x3(tinv, rhs)
            u = uw[:, :HD] - _dot(uw[:, HD:], s)
            outs.append(eg * _dot(qc, s) + _dot(qk, u))
            gl = gc[CHUNK - 1:CHUNK, :]
            s = jnp.exp(gl) * s + _dot_tn(kc * jnp.exp(gl - gc), u)
        sgdn_ref[0, hh] = s
        o = jnp.concatenate(outs, axis=0)
        gz = pm[:, OFF_GZ + hh * HD:OFF_GZ + (hh + 1) * HD]
        o_g.append(_gdn_out_norm(o, gdng_ref[...], gz))
    o_g = jnp.concatenate(o_g, axis=1)

    y_ref[0] = _merge(x, o_r, o_g, gates, wrb_ref, wgb_ref, wout_ref, post_ref)


def _ret_gammas():
    return 1.0 - 2.0 ** (-5.0 - np.arange(HEADS, dtype=np.float64))


def _rope_tables(pos):
    inv = ROPE_BASE ** (-np.arange(0, HD, 2, dtype=np.float64) / HD)
    ang = np.asarray(pos, np.float64)[:, None] * inv[None, :]
    cos = np.concatenate([np.cos(ang), np.cos(ang)], axis=1)
    sin = np.concatenate([-np.sin(ang), np.sin(ang)], axis=1)
    return jnp.asarray(cos, F32), jnp.asarray(sin, F32)


def _mixer_prompt(x, w):
    b, t, _ = x.shape
    assert t % TM == 0
    cos, sin = _rope_tables(np.arange(t))
    gam = _ret_gammas()
    i1 = np.arange(1, TM + 1, dtype=np.float64)
    rd = np.broadcast_to((gam[:, None] ** i1[None, :])[:, :, None], (HEADS, TM, HD))
    ku = np.broadcast_to((HD ** -0.5 * gam[:, None] ** (-i1[None, :]))[:, :, None], (HEADS, TM, HD))
    tile_decay = tuple(float(v) for v in gam ** TM)
    r = np.arange(TM)
    ltri = ((r[:, None] >= r[None, :]) & (r[:, None] // CHUNK == r[None, :] // CHUNK))

    body = functools.partial(_mixer_prompt_body, ret_tile_decay=tile_decay)
    state_spec = pl.BlockSpec((1, HEADS, HD, HD), lambda i, j: (i, 0, 0, 0))
    return pl.pallas_call(
        body,
        grid=(b, t // TM),
        in_specs=[
            pl.BlockSpec((1, TM, D_MODEL), lambda i, j: (i, j, 0)),
            pl.BlockSpec((TM, HD), lambda i, j: (j, 0)),
            pl.BlockSpec((TM, HD), lambda i, j: (j, 0)),
            _resident((HEADS, TM, HD)), _resident((HEADS, TM, HD)), _resident((TM, TM)),
            _resident((1, D_MODEL)), _resident((1, D_MODEL)),
            _resident((D_MODEL, OFF_AB)), _resident((D_MODEL, HD)), _resident((D_MODEL, 2 * D_MODEL)),
            _resident((CONV_W, CONV_DIM)), _resident((1, HD)), _resident((1, HD)),
            _resident((1, QK)), _resident((1, HD)),
            _resident((QK, D_MODEL)), _resident((QK, D_MODEL)), _resident((D_MODEL, D_MODEL)),
        ],
        out_specs=[
            pl.BlockSpec((1, TM, D_MODEL), lambda i, j: (i, j, 0)),
            state_spec, state_spec,
            pl.BlockSpec((1, CONV_W - 1, CONV_DIM), lambda i, j: (i, 0, 0)),
        ],
        out_shape=[
            jax.ShapeDtypeStruct((b, t, D_MODEL), F32),
            jax.ShapeDtypeStruct((b, HEADS, HD, HD), F32),
            jax.ShapeDtypeStruct((b, HEADS, HD, HD), F32),
            jax.ShapeDtypeStruct((b, CONV_W - 1, CONV_DIM), F32),
        ],
        scratch_shapes=[pltpu.VMEM((CONV_PAD + TM, CONV_DIM), F32)],
        compiler_params=pltpu.CompilerParams(dimension_semantics=("arbitrary", "arbitrary"),
                                             vmem_limit_bytes=VMEM_LIMIT),
        name="mixer_prompt",
    )(x, cos, sin, jnp.asarray(rd, F32), jnp.asarray(ku, F32), jnp.asarray(ltri, BF16),
      w["mix_pre_g"], w["mix_post_g"], w["w_main"], w["w_ab"], w["w_gates"], w["conv_w"],
      w["a_log"], w["dt_bias"], w["ret_norm_g"], w["gdn_norm_g"], w["w_rb"], w["w_gb"], w["w_out"])


def _mixer_sample_body(x_ref, cos_ref, sin_ref,
                       pre_ref, post_ref, wmain_ref, wab_ref, wgates_ref, convw_ref, alog_ref, dtb_ref,
                       retg_ref, gdng_ref, wrb_ref, wgb_ref, wout_ref,
                       sret_in, sgdn_in, conv_in,
                       y_ref, sret_out, sgdn_out, conv_out,
                       or_scr, og_scr, *, ret_gamma):
    x = x_ref[...]
    h = _rms(x, pre_ref[...]).astype(BF16)
    pm = jnp.dot(h, wmain_ref[...], preferred_element_type=F32)
    ab = jnp.dot(h, wab_ref[...], preferred_element_type=F32)
    gates = jnp.dot(h, wgates_ref[...], preferred_element_type=F32)
    cos = cos_ref[...]
    sin = sin_ref[...]

    cin = pm[:, OFF_CONV:OFF_CONV + CONV_DIM]
    acc = cin * convw_ref[CONV_W - 1:CONV_W, :]
    for i in range(CONV_W - 1):
        acc = acc + conv_in[:, i * CONV_DIM:(i + 1) * CONV_DIM] * convw_ref[i:i + 1, :]
    conv_out[:, 0:2 * CONV_DIM] = conv_in[:, CONV_DIM:3 * CONV_DIM]
    conv_out[:, 2 * CONV_DIM:3 * CONV_DIM] = cin
    cq = _silu(acc)

    g_all = -jnp.exp(alog_ref[...]) * _softplus(ab + dtb_ref[...])
    eg_all = jnp.exp(g_all)
    beta_all = jax.nn.sigmoid(ab)

    for hh in range(HEADS):
        sl = slice(hh * HD, (hh + 1) * HD)
        q = _rope(pm[:, OFF_RQ + hh * HD:OFF_RQ + (hh + 1) * HD], cos, sin)
        k = _rope(pm[:, OFF_RK + hh * HD:OFF_RK + (hh + 1) * HD], cos, sin) * (HD ** -0.5)
        v = pm[:, OFF_RV + hh * HD:OFF_RV + (hh + 1) * HD]
        q_t, k_t = q.T, k.T
        for i in range(TB):
            s = ret_gamma[hh] * sret_in[i, hh] + k_t[:, i:i + 1] * v[i:i + 1, :]
            sret_out[i, hh] = s
            or_scr[i:i + 1, sl] = jnp.sum(q_t[:, i:i + 1] * s, axis=0, keepdims=True)

        gq = _l2norm(cq[:, hh * HD:(hh + 1) * HD], HD ** -0.5)
        gk = _l2norm(cq[:, QK + hh * HD:QK + (hh + 1) * HD], 1.0)
        gv = cq[:, 2 * QK + hh * HD:2 * QK + (hh + 1) * HD]
        gq_t, gk_t = gq.T, gk.T
        for i in range(TB):
            s = sgdn_in[i, hh]
            eg = eg_all[i:i + 1, hh:hh + 1]
            beta = beta_all[i:i + 1, HEADS + hh:HEADS + hh + 1]
            kcol = gk_t[:, i:i + 1]
            ks = jnp.sum(kcol * s, axis=0, keepdims=True)
            u = beta * gv[i:i + 1, :] - (beta * eg) * ks
            s = eg * s + kcol * u
            sgdn_out[i, hh] = s
            og_scr[i:i + 1, sl] = jnp.sum(gq_t[:, i:i + 1] * s, axis=0, keepdims=True)

    o_r = []
    o_g = []
    for hh in range(HEADS):
        sl = slice(hh * HD, (hh + 1) * HD)
        o_r.append(_ret_out_norm(or_scr[:, sl], retg_ref[:, sl],
                                 pm[:, OFF_RG + hh * HD:OFF_RG + (hh + 1) * HD]))
        o_g.append(_gdn_out_norm(og_scr[:, sl], gdng_ref[...],
                                 pm[:, OFF_GZ + hh * HD:OFF_GZ + (hh + 1) * HD]))
    o_r = jnp.concatenate(o_r, axis=1)
    o_g = jnp.concatenate(o_g, axis=1)
    y_ref[...] = _merge(x, o_r, o_g, gates, wrb_ref, wgb_ref, wout_ref, post_ref)


def _mixer_sample(x, s_ret, s_gdn, s_conv, w, pos):
    n = x.shape[0]
    assert n % TB == 0
    cos, sin = _rope_tables([pos])
    body = functools.partial(_mixer_sample_body, ret_gamma=tuple(float(v) for v in _ret_gammas()))
    state_spec = pl.BlockSpec((TB, HEADS, HD, HD), lambda i: (i, 0, 0, 0))
    conv_spec = pl.BlockSpec((TB, (CONV_W - 1) * CONV_DIM), lambda i: (i, 0))
    row_spec = pl.BlockSpec((TB, D_MODEL), lambda i: (i, 0))
    return pl.pallas_call(
        body,
        grid=(n // TB,),
        in_specs=[
            row_spec, _resident((1, HD)), _resident((1, HD)),
            _resident((1, D_MODEL)), _resident((1, D_MODEL)),
            _resident((D_MODEL, OFF_AB)), _resident((D_MODEL, HD)), _resident((D_MODEL, 2 * D_MODEL)),
            _resident((CONV_W, CONV_DIM)), _resident((1, HD)), _resident((1, HD)),
            _resident((1, QK)), _resident((1, HD)),
            _resident((QK, D_MODEL)), _resident((QK, D_MODEL)), _resident((D_MODEL, D_MODEL)),
            state_spec, state_spec, conv_spec,
        ],
        out_specs=[row_spec, state_spec, state_spec, conv_spec],
        out_shape=[
            jax.ShapeDtypeStruct((n, D_MODEL), F32),
            jax.ShapeDtypeStruct((n, HEADS, HD, HD), F32),
            jax.ShapeDtypeStruct((n, HEADS, HD, HD), F32),
            jax.ShapeDtypeStruct((n, (CONV_W - 1) * CONV_DIM), F32),
        ],
        scratch_shapes=[pltpu.VMEM((TB, QK), F32), pltpu.VMEM((TB, QK), F32)],
        compiler_params=pltpu.CompilerParams(dimension_semantics=("arbitrary",),
                                             vmem_limit_bytes=VMEM_LIMIT),
        name="mixer_sample",
    )(x, cos, sin,
      w["mix_pre_g"], w["mix_post_g"], w["w_main"], w["w_ab"], w["w_gates"], w["conv_w"],
      w["a_log"], w["dt_bias"], w["ret_norm_g"], w["gdn_norm_g"], w["w_rb"], w["w_gb"], w["w_out"],
      s_ret, s_gdn, s_conv)


def _pad_lanes(v, n):
    return jnp.pad(v, ((0, 0), (0, n - v.shape[1])))


def kernel(x_prompt, x_sample, state_ret, state_gdn, state_conv, ffn1_pre_g, ffn1_post_g, ffn1_w_gate,
           ffn1_w_up, ffn1_w_down, mix_pre_g, mix_post_g, w_in, ret_norm_g, gdn_conv_w, gdn_a_log,
           gdn_dt_bias, gdn_norm_g, w_ret_branch, w_gdn_branch, w_out, ffn2_pre_g, ffn2_post_g,
           ffn2_w_gate, ffn2_w_up, ffn2_w_down):
    depth = w_in.shape[0]
    b, t, _ = x_prompt.shape
    n_s, t_s, _ = x_sample.shape
    assert t_s == 1
    yp = x_prompt.reshape(b * t, D_MODEL)
    ys = x_sample.reshape(n_s, D_MODEL)
    outs = [[] for _ in range(6)]
    for l in range(depth):
        row = lambda a: a[l][None, :]
        w = {
            "mix_pre_g": row(mix_pre_g), "mix_post_g": row(mix_post_g),
            "w_main": w_in[l, :, :OFF_AB].astype(BF16),
            "w_ab": _pad_lanes(w_in[l, :, OFF_AB:OFF_GATES], HD).astype(BF16),
            "w_gates": w_in[l, :, OFF_GATES:].astype(BF16),
            "conv_w": gdn_conv_w[l],
            "a_log": _pad_lanes(row(gdn_a_log), HD), "dt_bias": _pad_lanes(row(gdn_dt_bias), HD),
            "ret_norm_g": row(ret_norm_g), "gdn_norm_g": row(gdn_norm_g),
            "w_rb": w_ret_branch[l].astype(BF16), "w_gb": w_gdn_branch[l].astype(BF16),
            "w_out": w_out[l].astype(BF16),
        }
        f1 = (row(ffn1_pre_g), row(ffn1_post_g), ffn1_w_gate[l].astype(BF16),
              ffn1_w_up[l].astype(BF16), ffn1_w_down[l].astype(BF16))
        f2 = (row(ffn2_pre_g), row(ffn2_post_g), ffn2_w_gate[l].astype(BF16),
              ffn2_w_up[l].astype(BF16), ffn2_w_down[l].astype(BF16))

        yp = _ffn(yp, *f1, tm=FFN_TM)
        yp, r1, g1, c1 = _mixer_prompt(yp.reshape(b, t, D_MODEL), w)
        yp = _ffn(yp.reshape(b * t, D_MODEL), *f2, tm=FFN_TM)

        ys = _ffn(ys, *f1, tm=n_s)
        ys, r2, g2, c2 = _mixer_sample(ys, state_ret[l], state_gdn[l],
                                       state_conv[l].reshape(n_s, (CONV_W - 1) * CONV_DIM), w,
                                       PAST_LEN)
        ys = _ffn(ys, *f2, tm=n_s)
        c2 = c2.reshape(n_s, CONV_W - 1, CONV_DIM)
        for lst, val in zip(outs, (r1, g1, c1, r2, g2, c2)):
            lst.append(val)
    stacked = [jnp.stack(v) for v in outs]
    return (yp.reshape(b, t, D_MODEL), ys.reshape(n_s, t_s, D_MODEL), *stacked)
```

```python
import functools

import numpy as np
import jax
import jax.numpy as jnp
from jax import lax
from jax.experimental import pallas as pl
from jax.experimental.pallas import tpu as pltpu

F32 = jnp.float32
BF16 = jnp.bfloat16

D_MODEL = 1024
D_FF = 2816
HEADS = 4
HD = 128
QK = HEADS * HD
CONV_W = 4
CONV_DIM = 3 * QK
CHUNK = 64
ROPE_BASE = 10000.0
EPS = 1e-6
PAST_LEN = 16384

OFF_RQ, OFF_RK, OFF_RV, OFF_RG = 0, QK, 2 * QK, 3 * QK
OFF_CONV = 4 * QK
OFF_GZ = OFF_CONV + CONV_DIM
OFF_AB = OFF_GZ + QK
OFF_GATES = OFF_AB + 2 * HEADS
D_IN = OFF_GATES + 2 * D_MODEL

TM = 256
TB = 8
FFN_TM = 256
CONV_PAD = 8

VMEM_LIMIT = 56 * 1024 * 1024


def _silu(x):
    return x * jax.nn.sigmoid(x)


def _rms(x, g):
    return x * lax.rsqrt(jnp.mean(x * x, axis=-1, keepdims=True) + EPS) * g


def _dot(a, b):
    return jnp.dot(a.astype(BF16), b.astype(BF16), preferred_element_type=F32)


def _dot_nt(a, b):
    return lax.dot_general(a.astype(BF16), b.astype(BF16), (((1,), (1,)), ((), ())),
                           preferred_element_type=F32)


def _dot_tn(a, b):
    return lax.dot_general(a.astype(BF16), b.astype(BF16), (((0,), (0,)), ((), ())),
                           preferred_element_type=F32)


def _split3(a):
    hi = a.astype(BF16)
    r = a - hi.astype(F32)
    mid = r.astype(BF16)
    lo = (r - mid.astype(F32)).astype(BF16)
    return hi, mid, lo


def _dot_x3(a, b):
    a_hi = a.astype(BF16)
    a_lo = (a - a_hi.astype(F32)).astype(BF16)
    b_hi = b.astype(BF16)
    b_lo = (b - b_hi.astype(F32)).astype(BF16)
    d = functools.partial(jnp.dot, preferred_element_type=F32)
    return d(a_hi, b_hi) + (d(a_hi, b_lo) + d(a_lo, b_hi))


def _ffn_body(x_ref, pre_ref, post_ref, wg_ref, wu_ref, wd_ref, o_ref):
    x = x_ref[...]
    h = _rms(x, pre_ref[...]).astype(BF16)
    g = jnp.dot(h, wg_ref[...], preferred_element_type=F32)
    u = jnp.dot(h, wu_ref[...], preferred_element_type=F32)
    a = (_silu(g) * u).astype(BF16)
    y = jnp.dot(a, wd_ref[...], preferred_element_type=F32)
    o_ref[...] = x + 0.5 * _rms(y, post_ref[...])


def _resident(shape):
    nd = len(shape)
    return pl.BlockSpec(shape, lambda *_: (0,) * nd, pipeline_mode=pl.Buffered(1))


def _ffn(x, pre_g, post_g, wg, wu, wd, tm):
    n = x.shape[0]
    assert n % tm == 0
    return pl.pallas_call(
        _ffn_body,
        grid=(n // tm,),
        in_specs=[
            pl.BlockSpec((tm, D_MODEL), lambda i: (i, 0)),
            _resident((1, D_MODEL)), _resident((1, D_MODEL)),
            _resident((D_MODEL, D_FF)), _resident((D_MODEL, D_FF)), _resident((D_FF, D_MODEL)),
        ],
        out_specs=pl.BlockSpec((tm, D_MODEL), lambda i: (i, 0)),
        out_shape=jax.ShapeDtypeStruct((n, D_MODEL), F32),
        compiler_params=pltpu.CompilerParams(dimension_semantics=("arbitrary",),
                                             vmem_limit_bytes=VMEM_LIMIT),
        name="ffn",
    )(x, pre_g, post_g, wg, wu, wd)


def _rope(x, cos, sin_signed):
    return x * cos + pltpu.roll(x, HD // 2, 1) * sin_signed


def _ret_out_norm(o, g_row, gate):
    mu = jnp.mean(o, axis=-1, keepdims=True)
    d = o - mu
    var = jnp.mean(d * d, axis=-1, keepdims=True)
    return _silu(gate) * (d * lax.rsqrt(var + EPS) * g_row)


def _gdn_out_norm(o, g_row, gate):
    return o * lax.rsqrt(jnp.mean(o * o, axis=-1, keepdims=True) + EPS) * g_row * _silu(gate)


def _l2norm(x, scale):
    return x * (lax.rsqrt(jnp.sum(x * x, axis=-1, keepdims=True) + EPS) * scale)


def _softplus(x):
    return jnp.maximum(x, 0.0) + jnp.log(1.0 + jnp.exp(-jnp.abs(x)))


def _merge(x, o_r, o_g, gates, wrb_ref, wgb_ref, wout_ref, post_ref):
    y = (jax.nn.sigmoid(gates[:, :D_MODEL]) * _dot(o_r, wrb_ref[...])
         + jax.nn.sigmoid(gates[:, D_MODEL:]) * _dot(o_g, wgb_ref[...]))
    m = _dot(y, wout_ref[...])
    return x + _rms(m, post_ref[...])


def _unit_lower_inverse_many(a_list, masks):
    eye, m16, off32, off64 = masks
    ad = [a * m16 for a in a_list]
    x = [eye - v for v in ad]
    p = [_dot_x3(v, v) for v in ad]
    for level in range(3):
        x = [xi + _dot_x3(xi, pi) for xi, pi in zip(x, p)]
        if level < 2:
            p = [_dot_x3(pi, pi) for pi in p]
    for m in (off32, off64):
        t = [_dot_x3(xi, a * m) for xi, a in zip(x, a_list)]
        x = [xi - _dot_x3(ti, xi) for xi, ti in zip(x, t)]
    return x


def _inverse_masks():
    r = lax.broadcasted_iota(jnp.int32, (CHUNK, CHUNK), 0)
    c = lax.broadcasted_iota(jnp.int32, (CHUNK, CHUNK), 1)
    one, zero = jnp.float32(1.0), jnp.float32(0.0)
    eye = jnp.where(r == c, one, zero)
    same16 = (r >> 4) == (c >> 4)
    same32 = (r >> 5) == (c >> 5)
    m16 = jnp.where(same16, one, zero)
    m32 = jnp.where(same32, one, zero)
    return eye, m16, m32 - m16, 1.0 - m32


def _mixer_prompt_body(x_ref, cos_ref, sin_ref, rd_ref, ku_ref, ltri_ref,
                       pre_ref, post_ref, wmain_ref, wab_ref, wgates_ref, convw_ref, alog_ref, dtb_ref,
                       retg_ref, gdng_ref, wrb_ref, wgb_ref, wout_ref,
                       y_ref, sret_ref, sgdn_ref, conv_ref,
                       cbuf, *, ret_tile_decay):
    t = pl.program_id(1)

    @pl.when(t == 0)
    def _():
        sret_ref[...] = jnp.zeros_like(sret_ref)
        sgdn_ref[...] = jnp.zeros_like(sgdn_ref)
        cbuf[0:CONV_PAD, :] = jnp.zeros((CONV_PAD, CONV_DIM), F32)

    x = x_ref[0]
    h = _rms(x, pre_ref[...]).astype(BF16)
    pm = jnp.dot(h, wmain_ref[...], preferred_element_type=F32)
    ab = jnp.dot(h, wab_ref[...], preferred_element_type=F32)
    gates = jnp.dot(h, wgates_ref[...], preferred_element_type=F32)

    cos = cos_ref[...]
    sin = sin_ref[...]
    ri = lax.broadcasted_iota(jnp.int32, (TM, TM), 0)
    ci = lax.broadcasted_iota(jnp.int32, (TM, TM), 1)
    causal = ri >= ci
    o_r = []
    for hh in range(HEADS):
        sl = slice(hh * HD, (hh + 1) * HD)
        rq = pm[:, OFF_RQ + hh * HD:OFF_RQ + (hh + 1) * HD]
        rk = pm[:, OFF_RK + hh * HD:OFF_RK + (hh + 1) * HD]
        v = pm[:, OFF_RV + hh * HD:OFF_RV + (hh + 1) * HD]
        rg = pm[:, OFF_RG + hh * HD:OFF_RG + (hh + 1) * HD]
        qs = _rope(rq, cos, sin) * rd_ref[hh]
        ku = _rope(rk, cos, sin) * ku_ref[hh]
        s = sret_ref[0, hh]
        sc = jnp.where(causal, _dot_nt(qs, ku), 0.0)
        o = _dot(sc, v) + _dot(qs, s)
        sret_ref[0, hh] = ret_tile_decay[hh] * (s + _dot_tn(ku, v))
        o_r.append(_ret_out_norm(o, retg_ref[:, sl], rg))
    o_r = jnp.concatenate(o_r, axis=1)

    cbuf[CONV_PAD:CONV_PAD + TM, :] = pm[:, OFF_CONV:OFF_CONV + CONV_DIM]
    acc = cbuf[CONV_PAD - 3:CONV_PAD - 3 + TM, :] * convw_ref[0:1, :]
    for i in range(1, CONV_W):
        acc = acc + cbuf[CONV_PAD - 3 + i:CONV_PAD - 3 + i + TM, :] * convw_ref[i:i + 1, :]
    tail = cbuf[CONV_PAD + TM - 3:CONV_PAD + TM, :]
    conv_ref[0] = tail
    cbuf[CONV_PAD - 3:CONV_PAD, :] = tail
    cq = _silu(acc)

    g_all = -jnp.exp(alog_ref[...]) * _softplus(ab + dtb_ref[...])
    beta_all = jax.nn.sigmoid(ab)
    ltri = ltri_ref[...]
    g_hi, g_mid, g_lo = _split3(g_all)
    dd = functools.partial(jnp.dot, preferred_element_type=F32)
    gcum = dd(ltri, g_hi) + (dd(ltri, g_mid) + dd(ltri, g_lo))
    gcum_t = gcum.T

    r64 = lax.broadcasted_iota(jnp.int32, (CHUNK, CHUNK), 0)
    c64 = lax.broadcasted_iota(jnp.int32, (CHUNK, CHUNK), 1)
    causal64 = r64 >= c64
    strict64 = r64 > c64
    masks = _inverse_masks()

    n_ch = TM // CHUNK
    probs = [(hh, c) for hh in range(HEADS) for c in range(n_ch)]
    gq = [_l2norm(cq[:, hh * HD:(hh + 1) * HD], HD ** -0.5) for hh in range(HEADS)]
    gk = [_l2norm(cq[:, QK + hh * HD:QK + (hh + 1) * HD], 1.0) for hh in range(HEADS)]
    gv = [cq[:, 2 * QK + hh * HD:2 * QK + (hh + 1) * HD] for hh in range(HEADS)]
    rows = [slice(c * CHUNK, (c + 1) * CHUNK) for c in range(n_ch)]
    gc = {(hh, c): gcum[rows[c], hh:hh + 1] for hh, c in probs}
    bcol = {(hh, c): beta_all[rows[c], HEADS + hh:HEADS + hh + 1] for hh, c in probs}
    dm = {(hh, c): jnp.where(causal64,
                             jnp.exp(jnp.minimum(gc[hh, c] - gcum_t[hh:hh + 1, rows[c]], 0.0)), 0.0)
          for hh, c in probs}
    kk = {(hh, c): _dot_nt(gk[hh][rows[c]], gk[hh][rows[c]]) for hh, c in probs}
    qk = {(hh, c): _dot_nt(gq[hh][rows[c]], gk[hh][rows[c]]) * dm[hh, c] for hh, c in probs}
    a_mats = [jnp.where(strict64, bcol[p] * dm[p] * kk[p], 0.0) for p in probs]
    tinv = dict(zip(probs, _unit_lower_inverse_many(a_mats, masks)))
    eg = {p: jnp.exp(gc[p]) for p in probs}
    uw = {}
    for hh, c in probs:
        p = (hh, c)
        rhs = jnp.concatenate([bcol[p] * gv[hh][rows[c]], (bcol[p] * eg[p]) * gk[hh][rows[c]]], axis=1)
        uw[p] = _dot_x3(tinv[p], rhs)

    s = [sgdn_ref[0, hh] for hh in range(HEADS)]
    outs = [[] for _ in range(HEADS)]
    for c in range(n_ch):
        for hh in range(HEADS):
            p = (hh, c)
            kc = gk[hh][rows[c]]
            u = uw[p][:, :HD] - _dot(uw[p][:, HD:], s[hh])
            outs[hh].append(eg[p] * _dot(gq[hh][rows[c]], s[hh]) + _dot(qk[p], u))
            gl = gc[p][CHUNK - 1:CHUNK, :]
            s[hh] = jnp.exp(gl) * s[hh] + _dot_tn(kc * jnp.exp(gl - gc[p]), u)
    o_g = []
    for hh in range(HEADS):
        sgdn_ref[0, hh] = s[hh]
        o = jnp.concatenate(outs[hh], axis=0)
        gz = pm[:, OFF_GZ + hh * HD:OFF_GZ + (hh + 1) * HD]
        o_g.append(_gdn_out_norm(o, gdng_ref[...], gz))
    o_g = jnp.concatenate(o_g, axis=1)

    y_ref[0] = _merge(x, o_r, o_g, gates, wrb_ref, wgb_ref, wout_ref, post_ref)


def _ret_gammas():
    return 1.0 - 2.0 ** (-5.0 - np.arange(HEADS, dtype=np.float64))


def _rope_tables(pos):
    inv = ROPE_BASE ** (-np.arange(0, HD, 2, dtype=np.float64) / HD)
    ang = np.asarray(pos, np.float64)[:, None] * inv[None, :]
    cos = np.concatenate([np.cos(ang), np.cos(ang)], axis=1)
    sin = np.concatenate([-np.sin(ang), np.sin(ang)], axis=1)
    return jnp.asarray(cos, F32), jnp.asarray(sin, F32)


def _mixer_prompt(x, w):
    b, t, _ = x.shape
    assert t % TM == 0
    cos, sin = _rope_tables(np.arange(t))
    gam = _ret_gammas()
    i1 = np.arange(1, TM + 1, dtype=np.float64)
    rd = np.broadcast_to((gam[:, None] ** i1[None, :])[:, :, None], (HEADS, TM, HD))
    ku = np.broadcast_to((HD ** -0.5 * gam[:, None] ** (-i1[None, :]))[:, :, None], (HEADS, TM, HD))
    tile_decay = tuple(float(v) for v in gam ** TM)
    r = np.arange(TM)
    ltri = ((r[:, None] >= r[None, :]) & (r[:, None] // CHUNK == r[None, :] // CHUNK))

    body = functools.partial(_mixer_prompt_body, ret_tile_decay=tile_decay)
    state_spec = pl.BlockSpec((1, HEADS, HD, HD), lambda i, j: (i, 0, 0, 0))
    return pl.pallas_call(
        body,
        grid=(b, t // TM),
        in_specs=[
            pl.BlockSpec((1, TM, D_MODEL), lambda i, j: (i, j, 0)),
            pl.BlockSpec((TM, HD), lambda i, j: (j, 0)),
            pl.BlockSpec((TM, HD), lambda i, j: (j, 0)),
            _resident((HEADS, TM, HD)), _resident((HEADS, TM, HD)), _resident((TM, TM)),
            _resident((1, D_MODEL)), _resident((1, D_MODEL)),
            _resident((D_MODEL, OFF_AB)), _resident((D_MODEL, HD)), _resident((D_MODEL, 2 * D_MODEL)),
            _resident((CONV_W, CONV_DIM)), _resident((1, HD)), _resident((1, HD)),
            _resident((1, QK)), _resident((1, HD)),
            _resident((QK, D_MODEL)), _resident((QK, D_MODEL)), _resident((D_MODEL, D_MODEL)),
        ],
        out_specs=[
            pl.BlockSpec((1, TM, D_MODEL), lambda i, j: (i, j, 0)),
            state_spec, state_spec,
            pl.BlockSpec((1, CONV_W - 1, CONV_DIM), lambda i, j: (i, 0, 0)),
        ],
        out_shape=[
            jax.ShapeDtypeStruct((b, t, D_MODEL), F32),
            jax.ShapeDtypeStruct((b, HEADS, HD, HD), F32),
            jax.ShapeDtypeStruct((b, HEADS, HD, HD), F32),
            jax.ShapeDtypeStruct((b, CONV_W - 1, CONV_DIM), F32),
        ],
        scratch_shapes=[pltpu.VMEM((CONV_PAD + TM, CONV_DIM), F32)],
        compiler_params=pltpu.CompilerParams(dimension_semantics=("arbitrary", "arbitrary"),
                                             vmem_limit_bytes=VMEM_LIMIT),
        name="mixer_prompt",
    )(x, cos, sin, jnp.asarray(rd, F32), jnp.asarray(ku, F32), jnp.asarray(ltri, BF16),
      w["mix_pre_g"], w["mix_post_g"], w["w_main"], w["w_ab"], w["w_gates"], w["conv_w"],
      w["a_log"], w["dt_bias"], w["ret_norm_g"], w["gdn_norm_g"], w["w_rb"], w["w_gb"], w["w_out"])


def _mixer_sample_body(x_ref, cos_ref, sin_ref,
                       pre_ref, post_ref, wmain_ref, wab_ref, wgates_ref, convw_ref, alog_ref, dtb_ref,
                       retg_ref, gdng_ref, wrb_ref, wgb_ref, wout_ref,
                       sret_in, sgdn_in, conv_in,
                       y_ref, sret_out, sgdn_out, conv_out,
                       or_scr, og_scr, *, ret_gamma):
    x = x_ref[...]
    h = _rms(x, pre_ref[...]).astype(BF16)
    pm = jnp.dot(h, wmain_ref[...], preferred_element_type=F32)
    ab = jnp.dot(h, wab_ref[...], preferred_element_type=F32)
    gates = jnp.dot(h, wgates_ref[...], preferred_element_type=F32)
    cos = cos_ref[...]
    sin = sin_ref[...]

    cin = pm[:, OFF_CONV:OFF_CONV + CONV_DIM]
    acc = cin * convw_ref[CONV_W - 1:CONV_W, :]
    for i in range(CONV_W - 1):
        acc = acc + conv_in[:, i * CONV_DIM:(i + 1) * CONV_DIM] * convw_ref[i:i + 1, :]
    conv_out[:, 0:2 * CONV_DIM] = conv_in[:, CONV_DIM:3 * CONV_DIM]
    conv_out[:, 2 * CONV_DIM:3 * CONV_DIM] = cin
    cq = _silu(acc)

    g_all = -jnp.exp(alog_ref[...]) * _softplus(ab + dtb_ref[...])
    eg_all = jnp.exp(g_all)
    beta_all = jax.nn.sigmoid(ab)

    for hh in range(HEADS):
        sl = slice(hh * HD, (hh + 1) * HD)
        q = _rope(pm[:, OFF_RQ + hh * HD:OFF_RQ + (hh + 1) * HD], cos, sin)
        k = _rope(pm[:, OFF_RK + hh * HD:OFF_RK + (hh + 1) * HD], cos, sin) * (HD ** -0.5)
        v = pm[:, OFF_RV + hh * HD:OFF_RV + (hh + 1) * HD]
        q_t, k_t = q.T, k.T
        for i in range(TB):
            s = ret_gamma[hh] * sret_in[i, hh] + k_t[:, i:i + 1] * v[i:i + 1, :]
            sret_out[i, hh] = s
            or_scr[i:i + 1, sl] = jnp.sum(q_t[:, i:i + 1] * s, axis=0, keepdims=True)

        gq = _l2norm(cq[:, hh * HD:(hh + 1) * HD], HD ** -0.5)
        gk = _l2norm(cq[:, QK + hh * HD:QK + (hh + 1) * HD], 1.0)
        gv = cq[:, 2 * QK + hh * HD:2 * QK + (hh + 1) * HD]
        gq_t, gk_t = gq.T, gk.T
        for i in range(TB):
            s = sgdn_in[i, hh]
            eg = eg_all[i:i + 1, hh:hh + 1]
            beta = beta_all[i:i + 1, HEADS + hh:HEADS + hh + 1]
            kcol = gk_t[:, i:i + 1]
            ks = jnp.sum(kcol * s, axis=0, keepdims=True)
            u = beta * gv[i:i + 1, :] - (beta * eg) * ks
            s = eg * s + kcol * u
            sgdn_out[i, hh] = s
            og_scr[i:i + 1, sl] = jnp.sum(gq_t[:, i:i + 1] * s, axis=0, keepdims=True)

    o_r = []
    o_g = []
    for hh in range(HEADS):
        sl = slice(hh * HD, (hh + 1) * HD)
        o_r.append(_ret_out_norm(or_scr[:, sl], retg_ref[:, sl],
                                 pm[:, OFF_RG + hh * HD:OFF_RG + (hh + 1) * HD]))
        o_g.append(_gdn_out_norm(og_scr[:, sl], gdng_ref[...],
                                 pm[:, OFF_GZ + hh * HD:OFF_GZ + (hh + 1) * HD]))
    o_r = jnp.concatenate(o_r, axis=1)
    o_g = jnp.concatenate(o_g, axis=1)
    y_ref[...] = _merge(x, o_r, o_g, gates, wrb_ref, wgb_ref, wout_ref, post_ref)


def _mixer_sample(x, s_ret, s_gdn, s_conv, w, pos):
    n = x.shape[0]
    assert n % TB == 0
    cos, sin = _rope_tables([pos])
    body = functools.partial(_mixer_sample_body, ret_gamma=tuple(float(v) for v in _ret_gammas()))
    state_spec = pl.BlockSpec((TB, HEADS, HD, HD), lambda i: (i, 0, 0, 0))
    conv_spec = pl.BlockSpec((TB, (CONV_W - 1) * CONV_DIM), lambda i: (i, 0))
    row_spec = pl.BlockSpec((TB, D_MODEL), lambda i: (i, 0))
    return pl.pallas_call(
        body,
        grid=(n // TB,),
        in_specs=[
            row_spec, _resident((1, HD)), _resident((1, HD)),
            _resident((1, D_MODEL)), _resident((1, D_MODEL)),
            _resident((D_MODEL, OFF_AB)), _resident((D_MODEL, HD)), _resident((D_MODEL, 2 * D_MODEL)),
            _resident((CONV_W, CONV_DIM)), _resident((1, HD)), _resident((1, HD)),
            _resident((1, QK)), _resident((1, HD)),
            _resident((QK, D_MODEL)), _resident((QK, D_MODEL)), _resident((D_MODEL, D_MODEL)),
            state_spec, state_spec, conv_spec,
        ],
        out_specs=[row_spec, state_spec, state_spec, conv_spec],
        out_shape=[
            jax.ShapeDtypeStruct((n, D_MODEL), F32),
            jax.ShapeDtypeStruct((n, HEADS, HD, HD), F32),
            jax.ShapeDtypeStruct((n, HEADS, HD, HD), F32),
            jax.ShapeDtypeStruct((n, (CONV_W - 1) * CONV_DIM), F32),
        ],
        scratch_shapes=[pltpu.VMEM((TB, QK), F32), pltpu.VMEM((TB, QK), F32)],
        compiler_params=pltpu.CompilerParams(dimension_semantics=("arbitrary",),
                                             vmem_limit_bytes=VMEM_LIMIT),
        name="mixer_sample",
    )(x, cos, sin,
      w["mix_pre_g"], w["mix_post_g"], w["w_main"], w["w_ab"], w["w_gates"], w["conv_w"],
      w["a_log"], w["dt_bias"], w["ret_norm_g"], w["gdn_norm_g"], w["w_rb"], w["w_gb"], w["w_out"],
      s_ret, s_gdn, s_conv)


def _pad_lanes(v, n):
    return jnp.pad(v, ((0, 0), (0, n - v.shape[1])))


def kernel(x_prompt, x_sample, state_ret, state_gdn, state_conv, ffn1_pre_g, ffn1_post_g, ffn1_w_gate,
           ffn1_w_up, ffn1_w_down, mix_pre_g, mix_post_g, w_in, ret_norm_g, gdn_conv_w, gdn_a_log,
           gdn_dt_bias, gdn_norm_g, w_ret_branch, w_gdn_branch, w_out, ffn2_pre_g, ffn2_post_g,
           ffn2_w_gate, ffn2_w_up, ffn2_w_down):
    depth = w_in.shape[0]
    b, t, _ = x_prompt.shape
    n_s, t_s, _ = x_sample.shape
    assert t_s == 1
    yp = x_prompt.reshape(b * t, D_MODEL)
    ys = x_sample.reshape(n_s, D_MODEL)
    outs = [[] for _ in range(6)]
    for l in range(depth):
        row = lambda a: a[l][None, :]
        w = {
            "mix_pre_g": row(mix_pre_g), "mix_post_g": row(mix_post_g),
            "w_main": w_in[l, :, :OFF_AB].astype(BF16),
            "w_ab": _pad_lanes(w_in[l, :, OFF_AB:OFF_GATES], HD).astype(BF16),
            "w_gates": w_in[l, :, OFF_GATES:].astype(BF16),
            "conv_w": gdn_conv_w[l],
            "a_log": _pad_lanes(row(gdn_a_log), HD), "dt_bias": _pad_lanes(row(gdn_dt_bias), HD),
            "ret_norm_g": row(ret_norm_g), "gdn_norm_g": row(gdn_norm_g),
            "w_rb": w_ret_branch[l].astype(BF16), "w_gb": w_gdn_branch[l].astype(BF16),
            "w_out": w_out[l].astype(BF16),
        }
        f1 = (row(ffn1_pre_g), row(ffn1_post_g), ffn1_w_gate[l].astype(BF16),
              ffn1_w_up[l].astype(BF16), ffn1_w_down[l].astype(BF16))
        f2 = (row(ffn2_pre_g), row(ffn2_post_g), ffn2_w_gate[l].astype(BF16),
              ffn2_w_up[l].astype(BF16), ffn2_w_down[l].astype(BF16))

        yp = _ffn(yp, *f1, tm=FFN_TM)
        yp, r1, g1, c1 = _mixer_prompt(yp.reshape(b, t, D_MODEL), w)
        yp = _ffn(yp.reshape(b * t, D_MODEL), *f2, tm=FFN_TM)

        ys = _ffn(ys, *f1, tm=n_s)
        ys, r2, g2, c2 = _mixer_sample(ys, state_ret[l], state_gdn[l],
                                       state_conv[l].reshape(n_s, (CONV_W - 1) * CONV_DIM), w,
                                       PAST_LEN)
        ys = _ffn(ys, *f2, tm=n_s)
        c2 = c2.reshape(n_s, CONV_W - 1, CONV_DIM)
        for lst, val in zip(outs, (r1, g1, c1, r2, g2, c2)):
            lst.append(val)
    stacked = [jnp.stack(v) for v in outs]
    return (yp.reshape(b, t, D_MODEL), ys.reshape(n_s, t_s, D_MODEL), *stacked)
```

```python
import functools

import numpy as np
import jax
import jax.numpy as jnp
from jax import lax
from jax.experimental import pallas as pl
from jax.experimental.pallas import tpu as pltpu

F32 = jnp.float32
BF16 = jnp.bfloat16

D_MODEL = 1024
D_FF = 2816
HEADS = 4
HD = 128
QK = HEADS * HD
CONV_W = 4
CONV_DIM = 3 * QK
CHUNK = 64
ROPE_BASE = 10000.0
EPS = 1e-6
PAST_LEN = 16384

OFF_RQ, OFF_RK, OFF_RV, OFF_RG = 0, QK, 2 * QK, 3 * QK
OFF_CONV = 4 * QK
OFF_GZ = OFF_CONV + CONV_DIM
OFF_AB = OFF_GZ + QK
OFF_GATES = OFF_AB + 2 * HEADS
D_IN = OFF_GATES + 2 * D_MODEL

TM = 256
TB = 8
FFN_TM = 256
CONV_PAD = 8

VMEM_LIMIT = 56 * 1024 * 1024


def _silu(x):
    return x * jax.nn.sigmoid(x)


def _rms(x, g):
    return x * lax.rsqrt(jnp.mean(x * x, axis=-1, keepdims=True) + EPS) * g


def _dot(a, b):
    return jnp.dot(a.astype(BF16), b.astype(BF16), preferred_element_type=F32)


def _dot_nt(a, b):
    return lax.dot_general(a.astype(BF16), b.astype(BF16), (((1,), (1,)), ((), ())),
                           preferred_element_type=F32)


def _dot_tn(a, b):
    return lax.dot_general(a.astype(BF16), b.astype(BF16), (((0,), (0,)), ((), ())),
                           preferred_element_type=F32)


def _split3(a):
    hi = a.astype(BF16)
    r = a - hi.astype(F32)
    mid = r.astype(BF16)
    lo = (r - mid.astype(F32)).astype(BF16)
    return hi, mid, lo


def _ffn_body(x_ref, pre_ref, post_ref, wg_ref, wu_ref, wd_ref, o_ref):
    x = x_ref[...]
    h = _rms(x, pre_ref[...]).astype(BF16)
    g = jnp.dot(h, wg_ref[...], preferred_element_type=F32)
    u = jnp.dot(h, wu_ref[...], preferred_element_type=F32)
    a = (_silu(g) * u).astype(BF16)
    y = jnp.dot(a, wd_ref[...], preferred_element_type=F32)
    o_ref[...] = x + 0.5 * _rms(y, post_ref[...])


def _resident(shape):
    nd = len(shape)
    return pl.BlockSpec(shape, lambda *_: (0,) * nd, pipeline_mode=pl.Buffered(1))


def _ffn(x, pre_g, post_g, wg, wu, wd, tm):
    n = x.shape[0]
    assert n % tm == 0
    return pl.pallas_call(
        _ffn_body,
        grid=(n // tm,),
        in_specs=[
            pl.BlockSpec((tm, D_MODEL), lambda i: (i, 0)),
            _resident((1, D_MODEL)), _resident((1, D_MODEL)),
            _resident((D_MODEL, D_FF)), _resident((D_MODEL, D_FF)), _resident((D_FF, D_MODEL)),
        ],
        out_specs=pl.BlockSpec((tm, D_MODEL), lambda i: (i, 0)),
        out_shape=jax.ShapeDtypeStruct((n, D_MODEL), F32),
        compiler_params=pltpu.CompilerParams(dimension_semantics=("arbitrary",),
                                             vmem_limit_bytes=VMEM_LIMIT),
        name="ffn",
    )(x, pre_g, post_g, wg, wu, wd)


def _rope(x, cos, sin_signed):
    return x * cos + pltpu.roll(x, HD // 2, 1) * sin_signed


def _ret_out_norm(o, g_row, gate):
    mu = jnp.mean(o, axis=-1, keepdims=True)
    d = o - mu
    var = jnp.mean(d * d, axis=-1, keepdims=True)
    return _silu(gate) * (d * lax.rsqrt(var + EPS) * g_row)


def _gdn_out_norm(o, g_row, gate):
    return o * lax.rsqrt(jnp.mean(o * o, axis=-1, keepdims=True) + EPS) * g_row * _silu(gate)


def _l2norm(x, scale):
    return x * (lax.rsqrt(jnp.sum(x * x, axis=-1, keepdims=True) + EPS) * scale)


def _softplus(x):
    return jnp.maximum(x, 0.0) + jnp.log(1.0 + jnp.exp(-jnp.abs(x)))


def _merge(x, o_r, o_g, gates, wrb_ref, wgb_ref, wout_ref, post_ref):
    y = (jax.nn.sigmoid(gates[:, :D_MODEL]) * _dot(o_r, wrb_ref[...])
         + jax.nn.sigmoid(gates[:, D_MODEL:]) * _dot(o_g, wgb_ref[...]))
    m = _dot(y, wout_ref[...])
    return x + _rms(m, post_ref[...])


def _unit_lower_inverse_many(a_list, masks):
    eye, m16, off32, off64 = masks
    ad = [a * m16 for a in a_list]
    x = [eye - v for v in ad]
    p = [_dot(v, v) for v in ad]
    for level in range(3):
        x = [xi + _dot(xi, pi) for xi, pi in zip(x, p)]
        if level < 2:
            p = [_dot(pi, pi) for pi in p]
    for m in (off32, off64):
        t = [_dot(xi, a * m) for xi, a in zip(x, a_list)]
        x = [xi - _dot(ti, xi) for xi, ti in zip(x, t)]
    return x


def _inverse_masks():
    r = lax.broadcasted_iota(jnp.int32, (CHUNK, CHUNK), 0)
    c = lax.broadcasted_iota(jnp.int32, (CHUNK, CHUNK), 1)
    one, zero = jnp.float32(1.0), jnp.float32(0.0)
    eye = jnp.where(r == c, one, zero)
    same16 = (r >> 4) == (c >> 4)
    same32 = (r >> 5) == (c >> 5)
    m16 = jnp.where(same16, one, zero)
    m32 = jnp.where(same32, one, zero)
    return eye, m16, m32 - m16, 1.0 - m32


def _mixer_prompt_body(x_ref, cos_ref, sin_ref, rd_ref, ku_ref, ltri_ref,
                       pre_ref, post_ref, wmain_ref, wab_ref, wgates_ref, convw_ref, alog_ref, dtb_ref,
                       retg_ref, gdng_ref, wrb_ref, wgb_ref, wout_ref,
                       y_ref, sret_ref, sgdn_ref, conv_ref,
                       cbuf, *, ret_tile_decay):
    t = pl.program_id(1)

    @pl.when(t == 0)
    def _():
        sret_ref[...] = jnp.zeros_like(sret_ref)
        sgdn_ref[...] = jnp.zeros_like(sgdn_ref)
        cbuf[0:CONV_PAD, :] = jnp.zeros((CONV_PAD, CONV_DIM), F32)

    x = x_ref[0]
    h = _rms(x, pre_ref[...]).astype(BF16)
    pm = jnp.dot(h, wmain_ref[:, 0:OFF_CONV], preferred_element_type=F32)
    pg = jnp.dot(h, wmain_ref[:, OFF_CONV:OFF_AB], preferred_element_type=F32)
    ab = jnp.dot(h, wab_ref[...], preferred_element_type=F32)

    cos = cos_ref[...]
    sin = sin_ref[...]
    ri = lax.broadcasted_iota(jnp.int32, (TM, TM), 0)
    ci = lax.broadcasted_iota(jnp.int32, (TM, TM), 1)
    causal = ri >= ci
    o_r = []
    for hh in range(HEADS):
        sl = slice(hh * HD, (hh + 1) * HD)
        rq = pm[:, OFF_RQ + hh * HD:OFF_RQ + (hh + 1) * HD]
        rk = pm[:, OFF_RK + hh * HD:OFF_RK + (hh + 1) * HD]
        v = pm[:, OFF_RV + hh * HD:OFF_RV + (hh + 1) * HD]
        rg = pm[:, OFF_RG + hh * HD:OFF_RG + (hh + 1) * HD]
        qs = _rope(rq, cos, sin) * rd_ref[hh]
        ku = _rope(rk, cos, sin) * ku_ref[hh]
        s = sret_ref[0, hh]
        sc = jnp.where(causal, _dot_nt(qs, ku), 0.0)
        o = _dot(sc, v) + _dot(qs, s)
        sret_ref[0, hh] = ret_tile_decay[hh] * (s + _dot_tn(ku, v))
        o_r.append(_ret_out_norm(o, retg_ref[:, sl], rg))
    o_r = jnp.concatenate(o_r, axis=1)

    cbuf[CONV_PAD:CONV_PAD + TM, :] = pg[:, 0:CONV_DIM]
    acc = cbuf[CONV_PAD - 3:CONV_PAD - 3 + TM, :] * convw_ref[0:1, :]
    for i in range(1, CONV_W):
        acc = acc + cbuf[CONV_PAD - 3 + i:CONV_PAD - 3 + i + TM, :] * convw_ref[i:i + 1, :]
    tail = cbuf[CONV_PAD + TM - 3:CONV_PAD + TM, :]
    conv_ref[0] = tail
    cbuf[CONV_PAD - 3:CONV_PAD, :] = tail
    cq = _silu(acc)
    gates = jnp.dot(h, wgates_ref[...], preferred_element_type=F32)

    g_all = -jnp.exp(alog_ref[...]) * _softplus(ab + dtb_ref[...])
    beta_all = jax.nn.sigmoid(ab)
    ltri = ltri_ref[...]
    g_hi, g_mid, g_lo = _split3(g_all)
    dd = functools.partial(jnp.dot, preferred_element_type=F32)
    gcum = dd(ltri, g_hi) + (dd(ltri, g_mid) + dd(ltri, g_lo))
    gcum_t = gcum.T

    r64 = lax.broadcasted_iota(jnp.int32, (CHUNK, CHUNK), 0)
    c64 = lax.broadcasted_iota(jnp.int32, (CHUNK, CHUNK), 1)
    causal64 = r64 >= c64
    strict64 = r64 > c64
    masks = _inverse_masks()

    n_ch = TM // CHUNK
    probs = [(hh, c) for hh in range(HEADS) for c in range(n_ch)]
    gq = [_l2norm(cq[:, hh * HD:(hh + 1) * HD], HD ** -0.5) for hh in range(HEADS)]
    gk = [_l2norm(cq[:, QK + hh * HD:QK + (hh + 1) * HD], 1.0) for hh in range(HEADS)]
    gv = [cq[:, 2 * QK + hh * HD:2 * QK + (hh + 1) * HD] for hh in range(HEADS)]
    rows = [slice(c * CHUNK, (c + 1) * CHUNK) for c in range(n_ch)]
    gc = {(hh, c): gcum[rows[c], hh:hh + 1] for hh, c in probs}
    bcol = {(hh, c): beta_all[rows[c], HEADS + hh:HEADS + hh + 1] for hh, c in probs}
    dm = {(hh, c): jnp.where(causal64,
                             jnp.exp(jnp.minimum(gc[hh, c] - gcum_t[hh:hh + 1, rows[c]], 0.0)), 0.0)
          for hh, c in probs}
    kk = {(hh, c): _dot_nt(gk[hh][rows[c]], gk[hh][rows[c]]) for hh, c in probs}
    qk = {(hh, c): _dot_nt(gq[hh][rows[c]], gk[hh][rows[c]]) * dm[hh, c] for hh, c in probs}
    a_mats = [jnp.where(strict64, bcol[p] * dm[p] * kk[p], 0.0) for p in probs]
    tinv = dict(zip(probs, _unit_lower_inverse_many(a_mats, masks)))
    eg = {p: jnp.exp(gc[p]) for p in probs}
    uw = {}
    for hh, c in probs:
        p = (hh, c)
        rhs = jnp.concatenate([bcol[p] * gv[hh][rows[c]], (bcol[p] * eg[p]) * gk[hh][rows[c]]], axis=1)
        uw[p] = _dot(tinv[p], rhs)

    s = [sgdn_ref[0, hh] for hh in range(HEADS)]
    outs = [[] for _ in range(HEADS)]
    for c in range(n_ch):
        for hh in range(HEADS):
            p = (hh, c)
            kc = gk[hh][rows[c]]
            u = uw[p][:, :HD] - _dot(uw[p][:, HD:], s[hh])
            outs[hh].append(eg[p] * _dot(gq[hh][rows[c]], s[hh]) + _dot(qk[p], u))
            gl = gc[p][CHUNK - 1:CHUNK, :]
            s[hh] = jnp.exp(gl) * s[hh] + _dot_tn(kc * jnp.exp(gl - gc[p]), u)
    o_g = []
    for hh in range(HEADS):
        sgdn_ref[0, hh] = s[hh]
        o = jnp.concatenate(outs[hh], axis=0)
        gz = pg[:, CONV_DIM + hh * HD:CONV_DIM + (hh + 1) * HD]
        o_g.append(_gdn_out_norm(o, gdng_ref[...], gz))
    o_g = jnp.concatenate(o_g, axis=1)

    y_ref[0] = _merge(x, o_r, o_g, gates, wrb_ref, wgb_ref, wout_ref, post_ref)


def _ret_gammas():
    return 1.0 - 2.0 ** (-5.0 - np.arange(HEADS, dtype=np.float64))


def _rope_tables(pos):
    inv = ROPE_BASE ** (-np.arange(0, HD, 2, dtype=np.float64) / HD)
    ang = np.asarray(pos, np.float64)[:, None] * inv[None, :]
    cos = np.concatenate([np.cos(ang), np.cos(ang)], axis=1)
    sin = np.concatenate([-np.sin(ang), np.sin(ang)], axis=1)
    return jnp.asarray(cos, F32), jnp.asarray(sin, F32)


def _mixer_prompt(x, w):
    b, t, _ = x.shape
    assert t % TM == 0
    cos, sin = _rope_tables(np.arange(t))
    gam = _ret_gammas()
    i1 = np.arange(1, TM + 1, dtype=np.float64)
    rd = np.broadcast_to((gam[:, None] ** i1[None, :])[:, :, None], (HEADS, TM, HD))
    ku = np.broadcast_to((HD ** -0.5 * gam[:, None] ** (-i1[None, :]))[:, :, None], (HEADS, TM, HD))
    tile_decay = tuple(float(v) for v in gam ** TM)
    r = np.arange(TM)
    ltri = ((r[:, None] >= r[None, :]) & (r[:, None] // CHUNK == r[None, :] // CHUNK))

    body = functools.partial(_mixer_prompt_body, ret_tile_decay=tile_decay)
    state_spec = pl.BlockSpec((1, HEADS, HD, HD), lambda i, j: (i, 0, 0, 0))
    return pl.pallas_call(
        body,
        grid=(b, t // TM),
        in_specs=[
            pl.BlockSpec((1, TM, D_MODEL), lambda i, j: (i, j, 0)),
            pl.BlockSpec((TM, HD), lambda i, j: (j, 0)),
            pl.BlockSpec((TM, HD), lambda i, j: (j, 0)),
            _resident((HEADS, TM, HD)), _resident((HEADS, TM, HD)), _resident((TM, TM)),
            _resident((1, D_MODEL)), _resident((1, D_MODEL)),
            _resident((D_MODEL, OFF_AB)), _resident((D_MODEL, HD)), _resident((D_MODEL, 2 * D_MODEL)),
            _resident((CONV_W, CONV_DIM)), _resident((1, HD)), _resident((1, HD)),
            _resident((1, QK)), _resident((1, HD)),
            _resident((QK, D_MODEL)), _resident((QK, D_MODEL)), _resident((D_MODEL, D_MODEL)),
        ],
        out_specs=[
            pl.BlockSpec((1, TM, D_MODEL), lambda i, j: (i, j, 0)),
            state_spec, state_spec,
            pl.BlockSpec((1, CONV_W - 1, CONV_DIM), lambda i, j: (i, 0, 0)),
        ],
        out_shape=[
            jax.ShapeDtypeStruct((b, t, D_MODEL), F32),
            jax.ShapeDtypeStruct((b, HEADS, HD, HD), F32),
            jax.ShapeDtypeStruct((b, HEADS, HD, HD), F32),
            jax.ShapeDtypeStruct((b, CONV_W - 1, CONV_DIM), F32),
        ],
        scratch_shapes=[pltpu.VMEM((CONV_PAD + TM, CONV_DIM), F32)],
        compiler_params=pltpu.CompilerParams(dimension_semantics=("arbitrary", "arbitrary"),
                                             vmem_limit_bytes=VMEM_LIMIT),
        name="mixer_prompt",
    )(x, cos, sin, jnp.asarray(rd, F32), jnp.asarray(ku, F32), jnp.asarray(ltri, BF16),
      w["mix_pre_g"], w["mix_post_g"], w["w_main"], w["w_ab"], w["w_gates"], w["conv_w"],
      w["a_log"], w["dt_bias"], w["ret_norm_g"], w["gdn_norm_g"], w["w_rb"], w["w_gb"], w["w_out"])


def _mixer_sample_body(x_ref, cos_ref, sin_ref,
                       pre_ref, post_ref, wmain_ref, wab_ref, wgates_ref, convw_ref, alog_ref, dtb_ref,
                       retg_ref, gdng_ref, wrb_ref, wgb_ref, wout_ref,
                       sret_in, sgdn_in, conv_in,
                       y_ref, sret_out, sgdn_out, conv_out,
                       or_scr, og_scr, *, ret_gamma):
    x = x_ref[...]
    h = _rms(x, pre_ref[...]).astype(BF16)
    pm = jnp.dot(h, wmain_ref[...], preferred_element_type=F32)
    ab = jnp.dot(h, wab_ref[...], preferred_element_type=F32)
    gates = jnp.dot(h, wgates_ref[...], preferred_element_type=F32)
    cos = cos_ref[...]
    sin = sin_ref[...]

    cin = pm[:, OFF_CONV:OFF_CONV + CONV_DIM]
    acc = cin * convw_ref[CONV_W - 1:CONV_W, :]
    for i in range(CONV_W - 1):
        acc = acc + conv_in[:, i * CONV_DIM:(i + 1) * CONV_DIM] * convw_ref[i:i + 1, :]
    conv_out[:, 0:2 * CONV_DIM] = conv_in[:, CONV_DIM:3 * CONV_DIM]
    conv_out[:, 2 * CONV_DIM:3 * CONV_DIM] = cin
    cq = _silu(acc)

    g_all = -jnp.exp(alog_ref[...]) * _softplus(ab + dtb_ref[...])
    eg_all = jnp.exp(g_all)
    beta_all = jax.nn.sigmoid(ab)

    for hh in range(HEADS):
        sl = slice(hh * HD, (hh + 1) * HD)
        q = _rope(pm[:, OFF_RQ + hh * HD:OFF_RQ + (hh + 1) * HD], cos, sin)
        k = _rope(pm[:, OFF_RK + hh * HD:OFF_RK + (hh + 1) * HD], cos, sin) * (HD ** -0.5)
        v = pm[:, OFF_RV + hh * HD:OFF_RV + (hh + 1) * HD]
        q_t, k_t = q.T, k.T
        for i in range(TB):
            s = ret_gamma[hh] * sret_in[i, hh] + k_t[:, i:i + 1] * v[i:i + 1, :]
            sret_out[i, hh] = s
            or_scr[i:i + 1, sl] = jnp.sum(q_t[:, i:i + 1] * s, axis=0, keepdims=True)

        gq = _l2norm(cq[:, hh * HD:(hh + 1) * HD], HD ** -0.5)
        gk = _l2norm(cq[:, QK + hh * HD:QK + (hh + 1) * HD], 1.0)
        gv = cq[:, 2 * QK + hh * HD:2 * QK + (hh + 1) * HD]
        gq_t, gk_t = gq.T, gk.T
        for i in range(TB):
            s = sgdn_in[i, hh]
            eg = eg_all[i:i + 1, hh:hh + 1]
            beta = beta_all[i:i + 1, HEADS + hh:HEADS + hh + 1]
            kcol = gk_t[:, i:i + 1]
            ks = jnp.sum(kcol * s, axis=0, keepdims=True)
            u = beta * gv[i:i + 1, :] - (beta * eg) * ks
            s = eg * s + kcol * u
            sgdn_out[i, hh] = s
            og_scr[i:i + 1, sl] = jnp.sum(gq_t[:, i:i + 1] * s, axis=0, keepdims=True)

    o_r = []
    o_g = []
    for hh in range(HEADS):
        sl = slice(hh * HD, (hh + 1) * HD)
        o_r.append(_ret_out_norm(or_scr[:, sl], retg_ref[:, sl],
                                 pm[:, OFF_RG + hh * HD:OFF_RG + (hh + 1) * HD]))
        o_g.append(_gdn_out_norm(og_scr[:, sl], gdng_ref[...],
                                 pm[:, OFF_GZ + hh * HD:OFF_GZ + (hh + 1) * HD]))
    o_r = jnp.concatenate(o_r, axis=1)
    o_g = jnp.concatenate(o_g, axis=1)
    y_ref[...] = _merge(x, o_r, o_g, gates, wrb_ref, wgb_ref, wout_ref, post_ref)


def _mixer_sample(x, s_ret, s_gdn, s_conv, w, pos):
    n = x.shape[0]
    assert n % TB == 0
    cos, sin = _rope_tables([pos])
    body = functools.partial(_mixer_sample_body, ret_gamma=tuple(float(v) for v in _ret_gammas()))
    state_spec = pl.BlockSpec((TB, HEADS, HD, HD), lambda i: (i, 0, 0, 0))
    conv_spec = pl.BlockSpec((TB, (CONV_W - 1) * CONV_DIM), lambda i: (i, 0))
    row_spec = pl.BlockSpec((TB, D_MODEL), lambda i: (i, 0))
    return pl.pallas_call(
        body,
        grid=(n // TB,),
        in_specs=[
            row_spec, _resident((1, HD)), _resident((1, HD)),
            _resident((1, D_MODEL)), _resident((1, D_MODEL)),
            _resident((D_MODEL, OFF_AB)), _resident((D_MODEL, HD)), _resident((D_MODEL, 2 * D_MODEL)),
            _resident((CONV_W, CONV_DIM)), _resident((1, HD)), _resident((1, HD)),
            _resident((1, QK)), _resident((1, HD)),
            _resident((QK, D_MODEL)), _resident((QK, D_MODEL)), _resident((D_MODEL, D_MODEL)),
            state_spec, state_spec, conv_spec,
        ],
        out_specs=[row_spec, state_spec, state_spec, conv_spec],
        out_shape=[
            jax.ShapeDtypeStruct((n, D_MODEL), F32),
            jax.ShapeDtypeStruct((n, HEADS, HD, HD), F32),
            jax.ShapeDtypeStruct((n, HEADS, HD, HD), F32),
            jax.ShapeDtypeStruct((n, (CONV_W - 1) * CONV_DIM), F32),
        ],
        scratch_shapes=[pltpu.VMEM((TB, QK), F32), pltpu.VMEM((TB, QK), F32)],
        compiler_params=pltpu.CompilerParams(dimension_semantics=("arbitrary",),
                                             vmem_limit_bytes=VMEM_LIMIT),
        name="mixer_sample",
    )(x, cos, sin,
      w["mix_pre_g"], w["mix_post_g"], w["w_main"], w["w_ab"], w["w_gates"], w["conv_w"],
      w["a_log"], w["dt_bias"], w["ret_norm_g"], w["gdn_norm_g"], w["w_rb"], w["w_gb"], w["w_out"],
      s_ret, s_gdn, s_conv)


def _pad_lanes(v, n):
    return jnp.pad(v, ((0, 0), (0, n - v.shape[1])))


def kernel(x_prompt, x_sample, state_ret, state_gdn, state_conv, ffn1_pre_g, ffn1_post_g, ffn1_w_gate,
           ffn1_w_up, ffn1_w_down, mix_pre_g, mix_post_g, w_in, ret_norm_g, gdn_conv_w, gdn_a_log,
           gdn_dt_bias, gdn_norm_g, w_ret_branch, w_gdn_branch, w_out, ffn2_pre_g, ffn2_post_g,
           ffn2_w_gate, ffn2_w_up, ffn2_w_down):
    depth = w_in.shape[0]
    b, t, _ = x_prompt.shape
    n_s, t_s, _ = x_sample.shape
    assert t_s == 1
    yp = x_prompt.reshape(b * t, D_MODEL)
    ys = x_sample.reshape(n_s, D_MODEL)
    outs = [[] for _ in range(6)]
    for l in range(depth):
        row = lambda a: a[l][None, :]
        w = {
            "mix_pre_g": row(mix_pre_g), "mix_post_g": row(mix_post_g),
            "w_main": w_in[l, :, :OFF_AB].astype(BF16),
            "w_ab": _pad_lanes(w_in[l, :, OFF_AB:OFF_GATES], HD).astype(BF16),
            "w_gates": w_in[l, :, OFF_GATES:].astype(BF16),
            "conv_w": gdn_conv_w[l],
            "a_log": _pad_lanes(row(gdn_a_log), HD), "dt_bias": _pad_lanes(row(gdn_dt_bias), HD),
            "ret_norm_g": row(ret_norm_g), "gdn_norm_g": row(gdn_norm_g),
            "w_rb": w_ret_branch[l].astype(BF16), "w_gb": w_gdn_branch[l].astype(BF16),
            "w_out": w_out[l].astype(BF16),
        }
        f1 = (row(ffn1_pre_g), row(ffn1_post_g), ffn1_w_gate[l].astype(BF16),
              ffn1_w_up[l].astype(BF16), ffn1_w_down[l].astype(BF16))
        f2 = (row(ffn2_pre_g), row(ffn2_post_g), ffn2_w_gate[l].astype(BF16),
              ffn2_w_up[l].astype(BF16), ffn2_w_down[l].astype(BF16))

        yp = _ffn(yp, *f1, tm=FFN_TM)
        yp, r1, g1, c1 = _mixer_prompt(yp.reshape(b, t, D_MODEL), w)
        yp = _ffn(yp.reshape(b * t, D_MODEL), *f2, tm=FFN_TM)

        ys = _ffn(ys, *f1, tm=n_s)
        ys, r2, g2, c2 = _mixer_sample(ys, state_ret[l], state_gdn[l],
                                       state_conv[l].reshape(n_s, (CONV_W - 1) * CONV_DIM), w,
                                       PAST_LEN)
        ys = _ffn(ys, *f2, tm=n_s)
        c2 = c2.reshape(n_s, CONV_W - 1, CONV_DIM)
        for lst, val in zip(outs, (r1, g1, c1, r2, g2, c2)):
            lst.append(val)
    stacked = [v[0][None] if depth == 1 else jnp.stack(v) for v in outs]
    return (yp.reshape(b, t, D_MODEL), ys.reshape(n_s, t_s, D_MODEL), *stacked)
```

```python
import functools

import numpy as np
import jax
import jax.numpy as jnp
from jax import lax
from jax.experimental import pallas as pl
from jax.experimental.pallas import tpu as pltpu

F32 = jnp.float32
BF16 = jnp.bfloat16

D_MODEL = 1024
D_FF = 2816
HEADS = 4
HD = 128
QK = HEADS * HD
CONV_W = 4
CONV_DIM = 3 * QK
CHUNK = 64
ROPE_BASE = 10000.0
EPS = 1e-6
PAST_LEN = 16384

OFF_RQ, OFF_RK, OFF_RV, OFF_RG = 0, QK, 2 * QK, 3 * QK
OFF_CONV = 4 * QK
OFF_GZ = OFF_CONV + CONV_DIM
OFF_AB = OFF_GZ + QK
OFF_GATES = OFF_AB + 2 * HEADS
D_IN = OFF_GATES + 2 * D_MODEL

TM = 256
NB = 1
TB = 8
FFN_TM = 256
CONV_PAD = 8
BF16_ROWS = 16

VMEM_LIMIT = 56 * 1024 * 1024


def _silu(x):
    return x * jax.nn.sigmoid(x)


def _rms(x, g):
    return x * lax.rsqrt(jnp.mean(x * x, axis=-1, keepdims=True) + EPS) * g


def _dot(a, b):
    return jnp.dot(a.astype(BF16), b.astype(BF16), preferred_element_type=F32)


def _dot_nt(a, b):
    return lax.dot_general(a.astype(BF16), b.astype(BF16), (((1,), (1,)), ((), ())),
                           preferred_element_type=F32)


def _dot_tn(a, b):
    return lax.dot_general(a.astype(BF16), b.astype(BF16), (((0,), (0,)), ((), ())),
                           preferred_element_type=F32)


def _split3(a):
    hi = a.astype(BF16)
    r = a - hi.astype(F32)
    mid = r.astype(BF16)
    lo = (r - mid.astype(F32)).astype(BF16)
    return hi, mid, lo


def _resident(shape):
    nd = len(shape)
    return pl.BlockSpec(shape, lambda *_: (0,) * nd, pipeline_mode=pl.Buffered(1))


def _ffn_body(x_ref, pre_ref, post_ref, wg_ref, wu_ref, wd_ref, *rest, cast_cols):
    n_cast = len(cast_cols)
    cast_in, o_ref, cast_out = rest[:n_cast], rest[n_cast], rest[n_cast + 1:]
    x = x_ref[...]
    h = _rms(x, pre_ref[...]).astype(BF16)
    g = jnp.dot(h, wg_ref[...], preferred_element_type=F32)
    u = jnp.dot(h, wu_ref[...], preferred_element_type=F32)
    a = (_silu(g) * u).astype(BF16)
    y = jnp.dot(a, wd_ref[...], preferred_element_type=F32)
    o_ref[...] = x + 0.5 * _rms(y, post_ref[...])
    k = 0
    for src, cols in zip(cast_in, cast_cols):
        for c0, c1 in cols:
            cast_out[k][...] = src[:, c0:c1].astype(BF16)
            k += 1


def _ffn(x, pre_g, post_g, wg, wu, wd, tm, casts=()):
    n = x.shape[0]
    assert n % tm == 0
    steps = n // tm
    in_specs = [
        pl.BlockSpec((tm, D_MODEL), lambda i: (i, 0)),
        _resident((1, D_MODEL)), _resident((1, D_MODEL)),
        _resident((D_MODEL, D_FF)), _resident((D_MODEL, D_FF)), _resident((D_FF, D_MODEL)),
    ]
    out_specs = [pl.BlockSpec((tm, D_MODEL), lambda i: (i, 0))]
    out_shape = [jax.ShapeDtypeStruct((n, D_MODEL), F32)]
    for arr, rb, cols in casts:
        rows, width = arr.shape
        assert rows % rb == 0 and rb % BF16_ROWS == 0 and rows // rb <= steps
        last = rows // rb - 1
        in_specs.append(pl.BlockSpec((rb, width), lambda i, last=last: (jnp.minimum(i, last), 0)))
        for c0, c1 in cols:
            out_specs.append(pl.BlockSpec((rb, c1 - c0), lambda i, last=last: (jnp.minimum(i, last), 0)))
            out_shape.append(jax.ShapeDtypeStruct((rows, c1 - c0), BF16))
    outs = pl.pallas_call(
        functools.partial(_ffn_body, cast_cols=tuple(c[2] for c in casts)),
        grid=(steps,),
        in_specs=in_specs,
        out_specs=out_specs,
        out_shape=out_shape,
        compiler_params=pltpu.CompilerParams(dimension_semantics=("arbitrary",),
                                             vmem_limit_bytes=VMEM_LIMIT),
        name="ffn",
    )(x, pre_g, post_g, wg, wu, wd, *[c[0] for c in casts])
    return outs[0], list(outs[1:])


def _rope(x, cos, sin_signed):
    return x * cos + pltpu.roll(x, HD // 2, 1) * sin_signed


def _ret_out_norm(o, g_row, gate):
    mu = jnp.mean(o, axis=-1, keepdims=True)
    d = o - mu
    var = jnp.mean(d * d, axis=-1, keepdims=True)
    return _silu(gate) * (d * lax.rsqrt(var + EPS) * g_row)


def _gdn_out_norm(o, g_row, gate):
    return o * lax.rsqrt(jnp.mean(o * o, axis=-1, keepdims=True) + EPS) * g_row * _silu(gate)


def _l2norm(x, scale):
    return x * (lax.rsqrt(jnp.sum(x * x, axis=-1, keepdims=True) + EPS) * scale)


def _softplus(x):
    return jnp.maximum(x, 0.0) + jnp.log(1.0 + jnp.exp(-jnp.abs(x)))


def _merge(x, o_r, o_g, gates, wrb_ref, wgb_ref, wout_ref, post_ref):
    y = (jax.nn.sigmoid(gates[:, :D_MODEL]) * _dot(o_r, wrb_ref[...])
         + jax.nn.sigmoid(gates[:, D_MODEL:]) * _dot(o_g, wgb_ref[...]))
    m = _dot(y, wout_ref[...])
    return x + _rms(m, post_ref[...])


def _unit_lower_inverse_many(a_list, masks):
    eye, m16, off32, off64 = masks
    ad = [a * m16 for a in a_list]
    x = [eye - v for v in ad]
    p = [_dot(v, v) for v in ad]
    for level in range(3):
        x = [xi + _dot(xi, pi) for xi, pi in zip(x, p)]
        if level < 2:
            p = [_dot(pi, pi) for pi in p]
    for m in (off32, off64):
        t = [_dot(xi, a * m) for xi, a in zip(x, a_list)]
        x = [xi - _dot(ti, xi) for xi, ti in zip(x, t)]
    return x


def _inverse_masks():
    r = lax.broadcasted_iota(jnp.int32, (CHUNK, CHUNK), 0)
    c = lax.broadcasted_iota(jnp.int32, (CHUNK, CHUNK), 1)
    one, zero = jnp.float32(1.0), jnp.float32(0.0)
    eye = jnp.where(r == c, one, zero)
    same16 = (r >> 4) == (c >> 4)
    same32 = (r >> 5) == (c >> 5)
    m16 = jnp.where(same16, one, zero)
    m32 = jnp.where(same32, one, zero)
    return eye, m16, m32 - m16, 1.0 - m32


def _retention_tile(pm, cos, sin, rd_ref, ku_ref, retg_ref, s_ref, tile_decay):
    ri = lax.broadcasted_iota(jnp.int32, (TM, TM), 0)
    ci = lax.broadcasted_iota(jnp.int32, (TM, TM), 1)
    causal = ri >= ci
    heads = []
    for hh in range(HEADS):
        sl = slice(hh * HD, (hh + 1) * HD)
        rq = pm[:, OFF_RQ + hh * HD:OFF_RQ + (hh + 1) * HD]
        rk = pm[:, OFF_RK + hh * HD:OFF_RK + (hh + 1) * HD]
        v = pm[:, OFF_RV + hh * HD:OFF_RV + (hh + 1) * HD]
        rg = pm[:, OFF_RG + hh * HD:OFF_RG + (hh + 1) * HD]
        qs = _rope(rq, cos, sin) * rd_ref[hh]
        ku = _rope(rk, cos, sin) * ku_ref[hh]
        s = s_ref[hh]
        sc = jnp.where(causal, _dot_nt(qs, ku), 0.0)
        o = _dot(sc, v) + _dot(qs, s)
        s_ref[hh] = tile_decay[hh] * (s + _dot_tn(ku, v))
        heads.append(_ret_out_norm(o, retg_ref[:, sl], rg))
    return jnp.concatenate(heads, axis=1)


def _short_conv_tile(u, buf, convw_ref, tail_ref):
    buf[CONV_PAD:CONV_PAD + TM, :] = u
    acc = buf[CONV_PAD - 3:CONV_PAD - 3 + TM, :] * convw_ref[0:1, :]
    for i in range(1, CONV_W):
        acc = acc + buf[CONV_PAD - 3 + i:CONV_PAD - 3 + i + TM, :] * convw_ref[i:i + 1, :]
    tail = buf[CONV_PAD + TM - 3:CONV_PAD + TM, :]
    tail_ref[...] = tail
    buf[CONV_PAD - 3:CONV_PAD, :] = tail
    return _silu(acc)


def _gdn_decay_beta(ab, alog_ref, dtb_ref):
    g_all = -jnp.exp(alog_ref[...]) * _softplus(ab + dtb_ref[...])
    beta_all = jax.nn.sigmoid(ab)
    return g_all, beta_all


def _gdn_tile(cq, ab, gz, ltri, alog_ref, dtb_ref, gdng_ref, s_ref):
    g_all, beta_all = _gdn_decay_beta(ab, alog_ref, dtb_ref)
    g_hi, g_mid, g_lo = _split3(g_all)
    dd = functools.partial(jnp.dot, preferred_element_type=F32)
    gcum = dd(ltri, g_hi) + (dd(ltri, g_mid) + dd(ltri, g_lo))
    gcum_t = gcum.T

    r64 = lax.broadcasted_iota(jnp.int32, (CHUNK, CHUNK), 0)
    c64 = lax.broadcasted_iota(jnp.int32, (CHUNK, CHUNK), 1)
    causal64 = r64 >= c64
    strict64 = r64 > c64
    masks = _inverse_masks()

    n_ch = TM // CHUNK
    rows = [slice(c * CHUNK, (c + 1) * CHUNK) for c in range(n_ch)]
    probs = [(hh, c) for hh in range(HEADS) for c in range(n_ch)]
    gq = [_l2norm(cq[:, hh * HD:(hh + 1) * HD], HD ** -0.5) for hh in range(HEADS)]
    gk = [_l2norm(cq[:, QK + hh * HD:QK + (hh + 1) * HD], 1.0) for hh in range(HEADS)]
    gv = [cq[:, 2 * QK + hh * HD:2 * QK + (hh + 1) * HD] for hh in range(HEADS)]
    qc = {(hh, c): gq[hh][rows[c]] for hh, c in probs}
    kc = {(hh, c): gk[hh][rows[c]] for hh, c in probs}
    vc = {(hh, c): gv[hh][rows[c]] for hh, c in probs}
    gc = {(hh, c): gcum[rows[c], hh:hh + 1] for hh, c in probs}
    bcol = {(hh, c): beta_all[rows[c], HEADS + hh:HEADS + hh + 1] for hh, c in probs}
    dm = {(hh, c): jnp.where(causal64,
                             jnp.exp(jnp.minimum(gc[hh, c] - gcum_t[hh:hh + 1, rows[c]], 0.0)), 0.0)
          for hh, c in probs}
    kk = {p: _dot_nt(kc[p], kc[p]) for p in probs}
    qk = {p: _dot_nt(qc[p], kc[p]) * dm[p] for p in probs}
    a_mats = [jnp.where(strict64, bcol[p] * dm[p] * kk[p], 0.0) for p in probs]
    tinv = dict(zip(probs, _unit_lower_inverse_many(a_mats, masks)))
    eg = {p: jnp.exp(gc[p]) for p in probs}
    uw = {p: _dot(tinv[p], jnp.concatenate([bcol[p] * vc[p], (bcol[p] * eg[p]) * kc[p]], axis=1))
          for p in probs}

    s = [s_ref[hh] for hh in range(HEADS)]
    outs = [[] for _ in range(HEADS)]
    for c in range(n_ch):
        for hh in range(HEADS):
            p = (hh, c)
            u = uw[p][:, :HD] - _dot(uw[p][:, HD:], s[hh])
            outs[hh].append(eg[p] * _dot(qc[p], s[hh]) + _dot(qk[p], u))
            gl = gc[p][CHUNK - 1:CHUNK, :]
            s[hh] = jnp.exp(gl) * s[hh] + _dot_tn(kc[p] * jnp.exp(gl - gc[p]), u)
    heads = []
    for hh in range(HEADS):
        s_ref[hh] = s[hh]
        o = jnp.concatenate(outs[hh], axis=0)
        heads.append(_gdn_out_norm(o, gdng_ref[...], gz[:, hh * HD:(hh + 1) * HD]))
    return jnp.concatenate(heads, axis=1)


def _mixer_prompt_body(x_ref, cos_ref, sin_ref, rd_ref, ku_ref, ltri_ref,
                       pre_ref, post_ref, wmain_ref, wab_ref, wgates_ref, convw_ref, alog_ref, dtb_ref,
                       retg_ref, gdng_ref, wrb_ref, wgb_ref, wout_ref,
                       y_ref, sret_ref, sgdn_ref, conv_ref,
                       cbuf, *, ret_tile_decay):
    t = pl.program_id(1)
    nb = x_ref.shape[0]

    @pl.when(t == 0)
    def _():
        sret_ref[...] = jnp.zeros_like(sret_ref)
        sgdn_ref[...] = jnp.zeros_like(sgdn_ref)
        cbuf[:, 0:CONV_PAD, :] = jnp.zeros((nb, CONV_PAD, CONV_DIM), F32)

    cos = cos_ref[...]
    sin = sin_ref[...]
    ltri = ltri_ref[...]
    for b in range(nb):
        x = x_ref[b]
        h = _rms(x, pre_ref[...]).astype(BF16)
        pg = jnp.dot(h, wmain_ref[:, OFF_CONV:OFF_AB], preferred_element_type=F32)
        ab = jnp.dot(h, wab_ref[...], preferred_element_type=F32)
        cq = _short_conv_tile(pg[:, 0:CONV_DIM], cbuf.at[b], convw_ref, conv_ref.at[b])
        pm = jnp.dot(h, wmain_ref[:, 0:OFF_CONV], preferred_element_type=F32)
        o_r = _retention_tile(pm, cos, sin, rd_ref, ku_ref, retg_ref, sret_ref.at[b], ret_tile_decay)
        o_g = _gdn_tile(cq, ab, pg[:, CONV_DIM:], ltri, alog_ref, dtb_ref, gdng_ref, sgdn_ref.at[b])
        gates = jnp.dot(h, wgates_ref[...], preferred_element_type=F32)
        y_ref[b] = _merge(x, o_r, o_g, gates, wrb_ref, wgb_ref, wout_ref, post_ref)


def _ret_gammas():
    return 1.0 - 2.0 ** (-5.0 - np.arange(HEADS, dtype=np.float64))


def _rope_tables(pos):
    inv = ROPE_BASE ** (-np.arange(0, HD, 2, dtype=np.float64) / HD)
    ang = np.asarray(pos, np.float64)[:, None] * inv[None, :]
    cos = np.concatenate([np.cos(ang), np.cos(ang)], axis=1)
    sin = np.concatenate([-np.sin(ang), np.sin(ang)], axis=1)
    return jnp.asarray(cos, F32), jnp.asarray(sin, F32)


def _mixer_weight_specs():
    return [
        _resident((1, D_MODEL)), _resident((1, D_MODEL)),
        _resident((D_MODEL, OFF_AB)), _resident((D_MODEL, HD)), _resident((D_MODEL, 2 * D_MODEL)),
        _resident((CONV_W, CONV_DIM)), _resident((1, HD)), _resident((1, HD)),
        _resident((1, QK)), _resident((1, HD)),
        _resident((QK, D_MODEL)), _resident((QK, D_MODEL)), _resident((D_MODEL, D_MODEL)),
    ]


def _mixer_weight_args(w):
    return (w["mix_pre_g"], w["mix_post_g"], w["w_main"], w["w_ab"], w["w_gates"], w["conv_w"],
            w["a_log"], w["dt_bias"], w["ret_norm_g"], w["gdn_norm_g"], w["w_rb"], w["w_gb"], w["w_out"])


def _mixer_prompt(x, w):
    b, t, _ = x.shape
    assert t % TM == 0 and b % NB == 0
    cos, sin = _rope_tables(np.arange(t))
    gam = _ret_gammas()
    i1 = np.arange(1, TM + 1, dtype=np.float64)
    rd = np.broadcast_to((gam[:, None] ** i1[None, :])[:, :, None], (HEADS, TM, HD))
    ku = np.broadcast_to((HD ** -0.5 * gam[:, None] ** (-i1[None, :]))[:, :, None], (HEADS, TM, HD))
    tile_decay = tuple(float(v) for v in gam ** TM)
    r = np.arange(TM)
    ltri = ((r[:, None] >= r[None, :]) & (r[:, None] // CHUNK == r[None, :] // CHUNK))

    body = functools.partial(_mixer_prompt_body, ret_tile_decay=tile_decay)
    state_spec = pl.BlockSpec((NB, HEADS, HD, HD), lambda i, j: (i, 0, 0, 0))
    return pl.pallas_call(
        body,
        grid=(b // NB, t // TM),
        in_specs=[
            pl.BlockSpec((NB, TM, D_MODEL), lambda i, j: (i, j, 0)),
            pl.BlockSpec((TM, HD), lambda i, j: (j, 0)),
            pl.BlockSpec((TM, HD), lambda i, j: (j, 0)),
            _resident((HEADS, TM, HD)), _resident((HEADS, TM, HD)), _resident((TM, TM)),
        ] + _mixer_weight_specs(),
        out_specs=[
            pl.BlockSpec((NB, TM, D_MODEL), lambda i, j: (i, j, 0)),
            state_spec, state_spec,
            pl.BlockSpec((NB, CONV_W - 1, CONV_DIM), lambda i, j: (i, 0, 0)),
        ],
        out_shape=[
            jax.ShapeDtypeStruct((b, t, D_MODEL), F32),
            jax.ShapeDtypeStruct((b, HEADS, HD, HD), F32),
            jax.ShapeDtypeStruct((b, HEADS, HD, HD), F32),
            jax.ShapeDtypeStruct((b, CONV_W - 1, CONV_DIM), F32),
        ],
        scratch_shapes=[pltpu.VMEM((NB, CONV_PAD + TM, CONV_DIM), F32)],
        compiler_params=pltpu.CompilerParams(dimension_semantics=("arbitrary", "arbitrary"),
                                             vmem_limit_bytes=VMEM_LIMIT),
        name="mixer_prompt",
    )(x, cos, sin, jnp.asarray(rd, F32), jnp.asarray(ku, F32), jnp.asarray(ltri, BF16),
      *_mixer_weight_args(w))


def _pick_rows(rows):
    ri = lax.broadcasted_iota(jnp.int32, rows[0].shape, 0)
    out = rows[0]
    for j in range(1, len(rows)):
        out = jnp.where(ri == j, rows[j], out)
    return out


def _mixer_sample_body(x_ref, cos_ref, sin_ref,
                       pre_ref, post_ref, wmain_ref, wab_ref, wgates_ref, convw_ref, alog_ref, dtb_ref,
                       retg_ref, gdng_ref, wrb_ref, wgb_ref, wout_ref,
                       sret_in, sgdn_in, conv_in,
                       y_ref, sret_out, sgdn_out, conv_out,
                       pm_s, gates_s, rq_s, rk_s, gq_s, gk_s, gv_s, eg_s, beta_s, or_s, og_s, *, ret_gamma):
    i = pl.program_id(0)

    @pl.when(i == 0)
    def _():
        h = _rms(x_ref[...], pre_ref[...]).astype(BF16)
        pm = jnp.dot(h, wmain_ref[...], preferred_element_type=F32)
        ab = jnp.dot(h, wab_ref[...], preferred_element_type=F32)
        pm_s[...] = pm
        gates_s[...] = jnp.dot(h, wgates_ref[...], preferred_element_type=F32)
        cos = cos_ref[...]
        sin = sin_ref[...]
        cin = pm[:, OFF_CONV:OFF_CONV + CONV_DIM]
        acc = cin * convw_ref[CONV_W - 1:CONV_W, :]
        for r in range(CONV_W - 1):
            acc = acc + conv_in[:, r * CONV_DIM:(r + 1) * CONV_DIM] * convw_ref[r:r + 1, :]
        conv_out[:, 0:2 * CONV_DIM] = conv_in[:, CONV_DIM:3 * CONV_DIM]
        conv_out[:, 2 * CONV_DIM:3 * CONV_DIM] = cin
        cq = _silu(acc)
        g_all, beta_all = _gdn_decay_beta(ab, alog_ref, dtb_ref)
        eg_s[...] = jnp.exp(g_all)
        beta_s[...] = beta_all
        for hh in range(HEADS):
            sl = slice(hh * HD, (hh + 1) * HD)
            rq_s[:, sl] = _rope(pm[:, OFF_RQ + hh * HD:OFF_RQ + (hh + 1) * HD], cos, sin)
            rk_s[:, sl] = _rope(pm[:, OFF_RK + hh * HD:OFF_RK + (hh + 1) * HD], cos, sin) * (HD ** -0.5)
            gq_s[:, sl] = _l2norm(cq[:, hh * HD:(hh + 1) * HD], HD ** -0.5)
            gk_s[:, sl] = _l2norm(cq[:, QK + hh * HD:QK + (hh + 1) * HD], 1.0)
        gv_s[...] = cq[:, 2 * QK:3 * QK]

    rows = pl.ds(pl.multiple_of(i * TB, TB), TB)
    eg_all = eg_s[rows, :]
    beta_all = beta_s[rows, :]
    for hh in range(HEADS):
        sl = slice(hh * HD, (hh + 1) * HD)
        q = rq_s[rows, sl]
        k_t = rk_s[rows, sl].T
        v = pm_s[rows, OFF_RV + hh * HD:OFF_RV + (hh + 1) * HD]
        o_rows = []
        for j in range(TB):
            s = ret_gamma[hh] * sret_in[j, hh] + k_t[:, j:j + 1] * v[j:j + 1, :]
            sret_out[j, hh] = s
            o_rows.append(_dot(q, s))
        or_s[rows, sl] = _pick_rows(o_rows)

        gq = gq_s[rows, sl]
        gk = gk_s[rows, sl]
        gk_t = gk.T
        gv = gv_s[rows, sl]
        o_rows = []
        for j in range(TB):
            s = sgdn_in[j, hh]
            eg = eg_all[j:j + 1, hh:hh + 1]
            beta = beta_all[j:j + 1, HEADS + hh:HEADS + hh + 1]
            ks = _dot(gk, s)[j:j + 1, :]
            u = beta * gv[j:j + 1, :] - (beta * eg) * ks
            s = eg * s + gk_t[:, j:j + 1] * u
            sgdn_out[j, hh] = s
            o_rows.append(_dot(gq, s))
        og_s[rows, sl] = _pick_rows(o_rows)

    @pl.when(i == pl.num_programs(0) - 1)
    def _():
        o_r = []
        o_g = []
        for hh in range(HEADS):
            sl = slice(hh * HD, (hh + 1) * HD)
            o_r.append(_ret_out_norm(or_s[:, sl], retg_ref[:, sl],
                                     pm_s[:, OFF_RG + hh * HD:OFF_RG + (hh + 1) * HD]))
            o_g.append(_gdn_out_norm(og_s[:, sl], gdng_ref[...],
                                     pm_s[:, OFF_GZ + hh * HD:OFF_GZ + (hh + 1) * HD]))
        y_ref[...] = _merge(x_ref[...], jnp.concatenate(o_r, axis=1), jnp.concatenate(o_g, axis=1),
                            gates_s[...], wrb_ref, wgb_ref, wout_ref, post_ref)


def _mixer_sample(x, s_ret, s_gdn, s_conv, w, pos):
    n = x.shape[0]
    assert n % TB == 0
    cos, sin = _rope_tables([pos])
    body = functools.partial(_mixer_sample_body, ret_gamma=tuple(float(v) for v in _ret_gammas()))
    state_spec = pl.BlockSpec((TB, HEADS, HD, HD), lambda i: (i, 0, 0, 0))
    conv_cols = (CONV_W - 1) * CONV_DIM
    return pl.pallas_call(
        body,
        grid=(n // TB,),
        in_specs=[_resident((n, D_MODEL)), _resident((1, HD)), _resident((1, HD))]
        + _mixer_weight_specs()
        + [state_spec, state_spec, _resident((n, conv_cols))],
        out_specs=[pl.BlockSpec((n, D_MODEL), lambda i: (0, 0)), state_spec, state_spec,
                   pl.BlockSpec((n, conv_cols), lambda i: (0, 0))],
        out_shape=[
            jax.ShapeDtypeStruct((n, D_MODEL), F32),
            jax.ShapeDtypeStruct((n, HEADS, HD, HD), F32),
            jax.ShapeDtypeStruct((n, HEADS, HD, HD), F32),
            jax.ShapeDtypeStruct((n, conv_cols), F32),
        ],
        scratch_shapes=[pltpu.VMEM((n, OFF_AB), F32), pltpu.VMEM((n, 2 * D_MODEL), F32)]
        + [pltpu.VMEM((n, QK), F32) for _ in range(5)]
        + [pltpu.VMEM((n, HD), F32), pltpu.VMEM((n, HD), F32)]
        + [pltpu.VMEM((n, QK), F32), pltpu.VMEM((n, QK), F32)],
        compiler_params=pltpu.CompilerParams(dimension_semantics=("arbitrary",),
                                             vmem_limit_bytes=VMEM_LIMIT),
        name="mixer_sample",
    )(x, cos, sin, *_mixer_weight_args(w), s_ret, s_gdn, s_conv)


def _pad_lanes(v, n):
    return jnp.pad(v, ((0, 0), (0, n - v.shape[1])))


def kernel(x_prompt, x_sample, state_ret, state_gdn, state_conv, ffn1_pre_g, ffn1_post_g, ffn1_w_gate,
           ffn1_w_up, ffn1_w_down, mix_pre_g, mix_post_g, w_in, ret_norm_g, gdn_conv_w, gdn_a_log,
           gdn_dt_bias, gdn_norm_g, w_ret_branch, w_gdn_branch, w_out, ffn2_pre_g, ffn2_post_g,
           ffn2_w_gate, ffn2_w_up, ffn2_w_down):
    depth = w_in.shape[0]
    b, t, _ = x_prompt.shape
    n_s, t_s, _ = x_sample.shape
    assert t_s == 1
    yp = x_prompt.reshape(b * t, D_MODEL)
    ys = x_sample.reshape(n_s, D_MODEL)
    outs = [[] for _ in range(6)]
    for l in range(depth):
        row = lambda a: a[l][None, :]
        f1 = (row(ffn1_pre_g), row(ffn1_post_g), ffn1_w_gate[l].astype(BF16),
              ffn1_w_up[l].astype(BF16), ffn1_w_down[l].astype(BF16))

        full = lambda a: ((0, a.shape[1]),)
        casts = (
            (w_in[l], BF16_ROWS, ((0, OFF_AB), (OFF_GATES, D_IN))),
            (w_ret_branch[l], BF16_ROWS, full(w_ret_branch[l])),
            (w_gdn_branch[l], BF16_ROWS, full(w_gdn_branch[l])),
            (w_out[l], BF16_ROWS, full(w_out[l])),
            (ffn2_w_gate[l], BF16_ROWS, full(ffn2_w_gate[l])),
            (ffn2_w_up[l], BF16_ROWS, full(ffn2_w_up[l])),
            (ffn2_w_down[l], 4 * BF16_ROWS, full(ffn2_w_down[l])),
        )
        yp, (w_main, w_gates, w_rb, w_gb, w_o, g2, u2, d2) = _ffn(yp, *f1, tm=FFN_TM, casts=casts)
        w = {
            "mix_pre_g": row(mix_pre_g), "mix_post_g": row(mix_post_g),
            "w_main": w_main,
            "w_ab": _pad_lanes(w_in[l, :, OFF_AB:OFF_GATES], HD).astype(BF16),
            "w_gates": w_gates,
            "conv_w": gdn_conv_w[l],
            "a_log": _pad_lanes(row(gdn_a_log), HD), "dt_bias": _pad_lanes(row(gdn_dt_bias), HD),
            "ret_norm_g": row(ret_norm_g), "gdn_norm_g": row(gdn_norm_g),
            "w_rb": w_rb, "w_gb": w_gb, "w_out": w_o,
        }
        f2 = (row(ffn2_pre_g), row(ffn2_post_g), g2, u2, d2)

        yp, r1, g1, c1 = _mixer_prompt(yp.reshape(b, t, D_MODEL), w)
        yp, _ = _ffn(yp.reshape(b * t, D_MODEL), *f2, tm=FFN_TM)

        ys, _ = _ffn(ys, *f1, tm=n_s)
        ys, r2, g2s, c2 = _mixer_sample(ys, state_ret[l], state_gdn[l],
                                        state_conv[l].reshape(n_s, (CONV_W - 1) * CONV_DIM), w,
                                        PAST_LEN)
        ys, _ = _ffn(ys, *f2, tm=n_s)
        c2 = c2.reshape(n_s, CONV_W - 1, CONV_DIM)
        for lst, val in zip(outs, (r1, g1, c1, r2, g2s, c2)):
            lst.append(val)
    stacked = [v[0][None] if depth == 1 else jnp.stack(v) for v in outs]
    return (yp.reshape(b, t, D_MODEL), ys.reshape(n_s, t_s, D_MODEL), *stacked)
```

```python
import functools

import numpy as np
import jax
import jax.numpy as jnp
from jax import lax
from jax.experimental import pallas as pl
from jax.experimental.pallas import tpu as pltpu

F32 = jnp.float32
BF16 = jnp.bfloat16

D_MODEL = 1024
D_FF = 2816
HEADS = 4
HD = 128
QK = HEADS * HD
CONV_W = 4
CONV_DIM = 3 * QK
CHUNK = 64
ROPE_BASE = 10000.0
EPS = 1e-6
PAST_LEN = 16384

OFF_RQ, OFF_RK, OFF_RV, OFF_RG = 0, QK, 2 * QK, 3 * QK
OFF_CONV = 4 * QK
OFF_GZ = OFF_CONV + CONV_DIM
OFF_AB = OFF_GZ + QK
OFF_GATES = OFF_AB + 2 * HEADS
D_IN = OFF_GATES + 2 * D_MODEL

TM = 256
NB = 1
TB = 8
FFN_TM = 512
CONV_PAD = 8
BF16_ROWS = 16
F32_ROWS = 8

VMEM_LIMIT = 56 * 1024 * 1024


def _silu(x):
    return x * jax.nn.sigmoid(x)


def _rms(x, g):
    return x * lax.rsqrt(jnp.mean(x * x, axis=-1, keepdims=True) + EPS) * g


def _dot(a, b):
    return jnp.dot(a.astype(BF16), b.astype(BF16), preferred_element_type=F32)


def _dot_nt(a, b):
    return lax.dot_general(a.astype(BF16), b.astype(BF16), (((1,), (1,)), ((), ())),
                           preferred_element_type=F32)


def _dot_tn(a, b):
    return lax.dot_general(a.astype(BF16), b.astype(BF16), (((0,), (0,)), ((), ())),
                           preferred_element_type=F32)


def _split3(a):
    hi = a.astype(BF16)
    r = a - hi.astype(F32)
    mid = r.astype(BF16)
    lo = (r - mid.astype(F32)).astype(BF16)
    return hi, mid, lo


def _resident(shape):
    nd = len(shape)
    return pl.BlockSpec(shape, lambda *_: (0,) * nd, pipeline_mode=pl.Buffered(1))


def _ffn_body(x_ref, pre_ref, post_ref, wg_ref, wu_ref, wd_ref, *rest, cast_transposed):
    n_cast = len(cast_transposed)
    cast_in, o_ref, cast_out = rest[:n_cast], rest[n_cast], rest[n_cast + 1:]
    x = x_ref[...]
    h = _rms(x, pre_ref[...]).astype(BF16)
    g = jnp.dot(h, wg_ref[...], preferred_element_type=F32)
    u = jnp.dot(h, wu_ref[...], preferred_element_type=F32)
    a = (_silu(g) * u).astype(BF16)
    y = jnp.dot(a, wd_ref[...], preferred_element_type=F32)
    o_ref[...] = x + 0.5 * _rms(y, post_ref[...])
    for src, dst, transposed in zip(cast_in, cast_out, cast_transposed):
        blk = src[...]
        dst[...] = (blk.T if transposed else blk).astype(BF16)


def _cast_row_block(rows, steps):
    rb = BF16_ROWS
    while rows % rb or rows // rb > steps:
        rb += BF16_ROWS
    return rb


def _ffn(x, pre_g, post_g, wg, wu, wd, tm, casts=()):
    n = x.shape[0]
    assert n % tm == 0
    steps = n // tm
    in_specs = [
        pl.BlockSpec((tm, D_MODEL), lambda i: (i, 0)),
        _resident((1, D_MODEL)), _resident((1, D_MODEL)),
        _resident((D_MODEL, D_FF)), _resident((D_MODEL, D_FF)), _resident((D_FF, D_MODEL)),
    ]
    out_specs = [pl.BlockSpec((tm, D_MODEL), lambda i: (i, 0))]
    out_shape = [jax.ShapeDtypeStruct((n, D_MODEL), F32)]
    for arr, region in casts:
        rows, width = arr.shape
        if region is None:
            rb = _cast_row_block(rows, steps)
            last = rows // rb - 1
            spec = pl.BlockSpec((rb, width), lambda i, last=last: (jnp.minimum(i, last), 0))
            in_specs.append(spec)
            out_specs.append(spec)
            out_shape.append(jax.ShapeDtypeStruct((rows, width), BF16))
        else:
            row0, n_rows = region
            assert n_rows % HD == 0 and n_rows // HD <= steps
            last = n_rows // HD - 1
            in_specs.append(pl.BlockSpec(
                (pl.Element(HD), pl.Element(width)),
                lambda i, row0=row0, last=last: (
                    pl.multiple_of(row0 + HD * jnp.minimum(i, last), F32_ROWS), 0)))
            out_specs.append(pl.BlockSpec((width, HD), lambda i, last=last: (0, jnp.minimum(i, last))))
            out_shape.append(jax.ShapeDtypeStruct((width, n_rows), BF16))
    outs = pl.pallas_call(
        functools.partial(_ffn_body, cast_transposed=tuple(c[1] is not None for c in casts)),
        grid=(steps,),
        in_specs=in_specs,
        out_specs=out_specs,
        out_shape=out_shape,
        compiler_params=pltpu.CompilerParams(dimension_semantics=("arbitrary",),
                                             vmem_limit_bytes=VMEM_LIMIT),
        name="ffn",
    )(x, pre_g, post_g, wg, wu, wd, *[c[0] for c in casts])
    return outs[0], list(outs[1:])


def _rope(x, cos, sin_signed):
    return x * cos + pltpu.roll(x, HD // 2, 1) * sin_signed


def _ret_out_norm(o, g_row, gate):
    mu = jnp.mean(o, axis=-1, keepdims=True)
    d = o - mu
    var = jnp.mean(d * d, axis=-1, keepdims=True)
    return _silu(gate) * (d * lax.rsqrt(var + EPS) * g_row)


def _gdn_out_norm(o, g_row, gate):
    return o * lax.rsqrt(jnp.mean(o * o, axis=-1, keepdims=True) + EPS) * g_row * _silu(gate)


def _l2norm(x, scale):
    return x * (lax.rsqrt(jnp.sum(x * x, axis=-1, keepdims=True) + EPS) * scale)


def _softplus(x):
    return jnp.maximum(x, 0.0) + jnp.log(1.0 + jnp.exp(-jnp.abs(x)))


def _merge(x, o_r, o_g, gates, wrb_ref, wgb_ref, wout_ref, post_ref):
    y = (jax.nn.sigmoid(gates[:, :D_MODEL]) * _dot(o_r, wrb_ref[...])
         + jax.nn.sigmoid(gates[:, D_MODEL:]) * _dot(o_g, wgb_ref[...]))
    m = _dot(y, wout_ref[...])
    return x + _rms(m, post_ref[...])


def _unit_lower_inverse_many(a_list, masks):
    eye, m16, off32, off64 = masks
    ad = [a * m16 for a in a_list]
    x = [eye - v for v in ad]
    p = [_dot(v, v) for v in ad]
    for level in range(3):
        x = [xi + _dot(xi, pi) for xi, pi in zip(x, p)]
        if level < 2:
            p = [_dot(pi, pi) for pi in p]
    for m in (off32, off64):
        t = [_dot(xi, a * m) for xi, a in zip(x, a_list)]
        x = [xi - _dot(ti, xi) for xi, ti in zip(x, t)]
    return x


def _inverse_masks():
    r = lax.broadcasted_iota(jnp.int32, (CHUNK, CHUNK), 0)
    c = lax.broadcasted_iota(jnp.int32, (CHUNK, CHUNK), 1)
    one, zero = jnp.float32(1.0), jnp.float32(0.0)
    eye = jnp.where(r == c, one, zero)
    same16 = (r >> 4) == (c >> 4)
    same32 = (r >> 5) == (c >> 5)
    m16 = jnp.where(same16, one, zero)
    m32 = jnp.where(same32, one, zero)
    return eye, m16, m32 - m16, 1.0 - m32


def _retention_tile(pm, cos, sin, rd_ref, ku_ref, retg_ref, s_ref, tile_decay):
    ri = lax.broadcasted_iota(jnp.int32, (TM, TM), 0)
    ci = lax.broadcasted_iota(jnp.int32, (TM, TM), 1)
    causal = ri >= ci
    heads = []
    for hh in range(HEADS):
        sl = slice(hh * HD, (hh + 1) * HD)
        rq = pm[:, OFF_RQ + hh * HD:OFF_RQ + (hh + 1) * HD]
        rk = pm[:, OFF_RK + hh * HD:OFF_RK + (hh + 1) * HD]
        v = pm[:, OFF_RV + hh * HD:OFF_RV + (hh + 1) * HD]
        rg = pm[:, OFF_RG + hh * HD:OFF_RG + (hh + 1) * HD]
        qs = _rope(rq, cos, sin) * rd_ref[hh]
        ku = _rope(rk, cos, sin) * ku_ref[hh]
        s = s_ref[hh]
        sc = jnp.where(causal, _dot_nt(qs, ku), 0.0)
        o = _dot(sc, v) + _dot(qs, s)
        s_ref[hh] = tile_decay[hh] * (s + _dot_tn(ku, v))
        heads.append(_ret_out_norm(o, retg_ref[:, sl], rg))
    return jnp.concatenate(heads, axis=1)


def _short_conv_tile(u, buf, convw_ref, tail_ref):
    buf[CONV_PAD:CONV_PAD + TM, :] = u
    acc = buf[CONV_PAD - 3:CONV_PAD - 3 + TM, :] * convw_ref[0:1, :]
    for i in range(1, CONV_W):
        acc = acc + buf[CONV_PAD - 3 + i:CONV_PAD - 3 + i + TM, :] * convw_ref[i:i + 1, :]
    tail = buf[CONV_PAD + TM - 3:CONV_PAD + TM, :]
    tail_ref[...] = tail
    buf[CONV_PAD - 3:CONV_PAD, :] = tail
    return _silu(acc)


def _gdn_decay_beta(ab, alog_ref, dtb_ref):
    g_all = -jnp.exp(alog_ref[...]) * _softplus(ab + dtb_ref[...])
    beta_all = jax.nn.sigmoid(ab)
    return g_all, beta_all


def _gdn_tile(cq, ab, gz, ltri, alog_ref, dtb_ref, gdng_ref, s_ref):
    g_all, beta_all = _gdn_decay_beta(ab, alog_ref, dtb_ref)
    g_hi, g_mid, g_lo = _split3(g_all)
    dd = functools.partial(jnp.dot, preferred_element_type=F32)
    gcum = dd(ltri, g_hi) + (dd(ltri, g_mid) + dd(ltri, g_lo))
    gcum_t = gcum.T

    r64 = lax.broadcasted_iota(jnp.int32, (CHUNK, CHUNK), 0)
    c64 = lax.broadcasted_iota(jnp.int32, (CHUNK, CHUNK), 1)
    causal64 = r64 >= c64
    strict64 = r64 > c64
    masks = _inverse_masks()

    n_ch = TM // CHUNK
    rows = [slice(c * CHUNK, (c + 1) * CHUNK) for c in range(n_ch)]
    probs = [(hh, c) for hh in range(HEADS) for c in range(n_ch)]
    gq = [_l2norm(cq[:, hh * HD:(hh + 1) * HD], HD ** -0.5) for hh in range(HEADS)]
    gk = [_l2norm(cq[:, QK + hh * HD:QK + (hh + 1) * HD], 1.0) for hh in range(HEADS)]
    gv = [cq[:, 2 * QK + hh * HD:2 * QK + (hh + 1) * HD] for hh in range(HEADS)]
    qc = {(hh, c): gq[hh][rows[c]] for hh, c in probs}
    kc = {(hh, c): gk[hh][rows[c]] for hh, c in probs}
    vc = {(hh, c): gv[hh][rows[c]] for hh, c in probs}
    gc = {(hh, c): gcum[rows[c], hh:hh + 1] for hh, c in probs}
    bcol = {(hh, c): beta_all[rows[c], HEADS + hh:HEADS + hh + 1] for hh, c in probs}
    dm = {(hh, c): jnp.where(causal64,
                             jnp.exp(jnp.minimum(gc[hh, c] - gcum_t[hh:hh + 1, rows[c]], 0.0)), 0.0)
          for hh, c in probs}
    kk = {p: _dot_nt(kc[p], kc[p]) for p in probs}
    qk = {p: _dot_nt(qc[p], kc[p]) * dm[p] for p in probs}
    a_mats = [jnp.where(strict64, bcol[p] * dm[p] * kk[p], 0.0) for p in probs]
    tinv = dict(zip(probs, _unit_lower_inverse_many(a_mats, masks)))
    eg = {p: jnp.exp(gc[p]) for p in probs}
    uw = {p: _dot(tinv[p], jnp.concatenate([bcol[p] * vc[p], (bcol[p] * eg[p]) * kc[p]], axis=1))
          for p in probs}

    s = [s_ref[hh] for hh in range(HEADS)]
    outs = [[] for _ in range(HEADS)]
    for c in range(n_ch):
        for hh in range(HEADS):
            p = (hh, c)
            u = uw[p][:, :HD] - _dot(uw[p][:, HD:], s[hh])
            outs[hh].append(eg[p] * _dot(qc[p], s[hh]) + _dot(qk[p], u))
            gl = gc[p][CHUNK - 1:CHUNK, :]
            s[hh] = jnp.exp(gl) * s[hh] + _dot_tn(kc[p] * jnp.exp(gl - gc[p]), u)
    heads = []
    for hh in range(HEADS):
        s_ref[hh] = s[hh]
        o = jnp.concatenate(outs[hh], axis=0)
        heads.append(_gdn_out_norm(o, gdng_ref[...], gz[:, hh * HD:(hh + 1) * HD]))
    return jnp.concatenate(heads, axis=1)


def _mixer_prompt_body(x_ref, cos_ref, sin_ref, rd_ref, ku_ref, ltri_ref,
                       pre_ref, post_ref, wmain_ref, wab_ref, wgates_ref, convw_ref, alog_ref, dtb_ref,
                       retg_ref, gdng_ref, wrb_ref, wgb_ref, wout_ref,
                       y_ref, sret_ref, sgdn_ref, conv_ref,
                       cbuf, *, ret_tile_decay):
    t = pl.program_id(1)
    nb = x_ref.shape[0]

    @pl.when(t == 0)
    def _():
        sret_ref[...] = jnp.zeros_like(sret_ref)
        sgdn_ref[...] = jnp.zeros_like(sgdn_ref)
        cbuf[:, 0:CONV_PAD, :] = jnp.zeros((nb, CONV_PAD, CONV_DIM), F32)

    cos = cos_ref[...]
    sin = sin_ref[...]
    ltri = ltri_ref[...]
    for b in range(nb):
        x = x_ref[b]
        h = _rms(x, pre_ref[...]).astype(BF16)
        pg = jnp.dot(h, wmain_ref[:, OFF_CONV:OFF_AB], preferred_element_type=F32)
        ab = jnp.dot(h, wab_ref[...], preferred_element_type=F32)
        cq = _short_conv_tile(pg[:, 0:CONV_DIM], cbuf.at[b], convw_ref, conv_ref.at[b])
        pm = jnp.dot(h, wmain_ref[:, 0:OFF_CONV], preferred_element_type=F32)
        o_r = _retention_tile(pm, cos, sin, rd_ref, ku_ref, retg_ref, sret_ref.at[b], ret_tile_decay)
        o_g = _gdn_tile(cq, ab, pg[:, CONV_DIM:], ltri, alog_ref, dtb_ref, gdng_ref, sgdn_ref.at[b])
        gates = jnp.dot(h, wgates_ref[...], preferred_element_type=F32)
        y_ref[b] = _merge(x, o_r, o_g, gates, wrb_ref, wgb_ref, wout_ref, post_ref)


def _ret_gammas():
    return 1.0 - 2.0 ** (-5.0 - np.arange(HEADS, dtype=np.float64))


def _rope_tables(pos):
    inv = ROPE_BASE ** (-np.arange(0, HD, 2, dtype=np.float64) / HD)
    ang = np.asarray(pos, np.float64)[:, None] * inv[None, :]
    cos = np.concatenate([np.cos(ang), np.cos(ang)], axis=1)
    sin = np.concatenate([-np.sin(ang), np.sin(ang)], axis=1)
    return jnp.asarray(cos, F32), jnp.asarray(sin, F32)


def _mixer_weight_specs():
    return [
        _resident((1, D_MODEL)), _resident((1, D_MODEL)),
        _resident((D_MODEL, OFF_AB)), _resident((D_MODEL, HD)), _resident((D_MODEL, 2 * D_MODEL)),
        _resident((CONV_W, CONV_DIM)), _resident((1, HD)), _resident((1, HD)),
        _resident((1, QK)), _resident((1, HD)),
        _resident((QK, D_MODEL)), _resident((QK, D_MODEL)), _resident((D_MODEL, D_MODEL)),
    ]


def _mixer_weight_args(w):
    return (w["mix_pre_g"], w["mix_post_g"], w["w_main"], w["w_ab"], w["w_gates"], w["conv_w"],
            w["a_log"], w["dt_bias"], w["ret_norm_g"], w["gdn_norm_g"], w["w_rb"], w["w_gb"], w["w_out"])


def _mixer_prompt(x, w):
    b, t, _ = x.shape
    assert t % TM == 0 and b % NB == 0
    cos, sin = _rope_tables(np.arange(t))
    gam = _ret_gammas()
    i1 = np.arange(1, TM + 1, dtype=np.float64)
    rd = np.broadcast_to((gam[:, None] ** i1[None, :])[:, :, None], (HEADS, TM, HD))
    ku = np.broadcast_to((HD ** -0.5 * gam[:, None] ** (-i1[None, :]))[:, :, None], (HEADS, TM, HD))
    tile_decay = tuple(float(v) for v in gam ** TM)
    r = np.arange(TM)
    ltri = ((r[:, None] >= r[None, :]) & (r[:, None] // CHUNK == r[None, :] // CHUNK))

    body = functools.partial(_mixer_prompt_body, ret_tile_decay=tile_decay)
    state_spec = pl.BlockSpec((NB, HEADS, HD, HD), lambda i, j: (i, 0, 0, 0))
    return pl.pallas_call(
        body,
        grid=(b // NB, t // TM),
        in_specs=[
            pl.BlockSpec((NB, TM, D_MODEL), lambda i, j: (i, j, 0)),
            pl.BlockSpec((TM, HD), lambda i, j: (j, 0)),
            pl.BlockSpec((TM, HD), lambda i, j: (j, 0)),
            _resident((HEADS, TM, HD)), _resident((HEADS, TM, HD)), _resident((TM, TM)),
        ] + _mixer_weight_specs(),
        out_specs=[
            pl.BlockSpec((NB, TM, D_MODEL), lambda i, j: (i, j, 0)),
            state_spec, state_spec,
            pl.BlockSpec((NB, CONV_W - 1, CONV_DIM), lambda i, j: (i, 0, 0)),
        ],
        out_shape=[
            jax.ShapeDtypeStruct((b, t, D_MODEL), F32),
            jax.ShapeDtypeStruct((b, HEADS, HD, HD), F32),
            jax.ShapeDtypeStruct((b, HEADS, HD, HD), F32),
            jax.ShapeDtypeStruct((b, CONV_W - 1, CONV_DIM), F32),
        ],
        scratch_shapes=[pltpu.VMEM((NB, CONV_PAD + TM, CONV_DIM), F32)],
        compiler_params=pltpu.CompilerParams(dimension_semantics=("arbitrary", "arbitrary"),
                                             vmem_limit_bytes=VMEM_LIMIT),
        name="mixer_prompt",
    )(x, cos, sin, jnp.asarray(rd, F32), jnp.asarray(ku, F32), jnp.asarray(ltri, BF16),
      *_mixer_weight_args(w))


def _pick_rows(rows):
    ri = lax.broadcasted_iota(jnp.int32, rows[0].shape, 0)
    out = rows[0]
    for j in range(1, len(rows)):
        out = jnp.where(ri == j, rows[j], out)
    return out


def _mixer_sample_body(x_ref, cos_ref, sin_ref,
                       pre_ref, post_ref, wmain_ref, wab_ref, wgates_ref, convw_ref, alog_ref, dtb_ref,
                       retg_ref, gdng_ref, wrb_ref, wgb_ref, wout_ref,
                       sret_in, sgdn_in, conv_in,
                       y_ref, sret_out, sgdn_out, conv_out,
                       pm_s, gates_s, rq_s, rk_s, gq_s, gk_s, gv_s, eg_s, beta_s, or_s, og_s, *, ret_gamma):
    i = pl.program_id(0)

    @pl.when(i == 0)
    def _():
        h = _rms(x_ref[...], pre_ref[...]).astype(BF16)
        pm = jnp.dot(h, wmain_ref[...], preferred_element_type=F32)
        ab = jnp.dot(h, wab_ref[...], preferred_element_type=F32)
        pm_s[...] = pm
        gates_s[...] = jnp.dot(h, wgates_ref[...], preferred_element_type=F32)
        cos = cos_ref[...]
        sin = sin_ref[...]
        cin = pm[:, OFF_CONV:OFF_CONV + CONV_DIM]
        acc = cin * convw_ref[CONV_W - 1:CONV_W, :]
        for r in range(CONV_W - 1):
            acc = acc + conv_in[:, r * CONV_DIM:(r + 1) * CONV_DIM] * convw_ref[r:r + 1, :]
        conv_out[:, 0:2 * CONV_DIM] = conv_in[:, CONV_DIM:3 * CONV_DIM]
        conv_out[:, 2 * CONV_DIM:3 * CONV_DIM] = cin
        cq = _silu(acc)
        g_all, beta_all = _gdn_decay_beta(ab, alog_ref, dtb_ref)
        eg_s[...] = jnp.exp(g_all)
        beta_s[...] = beta_all
        for hh in range(HEADS):
            sl = slice(hh * HD, (hh + 1) * HD)
            rq_s[:, sl] = _rope(pm[:, OFF_RQ + hh * HD:OFF_RQ + (hh + 1) * HD], cos, sin)
            rk_s[:, sl] = _rope(pm[:, OFF_RK + hh * HD:OFF_RK + (hh + 1) * HD], cos, sin) * (HD ** -0.5)
            gq_s[:, sl] = _l2norm(cq[:, hh * HD:(hh + 1) * HD], HD ** -0.5)
            gk_s[:, sl] = _l2norm(cq[:, QK + hh * HD:QK + (hh + 1) * HD], 1.0)
        gv_s[...] = cq[:, 2 * QK:3 * QK]

    rows = pl.ds(pl.multiple_of(i * TB, TB), TB)
    eg_all = eg_s[rows, :]
    beta_all = beta_s[rows, :]
    for hh in range(HEADS):
        sl = slice(hh * HD, (hh + 1) * HD)
        q = rq_s[rows, sl]
        k_t = rk_s[rows, sl].T
        v = pm_s[rows, OFF_RV + hh * HD:OFF_RV + (hh + 1) * HD]
        o_rows = []
        for j in range(TB):
            s = ret_gamma[hh] * sret_in[j, hh] + k_t[:, j:j + 1] * v[j:j + 1, :]
            sret_out[j, hh] = s
            o_rows.append(_dot(q, s))
        or_s[rows, sl] = _pick_rows(o_rows)

        gq = gq_s[rows, sl]
        gk = gk_s[rows, sl]
        gk_t = gk.T
        gv = gv_s[rows, sl]
        o_rows = []
        for j in range(TB):
            s = sgdn_in[j, hh]
            eg = eg_all[j:j + 1, hh:hh + 1]
            beta = beta_all[j:j + 1, HEADS + hh:HEADS + hh + 1]
            ks = _dot(gk, s)[j:j + 1, :]
            u = beta * gv[j:j + 1, :] - (beta * eg) * ks
            s = eg * s + gk_t[:, j:j + 1] * u
            sgdn_out[j, hh] = s
            o_rows.append(_dot(gq, s))
        og_s[rows, sl] = _pick_rows(o_rows)

    @pl.when(i == pl.num_programs(0) - 1)
    def _():
        o_r = []
        o_g = []
        for hh in range(HEADS):
            sl = slice(hh * HD, (hh + 1) * HD)
            o_r.append(_ret_out_norm(or_s[:, sl], retg_ref[:, sl],
                                     pm_s[:, OFF_RG + hh * HD:OFF_RG + (hh + 1) * HD]))
            o_g.append(_gdn_out_norm(og_s[:, sl], gdng_ref[...],
                                     pm_s[:, OFF_GZ + hh * HD:OFF_GZ + (hh + 1) * HD]))
        y_ref[...] = _merge(x_ref[...], jnp.concatenate(o_r, axis=1), jnp.concatenate(o_g, axis=1),
                            gates_s[...], wrb_ref, wgb_ref, wout_ref, post_ref)


def _mixer_sample(x, s_ret, s_gdn, s_conv, w, pos):
    n = x.shape[0]
    assert n % TB == 0
    cos, sin = _rope_tables([pos])
    body = functools.partial(_mixer_sample_body, ret_gamma=tuple(float(v) for v in _ret_gammas()))
    state_spec = pl.BlockSpec((TB, HEADS, HD, HD), lambda i: (i, 0, 0, 0))
    conv_cols = (CONV_W - 1) * CONV_DIM
    return pl.pallas_call(
        body,
        grid=(n // TB,),
        in_specs=[_resident((n, D_MODEL)), _resident((1, HD)), _resident((1, HD))]
        + _mixer_weight_specs()
        + [state_spec, state_spec, _resident((n, conv_cols))],
        out_specs=[pl.BlockSpec((n, D_MODEL), lambda i: (0, 0)), state_spec, state_spec,
                   pl.BlockSpec((n, conv_cols), lambda i: (0, 0))],
        out_shape=[
            jax.ShapeDtypeStruct((n, D_MODEL), F32),
            jax.ShapeDtypeStruct((n, HEADS, HD, HD), F32),
            jax.ShapeDtypeStruct((n, HEADS, HD, HD), F32),
            jax.ShapeDtypeStruct((n, conv_cols), F32),
        ],
        scratch_shapes=[pltpu.VMEM((n, OFF_AB), F32), pltpu.VMEM((n, 2 * D_MODEL), F32)]
        + [pltpu.VMEM((n, QK), F32) for _ in range(5)]
        + [pltpu.VMEM((n, HD), F32), pltpu.VMEM((n, HD), F32)]
        + [pltpu.VMEM((n, QK), F32), pltpu.VMEM((n, QK), F32)],
        compiler_params=pltpu.CompilerParams(dimension_semantics=("arbitrary",),
                                             vmem_limit_bytes=VMEM_LIMIT),
        name="mixer_sample",
    )(x, cos, sin, *_mixer_weight_args(w), s_ret, s_gdn, s_conv)


def _pad_lanes(v, n):
    return jnp.pad(v, ((0, 0), (0, n - v.shape[1])))


def kernel(x_prompt, x_sample, state_ret, state_gdn, state_conv, ffn1_pre_g, ffn1_post_g, ffn1_w_gate,
           ffn1_w_up, ffn1_w_down, mix_pre_g, mix_post_g, w_in, ret_norm_g, gdn_conv_w, gdn_a_log,
           gdn_dt_bias, gdn_norm_g, w_ret_branch, w_gdn_branch, w_out, ffn2_pre_g, ffn2_post_g,
           ffn2_w_gate, ffn2_w_up, ffn2_w_down):
    depth = w_in.shape[0]
    b, t, _ = x_prompt.shape
    n_s, t_s, _ = x_sample.shape
    assert t_s == 1
    yp = x_prompt.reshape(b * t, D_MODEL)
    ys = x_sample.reshape(n_s, D_MODEL)
    outs = [[] for _ in range(6)]
    for l in range(depth):
        row = lambda a: a[l][None, :]
        f1 = (row(ffn1_pre_g), row(ffn1_post_g), ffn1_w_gate[l].astype(BF16),
              ffn1_w_up[l].astype(BF16), ffn1_w_down[l].astype(BF16))

        w_in_t = w_in[l].T
        casts = (
            (w_in_t, (0, OFF_AB)), (w_in_t, (OFF_GATES, 2 * D_MODEL)),
            (w_ret_branch[l], None), (w_gdn_branch[l], None), (w_out[l], None),
            (ffn2_w_gate[l], None), (ffn2_w_up[l], None), (ffn2_w_down[l], None),
        )
        yp, (w_main, w_gates, w_rb, w_gb, w_o, g2, u2, d2) = _ffn(yp, *f1, tm=FFN_TM, casts=casts)
        w = {
            "mix_pre_g": row(mix_pre_g), "mix_post_g": row(mix_post_g),
            "w_main": w_main,
            "w_ab": _pad_lanes(w_in[l, :, OFF_AB:OFF_GATES], HD).astype(BF16),
            "w_gates": w_gates,
            "conv_w": gdn_conv_w[l],
            "a_log": _pad_lanes(row(gdn_a_log), HD), "dt_bias": _pad_lanes(row(gdn_dt_bias), HD),
            "ret_norm_g": row(ret_norm_g), "gdn_norm_g": row(gdn_norm_g),
            "w_rb": w_rb, "w_gb": w_gb, "w_out": w_o,
        }
        f2 = (row(ffn2_pre_g), row(ffn2_post_g), g2, u2, d2)

        yp, r1, g1, c1 = _mixer_prompt(yp.reshape(b, t, D_MODEL), w)
        yp, _ = _ffn(yp.reshape(b * t, D_MODEL), *f2, tm=FFN_TM)

        ys, _ = _ffn(ys, *f1, tm=n_s)
        ys, r2, g2s, c2 = _mixer_sample(ys, state_ret[l], state_gdn[l],
                                        state_conv[l].reshape(n_s, (CONV_W - 1) * CONV_DIM), w,
                                        PAST_LEN)
        ys, _ = _ffn(ys, *f2, tm=n_s)
        c2 = c2.reshape(n_s, CONV_W - 1, CONV_DIM)
        for lst, val in zip(outs, (r1, g1, c1, r2, g2s, c2)):
            lst.append(val)
    stacked = [v[0][None] if depth == 1 else jnp.stack(v) for v in outs]
    return (yp.reshape(b, t, D_MODEL), ys.reshape(n_s, t_s, D_MODEL), *stacked)
```

```python
import functools

import numpy as np
import jax
import jax.numpy as jnp
from jax import lax
from jax.experimental import pallas as pl
from jax.experimental.pallas import tpu as pltpu

F32 = jnp.float32
BF16 = jnp.bfloat16

D_MODEL = 1024
D_FF = 2816
HEADS = 4
HD = 128
QK = HEADS * HD
CONV_W = 4
CONV_DIM = 3 * QK
CHUNK = 64
ROPE_BASE = 10000.0
EPS = 1e-6
PAST_LEN = 16384

OFF_RQ, OFF_RK, OFF_RV, OFF_RG = 0, QK, 2 * QK, 3 * QK
OFF_CONV = 4 * QK
OFF_GZ = OFF_CONV + CONV_DIM
OFF_AB = OFF_GZ + QK
OFF_GATES = OFF_AB + 2 * HEADS
D_IN = OFF_GATES + 2 * D_MODEL

TM = 256
NB = 2
TB = 8
FFN_TM = 512
FFN_SUB = 256
CONV_PAD = 8
BF16_ROWS = 16
F32_ROWS = 8

VMEM_LIMIT = 56 * 1024 * 1024


def _silu(x):
    return x * jax.nn.sigmoid(x)


def _rms(x, g):
    return x * lax.rsqrt(jnp.mean(x * x, axis=-1, keepdims=True) + EPS) * g


def _dot(a, b):
    return jnp.dot(a.astype(BF16), b.astype(BF16), preferred_element_type=F32)


def _dot_nt(a, b):
    return lax.dot_general(a.astype(BF16), b.astype(BF16), (((1,), (1,)), ((), ())),
                           preferred_element_type=F32)


def _dot_tn(a, b):
    return lax.dot_general(a.astype(BF16), b.astype(BF16), (((0,), (0,)), ((), ())),
                           preferred_element_type=F32)


def _split3(a):
    hi = a.astype(BF16)
    r = a - hi.astype(F32)
    mid = r.astype(BF16)
    lo = (r - mid.astype(F32)).astype(BF16)
    return hi, mid, lo


def _resident(shape):
    nd = len(shape)
    return pl.BlockSpec(shape, lambda *_: (0,) * nd, pipeline_mode=pl.Buffered(1))


def _ffn_body(x_ref, pre_ref, post_ref, wg_ref, wu_ref, wd_ref, *rest, cast_transposed):
    n_cast = len(cast_transposed)
    cast_in, o_ref, cast_out = rest[:n_cast], rest[n_cast], rest[n_cast + 1:]
    tm = x_ref.shape[0]
    sub = min(tm, FFN_SUB)
    parts = [slice(r, r + sub) for r in range(0, tm, sub)]
    xs = [x_ref[p, :] for p in parts]
    hs = [_rms(x, pre_ref[...]).astype(BF16) for x in xs]
    acts = []
    for h in hs:
        g = jnp.dot(h, wg_ref[...], preferred_element_type=F32)
        u = jnp.dot(h, wu_ref[...], preferred_element_type=F32)
        acts.append((_silu(g) * u).astype(BF16))
    for p, x, a in zip(parts, xs, acts):
        y = jnp.dot(a, wd_ref[...], preferred_element_type=F32)
        o_ref[p, :] = x + 0.5 * _rms(y, post_ref[...])
    for src, dst, transposed in zip(cast_in, cast_out, cast_transposed):
        blk = src[...]
        dst[...] = (blk.T if transposed else blk).astype(BF16)


def _cast_row_block(rows, steps):
    rb = BF16_ROWS
    while rows % rb or rows // rb > steps:
        rb += BF16_ROWS
    return rb


def _ffn(x, pre_g, post_g, wg, wu, wd, tm, casts=()):
    n = x.shape[0]
    assert n % tm == 0
    steps = n // tm
    in_specs = [
        pl.BlockSpec((tm, D_MODEL), lambda i: (i, 0)),
        _resident((1, D_MODEL)), _resident((1, D_MODEL)),
        _resident((D_MODEL, D_FF)), _resident((D_MODEL, D_FF)), _resident((D_FF, D_MODEL)),
    ]
    out_specs = [pl.BlockSpec((tm, D_MODEL), lambda i: (i, 0))]
    out_shape = [jax.ShapeDtypeStruct((n, D_MODEL), F32)]
    for arr, region in casts:
        rows, width = arr.shape
        if region is None:
            rb = _cast_row_block(rows, steps)
            last = rows // rb - 1
            spec = pl.BlockSpec((rb, width), lambda i, last=last: (jnp.minimum(i, last), 0))
            in_specs.append(spec)
            out_specs.append(spec)
            out_shape.append(jax.ShapeDtypeStruct((rows, width), BF16))
        else:
            row0, n_rows = region
            assert n_rows % HD == 0 and n_rows // HD <= steps
            last = n_rows // HD - 1
            in_specs.append(pl.BlockSpec(
                (pl.Element(HD), pl.Element(width)),
                lambda i, row0=row0, last=last: (
                    pl.multiple_of(row0 + HD * jnp.minimum(i, last), F32_ROWS), 0)))
            out_specs.append(pl.BlockSpec((width, HD), lambda i, last=last: (0, jnp.minimum(i, last))))
            out_shape.append(jax.ShapeDtypeStruct((width, n_rows), BF16))
    outs = pl.pallas_call(
        functools.partial(_ffn_body, cast_transposed=tuple(c[1] is not None for c in casts)),
        grid=(steps,),
        in_specs=in_specs,
        out_specs=out_specs,
        out_shape=out_shape,
        compiler_params=pltpu.CompilerParams(dimension_semantics=("arbitrary",),
                                             vmem_limit_bytes=VMEM_LIMIT),
        name="ffn",
    )(x, pre_g, post_g, wg, wu, wd, *[c[0] for c in casts])
    return outs[0], list(outs[1:])


def _rope(x, cos, sin_signed):
    return x * cos + pltpu.roll(x, HD // 2, 1) * sin_signed


def _ret_out_norm(o, g_row, gate):
    mu = jnp.mean(o, axis=-1, keepdims=True)
    d = o - mu
    var = jnp.mean(d * d, axis=-1, keepdims=True)
    return _silu(gate) * (d * lax.rsqrt(var + EPS) * g_row)


def _gdn_out_norm(o, g_row, gate):
    return o * lax.rsqrt(jnp.mean(o * o, axis=-1, keepdims=True) + EPS) * g_row * _silu(gate)


def _l2norm(x, scale):
    return x * (lax.rsqrt(jnp.sum(x * x, axis=-1, keepdims=True) + EPS) * scale)


def _softplus(x):
    return jnp.maximum(x, 0.0) + jnp.log(1.0 + jnp.exp(-jnp.abs(x)))


def _merge(x, o_r, o_g, gates, wrb_ref, wgb_ref, wout_ref, post_ref):
    y = (jax.nn.sigmoid(gates[:, :D_MODEL]) * _dot(o_r, wrb_ref[...])
         + jax.nn.sigmoid(gates[:, D_MODEL:]) * _dot(o_g, wgb_ref[...]))
    m = _dot(y, wout_ref[...])
    return x + _rms(m, post_ref[...])


def _unit_lower_inverse_many(a_list, masks):
    eye, m16, off32, off64 = masks
    ad = [a * m16 for a in a_list]
    x = [eye - v for v in ad]
    p = [_dot(v, v) for v in ad]
    for level in range(3):
        x = [xi + _dot(xi, pi) for xi, pi in zip(x, p)]
        if level < 2:
            p = [_dot(pi, pi) for pi in p]
    for m in (off32, off64):
        t = [_dot(xi, a * m) for xi, a in zip(x, a_list)]
        x = [xi - _dot(ti, xi) for xi, ti in zip(x, t)]
    return x


def _inverse_masks():
    r = lax.broadcasted_iota(jnp.int32, (CHUNK, CHUNK), 0)
    c = lax.broadcasted_iota(jnp.int32, (CHUNK, CHUNK), 1)
    one, zero = jnp.float32(1.0), jnp.float32(0.0)
    eye = jnp.where(r == c, one, zero)
    same16 = (r >> 4) == (c >> 4)
    same32 = (r >> 5) == (c >> 5)
    m16 = jnp.where(same16, one, zero)
    m32 = jnp.where(same32, one, zero)
    return eye, m16, m32 - m16, 1.0 - m32


def _retention_tile(pm, cos, sin, rd_ref, ku_ref, retg_ref, s_ref, tile_decay):
    ri = lax.broadcasted_iota(jnp.int32, (TM, TM), 0)
    ci = lax.broadcasted_iota(jnp.int32, (TM, TM), 1)
    causal = ri >= ci
    heads = []
    for hh in range(HEADS):
        sl = slice(hh * HD, (hh + 1) * HD)
        rq = pm[:, OFF_RQ + hh * HD:OFF_RQ + (hh + 1) * HD]
        rk = pm[:, OFF_RK + hh * HD:OFF_RK + (hh + 1) * HD]
        v = pm[:, OFF_RV + hh * HD:OFF_RV + (hh + 1) * HD]
        rg = pm[:, OFF_RG + hh * HD:OFF_RG + (hh + 1) * HD]
        qs = _rope(rq, cos, sin) * rd_ref[hh]
        ku = _rope(rk, cos, sin) * ku_ref[hh]
        s = s_ref[hh]
        sc = jnp.where(causal, _dot_nt(qs, ku), 0.0)
        o = _dot(sc, v) + _dot(qs, s)
        s_ref[hh] = tile_decay[hh] * (s + _dot_tn(ku, v))
        heads.append(_ret_out_norm(o, retg_ref[:, sl], rg))
    return jnp.concatenate(heads, axis=1)


def _short_conv_tile(u, buf, convw_ref, tail_ref):
    buf[CONV_PAD:CONV_PAD + TM, :] = u
    acc = buf[CONV_PAD - 3:CONV_PAD - 3 + TM, :] * convw_ref[0:1, :]
    for i in range(1, CONV_W):
        acc = acc + buf[CONV_PAD - 3 + i:CONV_PAD - 3 + i + TM, :] * convw_ref[i:i + 1, :]
    tail = buf[CONV_PAD + TM - 3:CONV_PAD + TM, :]
    tail_ref[...] = tail
    buf[CONV_PAD - 3:CONV_PAD, :] = tail
    return _silu(acc)


def _gdn_decay_beta(ab, alog_ref, dtb_ref):
    g_all = -jnp.exp(alog_ref[...]) * _softplus(ab + dtb_ref[...])
    beta_all = jax.nn.sigmoid(ab)
    return g_all, beta_all


def _gdn_tiles(cqs, abs_, gzs, ltri, alog_ref, dtb_ref, gdng_ref, s_refs):
    seqs = range(len(cqs))
    dd = functools.partial(jnp.dot, preferred_element_type=F32)
    gcum, gcum_t, beta_all = [], [], []
    for b in seqs:
        g_all, beta = _gdn_decay_beta(abs_[b], alog_ref, dtb_ref)
        beta_all.append(beta)
        g_hi, g_mid, g_lo = _split3(g_all)
        gcum.append(dd(ltri, g_hi) + (dd(ltri, g_mid) + dd(ltri, g_lo)))
        gcum_t.append(gcum[b].T)

    r64 = lax.broadcasted_iota(jnp.int32, (CHUNK, CHUNK), 0)
    c64 = lax.broadcasted_iota(jnp.int32, (CHUNK, CHUNK), 1)
    causal64 = r64 >= c64
    strict64 = r64 > c64
    masks = _inverse_masks()

    n_ch = TM // CHUNK
    rows = [slice(c * CHUNK, (c + 1) * CHUNK) for c in range(n_ch)]
    bh = [(b, hh) for b in seqs for hh in range(HEADS)]
    probs = [(b, hh, c) for b, hh in bh for c in range(n_ch)]
    gq = {(b, hh): _l2norm(cqs[b][:, hh * HD:(hh + 1) * HD], HD ** -0.5) for b, hh in bh}
    gk = {(b, hh): _l2norm(cqs[b][:, QK + hh * HD:QK + (hh + 1) * HD], 1.0) for b, hh in bh}
    gv = {(b, hh): cqs[b][:, 2 * QK + hh * HD:2 * QK + (hh + 1) * HD] for b, hh in bh}
    qc = {(b, hh, c): gq[b, hh][rows[c]] for b, hh, c in probs}
    kc = {(b, hh, c): gk[b, hh][rows[c]] for b, hh, c in probs}
    vc = {(b, hh, c): gv[b, hh][rows[c]] for b, hh, c in probs}
    gc = {(b, hh, c): gcum[b][rows[c], hh:hh + 1] for b, hh, c in probs}
    bcol = {(b, hh, c): beta_all[b][rows[c], HEADS + hh:HEADS + hh + 1] for b, hh, c in probs}
    dm = {(b, hh, c): jnp.where(
        causal64, jnp.exp(jnp.minimum(gc[b, hh, c] - gcum_t[b][hh:hh + 1, rows[c]], 0.0)), 0.0)
        for b, hh, c in probs}
    kk = {p: _dot_nt(kc[p], kc[p]) for p in probs}
    qk = {p: _dot_nt(qc[p], kc[p]) * dm[p] for p in probs}
    a_mats = [jnp.where(strict64, bcol[p] * dm[p] * kk[p], 0.0) for p in probs]
    tinv = dict(zip(probs, _unit_lower_inverse_many(a_mats, masks)))
    eg = {p: jnp.exp(gc[p]) for p in probs}
    uw = {p: _dot(tinv[p], jnp.concatenate([bcol[p] * vc[p], (bcol[p] * eg[p]) * kc[p]], axis=1))
          for p in probs}

    s = {(b, hh): s_refs[b][hh] for b, hh in bh}
    outs = {k: [] for k in bh}
    for c in range(n_ch):
        for b, hh in bh:
            p = (b, hh, c)
            u = uw[p][:, :HD] - _dot(uw[p][:, HD:], s[b, hh])
            outs[b, hh].append(eg[p] * _dot(qc[p], s[b, hh]) + _dot(qk[p], u))
            gl = gc[p][CHUNK - 1:CHUNK, :]
            s[b, hh] = jnp.exp(gl) * s[b, hh] + _dot_tn(kc[p] * jnp.exp(gl - gc[p]), u)
    o_gs = []
    for b in seqs:
        heads = []
        for hh in range(HEADS):
            s_refs[b][hh] = s[b, hh]
            o = jnp.concatenate(outs[b, hh], axis=0)
            heads.append(_gdn_out_norm(o, gdng_ref[...], gzs[b][:, hh * HD:(hh + 1) * HD]))
        o_gs.append(jnp.concatenate(heads, axis=1))
    return o_gs


def _mixer_prompt_body(x_ref, cos_ref, sin_ref, rd_ref, ku_ref, ltri_ref,
                       pre_ref, post_ref, wmain_ref, wab_ref, wgates_ref, convw_ref, alog_ref, dtb_ref,
                       retg_ref, gdng_ref, wrb_ref, wgb_ref, wout_ref,
                       y_ref, sret_ref, sgdn_ref, conv_ref,
                       cbuf, *, ret_tile_decay):
    t = pl.program_id(1)
    nb = x_ref.shape[0]

    @pl.when(t == 0)
    def _():
        sret_ref[...] = jnp.zeros_like(sret_ref)
        sgdn_ref[...] = jnp.zeros_like(sgdn_ref)
        cbuf[:, 0:CONV_PAD, :] = jnp.zeros((nb, CONV_PAD, CONV_DIM), F32)

    cos = cos_ref[...]
    sin = sin_ref[...]
    ltri = ltri_ref[...]
    seqs = range(nb)
    xs = [x_ref[b] for b in seqs]
    hs = [_rms(xs[b], pre_ref[...]).astype(BF16) for b in seqs]
    pgs = [jnp.dot(hs[b], wmain_ref[:, OFF_CONV:OFF_AB], preferred_element_type=F32) for b in seqs]
    abs_ = [jnp.dot(hs[b], wab_ref[...], preferred_element_type=F32) for b in seqs]
    cqs = [_short_conv_tile(pgs[b][:, 0:CONV_DIM], cbuf.at[b], convw_ref, conv_ref.at[b]) for b in seqs]
    pms = [jnp.dot(hs[b], wmain_ref[:, 0:OFF_CONV], preferred_element_type=F32) for b in seqs]
    o_rs = [_retention_tile(pms[b], cos, sin, rd_ref, ku_ref, retg_ref, sret_ref.at[b], ret_tile_decay)
            for b in seqs]
    o_gs = _gdn_tiles(cqs, abs_, [pgs[b][:, CONV_DIM:] for b in seqs], ltri, alog_ref, dtb_ref, gdng_ref,
                      [sgdn_ref.at[b] for b in seqs])
    for b in seqs:
        gates = jnp.dot(hs[b], wgates_ref[...], preferred_element_type=F32)
        y_ref[b] = _merge(xs[b], o_rs[b], o_gs[b], gates, wrb_ref, wgb_ref, wout_ref, post_ref)


def _ret_gammas():
    return 1.0 - 2.0 ** (-5.0 - np.arange(HEADS, dtype=np.float64))


def _rope_tables(pos):
    inv = ROPE_BASE ** (-np.arange(0, HD, 2, dtype=np.float64) / HD)
    ang = np.asarray(pos, np.float64)[:, None] * inv[None, :]
    cos = np.concatenate([np.cos(ang), np.cos(ang)], axis=1)
    sin = np.concatenate([-np.sin(ang), np.sin(ang)], axis=1)
    return jnp.asarray(cos, F32), jnp.asarray(sin, F32)


def _mixer_weight_specs():
    return [
        _resident((1, D_MODEL)), _resident((1, D_MODEL)),
        _resident((D_MODEL, OFF_AB)), _resident((D_MODEL, HD)), _resident((D_MODEL, 2 * D_MODEL)),
        _resident((CONV_W, CONV_DIM)), _resident((1, HD)), _resident((1, HD)),
        _resident((1, QK)), _resident((1, HD)),
        _resident((QK, D_MODEL)), _resident((QK, D_MODEL)), _resident((D_MODEL, D_MODEL)),
    ]


def _mixer_weight_args(w):
    return (w["mix_pre_g"], w["mix_post_g"], w["w_main"], w["w_ab"], w["w_gates"], w["conv_w"],
            w["a_log"], w["dt_bias"], w["ret_norm_g"], w["gdn_norm_g"], w["w_rb"], w["w_gb"], w["w_out"])


def _mixer_prompt(x, w):
    b, t, _ = x.shape
    assert t % TM == 0 and b % NB == 0
    cos, sin = _rope_tables(np.arange(t))
    gam = _ret_gammas()
    i1 = np.arange(1, TM + 1, dtype=np.float64)
    rd = np.broadcast_to((gam[:, None] ** i1[None, :])[:, :, None], (HEADS, TM, HD))
    ku = np.broadcast_to((HD ** -0.5 * gam[:, None] ** (-i1[None, :]))[:, :, None], (HEADS, TM, HD))
    tile_decay = tuple(float(v) for v in gam ** TM)
    r = np.arange(TM)
    ltri = ((r[:, None] >= r[None, :]) & (r[:, None] // CHUNK == r[None, :] // CHUNK))

    body = functools.partial(_mixer_prompt_body, ret_tile_decay=tile_decay)
    state_spec = pl.BlockSpec((NB, HEADS, HD, HD), lambda i, j: (i, 0, 0, 0))
    return pl.pallas_call(
        body,
        grid=(b // NB, t // TM),
        in_specs=[
            pl.BlockSpec((NB, TM, D_MODEL), lambda i, j: (i, j, 0)),
            pl.BlockSpec((TM, HD), lambda i, j: (j, 0)),
            pl.BlockSpec((TM, HD), lambda i, j: (j, 0)),
            _resident((HEADS, TM, HD)), _resident((HEADS, TM, HD)), _resident((TM, TM)),
        ] + _mixer_weight_specs(),
        out_specs=[
            pl.BlockSpec((NB, TM, D_MODEL), lambda i, j: (i, j, 0)),
            state_spec, state_spec,
            pl.BlockSpec((NB, CONV_W - 1, CONV_DIM), lambda i, j: (i, 0, 0)),
        ],
        out_shape=[
            jax.ShapeDtypeStruct((b, t, D_MODEL), F32),
            jax.ShapeDtypeStruct((b, HEADS, HD, HD), F32),
            jax.ShapeDtypeStruct((b, HEADS, HD, HD), F32),
            jax.ShapeDtypeStruct((b, CONV_W - 1, CONV_DIM), F32),
        ],
        scratch_shapes=[pltpu.VMEM((NB, CONV_PAD + TM, CONV_DIM), F32)],
        compiler_params=pltpu.CompilerParams(dimension_semantics=("arbitrary", "arbitrary"),
                                             vmem_limit_bytes=VMEM_LIMIT),
        name="mixer_prompt",
    )(x, cos, sin, jnp.asarray(rd, F32), jnp.asarray(ku, F32), jnp.asarray(ltri, BF16),
      *_mixer_weight_args(w))


def _pick_rows(rows):
    ri = lax.broadcasted_iota(jnp.int32, rows[0].shape, 0)
    out = rows[0]
    for j in range(1, len(rows)):
        out = jnp.where(ri == j, rows[j], out)
    return out


def _mixer_sample_body(x_ref, cos_ref, sin_ref,
                       pre_ref, post_ref, wmain_ref, wab_ref, wgates_ref, convw_ref, alog_ref, dtb_ref,
                       retg_ref, gdng_ref, wrb_ref, wgb_ref, wout_ref,
                       sret_in, sgdn_in, conv_in,
                       y_ref, sret_out, sgdn_out, conv_out,
                       pm_s, gates_s, rq_s, rk_s, gq_s, gk_s, gv_s, eg_s, beta_s, or_s, og_s, *, ret_gamma):
    i = pl.program_id(0)

    @pl.when(i == 0)
    def _():
        h = _rms(x_ref[...], pre_ref[...]).astype(BF16)
        pm = jnp.dot(h, wmain_ref[...], preferred_element_type=F32)
        ab = jnp.dot(h, wab_ref[...], preferred_element_type=F32)
        pm_s[...] = pm
        gates_s[...] = jnp.dot(h, wgates_ref[...], preferred_element_type=F32)
        cos = cos_ref[...]
        sin = sin_ref[...]
        cin = pm[:, OFF_CONV:OFF_CONV + CONV_DIM]
        acc = cin * convw_ref[CONV_W - 1:CONV_W, :]
        for r in range(CONV_W - 1):
            acc = acc + conv_in[:, r * CONV_DIM:(r + 1) * CONV_DIM] * convw_ref[r:r + 1, :]
        conv_out[:, 0:2 * CONV_DIM] = conv_in[:, CONV_DIM:3 * CONV_DIM]
        conv_out[:, 2 * CONV_DIM:3 * CONV_DIM] = cin
        cq = _silu(acc)
        g_all, beta_all = _gdn_decay_beta(ab, alog_ref, dtb_ref)
        eg_s[...] = jnp.exp(g_all)
        beta_s[...] = beta_all
        for hh in range(HEADS):
            sl = slice(hh * HD, (hh + 1) * HD)
            rq_s[:, sl] = _rope(pm[:, OFF_RQ + hh * HD:OFF_RQ + (hh + 1) * HD], cos, sin)
            rk_s[:, sl] = _rope(pm[:, OFF_RK + hh * HD:OFF_RK + (hh + 1) * HD], cos, sin) * (HD ** -0.5)
            gq_s[:, sl] = _l2norm(cq[:, hh * HD:(hh + 1) * HD], HD ** -0.5)
            gk_s[:, sl] = _l2norm(cq[:, QK + hh * HD:QK + (hh + 1) * HD], 1.0)
        gv_s[...] = cq[:, 2 * QK:3 * QK]

    rows = pl.ds(pl.multiple_of(i * TB, TB), TB)
    eg_all = eg_s[rows, :]
    beta_all = beta_s[rows, :]
    for hh in range(HEADS):
        sl = slice(hh * HD, (hh + 1) * HD)
        q = rq_s[rows, sl]
        k_t = rk_s[rows, sl].T
        v = pm_s[rows, OFF_RV + hh * HD:OFF_RV + (hh + 1) * HD]
        o_rows = []
        for j in range(TB):
            s = ret_gamma[hh] * sret_in[j, hh] + k_t[:, j:j + 1] * v[j:j + 1, :]
            sret_out[j, hh] = s
            o_rows.append(_dot(q, s))
        or_s[rows, sl] = _pick_rows(o_rows)

        gq = gq_s[rows, sl]
        gk = gk_s[rows, sl]
        gk_t = gk.T
        gv = gv_s[rows, sl]
        o_rows = []
        for j in range(TB):
            s = sgdn_in[j, hh]
            eg = eg_all[j:j + 1, hh:hh + 1]
            beta = beta_all[j:j + 1, HEADS + hh:HEADS + hh + 1]
            ks = _dot(gk, s)[j:j + 1, :]
            u = beta * gv[j:j + 1, :] - (beta * eg) * ks
            s = eg * s + gk_t[:, j:j + 1] * u
            sgdn_out[j, hh] = s
            o_rows.append(_dot(gq, s))
        og_s[rows, sl] = _pick_rows(o_rows)

    @pl.when(i == pl.num_programs(0) - 1)
    def _():
        o_r = []
        o_g = []
        for hh in range(HEADS):
            sl = slice(hh * HD, (hh + 1) * HD)
            o_r.append(_ret_out_norm(or_s[:, sl], retg_ref[:, sl],
                                     pm_s[:, OFF_RG + hh * HD:OFF_RG + (hh + 1) * HD]))
            o_g.append(_gdn_out_norm(og_s[:, sl], gdng_ref[...],
                                     pm_s[:, OFF_GZ + hh * HD:OFF_GZ + (hh + 1) * HD]))
        y_ref[...] = _merge(x_ref[...], jnp.concatenate(o_r, axis=1), jnp.concatenate(o_g, axis=1),
                            gates_s[...], wrb_ref, wgb_ref, wout_ref, post_ref)


def _mixer_sample(x, s_ret, s_gdn, s_conv, w, pos):
    n = x.shape[0]
    assert n % TB == 0
    cos, sin = _rope_tables([pos])
    body = functools.partial(_mixer_sample_body, ret_gamma=tuple(float(v) for v in _ret_gammas()))
    state_spec = pl.BlockSpec((TB, HEADS, HD, HD), lambda i: (i, 0, 0, 0))
    conv_cols = (CONV_W - 1) * CONV_DIM
    return pl.pallas_call(
        body,
        grid=(n // TB,),
        in_specs=[_resident((n, D_MODEL)), _resident((1, HD)), _resident((1, HD))]
        + _mixer_weight_specs()
        + [state_spec, state_spec, _resident((n, conv_cols))],
        out_specs=[pl.BlockSpec((n, D_MODEL), lambda i: (0, 0)), state_spec, state_spec,
                   pl.BlockSpec((n, conv_cols), lambda i: (0, 0))],
        out_shape=[
            jax.ShapeDtypeStruct((n, D_MODEL), F32),
            jax.ShapeDtypeStruct((n, HEADS, HD, HD), F32),
            jax.ShapeDtypeStruct((n, HEADS, HD, HD), F32),
            jax.ShapeDtypeStruct((n, conv_cols), F32),
        ],
        scratch_shapes=[pltpu.VMEM((n, OFF_AB), F32), pltpu.VMEM((n, 2 * D_MODEL), F32)]
        + [pltpu.VMEM((n, QK), F32) for _ in range(5)]
        + [pltpu.VMEM((n, HD), F32), pltpu.VMEM((n, HD), F32)]
        + [pltpu.VMEM((n, QK), F32), pltpu.VMEM((n, QK), F32)],
        compiler_params=pltpu.CompilerParams(dimension_semantics=("arbitrary",),
                                             vmem_limit_bytes=VMEM_LIMIT),
        name="mixer_sample",
    )(x, cos, sin, *_mixer_weight_args(w), s_ret, s_gdn, s_conv)


def _pad_lanes(v, n):
    return jnp.pad(v, ((0, 0), (0, n - v.shape[1])))


def kernel(x_prompt, x_sample, state_ret, state_gdn, state_conv, ffn1_pre_g, ffn1_post_g, ffn1_w_gate,
           ffn1_w_up, ffn1_w_down, mix_pre_g, mix_post_g, w_in, ret_norm_g, gdn_conv_w, gdn_a_log,
           gdn_dt_bias, gdn_norm_g, w_ret_branch, w_gdn_branch, w_out, ffn2_pre_g, ffn2_post_g,
           ffn2_w_gate, ffn2_w_up, ffn2_w_down):
    depth = w_in.shape[0]
    b, t, _ = x_prompt.shape
    n_s, t_s, _ = x_sample.shape
    assert t_s == 1
    yp = x_prompt.reshape(b * t, D_MODEL)
    ys = x_sample.reshape(n_s, D_MODEL)
    outs = [[] for _ in range(6)]
    for l in range(depth):
        row = lambda a: a[l][None, :]
        f1 = (row(ffn1_pre_g), row(ffn1_post_g), ffn1_w_gate[l].astype(BF16),
              ffn1_w_up[l].astype(BF16), ffn1_w_down[l].astype(BF16))

        w_in_t = w_in[l].T
        casts = (
            (w_in_t, (0, OFF_AB)), (w_in_t, (OFF_GATES, 2 * D_MODEL)),
            (w_ret_branch[l], None), (w_gdn_branch[l], None), (w_out[l], None),
            (ffn2_w_gate[l], None), (ffn2_w_up[l], None), (ffn2_w_down[l], None),
        )
        yp, (w_main, w_gates, w_rb, w_gb, w_o, g2, u2, d2) = _ffn(yp, *f1, tm=FFN_TM, casts=casts)
        w = {
            "mix_pre_g": row(mix_pre_g), "mix_post_g": row(mix_post_g),
            "w_main": w_main,
            "w_ab": _pad_lanes(w_in[l, :, OFF_AB:OFF_GATES], HD).astype(BF16),
            "w_gates": w_gates,
            "conv_w": gdn_conv_w[l],
            "a_log": _pad_lanes(row(gdn_a_log), HD), "dt_bias": _pad_lanes(row(gdn_dt_bias), HD),
            "ret_norm_g": row(ret_norm_g), "gdn_norm_g": row(gdn_norm_g),
            "w_rb": w_rb, "w_gb": w_gb, "w_out": w_o,
        }
        f2 = (row(ffn2_pre_g), row(ffn2_post_g), g2, u2, d2)

        yp, r1, g1, c1 = _mixer_prompt(yp.reshape(b, t, D_MODEL), w)
        yp, _ = _ffn(yp.reshape(b * t, D_MODEL), *f2, tm=FFN_TM)

        ys, _ = _ffn(ys, *f1, tm=n_s)
        ys, r2, g2s, c2 = _mixer_sample(ys, state_ret[l], state_gdn[l],
                                        state_conv[l].reshape(n_s, (CONV_W - 1) * CONV_DIM), w,
                                        PAST_LEN)
        ys, _ = _ffn(ys, *f2, tm=n_s)
        c2 = c2.reshape(n_s, CONV_W - 1, CONV_DIM)
        for lst, val in zip(outs, (r1, g1, c1, r2, g2s, c2)):
            lst.append(val)
    stacked = [v[0][None] if depth == 1 else jnp.stack(v) for v in outs]
    return (yp.reshape(b, t, D_MODEL), ys.reshape(n_s, t_s, D_MODEL), *stacked)
```

```python
import functools

import numpy as np
import jax
import jax.numpy as jnp
from jax import lax
from jax.experimental import pallas as pl
from jax.experimental.pallas import tpu as pltpu

F32 = jnp.float32
BF16 = jnp.bfloat16

D_MODEL = 1024
D_FF = 2816
HEADS = 4
HD = 128
QK = HEADS * HD
CONV_W = 4
CONV_DIM = 3 * QK
CHUNK = 64
ROPE_BASE = 10000.0
EPS = 1e-6
PAST_LEN = 16384

OFF_RQ, OFF_RK, OFF_RV, OFF_RG = 0, QK, 2 * QK, 3 * QK
OFF_CONV = 4 * QK
OFF_GZ = OFF_CONV + CONV_DIM
OFF_AB = OFF_GZ + QK
OFF_GATES = OFF_AB + 2 * HEADS
D_IN = OFF_GATES + 2 * D_MODEL

TM = 256
NB = 2
TB = 8
FFN_TM = 512
FFN_SUB = 256
CONV_PAD = 8
BF16_ROWS = 16
F32_ROWS = 8

VMEM_LIMIT = 56 * 1024 * 1024


def _silu(x):
    return x * jax.nn.sigmoid(x)


def _rms(x, g):
    return x * lax.rsqrt(jnp.mean(x * x, axis=-1, keepdims=True) + EPS) * g


def _dot(a, b):
    return jnp.dot(a.astype(BF16), b.astype(BF16), preferred_element_type=F32)


def _dot_nt(a, b):
    return lax.dot_general(a.astype(BF16), b.astype(BF16), (((1,), (1,)), ((), ())),
                           preferred_element_type=F32)


def _dot_tn(a, b):
    return lax.dot_general(a.astype(BF16), b.astype(BF16), (((0,), (0,)), ((), ())),
                           preferred_element_type=F32)


def _split3(a):
    hi = a.astype(BF16)
    r = a - hi.astype(F32)
    mid = r.astype(BF16)
    lo = (r - mid.astype(F32)).astype(BF16)
    return hi, mid, lo


def _resident(shape):
    nd = len(shape)
    return pl.BlockSpec(shape, lambda *_: (0,) * nd, pipeline_mode=pl.Buffered(1))


def _ffn_body(x_ref, pre_ref, post_ref, wg_ref, wu_ref, wd_ref, *rest, cast_transposed):
    n_cast = len(cast_transposed)
    cast_in, o_ref, cast_out = rest[:n_cast], rest[n_cast], rest[n_cast + 1:]
    tm = x_ref.shape[0]
    sub = min(tm, FFN_SUB)
    parts = [slice(r, r + sub) for r in range(0, tm, sub)]
    xs = [x_ref[p, :] for p in parts]
    hs = [_rms(x, pre_ref[...]).astype(BF16) for x in xs]
    acts = []
    for h in hs:
        g = jnp.dot(h, wg_ref[...], preferred_element_type=F32)
        u = jnp.dot(h, wu_ref[...], preferred_element_type=F32)
        acts.append((_silu(g) * u).astype(BF16))
    for p, x, a in zip(parts, xs, acts):
        y = jnp.dot(a, wd_ref[...], preferred_element_type=F32)
        o_ref[p, :] = x + 0.5 * _rms(y, post_ref[...])
    for src, dst, transposed in zip(cast_in, cast_out, cast_transposed):
        blk = src[...]
        dst[...] = (blk.T if transposed else blk).astype(BF16)


def _cast_row_block(rows, steps):
    rb = BF16_ROWS
    while rows % rb or rows // rb > steps:
        rb += BF16_ROWS
    return rb


def _ffn(x, pre_g, post_g, wg, wu, wd, tm, casts=()):
    n = x.shape[0]
    assert n % tm == 0
    steps = n // tm
    in_specs = [
        pl.BlockSpec((tm, D_MODEL), lambda i: (i, 0)),
        _resident((1, D_MODEL)), _resident((1, D_MODEL)),
        _resident((D_MODEL, D_FF)), _resident((D_MODEL, D_FF)), _resident((D_FF, D_MODEL)),
    ]
    out_specs = [pl.BlockSpec((tm, D_MODEL), lambda i: (i, 0))]
    out_shape = [jax.ShapeDtypeStruct((n, D_MODEL), F32)]
    for arr, region in casts:
        rows, width = arr.shape
        if region is None:
            rb = _cast_row_block(rows, steps)
            last = rows // rb - 1
            spec = pl.BlockSpec((rb, width), lambda i, last=last: (jnp.minimum(i, last), 0))
            in_specs.append(spec)
            out_specs.append(spec)
            out_shape.append(jax.ShapeDtypeStruct((rows, width), BF16))
        else:
            row0, n_rows = region
            assert n_rows % HD == 0 and n_rows // HD <= steps
            last = n_rows // HD - 1
            in_specs.append(pl.BlockSpec(
                (pl.Element(HD), pl.Element(width)),
                lambda i, row0=row0, last=last: (
                    pl.multiple_of(row0 + HD * jnp.minimum(i, last), F32_ROWS), 0)))
            out_specs.append(pl.BlockSpec((width, HD), lambda i, last=last: (0, jnp.minimum(i, last))))
            out_shape.append(jax.ShapeDtypeStruct((width, n_rows), BF16))
    outs = pl.pallas_call(
        functools.partial(_ffn_body, cast_transposed=tuple(c[1] is not None for c in casts)),
        grid=(steps,),
        in_specs=in_specs,
        out_specs=out_specs,
        out_shape=out_shape,
        compiler_params=pltpu.CompilerParams(dimension_semantics=("arbitrary",),
                                             vmem_limit_bytes=VMEM_LIMIT),
        name="ffn",
    )(x, pre_g, post_g, wg, wu, wd, *[c[0] for c in casts])
    return outs[0], list(outs[1:])


def _rope(x, cos, sin_signed):
    return x * cos + pltpu.roll(x, HD // 2, 1) * sin_signed


def _ret_out_norm(o, g_row, gate_act):
    mu = jnp.mean(o, axis=-1, keepdims=True)
    d = o - mu
    var = jnp.mean(d * d, axis=-1, keepdims=True)
    return gate_act * (d * lax.rsqrt(var + EPS) * g_row)


def _gdn_out_norm(o, g_row, gate_act):
    return o * lax.rsqrt(jnp.mean(o * o, axis=-1, keepdims=True) + EPS) * g_row * gate_act


def _l2norm(x, scale):
    return x * (lax.rsqrt(jnp.sum(x * x, axis=-1, keepdims=True) + EPS) * scale)


def _softplus(x):
    return jnp.maximum(x, 0.0) + jnp.log(1.0 + jnp.exp(-jnp.abs(x)))


def _merge(x, o_r, o_g, sig_gates, wrb_ref, wgb_ref, wout_ref, post_ref):
    y = (sig_gates[:, :D_MODEL] * _dot(o_r, wrb_ref[...])
         + sig_gates[:, D_MODEL:] * _dot(o_g, wgb_ref[...]))
    m = _dot(y, wout_ref[...])
    return x + _rms(m, post_ref[...])


def _unit_lower_inverse_many(a_list, masks):
    eye, m16, off32, off64 = masks
    ad = [a * m16 for a in a_list]
    x = [eye - v for v in ad]
    p = [_dot(v, v) for v in ad]
    for level in range(3):
        x = [xi + _dot(xi, pi) for xi, pi in zip(x, p)]
        if level < 2:
            p = [_dot(pi, pi) for pi in p]
    for m in (off32, off64):
        t = [_dot(xi, a * m) for xi, a in zip(x, a_list)]
        x = [xi - _dot(ti, xi) for xi, ti in zip(x, t)]
    return x


def _inverse_masks():
    r = lax.broadcasted_iota(jnp.int32, (CHUNK, CHUNK), 0)
    c = lax.broadcasted_iota(jnp.int32, (CHUNK, CHUNK), 1)
    one, zero = jnp.float32(1.0), jnp.float32(0.0)
    eye = jnp.where(r == c, one, zero)
    same16 = (r >> 4) == (c >> 4)
    same32 = (r >> 5) == (c >> 5)
    m16 = jnp.where(same16, one, zero)
    m32 = jnp.where(same32, one, zero)
    return eye, m16, m32 - m16, 1.0 - m32


def _retention_tile(pm, cos, sin, rd_ref, ku_ref, retg_ref, s_ref, tile_decay):
    ri = lax.broadcasted_iota(jnp.int32, (TM, TM), 0)
    ci = lax.broadcasted_iota(jnp.int32, (TM, TM), 1)
    causal = ri >= ci
    heads = []
    for hh in range(HEADS):
        sl = slice(hh * HD, (hh + 1) * HD)
        rq = pm[:, OFF_RQ + hh * HD:OFF_RQ + (hh + 1) * HD]
        rk = pm[:, OFF_RK + hh * HD:OFF_RK + (hh + 1) * HD]
        v = pm[:, OFF_RV + hh * HD:OFF_RV + (hh + 1) * HD]
        rg = pm[:, OFF_RG + hh * HD:OFF_RG + (hh + 1) * HD]
        qs = _rope(rq, cos, sin) * rd_ref[hh]
        ku = _rope(rk, cos, sin) * ku_ref[hh]
        s = s_ref[hh]
        sc = jnp.where(causal, _dot_nt(qs, ku), 0.0)
        o = _dot(sc, v) + _dot(qs, s)
        s_ref[hh] = tile_decay[hh] * (s + _dot_tn(ku, v))
        heads.append(_ret_out_norm(o, retg_ref[:, sl], rg))
    return jnp.concatenate(heads, axis=1)


def _short_conv_tile(u, buf, convw_ref, tail_ref):
    buf[CONV_PAD:CONV_PAD + TM, :] = u
    acc = buf[CONV_PAD - 3:CONV_PAD - 3 + TM, :] * convw_ref[0:1, :]
    for i in range(1, CONV_W):
        acc = acc + buf[CONV_PAD - 3 + i:CONV_PAD - 3 + i + TM, :] * convw_ref[i:i + 1, :]
    tail = buf[CONV_PAD + TM - 3:CONV_PAD + TM, :]
    tail_ref[...] = tail
    buf[CONV_PAD - 3:CONV_PAD, :] = tail
    return _silu(acc)


def _gdn_decay_beta(ab, alog_ref, dtb_ref):
    g_all = -jnp.exp(alog_ref[...]) * _softplus(ab + dtb_ref[...])
    beta_all = jax.nn.sigmoid(ab)
    return g_all, beta_all


def _gdn_tiles(cqs, abs_, gzs, ltri, alog_ref, dtb_ref, gdng_ref, s_refs):
    seqs = range(len(cqs))
    dd = functools.partial(jnp.dot, preferred_element_type=F32)
    gcum, gcum_t, beta_all = [], [], []
    for b in seqs:
        g_all, beta = _gdn_decay_beta(abs_[b], alog_ref, dtb_ref)
        beta_all.append(beta)
        g_hi, g_mid, g_lo = _split3(g_all)
        gcum.append(dd(ltri, g_hi) + (dd(ltri, g_mid) + dd(ltri, g_lo)))
        gcum_t.append(gcum[b].T)

    r64 = lax.broadcasted_iota(jnp.int32, (CHUNK, CHUNK), 0)
    c64 = lax.broadcasted_iota(jnp.int32, (CHUNK, CHUNK), 1)
    causal64 = r64 >= c64
    strict64 = r64 > c64
    masks = _inverse_masks()

    n_ch = TM // CHUNK
    rows = [slice(c * CHUNK, (c + 1) * CHUNK) for c in range(n_ch)]
    bh = [(b, hh) for b in seqs for hh in range(HEADS)]
    probs = [(b, hh, c) for b, hh in bh for c in range(n_ch)]
    gq = {(b, hh): _l2norm(cqs[b][:, hh * HD:(hh + 1) * HD], HD ** -0.5) for b, hh in bh}
    gk = {(b, hh): _l2norm(cqs[b][:, QK + hh * HD:QK + (hh + 1) * HD], 1.0) for b, hh in bh}
    gv = {(b, hh): cqs[b][:, 2 * QK + hh * HD:2 * QK + (hh + 1) * HD] for b, hh in bh}
    qc = {(b, hh, c): gq[b, hh][rows[c]] for b, hh, c in probs}
    kc = {(b, hh, c): gk[b, hh][rows[c]] for b, hh, c in probs}
    vc = {(b, hh, c): gv[b, hh][rows[c]] for b, hh, c in probs}
    gc = {(b, hh, c): gcum[b][rows[c], hh:hh + 1] for b, hh, c in probs}
    bcol = {(b, hh, c): beta_all[b][rows[c], HEADS + hh:HEADS + hh + 1] for b, hh, c in probs}
    dm = {(b, hh, c): jnp.where(
        causal64, jnp.exp(jnp.minimum(gc[b, hh, c] - gcum_t[b][hh:hh + 1, rows[c]], 0.0)), 0.0)
        for b, hh, c in probs}
    kk = {p: _dot_nt(kc[p], kc[p]) for p in probs}
    qk = {p: _dot_nt(qc[p], kc[p]) * dm[p] for p in probs}
    a_mats = [jnp.where(strict64, bcol[p] * dm[p] * kk[p], 0.0) for p in probs]
    tinv = dict(zip(probs, _unit_lower_inverse_many(a_mats, masks)))
    eg = {p: jnp.exp(gc[p]) for p in probs}
    uw = {p: _dot(tinv[p], jnp.concatenate([bcol[p] * vc[p], (bcol[p] * eg[p]) * kc[p]], axis=1))
          for p in probs}

    s = {(b, hh): s_refs[b][hh] for b, hh in bh}
    outs = {k: [] for k in bh}
    for c in range(n_ch):
        for b, hh in bh:
            p = (b, hh, c)
            u = uw[p][:, :HD] - _dot(uw[p][:, HD:], s[b, hh])
            outs[b, hh].append(eg[p] * _dot(qc[p], s[b, hh]) + _dot(qk[p], u))
            gl = gc[p][CHUNK - 1:CHUNK, :]
            s[b, hh] = jnp.exp(gl) * s[b, hh] + _dot_tn(kc[p] * jnp.exp(gl - gc[p]), u)
    o_gs = []
    for b in seqs:
        heads = []
        for hh in range(HEADS):
            s_refs[b][hh] = s[b, hh]
            o = jnp.concatenate(outs[b, hh], axis=0)
            heads.append(_gdn_out_norm(o, gdng_ref[...], gzs[b][:, hh * HD:(hh + 1) * HD]))
        o_gs.append(jnp.concatenate(heads, axis=1))
    return o_gs


def _mixer_prompt_body(x_ref, cos_ref, sin_ref, rd_ref, ku_ref, ltri_ref,
                       pre_ref, post_ref, wmain_ref, wab_ref, wgates_ref, convw_ref, alog_ref, dtb_ref,
                       retg_ref, gdng_ref, wrb_ref, wgb_ref, wout_ref,
                       y_ref, sret_ref, sgdn_ref, conv_ref,
                       cbuf, *, ret_tile_decay):
    t = pl.program_id(1)
    nb = x_ref.shape[0]

    @pl.when(t == 0)
    def _():
        sret_ref[...] = jnp.zeros_like(sret_ref)
        sgdn_ref[...] = jnp.zeros_like(sgdn_ref)
        cbuf[:, 0:CONV_PAD, :] = jnp.zeros((nb, CONV_PAD, CONV_DIM), F32)

    cos = cos_ref[...]
    sin = sin_ref[...]
    ltri = ltri_ref[...]
    seqs = range(nb)
    xs = [x_ref[b] for b in seqs]
    hs = [_rms(xs[b], pre_ref[...]).astype(BF16) for b in seqs]
    pgs = [jnp.dot(hs[b], wmain_ref[:, OFF_CONV:OFF_AB], preferred_element_type=F32) for b in seqs]
    abs_ = [jnp.dot(hs[b], wab_ref[...], preferred_element_type=F32) for b in seqs]
    cqs = [_short_conv_tile(pgs[b][:, 0:CONV_DIM], cbuf.at[b], convw_ref, conv_ref.at[b]) for b in seqs]
    pms = [jnp.dot(hs[b], wmain_ref[:, 0:OFF_CONV], preferred_element_type=F32) for b in seqs]
    o_rs = [_retention_tile(pms[b], cos, sin, rd_ref, ku_ref, retg_ref, sret_ref.at[b], ret_tile_decay)
            for b in seqs]
    o_gs = _gdn_tiles(cqs, abs_, [pgs[b][:, CONV_DIM:] for b in seqs], ltri, alog_ref, dtb_ref, gdng_ref,
                      [sgdn_ref.at[b] for b in seqs])
    for b in seqs:
        gates = jnp.dot(hs[b], wgates_ref[...], preferred_element_type=F32)
        y_ref[b] = _merge(xs[b], o_rs[b], o_gs[b], gates, wrb_ref, wgb_ref, wout_ref, post_ref)


def _ret_gammas():
    return 1.0 - 2.0 ** (-5.0 - np.arange(HEADS, dtype=np.float64))


def _rope_tables(pos):
    inv = ROPE_BASE ** (-np.arange(0, HD, 2, dtype=np.float64) / HD)
    ang = np.asarray(pos, np.float64)[:, None] * inv[None, :]
    cos = np.concatenate([np.cos(ang), np.cos(ang)], axis=1)
    sin = np.concatenate([-np.sin(ang), np.sin(ang)], axis=1)
    return jnp.asarray(cos, F32), jnp.asarray(sin, F32)


def _mixer_weight_specs():
    return [
        _resident((1, D_MODEL)), _resident((1, D_MODEL)),
        _resident((D_MODEL, OFF_AB)), _resident((D_MODEL, HD)), _resident((D_MODEL, 2 * D_MODEL)),
        _resident((CONV_W, CONV_DIM)), _resident((1, HD)), _resident((1, HD)),
        _resident((1, QK)), _resident((1, HD)),
        _resident((QK, D_MODEL)), _resident((QK, D_MODEL)), _resident((D_MODEL, D_MODEL)),
    ]


def _mixer_weight_args(w):
    return (w["mix_pre_g"], w["mix_post_g"], w["w_main"], w["w_ab"], w["w_gates"], w["conv_w"],
            w["a_log"], w["dt_bias"], w["ret_norm_g"], w["gdn_norm_g"], w["w_rb"], w["w_gb"], w["w_out"])


def _mixer_prompt(x, w):
    b, t, _ = x.shape
    assert t % TM == 0 and b % NB == 0
    cos, sin = _rope_tables(np.arange(t))
    gam = _ret_gammas()
    i1 = np.arange(1, TM + 1, dtype=np.float64)
    rd = np.broadcast_to((gam[:, None] ** i1[None, :])[:, :, None], (HEADS, TM, HD))
    ku = np.broadcast_to((HD ** -0.5 * gam[:, None] ** (-i1[None, :]))[:, :, None], (HEADS, TM, HD))
    tile_decay = tuple(float(v) for v in gam ** TM)
    r = np.arange(TM)
    ltri = ((r[:, None] >= r[None, :]) & (r[:, None] // CHUNK == r[None, :] // CHUNK))

    body = functools.partial(_mixer_prompt_body, ret_tile_decay=tile_decay)
    state_spec = pl.BlockSpec((NB, HEADS, HD, HD), lambda i, j: (i, 0, 0, 0))
    return pl.pallas_call(
        body,
        grid=(b // NB, t // TM),
        in_specs=[
            pl.BlockSpec((NB, TM, D_MODEL), lambda i, j: (i, j, 0)),
            pl.BlockSpec((TM, HD), lambda i, j: (j, 0)),
            pl.BlockSpec((TM, HD), lambda i, j: (j, 0)),
            _resident((HEADS, TM, HD)), _resident((HEADS, TM, HD)), _resident((TM, TM)),
        ] + _mixer_weight_specs(),
        out_specs=[
            pl.BlockSpec((NB, TM, D_MODEL), lambda i, j: (i, j, 0)),
            state_spec, state_spec,
            pl.BlockSpec((NB, CONV_W - 1, CONV_DIM), lambda i, j: (i, 0, 0)),
        ],
        out_shape=[
            jax.ShapeDtypeStruct((b, t, D_MODEL), F32),
            jax.ShapeDtypeStruct((b, HEADS, HD, HD), F32),
            jax.ShapeDtypeStruct((b, HEADS, HD, HD), F32),
            jax.ShapeDtypeStruct((b, CONV_W - 1, CONV_DIM), F32),
        ],
        scratch_shapes=[pltpu.VMEM((NB, CONV_PAD + TM, CONV_DIM), F32)],
        compiler_params=pltpu.CompilerParams(dimension_semantics=("arbitrary", "arbitrary"),
                                             vmem_limit_bytes=VMEM_LIMIT),
        name="mixer_prompt",
    )(x, cos, sin, jnp.asarray(rd, F32), jnp.asarray(ku, F32), jnp.asarray(ltri, BF16),
      *_mixer_weight_args(w))


A_WIDTH = 4 * QK
PROJ_TM = 2 * TM


def _short_conv_tile(u, buf, convw_ref, tail_ref):
    buf[CONV_PAD:CONV_PAD + TM, :] = u
    acc = buf[CONV_PAD - 3:CONV_PAD - 3 + TM, :] * convw_ref[0:1, :]
    for i in range(1, CONV_W):
        acc = acc + buf[CONV_PAD - 3 + i:CONV_PAD - 3 + i + TM, :] * convw_ref[i:i + 1, :]
    tail = buf[CONV_PAD + TM - 3:CONV_PAD + TM, :]
    tail_ref[...] = tail
    buf[CONV_PAD - 3:CONV_PAD, :] = tail
    return _silu(acc)


def _gdn_decay_beta(ab, alog_ref, dtb_ref):
    g_all = -jnp.exp(alog_ref[...]) * _softplus(ab + dtb_ref[...])
    beta_all = jax.nn.sigmoid(ab)
    return g_all, beta_all


def _mixer_proj_body(x_ref, cos_ref, sin_ref, rd_ref, ku_ref, pre_ref, wmain_ref, wab_ref, wgates_ref,
                     convw_ref, alog_ref, dtb_ref,
                     ret_ref, gdn_ref, sig_ref, gb_ref, conv_ref,
                     cbuf):
    t = pl.program_id(1)

    @pl.when(t == 0)
    def _():
        cbuf[0:CONV_PAD, :] = jnp.zeros((CONV_PAD, CONV_DIM), F32)

    n_sub = x_ref.shape[1] // TM
    subs = [slice(k * TM, (k + 1) * TM) for k in range(n_sub)]
    hs = [_rms(x_ref[0, r, :], pre_ref[...]).astype(BF16) for r in subs]

    def dot_h(k, w):
        return jnp.dot(hs[k], w, preferred_element_type=F32)

    def gdn_features(k, pg, ab):
        r = subs[k]
        cq = _short_conv_tile(pg[:, 0:CONV_DIM], cbuf, convw_ref, conv_ref.at[0])
        for hh in range(HEADS):
            sl = slice(hh * HD, (hh + 1) * HD)
            gdn_ref[0, r, sl] = _l2norm(cq[:, sl], HD ** -0.5).astype(BF16)
            gdn_ref[0, r, QK + hh * HD:QK + (hh + 1) * HD] = _l2norm(
                cq[:, QK + hh * HD:QK + (hh + 1) * HD], 1.0).astype(BF16)
        gdn_ref[0, r, 2 * QK:3 * QK] = cq[:, 2 * QK:3 * QK].astype(BF16)
        gdn_ref[0, r, 3 * QK:4 * QK] = _silu(pg[:, CONV_DIM:]).astype(BF16)
        g_all, beta_all = _gdn_decay_beta(ab, alog_ref, dtb_ref)
        lane = lax.broadcasted_iota(jnp.int32, (TM, HD), 1)
        gb_ref[0, r, :] = jnp.where(lane < HEADS, g_all, beta_all)

    def ret_features(k, pm):
        r = subs[k]
        cos = cos_ref[r, :]
        sin = sin_ref[r, :]
        for hh in range(HEADS):
            sl = slice(hh * HD, (hh + 1) * HD)
            ret_ref[0, r, sl] = (_rope(pm[:, OFF_RQ + hh * HD:OFF_RQ + (hh + 1) * HD], cos, sin)
                                 * rd_ref[hh]).astype(BF16)
            ret_ref[0, r, QK + hh * HD:QK + (hh + 1) * HD] = (
                _rope(pm[:, OFF_RK + hh * HD:OFF_RK + (hh + 1) * HD], cos, sin) * ku_ref[hh]).astype(BF16)
        ret_ref[0, r, 2 * QK:3 * QK] = pm[:, OFF_RV:OFF_RV + QK].astype(BF16)
        ret_ref[0, r, 3 * QK:4 * QK] = _silu(pm[:, OFF_RG:OFF_RG + QK]).astype(BF16)

    def gate_features(k, gates):
        sig_ref[0, subs[k], :] = jax.nn.sigmoid(gates).astype(BF16)

    w_pg, w_pm = wmain_ref.at[:, OFF_CONV:OFF_AB], wmain_ref.at[:, 0:OFF_CONV]
    pg = dot_h(0, w_pg[...])
    ab = dot_h(0, wab_ref[...])
    for k in range(n_sub):
        gdn_features(k, pg, ab)
        if k + 1 < n_sub:
            pg = dot_h(k + 1, w_pg[...])
            ab = dot_h(k + 1, wab_ref[...])
        pm = dot_h(k, w_pm[...])
        if k > 0:
            gate_features(k - 1, gates)
        ret_features(k, pm)
        gates = dot_h(k, wgates_ref[...])
    gate_features(n_sub - 1, gates)


def _ret_gammas():
    return 1.0 - 2.0 ** (-5.0 - np.arange(HEADS, dtype=np.float64))


def _rope_tables(pos):
    inv = ROPE_BASE ** (-np.arange(0, HD, 2, dtype=np.float64) / HD)
    ang = np.asarray(pos, np.float64)[:, None] * inv[None, :]
    cos = np.concatenate([np.cos(ang), np.cos(ang)], axis=1)
    sin = np.concatenate([-np.sin(ang), np.sin(ang)], axis=1)
    return jnp.asarray(cos, F32), jnp.asarray(sin, F32)


def _mixer_proj(x, w):
    b, t, _ = x.shape
    assert t % PROJ_TM == 0
    cos, sin = _rope_tables(np.arange(t))
    gam = _ret_gammas()
    i1 = np.arange(1, TM + 1, dtype=np.float64)
    rd = np.broadcast_to((gam[:, None] ** i1[None, :])[:, :, None], (HEADS, TM, HD))
    ku = np.broadcast_to((HD ** -0.5 * gam[:, None] ** (-i1[None, :]))[:, :, None], (HEADS, TM, HD))
    tile = lambda width: pl.BlockSpec((1, PROJ_TM, width), lambda i, j: (i, j, 0))
    return pl.pallas_call(
        _mixer_proj_body,
        grid=(b, t // PROJ_TM),
        in_specs=[
            tile(D_MODEL),
            pl.BlockSpec((PROJ_TM, HD), lambda i, j: (j, 0)),
            pl.BlockSpec((PROJ_TM, HD), lambda i, j: (j, 0)),
            _resident((HEADS, TM, HD)), _resident((HEADS, TM, HD)),
            _resident((1, D_MODEL)),
            _resident((D_MODEL, OFF_AB)), _resident((D_MODEL, HD)), _resident((D_MODEL, 2 * D_MODEL)),
            _resident((CONV_W, CONV_DIM)), _resident((1, HD)), _resident((1, HD)),
        ],
        out_specs=[
            tile(A_WIDTH), tile(A_WIDTH), tile(2 * D_MODEL), tile(HD),
            pl.BlockSpec((1, CONV_W - 1, CONV_DIM), lambda i, j: (i, 0, 0)),
        ],
        out_shape=[
            jax.ShapeDtypeStruct((b, t, A_WIDTH), BF16),
            jax.ShapeDtypeStruct((b, t, A_WIDTH), BF16),
            jax.ShapeDtypeStruct((b, t, 2 * D_MODEL), BF16),
            jax.ShapeDtypeStruct((b, t, HD), F32),
            jax.ShapeDtypeStruct((b, CONV_W - 1, CONV_DIM), F32),
        ],
        scratch_shapes=[pltpu.VMEM((CONV_PAD + TM, CONV_DIM), F32)],
        compiler_params=pltpu.CompilerParams(dimension_semantics=("arbitrary", "arbitrary"),
                                             vmem_limit_bytes=VMEM_LIMIT),
        name="mixer_proj",
    )(x, cos, sin, jnp.asarray(rd, F32), jnp.asarray(ku, F32),
      w["mix_pre_g"], w["w_main"], w["w_ab"], w["w_gates"], w["conv_w"], w["a_log"], w["dt_bias"])


def _retention_tile(feat, retg_ref, s_ref, tile_decay):
    ri = lax.broadcasted_iota(jnp.int32, (TM, TM), 0)
    ci = lax.broadcasted_iota(jnp.int32, (TM, TM), 1)
    causal = ri >= ci
    heads = []
    for hh in range(HEADS):
        sl = slice(hh * HD, (hh + 1) * HD)
        qs = feat[:, sl]
        ku = feat[:, QK + hh * HD:QK + (hh + 1) * HD]
        v = feat[:, 2 * QK + hh * HD:2 * QK + (hh + 1) * HD]
        gate_act = feat[:, 3 * QK + hh * HD:3 * QK + (hh + 1) * HD]
        s = s_ref[hh]
        sc = jnp.where(causal, _dot_nt(qs, ku), 0.0)
        o = _dot(sc, v) + _dot(qs, s)
        s_ref[hh] = tile_decay[hh] * (s + _dot_tn(ku, v))
        heads.append(_ret_out_norm(o, retg_ref[:, sl], gate_act))
    return jnp.concatenate(heads, axis=1)


def _gdn_tiles(feats, gbs, ltri, gdng_ref, s_refs):
    seqs = range(len(feats))
    dd = functools.partial(jnp.dot, preferred_element_type=F32)
    gcum, gcum_t = [], []
    for b in seqs:
        g_hi, g_mid, g_lo = _split3(gbs[b])
        gcum.append(dd(ltri, g_hi) + (dd(ltri, g_mid) + dd(ltri, g_lo)))
        gcum_t.append(gcum[b].T)

    r64 = lax.broadcasted_iota(jnp.int32, (CHUNK, CHUNK), 0)
    c64 = lax.broadcasted_iota(jnp.int32, (CHUNK, CHUNK), 1)
    causal64 = r64 >= c64
    strict64 = r64 > c64
    masks = _inverse_masks()

    n_ch = TM // CHUNK
    rows = [slice(c * CHUNK, (c + 1) * CHUNK) for c in range(n_ch)]
    bh = [(b, hh) for b in seqs for hh in range(HEADS)]
    probs = [(b, hh, c) for b, hh in bh for c in range(n_ch)]
    qc = {(b, hh, c): feats[b][rows[c], hh * HD:(hh + 1) * HD] for b, hh, c in probs}
    kc = {(b, hh, c): feats[b][rows[c], QK + hh * HD:QK + (hh + 1) * HD] for b, hh, c in probs}
    vc = {(b, hh, c): feats[b][rows[c], 2 * QK + hh * HD:2 * QK + (hh + 1) * HD] for b, hh, c in probs}
    gc = {(b, hh, c): gcum[b][rows[c], hh:hh + 1] for b, hh, c in probs}
    bcol = {(b, hh, c): gbs[b][rows[c], HEADS + hh:HEADS + hh + 1] for b, hh, c in probs}
    dm = {(b, hh, c): jnp.where(
        causal64, jnp.exp(jnp.minimum(gc[b, hh, c] - gcum_t[b][hh:hh + 1, rows[c]], 0.0)), 0.0)
        for b, hh, c in probs}
    kk = {p: _dot_nt(kc[p], kc[p]) for p in probs}
    qk = {p: _dot_nt(qc[p], kc[p]) * dm[p] for p in probs}
    a_mats = [jnp.where(strict64, bcol[p] * dm[p] * kk[p], 0.0) for p in probs]
    tinv = dict(zip(probs, _unit_lower_inverse_many(a_mats, masks)))
    eg = {p: jnp.exp(gc[p]) for p in probs}
    uw = {p: _dot(tinv[p], jnp.concatenate([bcol[p] * vc[p], (bcol[p] * eg[p]) * kc[p]], axis=1))
          for p in probs}
    gl = {p: gc[p][CHUNK - 1:CHUNK, :] for p in probs}
    kdw = {p: _dot_tn(kc[p] * jnp.exp(gl[p] - gc[p]), uw[p]) for p in probs}
    qkw = {p: _dot(qk[p], uw[p]) for p in probs}
    lhs = {p: jnp.concatenate([kdw[p][:, HD:], eg[p] * qc[p] - qkw[p][:, HD:]], axis=0).astype(BF16)
           for p in probs}

    s = {(b, hh): s_refs[b][hh] for b, hh in bh}
    outs = {k: [] for k in bh}
    for c in range(n_ch):
        for b, hh in bh:
            p = (b, hh, c)
            ps = _dot(lhs[p], s[b, hh])
            outs[b, hh].append(ps[HD:, :] + qkw[p][:, :HD])
            s[b, hh] = jnp.exp(gl[p]) * s[b, hh] + (kdw[p][:, :HD] - ps[:HD, :])
    o_gs = []
    for b in seqs:
        heads = []
        for hh in range(HEADS):
            s_refs[b][hh] = s[b, hh]
            o = jnp.concatenate(outs[b, hh], axis=0)
            heads.append(_gdn_out_norm(o, gdng_ref[...], feats[b][:, 3 * QK + hh * HD:3 * QK + (hh + 1) * HD]))
        o_gs.append(jnp.concatenate(heads, axis=1))
    return o_gs


def _mixer_rec_body(x_ref, ret_ref, gdn_ref, sig_ref, gb_ref, ltri_ref,
                    post_ref, retg_ref, gdng_ref, wrb_ref, wgb_ref, wout_ref,
                    y_ref, sret_ref, sgdn_ref, *, ret_tile_decay):
    t = pl.program_id(1)
    nb = x_ref.shape[0]

    @pl.when(t == 0)
    def _():
        sret_ref[...] = jnp.zeros_like(sret_ref)
        sgdn_ref[...] = jnp.zeros_like(sgdn_ref)

    seqs = range(nb)
    o_rs = [_retention_tile(ret_ref[b], retg_ref, sret_ref.at[b], ret_tile_decay) for b in seqs]
    o_gs = _gdn_tiles([gdn_ref[b] for b in seqs], [gb_ref[b] for b in seqs], ltri_ref[...], gdng_ref,
                      [sgdn_ref.at[b] for b in seqs])
    for b in seqs:
        y_ref[b] = _merge(x_ref[b], o_rs[b], o_gs[b], sig_ref[b], wrb_ref, wgb_ref, wout_ref, post_ref)


def _mixer_rec(x, ret_f, gdn_f, sig_f, gb_f, w):
    b, t, _ = x.shape
    assert t % TM == 0 and b % NB == 0
    tile_decay = tuple(float(v) for v in _ret_gammas() ** TM)
    r = np.arange(TM)
    ltri = ((r[:, None] >= r[None, :]) & (r[:, None] // CHUNK == r[None, :] // CHUNK))
    tile = lambda width: pl.BlockSpec((NB, TM, width), lambda i, j: (i, j, 0))
    state_spec = pl.BlockSpec((NB, HEADS, HD, HD), lambda i, j: (i, 0, 0, 0))
    return pl.pallas_call(
        functools.partial(_mixer_rec_body, ret_tile_decay=tile_decay),
        grid=(b // NB, t // TM),
        in_specs=[
            tile(D_MODEL), tile(A_WIDTH), tile(A_WIDTH), tile(2 * D_MODEL), tile(HD),
            _resident((TM, TM)),
            _resident((1, D_MODEL)), _resident((1, QK)), _resident((1, HD)),
            _resident((QK, D_MODEL)), _resident((QK, D_MODEL)), _resident((D_MODEL, D_MODEL)),
        ],
        out_specs=[tile(D_MODEL), state_spec, state_spec],
        out_shape=[
            jax.ShapeDtypeStruct((b, t, D_MODEL), F32),
            jax.ShapeDtypeStruct((b, HEADS, HD, HD), F32),
            jax.ShapeDtypeStruct((b, HEADS, HD, HD), F32),
        ],
        compiler_params=pltpu.CompilerParams(dimension_semantics=("arbitrary", "arbitrary"),
                                             vmem_limit_bytes=VMEM_LIMIT),
        name="mixer_rec",
    )(x, ret_f, gdn_f, sig_f, gb_f, jnp.asarray(ltri, BF16),
      w["mix_post_g"], w["ret_norm_g"], w["gdn_norm_g"], w["w_rb"], w["w_gb"], w["w_out"])


def _mixer_weight_specs():
    return [
        _resident((1, D_MODEL)), _resident((1, D_MODEL)),
        _resident((D_MODEL, OFF_AB)), _resident((D_MODEL, HD)), _resident((D_MODEL, 2 * D_MODEL)),
        _resident((CONV_W, CONV_DIM)), _resident((1, HD)), _resident((1, HD)),
        _resident((1, QK)), _resident((1, HD)),
        _resident((QK, D_MODEL)), _resident((QK, D_MODEL)), _resident((D_MODEL, D_MODEL)),
    ]


def _mixer_weight_args(w):
    return (w["mix_pre_g"], w["mix_post_g"], w["w_main"], w["w_ab"], w["w_gates"], w["conv_w"],
            w["a_log"], w["dt_bias"], w["ret_norm_g"], w["gdn_norm_g"], w["w_rb"], w["w_gb"], w["w_out"])


def _pick_rows(rows):
    ri = lax.broadcasted_iota(jnp.int32, rows[0].shape, 0)
    out = rows[0]
    for j in range(1, len(rows)):
        out = jnp.where(ri == j, rows[j], out)
    return out


def _mixer_sample_body(x_ref, cos_ref, sin_ref,
                       pre_ref, post_ref, wmain_ref, wab_ref, wgates_ref, convw_ref, alog_ref, dtb_ref,
                       retg_ref, gdng_ref, wrb_ref, wgb_ref, wout_ref,
                       sret_in, sgdn_in, conv_in,
                       y_ref, sret_out, sgdn_out, conv_out,
                       pm_s, gates_s, rq_s, rk_s, gq_s, gk_s, gv_s, eg_s, beta_s, or_s, og_s, *, ret_gamma):
    i = pl.program_id(0)

    @pl.when(i == 0)
    def _():
        h = _rms(x_ref[...], pre_ref[...]).astype(BF16)
        pm = jnp.dot(h, wmain_ref[...], preferred_element_type=F32)
        ab = jnp.dot(h, wab_ref[...], preferred_element_type=F32)
        pm_s[...] = pm
        gates_s[...] = jnp.dot(h, wgates_ref[...], preferred_element_type=F32)
        cos = cos_ref[...]
        sin = sin_ref[...]
        cin = pm[:, OFF_CONV:OFF_CONV + CONV_DIM]
        acc = cin * convw_ref[CONV_W - 1:CONV_W, :]
        for r in range(CONV_W - 1):
            acc = acc + conv_in[:, r * CONV_DIM:(r + 1) * CONV_DIM] * convw_ref[r:r + 1, :]
        conv_out[:, 0:2 * CONV_DIM] = conv_in[:, CONV_DIM:3 * CONV_DIM]
        conv_out[:, 2 * CONV_DIM:3 * CONV_DIM] = cin
        cq = _silu(acc)
        g_all, beta_all = _gdn_decay_beta(ab, alog_ref, dtb_ref)
        eg_s[...] = jnp.exp(g_all)
        beta_s[...] = beta_all
        for hh in range(HEADS):
            sl = slice(hh * HD, (hh + 1) * HD)
            rq_s[:, sl] = _rope(pm[:, OFF_RQ + hh * HD:OFF_RQ + (hh + 1) * HD], cos, sin)
            rk_s[:, sl] = _rope(pm[:, OFF_RK + hh * HD:OFF_RK + (hh + 1) * HD], cos, sin) * (HD ** -0.5)
            gq_s[:, sl] = _l2norm(cq[:, hh * HD:(hh + 1) * HD], HD ** -0.5)
            gk_s[:, sl] = _l2norm(cq[:, QK + hh * HD:QK + (hh + 1) * HD], 1.0)
        gv_s[...] = cq[:, 2 * QK:3 * QK]

    rows = pl.ds(pl.multiple_of(i * TB, TB), TB)
    eg_all = eg_s[rows, :]
    beta_all = beta_s[rows, :]
    for hh in range(HEADS):
        sl = slice(hh * HD, (hh + 1) * HD)
        q = rq_s[rows, sl]
        k_t = rk_s[rows, sl].T
        v = pm_s[rows, OFF_RV + hh * HD:OFF_RV + (hh + 1) * HD]
        o_rows = []
        for j in range(TB):
            s = ret_gamma[hh] * sret_in[j, hh] + k_t[:, j:j + 1] * v[j:j + 1, :]
            sret_out[j, hh] = s
            o_rows.append(_dot(q, s))
        or_s[rows, sl] = _pick_rows(o_rows)

        gq = gq_s[rows, sl]
        gk = gk_s[rows, sl]
        gk_t = gk.T
        gv = gv_s[rows, sl]
        o_rows = []
        for j in range(TB):
            s = sgdn_in[j, hh]
            eg = eg_all[j:j + 1, hh:hh + 1]
            beta = beta_all[j:j + 1, HEADS + hh:HEADS + hh + 1]
            ks = _dot(gk, s)[j:j + 1, :]
            u = beta * gv[j:j + 1, :] - (beta * eg) * ks
            s = eg * s + gk_t[:, j:j + 1] * u
            sgdn_out[j, hh] = s
            o_rows.append(_dot(gq, s))
        og_s[rows, sl] = _pick_rows(o_rows)

    @pl.when(i == pl.num_programs(0) - 1)
    def _():
        o_r = []
        o_g = []
        for hh in range(HEADS):
            sl = slice(hh * HD, (hh + 1) * HD)
            o_r.append(_ret_out_norm(or_s[:, sl], retg_ref[:, sl],
                                     _silu(pm_s[:, OFF_RG + hh * HD:OFF_RG + (hh + 1) * HD])))
            o_g.append(_gdn_out_norm(og_s[:, sl], gdng_ref[...],
                                     _silu(pm_s[:, OFF_GZ + hh * HD:OFF_GZ + (hh + 1) * HD])))
        y_ref[...] = _merge(x_ref[...], jnp.concatenate(o_r, axis=1), jnp.concatenate(o_g, axis=1),
                            jax.nn.sigmoid(gates_s[...]), wrb_ref, wgb_ref, wout_ref, post_ref)


def _mixer_sample(x, s_ret, s_gdn, s_conv, w, pos):
    n = x.shape[0]
    assert n % TB == 0
    cos, sin = _rope_tables([pos])
    body = functools.partial(_mixer_sample_body, ret_gamma=tuple(float(v) for v in _ret_gammas()))
    state_spec = pl.BlockSpec((TB, HEADS, HD, HD), lambda i: (i, 0, 0, 0))
    conv_cols = (CONV_W - 1) * CONV_DIM
    return pl.pallas_call(
        body,
        grid=(n // TB,),
        in_specs=[_resident((n, D_MODEL)), _resident((1, HD)), _resident((1, HD))]
        + _mixer_weight_specs()
        + [state_spec, state_spec, _resident((n, conv_cols))],
        out_specs=[pl.BlockSpec((n, D_MODEL), lambda i: (0, 0)), state_spec, state_spec,
                   pl.BlockSpec((n, conv_cols), lambda i: (0, 0))],
        out_shape=[
            jax.ShapeDtypeStruct((n, D_MODEL), F32),
            jax.ShapeDtypeStruct((n, HEADS, HD, HD), F32),
            jax.ShapeDtypeStruct((n, HEADS, HD, HD), F32),
            jax.ShapeDtypeStruct((n, conv_cols), F32),
        ],
        scratch_shapes=[pltpu.VMEM((n, OFF_AB), F32), pltpu.VMEM((n, 2 * D_MODEL), F32)]
        + [pltpu.VMEM((n, QK), F32) for _ in range(5)]
        + [pltpu.VMEM((n, HD), F32), pltpu.VMEM((n, HD), F32)]
        + [pltpu.VMEM((n, QK), F32), pltpu.VMEM((n, QK), F32)],
        compiler_params=pltpu.CompilerParams(dimension_semantics=("arbitrary",),
                                             vmem_limit_bytes=VMEM_LIMIT),
        name="mixer_sample",
    )(x, cos, sin, *_mixer_weight_args(w), s_ret, s_gdn, s_conv)


def _pad_lanes(v, n):
    return jnp.pad(v, ((0, 0), (0, n - v.shape[1])))


def kernel(x_prompt, x_sample, state_ret, state_gdn, state_conv, ffn1_pre_g, ffn1_post_g, ffn1_w_gate,
           ffn1_w_up, ffn1_w_down, mix_pre_g, mix_post_g, w_in, ret_norm_g, gdn_conv_w, gdn_a_log,
           gdn_dt_bias, gdn_norm_g, w_ret_branch, w_gdn_branch, w_out, ffn2_pre_g, ffn2_post_g,
           ffn2_w_gate, ffn2_w_up, ffn2_w_down):
    depth = w_in.shape[0]
    b, t, _ = x_prompt.shape
    n_s, t_s, _ = x_sample.shape
    assert t_s == 1
    yp = x_prompt.reshape(b * t, D_MODEL)
    ys = x_sample.reshape(n_s, D_MODEL)
    outs = [[] for _ in range(6)]
    for l in range(depth):
        row = lambda a: a[l][None, :]
        f1 = (row(ffn1_pre_g), row(ffn1_post_g), ffn1_w_gate[l].astype(BF16),
              ffn1_w_up[l].astype(BF16), ffn1_w_down[l].astype(BF16))

        w_in_t = w_in[l].T
        casts = (
            (w_in_t, (0, OFF_AB)), (w_in_t, (OFF_GATES, 2 * D_MODEL)),
            (w_ret_branch[l], None), (w_gdn_branch[l], None), (w_out[l], None),
            (ffn2_w_gate[l], None), (ffn2_w_up[l], None), (ffn2_w_down[l], None),
        )
        yp, (w_main, w_gates, w_rb, w_gb, w_o, g2, u2, d2) = _ffn(yp, *f1, tm=FFN_TM, casts=casts)
        w = {
            "mix_pre_g": row(mix_pre_g), "mix_post_g": row(mix_post_g),
            "w_main": w_main,
            "w_ab": _pad_lanes(w_in[l, :, OFF_AB:OFF_GATES], HD).astype(BF16),
            "w_gates": w_gates,
            "conv_w": gdn_conv_w[l],
            "a_log": _pad_lanes(row(gdn_a_log), HD), "dt_bias": _pad_lanes(row(gdn_dt_bias), HD),
            "ret_norm_g": row(ret_norm_g), "gdn_norm_g": row(gdn_norm_g),
            "w_rb": w_rb, "w_gb": w_gb, "w_out": w_o,
        }
        f2 = (row(ffn2_pre_g), row(ffn2_post_g), g2, u2, d2)

        yp = yp.reshape(b, t, D_MODEL)
        ret_f, gdn_f, sig_f, gb_f, c1 = _mixer_proj(yp, w)
        yp, r1, g1 = _mixer_rec(yp, ret_f, gdn_f, sig_f, gb_f, w)
        yp, _ = _ffn(yp.reshape(b * t, D_MODEL), *f2, tm=FFN_TM)

        ys, _ = _ffn(ys, *f1, tm=n_s)
        ys, r2, g2s, c2 = _mixer_sample(ys, state_ret[l], state_gdn[l],
                                        state_conv[l].reshape(n_s, (CONV_W - 1) * CONV_DIM), w,
                                        PAST_LEN)
        ys, _ = _ffn(ys, *f2, tm=n_s)
        c2 = c2.reshape(n_s, CONV_W - 1, CONV_DIM)
        for lst, val in zip(outs, (r1, g1, c1, r2, g2s, c2)):
            lst.append(val)
    stacked = [v[0][None] if depth == 1 else jnp.stack(v) for v in outs]
    return (yp.reshape(b, t, D_MODEL), ys.reshape(n_s, t_s, D_MODEL), *stacked)
```

```python
import functools

import numpy as np
import jax
import jax.numpy as jnp
from jax import lax
from jax.experimental import pallas as pl
from jax.experimental.pallas import tpu as pltpu

F32 = jnp.float32
BF16 = jnp.bfloat16

D_MODEL = 1024
D_FF = 2816
HEADS = 4
HD = 128
QK = HEADS * HD
CONV_W = 4
CONV_DIM = 3 * QK
CHUNK = 64
PACK = 2
ROPE_BASE = 10000.0
EPS = 1e-6
PAST_LEN = 16384

OFF_RQ, OFF_RK, OFF_RV, OFF_RG = 0, QK, 2 * QK, 3 * QK
OFF_CONV = 4 * QK
OFF_GZ = OFF_CONV + CONV_DIM
OFF_AB = OFF_GZ + QK
OFF_GATES = OFF_AB + 2 * HEADS
D_IN = OFF_GATES + 2 * D_MODEL

TM = 256
NB = 2
TB = 8
FFN_TM = 512
FFN_SUB = 256
CONV_PAD = 8
BF16_ROWS = 16
F32_ROWS = 8

VMEM_LIMIT = 56 * 1024 * 1024


def _silu(x):
    return x * jax.nn.sigmoid(x)


def _rms(x, g):
    return x * lax.rsqrt(jnp.mean(x * x, axis=-1, keepdims=True) + EPS) * g


def _dot(a, b):
    return jnp.dot(a.astype(BF16), b.astype(BF16), preferred_element_type=F32)


def _dot_nt(a, b):
    return lax.dot_general(a.astype(BF16), b.astype(BF16), (((1,), (1,)), ((), ())),
                           preferred_element_type=F32)


def _dot_tn(a, b):
    return lax.dot_general(a.astype(BF16), b.astype(BF16), (((0,), (0,)), ((), ())),
                           preferred_element_type=F32)


def _split3(a):
    hi = a.astype(BF16)
    r = a - hi.astype(F32)
    mid = r.astype(BF16)
    lo = (r - mid.astype(F32)).astype(BF16)
    return hi, mid, lo


def _resident(shape):
    nd = len(shape)
    return pl.BlockSpec(shape, lambda *_: (0,) * nd, pipeline_mode=pl.Buffered(1))


def _ffn_body(x_ref, pre_ref, post_ref, wg_ref, wu_ref, wd_ref, *rest, cast_transposed):
    n_cast = len(cast_transposed)
    cast_in, o_ref, cast_out = rest[:n_cast], rest[n_cast], rest[n_cast + 1:]
    tm = x_ref.shape[0]
    sub = min(tm, FFN_SUB)
    parts = [slice(r, r + sub) for r in range(0, tm, sub)]
    xs = [x_ref[p, :] for p in parts]
    hs = [_rms(x, pre_ref[...]).astype(BF16) for x in xs]
    acts = []
    for h in hs:
        g = jnp.dot(h, wg_ref[...], preferred_element_type=F32)
        u = jnp.dot(h, wu_ref[...], preferred_element_type=F32)
        acts.append((_silu(g) * u).astype(BF16))
    for p, x, a in zip(parts, xs, acts):
        y = jnp.dot(a, wd_ref[...], preferred_element_type=F32)
        o_ref[p, :] = x + 0.5 * _rms(y, post_ref[...])
    for src, dst, transposed in zip(cast_in, cast_out, cast_transposed):
        blk = src[...]
        dst[...] = (blk.T if transposed else blk).astype(BF16)


def _cast_row_block(rows, steps):
    rb = BF16_ROWS
    while rows % rb or rows // rb > steps:
        rb += BF16_ROWS
    return rb


def _ffn(x, pre_g, post_g, wg, wu, wd, tm, casts=()):
    n = x.shape[0]
    assert n % tm == 0
    steps = n // tm
    in_specs = [
        pl.BlockSpec((tm, D_MODEL), lambda i: (i, 0)),
        _resident((1, D_MODEL)), _resident((1, D_MODEL)),
        _resident((D_MODEL, D_FF)), _resident((D_MODEL, D_FF)), _resident((D_FF, D_MODEL)),
    ]
    out_specs = [pl.BlockSpec((tm, D_MODEL), lambda i: (i, 0))]
    out_shape = [jax.ShapeDtypeStruct((n, D_MODEL), F32)]
    for arr, region in casts:
        rows, width = arr.shape
        if region is None:
            rb = _cast_row_block(rows, steps)
            last = rows // rb - 1
            spec = pl.BlockSpec((rb, width), lambda i, last=last: (jnp.minimum(i, last), 0))
            in_specs.append(spec)
            out_specs.append(spec)
            out_shape.append(jax.ShapeDtypeStruct((rows, width), BF16))
        else:
            row0, n_rows = region
            assert n_rows % HD == 0 and n_rows // HD <= steps
            last = n_rows // HD - 1
            in_specs.append(pl.BlockSpec(
                (pl.Element(HD), pl.Element(width)),
                lambda i, row0=row0, last=last: (
                    pl.multiple_of(row0 + HD * jnp.minimum(i, last), F32_ROWS), 0)))
            out_specs.append(pl.BlockSpec((width, HD), lambda i, last=last: (0, jnp.minimum(i, last))))
            out_shape.append(jax.ShapeDtypeStruct((width, n_rows), BF16))
    outs = pl.pallas_call(
        functools.partial(_ffn_body, cast_transposed=tuple(c[1] is not None for c in casts)),
        grid=(steps,),
        in_specs=in_specs,
        out_specs=out_specs,
        out_shape=out_shape,
        compiler_params=pltpu.CompilerParams(dimension_semantics=("arbitrary",),
                                             vmem_limit_bytes=VMEM_LIMIT),
        name="ffn",
    )(x, pre_g, post_g, wg, wu, wd, *[c[0] for c in casts])
    return outs[0], list(outs[1:])


def _rope(x, cos, sin_signed):
    return x * cos + pltpu.roll(x, HD // 2, 1) * sin_signed


def _ret_out_norm(o, g_row, gate_act):
    mu = jnp.mean(o, axis=-1, keepdims=True)
    d = o - mu
    var = jnp.mean(d * d, axis=-1, keepdims=True)
    return gate_act * (d * lax.rsqrt(var + EPS) * g_row)


def _gdn_out_norm(o, g_row, gate_act):
    return o * lax.rsqrt(jnp.mean(o * o, axis=-1, keepdims=True) + EPS) * g_row * gate_act


def _l2norm(x, scale):
    return x * (lax.rsqrt(jnp.sum(x * x, axis=-1, keepdims=True) + EPS) * scale)


def _softplus(x):
    return jnp.maximum(x, 0.0) + jnp.log(1.0 + jnp.exp(-jnp.abs(x)))


def _merge(x, o_r, o_g, sig_gates, wrb_ref, wgb_ref, wout_ref, post_ref):
    y = (sig_gates[:, :D_MODEL] * _dot(o_r, wrb_ref[...])
         + sig_gates[:, D_MODEL:] * _dot(o_g, wgb_ref[...]))
    m = _dot(y, wout_ref[...])
    return x + _rms(m, post_ref[...])


def _unit_lower_inverse_many(a_list, masks):
    eye, m16, off32, off64 = masks
    ad = [a * m16 for a in a_list]
    x = [eye - v for v in ad]
    p = [_dot(v, v) for v in ad]
    for level in range(3):
        x = [xi + _dot(xi, pi) for xi, pi in zip(x, p)]
        if level < 2:
            p = [_dot(pi, pi) for pi in p]
    for m in (off32, off64):
        t = [_dot(xi, a * m) for xi, a in zip(x, a_list)]
        x = [xi - _dot(ti, xi) for xi, ti in zip(x, t)]
    return x


def _dot_packed(x, p, bd_mask):
    p16 = p.astype(BF16)
    rhs = jnp.concatenate([p16] * PACK, axis=0) * bd_mask
    return jnp.dot(x.astype(BF16), rhs, preferred_element_type=F32)


def _unit_lower_inverse_packed(a_list, masks, bd_mask):
    eye, m16, off32, off64 = masks
    dotp = functools.partial(_dot_packed, bd_mask=bd_mask)
    ad = [a * m16 for a in a_list]
    x = [eye - v for v in ad]
    p = [dotp(v, v) for v in ad]
    for level in range(3):
        x = [xi + dotp(xi, pi) for xi, pi in zip(x, p)]
        if level < 2:
            p = [dotp(pi, pi) for pi in p]
    for m in (off32, off64):
        t = [dotp(xi, a * m) for xi, a in zip(x, a_list)]
        x = [xi - dotp(ti, xi) for xi, ti in zip(x, t)]
    return x


def _inverse_masks(packed=1):
    r = lax.broadcasted_iota(jnp.int32, (CHUNK, packed * CHUNK), 0)
    c = lax.broadcasted_iota(jnp.int32, (CHUNK, packed * CHUNK), 1) & (CHUNK - 1)
    one, zero = jnp.float32(1.0), jnp.float32(0.0)
    eye = jnp.where(r == c, one, zero)
    same16 = (r >> 4) == (c >> 4)
    same32 = (r >> 5) == (c >> 5)
    m16 = jnp.where(same16, one, zero)
    m32 = jnp.where(same32, one, zero)
    return eye, m16, m32 - m16, 1.0 - m32


def _retention_tile(pm, cos, sin, rd_ref, ku_ref, retg_ref, s_ref, tile_decay):
    ri = lax.broadcasted_iota(jnp.int32, (TM, TM), 0)
    ci = lax.broadcasted_iota(jnp.int32, (TM, TM), 1)
    causal = ri >= ci
    heads = []
    for hh in range(HEADS):
        sl = slice(hh * HD, (hh + 1) * HD)
        rq = pm[:, OFF_RQ + hh * HD:OFF_RQ + (hh + 1) * HD]
        rk = pm[:, OFF_RK + hh * HD:OFF_RK + (hh + 1) * HD]
        v = pm[:, OFF_RV + hh * HD:OFF_RV + (hh + 1) * HD]
        rg = pm[:, OFF_RG + hh * HD:OFF_RG + (hh + 1) * HD]
        qs = _rope(rq, cos, sin) * rd_ref[hh]
        ku = _rope(rk, cos, sin) * ku_ref[hh]
        s = s_ref[hh]
        sc = jnp.where(causal, _dot_nt(qs, ku), 0.0)
        o = _dot(sc, v) + _dot(qs, s)
        s_ref[hh] = tile_decay[hh] * (s + _dot_tn(ku, v))
        heads.append(_ret_out_norm(o, retg_ref[:, sl], _silu(rg)))
    return jnp.concatenate(heads, axis=1)


def _short_conv_tile(u, buf, convw_ref, tail_ref):
    buf[CONV_PAD:CONV_PAD + TM, :] = u
    acc = buf[CONV_PAD - 3:CONV_PAD - 3 + TM, :] * convw_ref[0:1, :]
    for i in range(1, CONV_W):
        acc = acc + buf[CONV_PAD - 3 + i:CONV_PAD - 3 + i + TM, :] * convw_ref[i:i + 1, :]
    tail = buf[CONV_PAD + TM - 3:CONV_PAD + TM, :]
    tail_ref[...] = tail
    buf[CONV_PAD - 3:CONV_PAD, :] = tail
    return _silu(acc)


def _gdn_decay_beta(ab, alog_ref, dtb_ref):
    g_all = -jnp.exp(alog_ref[...]) * _softplus(ab + dtb_ref[...])
    beta_all = jax.nn.sigmoid(ab)
    return g_all, beta_all


def _gdn_tiles(cqs, abs_, gzs, ltri, alog_ref, dtb_ref, gdng_ref, s_refs):
    seqs = range(len(cqs))
    dd = functools.partial(jnp.dot, preferred_element_type=F32)
    gcum, gcum_t, beta_all = [], [], []
    for b in seqs:
        g_all, beta = _gdn_decay_beta(abs_[b], alog_ref, dtb_ref)
        beta_all.append(beta)
        g_hi, g_mid, g_lo = _split3(g_all)
        gcum.append(dd(ltri, g_hi) + (dd(ltri, g_mid) + dd(ltri, g_lo)))
        gcum_t.append(gcum[b].T)

    r64 = lax.broadcasted_iota(jnp.int32, (CHUNK, CHUNK), 0)
    c64 = lax.broadcasted_iota(jnp.int32, (CHUNK, CHUNK), 1)
    causal64 = r64 >= c64
    strict64 = r64 > c64
    masks = _inverse_masks()

    n_ch = TM // CHUNK
    rows = [slice(c * CHUNK, (c + 1) * CHUNK) for c in range(n_ch)]
    bh = [(b, hh) for b in seqs for hh in range(HEADS)]
    probs = [(b, hh, c) for b, hh in bh for c in range(n_ch)]
    gq = {(b, hh): _l2norm(cqs[b][:, hh * HD:(hh + 1) * HD], HD ** -0.5) for b, hh in bh}
    gk = {(b, hh): _l2norm(cqs[b][:, QK + hh * HD:QK + (hh + 1) * HD], 1.0) for b, hh in bh}
    gv = {(b, hh): cqs[b][:, 2 * QK + hh * HD:2 * QK + (hh + 1) * HD] for b, hh in bh}
    qc = {(b, hh, c): gq[b, hh][rows[c]] for b, hh, c in probs}
    kc = {(b, hh, c): gk[b, hh][rows[c]] for b, hh, c in probs}
    vc = {(b, hh, c): gv[b, hh][rows[c]] for b, hh, c in probs}
    gc = {(b, hh, c): gcum[b][rows[c], hh:hh + 1] for b, hh, c in probs}
    bcol = {(b, hh, c): beta_all[b][rows[c], HEADS + hh:HEADS + hh + 1] for b, hh, c in probs}
    dm = {(b, hh, c): jnp.where(
        causal64, jnp.exp(jnp.minimum(gc[b, hh, c] - gcum_t[b][hh:hh + 1, rows[c]], 0.0)), 0.0)
        for b, hh, c in probs}
    kk = {p: _dot_nt(kc[p], kc[p]) for p in probs}
    qk = {p: _dot_nt(qc[p], kc[p]) * dm[p] for p in probs}
    a_mats = [jnp.where(strict64, bcol[p] * dm[p] * kk[p], 0.0) for p in probs]
    tinv = dict(zip(probs, _unit_lower_inverse_many(a_mats, masks)))
    eg = {p: jnp.exp(gc[p]) for p in probs}
    uw = {p: _dot(tinv[p], jnp.concatenate([bcol[p] * vc[p], (bcol[p] * eg[p]) * kc[p]], axis=1))
          for p in probs}
    gl = {p: gc[p][CHUNK - 1:CHUNK, :] for p in probs}
    kdw = {p: _dot_tn(kc[p] * jnp.exp(gl[p] - gc[p]), uw[p]) for p in probs}
    qkw = {p: _dot(qk[p], uw[p]) for p in probs}
    lhs = {p: jnp.concatenate([kdw[p][:, HD:], eg[p] * qc[p] - qkw[p][:, HD:]], axis=0).astype(BF16)
           for p in probs}

    s = {(b, hh): s_refs[b][hh] for b, hh in bh}
    outs = {k: [] for k in bh}
    for c in range(n_ch):
        for b, hh in bh:
            p = (b, hh, c)
            ps = _dot(lhs[p], s[b, hh])
            outs[b, hh].append(ps[HD:, :] + qkw[p][:, :HD])
            s[b, hh] = jnp.exp(gl[p]) * s[b, hh] + (kdw[p][:, :HD] - ps[:HD, :])
    o_gs = []
    for b in seqs:
        heads = []
        for hh in range(HEADS):
            s_refs[b][hh] = s[b, hh]
            o = jnp.concatenate(outs[b, hh], axis=0)
            heads.append(_gdn_out_norm(o, gdng_ref[...], _silu(gzs[b][:, hh * HD:(hh + 1) * HD])))
        o_gs.append(jnp.concatenate(heads, axis=1))
    return o_gs


def _mixer_prompt_body(x_ref, cos_ref, sin_ref, rd_ref, ku_ref, ltri_ref,
                       pre_ref, post_ref, wmain_ref, wab_ref, wgates_ref, convw_ref, alog_ref, dtb_ref,
                       retg_ref, gdng_ref, wrb_ref, wgb_ref, wout_ref,
                       y_ref, sret_ref, sgdn_ref, conv_ref,
                       cbuf, *, ret_tile_decay):
    t = pl.program_id(1)
    nb = x_ref.shape[0]

    @pl.when(t == 0)
    def _():
        sret_ref[...] = jnp.zeros_like(sret_ref)
        sgdn_ref[...] = jnp.zeros_like(sgdn_ref)
        cbuf[:, 0:CONV_PAD, :] = jnp.zeros((nb, CONV_PAD, CONV_DIM), F32)

    cos = cos_ref[...]
    sin = sin_ref[...]
    ltri = ltri_ref[...]
    seqs = range(nb)
    xs = [x_ref[b] for b in seqs]
    hs = [_rms(xs[b], pre_ref[...]).astype(BF16) for b in seqs]
    pgs = [jnp.dot(hs[b], wmain_ref[:, OFF_CONV:OFF_AB], preferred_element_type=F32) for b in seqs]
    abs_ = [jnp.dot(hs[b], wab_ref[...], preferred_element_type=F32) for b in seqs]
    cqs = [_short_conv_tile(pgs[b][:, 0:CONV_DIM], cbuf.at[b], convw_ref, conv_ref.at[b]) for b in seqs]
    pms = [jnp.dot(hs[b], wmain_ref[:, 0:OFF_CONV], preferred_element_type=F32) for b in seqs]
    o_rs = [_retention_tile(pms[b], cos, sin, rd_ref, ku_ref, retg_ref, sret_ref.at[b], ret_tile_decay)
            for b in seqs]
    o_gs = _gdn_tiles(cqs, abs_, [pgs[b][:, CONV_DIM:] for b in seqs], ltri, alog_ref, dtb_ref, gdng_ref,
                      [sgdn_ref.at[b] for b in seqs])
    for b in seqs:
        gates = jnp.dot(hs[b], wgates_ref[...], preferred_element_type=F32)
        y_ref[b] = _merge(xs[b], o_rs[b], o_gs[b], jax.nn.sigmoid(gates), wrb_ref, wgb_ref, wout_ref, post_ref)


def _ret_gammas():
    return 1.0 - 2.0 ** (-5.0 - np.arange(HEADS, dtype=np.float64))


def _rope_tables(pos):
    inv = ROPE_BASE ** (-np.arange(0, HD, 2, dtype=np.float64) / HD)
    ang = np.asarray(pos, np.float64)[:, None] * inv[None, :]
    cos = np.concatenate([np.cos(ang), np.cos(ang)], axis=1)
    sin = np.concatenate([-np.sin(ang), np.sin(ang)], axis=1)
    return jnp.asarray(cos, F32), jnp.asarray(sin, F32)


def _mixer_weight_specs():
    return [
        _resident((1, D_MODEL)), _resident((1, D_MODEL)),
        _resident((D_MODEL, OFF_AB)), _resident((D_MODEL, HD)), _resident((D_MODEL, 2 * D_MODEL)),
        _resident((CONV_W, CONV_DIM)), _resident((1, HD)), _resident((1, HD)),
        _resident((1, QK)), _resident((1, HD)),
        _resident((QK, D_MODEL)), _resident((QK, D_MODEL)), _resident((D_MODEL, D_MODEL)),
    ]


def _mixer_weight_args(w):
    return (w["mix_pre_g"], w["mix_post_g"], w["w_main"], w["w_ab"], w["w_gates"], w["conv_w"],
            w["a_log"], w["dt_bias"], w["ret_norm_g"], w["gdn_norm_g"], w["w_rb"], w["w_gb"], w["w_out"])


def _mixer_prompt(x, w):
    b, t, _ = x.shape
    assert t % TM == 0 and b % NB == 0
    cos, sin = _rope_tables(np.arange(t))
    gam = _ret_gammas()
    i1 = np.arange(1, TM + 1, dtype=np.float64)
    rd = np.broadcast_to((gam[:, None] ** i1[None, :])[:, :, None], (HEADS, TM, HD))
    ku = np.broadcast_to((HD ** -0.5 * gam[:, None] ** (-i1[None, :]))[:, :, None], (HEADS, TM, HD))
    tile_decay = tuple(float(v) for v in gam ** TM)
    r = np.arange(TM)
    ltri = ((r[:, None] >= r[None, :]) & (r[:, None] // CHUNK == r[None, :] // CHUNK))

    body = functools.partial(_mixer_prompt_body, ret_tile_decay=tile_decay)
    state_spec = pl.BlockSpec((NB, HEADS, HD, HD), lambda i, j: (i, 0, 0, 0))
    return pl.pallas_call(
        body,
        grid=(b // NB, t // TM),
        in_specs=[
            pl.BlockSpec((NB, TM, D_MODEL), lambda i, j: (i, j, 0)),
            pl.BlockSpec((TM, HD), lambda i, j: (j, 0)),
            pl.BlockSpec((TM, HD), lambda i, j: (j, 0)),
            _resident((HEADS, TM, HD)), _resident((HEADS, TM, HD)), _resident((TM, TM)),
        ] + _mixer_weight_specs(),
        out_specs=[
            pl.BlockSpec((NB, TM, D_MODEL), lambda i, j: (i, j, 0)),
            state_spec, state_spec,
            pl.BlockSpec((NB, CONV_W - 1, CONV_DIM), lambda i, j: (i, 0, 0)),
        ],
        out_shape=[
            jax.ShapeDtypeStruct((b, t, D_MODEL), F32),
            jax.ShapeDtypeStruct((b, HEADS, HD, HD), F32),
            jax.ShapeDtypeStruct((b, HEADS, HD, HD), F32),
            jax.ShapeDtypeStruct((b, CONV_W - 1, CONV_DIM), F32),
        ],
        scratch_shapes=[pltpu.VMEM((NB, CONV_PAD + TM, CONV_DIM), F32)],
        compiler_params=pltpu.CompilerParams(dimension_semantics=("arbitrary", "arbitrary"),
                                             vmem_limit_bytes=VMEM_LIMIT),
        name="mixer_prompt",
    )(x, cos, sin, jnp.asarray(rd, F32), jnp.asarray(ku, F32), jnp.asarray(ltri, BF16),
      *_mixer_weight_args(w))


A_WIDTH = 4 * QK
PROJ_TM = 2 * TM


def _short_conv_tile(u, buf, convw_ref, tail_ref):
    buf[CONV_PAD:CONV_PAD + TM, :] = u
    acc = buf[CONV_PAD - 3:CONV_PAD - 3 + TM, :] * convw_ref[0:1, :]
    for i in range(1, CONV_W):
        acc = acc + buf[CONV_PAD - 3 + i:CONV_PAD - 3 + i + TM, :] * convw_ref[i:i + 1, :]
    tail = buf[CONV_PAD + TM - 3:CONV_PAD + TM, :]
    tail_ref[...] = tail
    buf[CONV_PAD - 3:CONV_PAD, :] = tail
    return _silu(acc)


def _gdn_decay_beta(ab, alog_ref, dtb_ref):
    g_all = -jnp.exp(alog_ref[...]) * _softplus(ab + dtb_ref[...])
    beta_all = jax.nn.sigmoid(ab)
    return g_all, beta_all


def _mixer_proj_body(x_ref, cos_ref, sin_ref, rd_ref, ku_ref, pre_ref, wmain_ref, wab_ref, wgates_ref,
                     convw_ref, alog_ref, dtb_ref,
                     ret_ref, gdn_ref, sig_ref, gb_ref, conv_ref,
                     cbuf):
    t = pl.program_id(1)

    @pl.when(t == 0)
    def _():
        cbuf[0:CONV_PAD, :] = jnp.zeros((CONV_PAD, CONV_DIM), F32)

    n_sub = x_ref.shape[1] // TM
    subs = [slice(k * TM, (k + 1) * TM) for k in range(n_sub)]
    hs = [_rms(x_ref[0, r, :], pre_ref[...]).astype(BF16) for r in subs]

    def dot_h(k, w):
        return jnp.dot(hs[k], w, preferred_element_type=F32)

    def gdn_features(k, pg, ab):
        r = subs[k]
        cq = _short_conv_tile(pg[:, 0:CONV_DIM], cbuf, convw_ref, conv_ref.at[0])
        for hh in range(HEADS):
            sl = slice(hh * HD, (hh + 1) * HD)
            gdn_ref[0, r, sl] = _l2norm(cq[:, sl], HD ** -0.5).astype(BF16)
            gdn_ref[0, r, QK + hh * HD:QK + (hh + 1) * HD] = _l2norm(
                cq[:, QK + hh * HD:QK + (hh + 1) * HD], 1.0).astype(BF16)
        gdn_ref[0, r, 2 * QK:3 * QK] = cq[:, 2 * QK:3 * QK].astype(BF16)
        gdn_ref[0, r, 3 * QK:4 * QK] = _silu(pg[:, CONV_DIM:]).astype(BF16)
        g_all, beta_all = _gdn_decay_beta(ab, alog_ref, dtb_ref)
        lane = lax.broadcasted_iota(jnp.int32, (TM, HD), 1)
        gb_ref[0, r, :] = jnp.where(lane < HEADS, g_all, beta_all)

    def ret_features(k, pm):
        r = subs[k]
        cos = cos_ref[r, :]
        sin = sin_ref[r, :]
        for hh in range(HEADS):
            sl = slice(hh * HD, (hh + 1) * HD)
            ret_ref[0, r, sl] = (_rope(pm[:, OFF_RQ + hh * HD:OFF_RQ + (hh + 1) * HD], cos, sin)
                                 * rd_ref[hh]).astype(BF16)
            ret_ref[0, r, QK + hh * HD:QK + (hh + 1) * HD] = (
                _rope(pm[:, OFF_RK + hh * HD:OFF_RK + (hh + 1) * HD], cos, sin) * ku_ref[hh]).astype(BF16)
        ret_ref[0, r, 2 * QK:3 * QK] = pm[:, OFF_RV:OFF_RV + QK].astype(BF16)
        ret_ref[0, r, 3 * QK:4 * QK] = _silu(pm[:, OFF_RG:OFF_RG + QK]).astype(BF16)

    def gate_features(k, gates):
        sig_ref[0, subs[k], :] = jax.nn.sigmoid(gates).astype(BF16)

    w_pg, w_pm = wmain_ref.at[:, OFF_CONV:OFF_AB], wmain_ref.at[:, 0:OFF_CONV]
    pg = dot_h(0, w_pg[...])
    ab = dot_h(0, wab_ref[...])
    for k in range(n_sub):
        gdn_features(k, pg, ab)
        if k + 1 < n_sub:
            pg = dot_h(k + 1, w_pg[...])
            ab = dot_h(k + 1, wab_ref[...])
        pm = dot_h(k, w_pm[...])
        if k > 0:
            gate_features(k - 1, gates)
        ret_features(k, pm)
        gates = dot_h(k, wgates_ref[...])
    gate_features(n_sub - 1, gates)


def _ret_gammas():
    return 1.0 - 2.0 ** (-5.0 - np.arange(HEADS, dtype=np.float64))


def _rope_tables(pos):
    inv = ROPE_BASE ** (-np.arange(0, HD, 2, dtype=np.float64) / HD)
    ang = np.asarray(pos, np.float64)[:, None] * inv[None, :]
    cos = np.concatenate([np.cos(ang), np.cos(ang)], axis=1)
    sin = np.concatenate([-np.sin(ang), np.sin(ang)], axis=1)
    return jnp.asarray(cos, F32), jnp.asarray(sin, F32)


def _mixer_proj(x, w):
    b, t, _ = x.shape
    assert t % PROJ_TM == 0
    cos, sin = _rope_tables(np.arange(t))
    gam = _ret_gammas()
    i1 = np.arange(1, TM + 1, dtype=np.float64)
    rd = np.broadcast_to((gam[:, None] ** i1[None, :])[:, :, None], (HEADS, TM, HD))
    ku = np.broadcast_to((HD ** -0.5 * gam[:, None] ** (-i1[None, :]))[:, :, None], (HEADS, TM, HD))
    tile = lambda width: pl.BlockSpec((1, PROJ_TM, width), lambda i, j: (i, j, 0))
    return pl.pallas_call(
        _mixer_proj_body,
        grid=(b, t // PROJ_TM),
        in_specs=[
            tile(D_MODEL),
            pl.BlockSpec((PROJ_TM, HD), lambda i, j: (j, 0)),
            pl.BlockSpec((PROJ_TM, HD), lambda i, j: (j, 0)),
            _resident((HEADS, TM, HD)), _resident((HEADS, TM, HD)),
            _resident((1, D_MODEL)),
            _resident((D_MODEL, OFF_AB)), _resident((D_MODEL, HD)), _resident((D_MODEL, 2 * D_MODEL)),
            _resident((CONV_W, CONV_DIM)), _resident((1, HD)), _resident((1, HD)),
        ],
        out_specs=[
            tile(A_WIDTH), tile(A_WIDTH), tile(2 * D_MODEL), tile(HD),
            pl.BlockSpec((1, CONV_W - 1, CONV_DIM), lambda i, j: (i, 0, 0)),
        ],
        out_shape=[
            jax.ShapeDtypeStruct((b, t, A_WIDTH), BF16),
            jax.ShapeDtypeStruct((b, t, A_WIDTH), BF16),
            jax.ShapeDtypeStruct((b, t, 2 * D_MODEL), BF16),
            jax.ShapeDtypeStruct((b, t, HD), F32),
            jax.ShapeDtypeStruct((b, CONV_W - 1, CONV_DIM), F32),
        ],
        scratch_shapes=[pltpu.VMEM((CONV_PAD + TM, CONV_DIM), F32)],
        compiler_params=pltpu.CompilerParams(dimension_semantics=("arbitrary", "arbitrary"),
                                             vmem_limit_bytes=VMEM_LIMIT),
        name="mixer_proj",
    )(x, cos, sin, jnp.asarray(rd, F32), jnp.asarray(ku, F32),
      w["mix_pre_g"], w["w_main"], w["w_ab"], w["w_gates"], w["conv_w"], w["a_log"], w["dt_bias"])


def _retention_feat_tile(feat, retg_ref, s_ref, tile_decay):
    ri = lax.broadcasted_iota(jnp.int32, (TM, TM), 0)
    ci = lax.broadcasted_iota(jnp.int32, (TM, TM), 1)
    causal = ri >= ci
    heads = []
    for hh in range(HEADS):
        sl = slice(hh * HD, (hh + 1) * HD)
        qs = feat[:, sl]
        ku = feat[:, QK + hh * HD:QK + (hh + 1) * HD]
        v = feat[:, 2 * QK + hh * HD:2 * QK + (hh + 1) * HD]
        gate_act = feat[:, 3 * QK + hh * HD:3 * QK + (hh + 1) * HD]
        s = s_ref[hh]
        sc = jnp.where(causal, _dot_nt(qs, ku), 0.0)
        o = _dot(sc, v) + _dot(qs, s)
        s_ref[hh] = tile_decay[hh] * (s + _dot_tn(ku, v))
        heads.append(_ret_out_norm(o, retg_ref[:, sl], gate_act))
    return jnp.concatenate(heads, axis=1)


def _gdn_feat_tiles(feats, gbs, ltri, bd_mask, gdng_ref, s_refs):
    seqs = range(len(feats))
    dd = functools.partial(jnp.dot, preferred_element_type=F32)
    gcum, gcum_t = [], []
    for b in seqs:
        g_hi, g_mid, g_lo = _split3(gbs[b])
        gcum.append(dd(ltri, g_hi) + (dd(ltri, g_mid) + dd(ltri, g_lo)))
        gcum_t.append(gcum[b].T)

    r64 = lax.broadcasted_iota(jnp.int32, (CHUNK, CHUNK), 0)
    c64 = lax.broadcasted_iota(jnp.int32, (CHUNK, CHUNK), 1)
    causal64 = r64 >= c64
    strict64 = r64 > c64
    masks = _inverse_masks()

    n_ch = TM // CHUNK
    rows = [slice(c * CHUNK, (c + 1) * CHUNK) for c in range(n_ch)]
    bh = [(b, hh) for b in seqs for hh in range(HEADS)]
    probs = [(b, hh, c) for b, hh in bh for c in range(n_ch)]
    qc = {(b, hh, c): feats[b][rows[c], hh * HD:(hh + 1) * HD] for b, hh, c in probs}
    kc = {(b, hh, c): feats[b][rows[c], QK + hh * HD:QK + (hh + 1) * HD] for b, hh, c in probs}
    vc = {(b, hh, c): feats[b][rows[c], 2 * QK + hh * HD:2 * QK + (hh + 1) * HD] for b, hh, c in probs}
    gc = {(b, hh, c): gcum[b][rows[c], hh:hh + 1] for b, hh, c in probs}
    bcol = {(b, hh, c): gbs[b][rows[c], HEADS + hh:HEADS + hh + 1] for b, hh, c in probs}
    dm = {(b, hh, c): jnp.where(
        causal64, jnp.exp(jnp.minimum(gc[b, hh, c] - gcum_t[b][hh:hh + 1, rows[c]], 0.0)), 0.0)
        for b, hh, c in probs}
    kk = {p: _dot_nt(kc[p], kc[p]) for p in probs}
    qk = {p: _dot_nt(qc[p], kc[p]) * dm[p] for p in probs}
    a_mats = [jnp.where(strict64, bcol[p] * dm[p] * kk[p], 0.0) for p in probs]
    tinv = dict(zip(probs, _unit_lower_inverse_many(a_mats, masks)))
    eg = {p: jnp.exp(gc[p]) for p in probs}
    uw = {p: _dot(tinv[p], jnp.concatenate([bcol[p] * vc[p], (bcol[p] * eg[p]) * kc[p]], axis=1))
          for p in probs}
    gl = {p: gc[p][CHUNK - 1:CHUNK, :] for p in probs}
    kdw = {p: _dot_tn(kc[p] * jnp.exp(gl[p] - gc[p]), uw[p]) for p in probs}
    qkw = {p: _dot(qk[p], uw[p]) for p in probs}
    lhs = {p: jnp.concatenate([kdw[p][:, HD:], eg[p] * qc[p] - qkw[p][:, HD:]], axis=0).astype(BF16)
           for p in probs}

    s = {(b, hh): s_refs[b][hh] for b, hh in bh}
    outs = {k: [] for k in bh}
    for c in range(n_ch):
        for b, hh in bh:
            p = (b, hh, c)
            ps = _dot(lhs[p], s[b, hh])
            outs[b, hh].append(ps[HD:, :] + qkw[p][:, :HD])
            s[b, hh] = jnp.exp(gl[p]) * s[b, hh] + (kdw[p][:, :HD] - ps[:HD, :])
    o_gs = []
    for b in seqs:
        heads = []
        for hh in range(HEADS):
            s_refs[b][hh] = s[b, hh]
            o = jnp.concatenate(outs[b, hh], axis=0)
            heads.append(_gdn_out_norm(o, gdng_ref[...], feats[b][:, 3 * QK + hh * HD:3 * QK + (hh + 1) * HD]))
        o_gs.append(jnp.concatenate(heads, axis=1))
    return o_gs


def _mixer_rec_body(x_ref, ret_ref, gdn_ref, sig_ref, gb_ref, ltri_ref, bd_ref,
                    post_ref, retg_ref, gdng_ref, wrb_ref, wgb_ref, wout_ref,
                    y_ref, sret_ref, sgdn_ref, *, ret_tile_decay):
    t = pl.program_id(1)
    nb = x_ref.shape[0]

    @pl.when(t == 0)
    def _():
        sret_ref[...] = jnp.zeros_like(sret_ref)
        sgdn_ref[...] = jnp.zeros_like(sgdn_ref)

    seqs = range(nb)
    o_rs = [_retention_feat_tile(ret_ref[b], retg_ref, sret_ref.at[b], ret_tile_decay) for b in seqs]
    o_gs = _gdn_feat_tiles([gdn_ref[b] for b in seqs], [gb_ref[b] for b in seqs], ltri_ref[...], bd_ref[...],
                      gdng_ref, [sgdn_ref.at[b] for b in seqs])
    for b in seqs:
        y_ref[b] = _merge(x_ref[b], o_rs[b], o_gs[b], sig_ref[b], wrb_ref, wgb_ref, wout_ref, post_ref)


def _mixer_rec(x, ret_f, gdn_f, sig_f, gb_f, w):
    b, t, _ = x.shape
    assert t % TM == 0 and b % NB == 0
    tile_decay = tuple(float(v) for v in _ret_gammas() ** TM)
    r = np.arange(TM)
    ltri = ((r[:, None] >= r[None, :]) & (r[:, None] // CHUNK == r[None, :] // CHUNK))
    rp = np.arange(PACK * CHUNK)
    bd_mask = rp[:, None] // CHUNK == rp[None, :] // CHUNK
    tile = lambda width: pl.BlockSpec((NB, TM, width), lambda i, j: (i, j, 0))
    state_spec = pl.BlockSpec((NB, HEADS, HD, HD), lambda i, j: (i, 0, 0, 0))
    return pl.pallas_call(
        functools.partial(_mixer_rec_body, ret_tile_decay=tile_decay),
        grid=(b // NB, t // TM),
        in_specs=[
            tile(D_MODEL), tile(A_WIDTH), tile(A_WIDTH), tile(2 * D_MODEL), tile(HD),
            _resident((TM, TM)), _resident((PACK * CHUNK, PACK * CHUNK)),
            _resident((1, D_MODEL)), _resident((1, QK)), _resident((1, HD)),
            _resident((QK, D_MODEL)), _resident((QK, D_MODEL)), _resident((D_MODEL, D_MODEL)),
        ],
        out_specs=[tile(D_MODEL), state_spec, state_spec],
        out_shape=[
            jax.ShapeDtypeStruct((b, t, D_MODEL), F32),
            jax.ShapeDtypeStruct((b, HEADS, HD, HD), F32),
            jax.ShapeDtypeStruct((b, HEADS, HD, HD), F32),
        ],
        compiler_params=pltpu.CompilerParams(dimension_semantics=("arbitrary", "arbitrary"),
                                             vmem_limit_bytes=VMEM_LIMIT),
        name="mixer_rec",
    )(x, ret_f, gdn_f, sig_f, gb_f, jnp.asarray(ltri, BF16), jnp.asarray(bd_mask, BF16),
      w["mix_post_g"], w["ret_norm_g"], w["gdn_norm_g"], w["w_rb"], w["w_gb"], w["w_out"])


def _mixer_weight_specs():
    return [
        _resident((1, D_MODEL)), _resident((1, D_MODEL)),
        _resident((D_MODEL, OFF_AB)), _resident((D_MODEL, HD)), _resident((D_MODEL, 2 * D_MODEL)),
        _resident((CONV_W, CONV_DIM)), _resident((1, HD)), _resident((1, HD)),
        _resident((1, QK)), _resident((1, HD)),
        _resident((QK, D_MODEL)), _resident((QK, D_MODEL)), _resident((D_MODEL, D_MODEL)),
    ]


def _mixer_weight_args(w):
    return (w["mix_pre_g"], w["mix_post_g"], w["w_main"], w["w_ab"], w["w_gates"], w["conv_w"],
            w["a_log"], w["dt_bias"], w["ret_norm_g"], w["gdn_norm_g"], w["w_rb"], w["w_gb"], w["w_out"])


def _pick_rows(rows):
    ri = lax.broadcasted_iota(jnp.int32, rows[0].shape, 0)
    out = rows[0]
    for j in range(1, len(rows)):
        out = jnp.where(ri == j, rows[j], out)
    return out


def _mixer_sample_body(x_ref, cos_ref, sin_ref,
                       pre_ref, post_ref, wmain_ref, wab_ref, wgates_ref, convw_ref, alog_ref, dtb_ref,
                       retg_ref, gdng_ref, wrb_ref, wgb_ref, wout_ref,
                       sret_in, sgdn_in, conv_in,
                       y_ref, sret_out, sgdn_out, conv_out,
                       pm_s, gates_s, rq_s, rk_s, gq_s, gk_s, gv_s, eg_s, beta_s, or_s, og_s, *, ret_gamma):
    i = pl.program_id(0)

    @pl.when(i == 0)
    def _():
        h = _rms(x_ref[...], pre_ref[...]).astype(BF16)
        pm = jnp.dot(h, wmain_ref[...], preferred_element_type=F32)
        ab = jnp.dot(h, wab_ref[...], preferred_element_type=F32)
        pm_s[...] = pm
        gates_s[...] = jnp.dot(h, wgates_ref[...], preferred_element_type=F32)
        cos = cos_ref[...]
        sin = sin_ref[...]
        cin = pm[:, OFF_CONV:OFF_CONV + CONV_DIM]
        acc = cin * convw_ref[CONV_W - 1:CONV_W, :]
        for r in range(CONV_W - 1):
            acc = acc + conv_in[:, r * CONV_DIM:(r + 1) * CONV_DIM] * convw_ref[r:r + 1, :]
        conv_out[:, 0:2 * CONV_DIM] = conv_in[:, CONV_DIM:3 * CONV_DIM]
        conv_out[:, 2 * CONV_DIM:3 * CONV_DIM] = cin
        cq = _silu(acc)
        g_all, beta_all = _gdn_decay_beta(ab, alog_ref, dtb_ref)
        eg_s[...] = jnp.exp(g_all)
        beta_s[...] = beta_all
        for hh in range(HEADS):
            sl = slice(hh * HD, (hh + 1) * HD)
            rq_s[:, sl] = _rope(pm[:, OFF_RQ + hh * HD:OFF_RQ + (hh + 1) * HD], cos, sin)
            rk_s[:, sl] = _rope(pm[:, OFF_RK + hh * HD:OFF_RK + (hh + 1) * HD], cos, sin) * (HD ** -0.5)
            gq_s[:, sl] = _l2norm(cq[:, hh * HD:(hh + 1) * HD], HD ** -0.5)
            gk_s[:, sl] = _l2norm(cq[:, QK + hh * HD:QK + (hh + 1) * HD], 1.0)
        gv_s[...] = cq[:, 2 * QK:3 * QK]

    rows = pl.ds(pl.multiple_of(i * TB, TB), TB)
    eg_all = eg_s[rows, :]
    beta_all = beta_s[rows, :]
    for hh in range(HEADS):
        sl = slice(hh * HD, (hh + 1) * HD)
        q = rq_s[rows, sl]
        k_t = rk_s[rows, sl].T
        v = pm_s[rows, OFF_RV + hh * HD:OFF_RV + (hh + 1) * HD]
        o_rows = []
        for j in range(TB):
            s = ret_gamma[hh] * sret_in[j, hh] + k_t[:, j:j + 1] * v[j:j + 1, :]
            sret_out[j, hh] = s
            o_rows.append(_dot(q, s))
        or_s[rows, sl] = _pick_rows(o_rows)

        gq = gq_s[rows, sl]
        gk = gk_s[rows, sl]
        gk_t = gk.T
        gv = gv_s[rows, sl]
        o_rows = []
        for j in range(TB):
            s = sgdn_in[j, hh]
            eg = eg_all[j:j + 1, hh:hh + 1]
            beta = beta_all[j:j + 1, HEADS + hh:HEADS + hh + 1]
            ks = _dot(gk, s)[j:j + 1, :]
            u = beta * gv[j:j + 1, :] - (beta * eg) * ks
            s = eg * s + gk_t[:, j:j + 1] * u
            sgdn_out[j, hh] = s
            o_rows.append(_dot(gq, s))
        og_s[rows, sl] = _pick_rows(o_rows)

    @pl.when(i == pl.num_programs(0) - 1)
    def _():
        o_r = []
        o_g = []
        for hh in range(HEADS):
            sl = slice(hh * HD, (hh + 1) * HD)
            o_r.append(_ret_out_norm(or_s[:, sl], retg_ref[:, sl],
                                     _silu(pm_s[:, OFF_RG + hh * HD:OFF_RG + (hh + 1) * HD])))
            o_g.append(_gdn_out_norm(og_s[:, sl], gdng_ref[...],
                                     _silu(pm_s[:, OFF_GZ + hh * HD:OFF_GZ + (hh + 1) * HD])))
        y_ref[...] = _merge(x_ref[...], jnp.concatenate(o_r, axis=1), jnp.concatenate(o_g, axis=1),
                            jax.nn.sigmoid(gates_s[...]), wrb_ref, wgb_ref, wout_ref, post_ref)


def _mixer_sample(x, s_ret, s_gdn, s_conv, w, pos):
    n = x.shape[0]
    assert n % TB == 0
    cos, sin = _rope_tables([pos])
    body = functools.partial(_mixer_sample_body, ret_gamma=tuple(float(v) for v in _ret_gammas()))
    state_spec = pl.BlockSpec((TB, HEADS, HD, HD), lambda i: (i, 0, 0, 0))
    conv_cols = (CONV_W - 1) * CONV_DIM
    return pl.pallas_call(
        body,
        grid=(n // TB,),
        in_specs=[_resident((n, D_MODEL)), _resident((1, HD)), _resident((1, HD))]
        + _mixer_weight_specs()
        + [state_spec, state_spec, _resident((n, conv_cols))],
        out_specs=[pl.BlockSpec((n, D_MODEL), lambda i: (0, 0)), state_spec, state_spec,
                   pl.BlockSpec((n, conv_cols), lambda i: (0, 0))],
        out_shape=[
            jax.ShapeDtypeStruct((n, D_MODEL), F32),
            jax.ShapeDtypeStruct((n, HEADS, HD, HD), F32),
            jax.ShapeDtypeStruct((n, HEADS, HD, HD), F32),
            jax.ShapeDtypeStruct((n, conv_cols), F32),
        ],
        scratch_shapes=[pltpu.VMEM((n, OFF_AB), F32), pltpu.VMEM((n, 2 * D_MODEL), F32)]
        + [pltpu.VMEM((n, QK), F32) for _ in range(5)]
        + [pltpu.VMEM((n, HD), F32), pltpu.VMEM((n, HD), F32)]
        + [pltpu.VMEM((n, QK), F32), pltpu.VMEM((n, QK), F32)],
        compiler_params=pltpu.CompilerParams(dimension_semantics=("arbitrary",),
                                             vmem_limit_bytes=VMEM_LIMIT),
        name="mixer_sample",
    )(x, cos, sin, *_mixer_weight_args(w), s_ret, s_gdn, s_conv)


def _pad_lanes(v, n):
    return jnp.pad(v, ((0, 0), (0, n - v.shape[1])))


def kernel(x_prompt, x_sample, state_ret, state_gdn, state_conv, ffn1_pre_g, ffn1_post_g, ffn1_w_gate,
           ffn1_w_up, ffn1_w_down, mix_pre_g, mix_post_g, w_in, ret_norm_g, gdn_conv_w, gdn_a_log,
           gdn_dt_bias, gdn_norm_g, w_ret_branch, w_gdn_branch, w_out, ffn2_pre_g, ffn2_post_g,
           ffn2_w_gate, ffn2_w_up, ffn2_w_down):
    depth = w_in.shape[0]
    b, t, _ = x_prompt.shape
    n_s, t_s, _ = x_sample.shape
    assert t_s == 1
    yp = x_prompt.reshape(b * t, D_MODEL)
    ys = x_sample.reshape(n_s, D_MODEL)
    outs = [[] for _ in range(6)]
    for l in range(depth):
        row = lambda a: a[l][None, :]
        f1 = (row(ffn1_pre_g), row(ffn1_post_g), ffn1_w_gate[l].astype(BF16),
              ffn1_w_up[l].astype(BF16), ffn1_w_down[l].astype(BF16))

        w_in_t = w_in[l].T
        casts = (
            (w_in_t, (0, OFF_AB)), (w_in_t, (OFF_GATES, 2 * D_MODEL)),
            (w_ret_branch[l], None), (w_gdn_branch[l], None), (w_out[l], None),
            (ffn2_w_gate[l], None), (ffn2_w_up[l], None), (ffn2_w_down[l], None),
        )
        yp, (w_main, w_gates, w_rb, w_gb, w_o, g2, u2, d2) = _ffn(yp, *f1, tm=FFN_TM, casts=casts)
        w = {
            "mix_pre_g": row(mix_pre_g), "mix_post_g": row(mix_post_g),
            "w_main": w_main,
            "w_ab": _pad_lanes(w_in[l, :, OFF_AB:OFF_GATES], HD).astype(BF16),
            "w_gates": w_gates,
            "conv_w": gdn_conv_w[l],
            "a_log": _pad_lanes(row(gdn_a_log), HD), "dt_bias": _pad_lanes(row(gdn_dt_bias), HD),
            "ret_norm_g": row(ret_norm_g), "gdn_norm_g": row(gdn_norm_g),
            "w_rb": w_rb, "w_gb": w_gb, "w_out": w_o,
        }
        f2 = (row(ffn2_pre_g), row(ffn2_post_g), g2, u2, d2)

        yp, r1, g1, c1 = _mixer_prompt(yp.reshape(b, t, D_MODEL), w)
        yp, _ = _ffn(yp.reshape(b * t, D_MODEL), *f2, tm=FFN_TM)

        ys, _ = _ffn(ys, *f1, tm=n_s)
        ys, r2, g2s, c2 = _mixer_sample(ys, state_ret[l], state_gdn[l],
                                        state_conv[l].reshape(n_s, (CONV_W - 1) * CONV_DIM), w,
                                        PAST_LEN)
        ys, _ = _ffn(ys, *f2, tm=n_s)
        c2 = c2.reshape(n_s, CONV_W - 1, CONV_DIM)
        for lst, val in zip(outs, (r1, g1, c1, r2, g2s, c2)):
            lst.append(val)
    stacked = [v[0][None] if depth == 1 else jnp.stack(v) for v in outs]
    return (yp.reshape(b, t, D_MODEL), ys.reshape(n_s, t_s, D_MODEL), *stacked)
```

```python
import functools

import numpy as np
import jax
import jax.numpy as jnp
from jax import lax
from jax.experimental import pallas as pl
from jax.experimental.pallas import tpu as pltpu

F32 = jnp.float32
BF16 = jnp.bfloat16

D_MODEL = 1024
D_FF = 2816
HEADS = 4
HD = 128
QK = HEADS * HD
CONV_W = 4
CONV_DIM = 3 * QK
CHUNK = 64
ROPE_BASE = 10000.0
EPS = 1e-6
PAST_LEN = 16384

OFF_RQ, OFF_RK, OFF_RV, OFF_RG = 0, QK, 2 * QK, 3 * QK
OFF_CONV = 4 * QK
OFF_GZ = OFF_CONV + CONV_DIM
OFF_AB = OFF_GZ + QK
OFF_GATES = OFF_AB + 2 * HEADS
D_IN = OFF_GATES + 2 * D_MODEL

TM = 256
NB = 2
TB = 8
FFN_TM = 1024
FFN_SUB = 256
CONV_PAD = 8
BF16_ROWS = 16
F32_ROWS = 8

VMEM_LIMIT = 56 * 1024 * 1024


def _silu(x):
    return x * jax.nn.sigmoid(x)


def _rms(x, g):
    return x * lax.rsqrt(jnp.mean(x * x, axis=-1, keepdims=True) + EPS) * g


def _dot(a, b):
    return jnp.dot(a.astype(BF16), b.astype(BF16), preferred_element_type=F32)


def _dot_nt(a, b):
    return lax.dot_general(a.astype(BF16), b.astype(BF16), (((1,), (1,)), ((), ())),
                           preferred_element_type=F32)


def _dot_tn(a, b):
    return lax.dot_general(a.astype(BF16), b.astype(BF16), (((0,), (0,)), ((), ())),
                           preferred_element_type=F32)


def _split3(a):
    hi = a.astype(BF16)
    r = a - hi.astype(F32)
    mid = r.astype(BF16)
    lo = (r - mid.astype(F32)).astype(BF16)
    return hi, mid, lo


def _resident(shape):
    nd = len(shape)
    return pl.BlockSpec(shape, lambda *_: (0,) * nd, pipeline_mode=pl.Buffered(1))


def _ffn_body(x_ref, pre_ref, post_ref, wg_ref, wu_ref, wd_ref, *rest, cast_transposed):
    n_cast = len(cast_transposed)
    cast_in, o_ref, cast_out = rest[:n_cast], rest[n_cast], rest[n_cast + 1:]
    tm = x_ref.shape[0]
    sub = min(tm, FFN_SUB)
    parts = [slice(r, r + sub) for r in range(0, tm, sub)]
    xs = [x_ref[p, :] for p in parts]
    hs = [_rms(x, pre_ref[...]).astype(BF16) for x in xs]
    acts = []
    for h in hs:
        g = jnp.dot(h, wg_ref[...], preferred_element_type=F32)
        u = jnp.dot(h, wu_ref[...], preferred_element_type=F32)
        acts.append((_silu(g) * u).astype(BF16))
    for p, x, a in zip(parts, xs, acts):
        y = jnp.dot(a, wd_ref[...], preferred_element_type=F32)
        o_ref[p, :] = x + 0.5 * _rms(y, post_ref[...])
    for src, dst, transposed in zip(cast_in, cast_out, cast_transposed):
        blk = src[...]
        dst[...] = (blk.T if transposed else blk).astype(BF16)


def _cast_row_block(rows, steps):
    rb = BF16_ROWS
    while rows % rb or rows // rb > steps:
        rb += BF16_ROWS
    return rb


def _ffn(x, pre_g, post_g, wg, wu, wd, tm, casts=()):
    n = x.shape[0]
    assert n % tm == 0
    steps = n // tm
    in_specs = [
        pl.BlockSpec((tm, D_MODEL), lambda i: (i, 0)),
        _resident((1, D_MODEL)), _resident((1, D_MODEL)),
        _resident((D_MODEL, D_FF)), _resident((D_MODEL, D_FF)), _resident((D_FF, D_MODEL)),
    ]
    out_specs = [pl.BlockSpec((tm, D_MODEL), lambda i: (i, 0))]
    out_shape = [jax.ShapeDtypeStruct((n, D_MODEL), F32)]
    for arr, region in casts:
        rows, width = arr.shape
        if region is None:
            rb = _cast_row_block(rows, steps)
            last = rows // rb - 1
            spec = pl.BlockSpec((rb, width), lambda i, last=last: (jnp.minimum(i, last), 0))
            in_specs.append(spec)
            out_specs.append(spec)
            out_shape.append(jax.ShapeDtypeStruct((rows, width), BF16))
        else:
            row0, n_rows = region
            cb = HD
            while n_rows % cb or n_rows // cb > steps:
                cb += HD
            last = n_rows // cb - 1
            in_specs.append(pl.BlockSpec(
                (pl.Element(cb), pl.Element(width)),
                lambda i, row0=row0, last=last, cb=cb: (
                    pl.multiple_of(row0 + cb * jnp.minimum(i, last), F32_ROWS), 0)))
            out_specs.append(pl.BlockSpec((width, cb), lambda i, last=last: (0, jnp.minimum(i, last))))
            out_shape.append(jax.ShapeDtypeStruct((width, n_rows), BF16))
    outs = pl.pallas_call(
        functools.partial(_ffn_body, cast_transposed=tuple(c[1] is not None for c in casts)),
        grid=(steps,),
        in_specs=in_specs,
        out_specs=out_specs,
        out_shape=out_shape,
        compiler_params=pltpu.CompilerParams(dimension_semantics=("arbitrary",),
                                             vmem_limit_bytes=VMEM_LIMIT),
        name="ffn",
    )(x, pre_g, post_g, wg, wu, wd, *[c[0] for c in casts])
    return outs[0], list(outs[1:])


def _rope(x, cos, sin_signed):
    return x * cos + pltpu.roll(x, HD // 2, 1) * sin_signed


def _ret_out_norm(o, g_row, gate_act):
    mu = jnp.mean(o, axis=-1, keepdims=True)
    d = o - mu
    var = jnp.mean(d * d, axis=-1, keepdims=True)
    return gate_act * (d * lax.rsqrt(var + EPS) * g_row)


def _gdn_out_norm(o, g_row, gate_act):
    return o * lax.rsqrt(jnp.mean(o * o, axis=-1, keepdims=True) + EPS) * g_row * gate_act


def _l2norm(x, scale):
    return x * (lax.rsqrt(jnp.sum(x * x, axis=-1, keepdims=True) + EPS) * scale)


def _softplus(x):
    return jnp.maximum(x, 0.0) + jnp.log(1.0 + jnp.exp(-jnp.abs(x)))


def _merge(x, o_r, o_g, sig_gates, wrb_ref, wgb_ref, wout_ref, post_ref):
    y = (sig_gates[:, :D_MODEL] * _dot(o_r, wrb_ref[...])
         + sig_gates[:, D_MODEL:] * _dot(o_g, wgb_ref[...]))
    m = _dot(y, wout_ref[...])
    return x + _rms(m, post_ref[...])


def _unit_lower_inverse_many(a_list, masks):
    eye, m16, off32, off64 = masks
    ad = [a * m16 for a in a_list]
    x = [eye - v for v in ad]
    p = [_dot(v, v) for v in ad]
    for level in range(3):
        x = [xi + _dot(xi, pi) for xi, pi in zip(x, p)]
        if level < 2:
            p = [_dot(pi, pi) for pi in p]
    for m in (off32, off64):
        t = [_dot(xi, a * m) for xi, a in zip(x, a_list)]
        x = [xi - _dot(ti, xi) for xi, ti in zip(x, t)]
    return x


def _inverse_masks():
    r = lax.broadcasted_iota(jnp.int32, (CHUNK, CHUNK), 0)
    c = lax.broadcasted_iota(jnp.int32, (CHUNK, CHUNK), 1)
    one, zero = jnp.float32(1.0), jnp.float32(0.0)
    eye = jnp.where(r == c, one, zero)
    same16 = (r >> 4) == (c >> 4)
    same32 = (r >> 5) == (c >> 5)
    m16 = jnp.where(same16, one, zero)
    m32 = jnp.where(same32, one, zero)
    return eye, m16, m32 - m16, 1.0 - m32


def _retention_tile(pm, cos, sin, rd_ref, ku_ref, retg_ref, s_ref, tile_decay):
    ri = lax.broadcasted_iota(jnp.int32, (TM, TM), 0)
    ci = lax.broadcasted_iota(jnp.int32, (TM, TM), 1)
    causal = ri >= ci
    heads = []
    for hh in range(HEADS):
        sl = slice(hh * HD, (hh + 1) * HD)
        rq = pm[:, OFF_RQ + hh * HD:OFF_RQ + (hh + 1) * HD]
        rk = pm[:, OFF_RK + hh * HD:OFF_RK + (hh + 1) * HD]
        v = pm[:, OFF_RV + hh * HD:OFF_RV + (hh + 1) * HD]
        rg = pm[:, OFF_RG + hh * HD:OFF_RG + (hh + 1) * HD]
        qs = _rope(rq, cos, sin) * rd_ref[hh]
        ku = _rope(rk, cos, sin) * ku_ref[hh]
        s = s_ref[hh]
        sc = jnp.where(causal, _dot_nt(qs, ku), 0.0)
        o = _dot(sc, v) + _dot(qs, s)
        s_ref[hh] = tile_decay[hh] * (s + _dot_tn(ku, v))
        heads.append(_ret_out_norm(o, retg_ref[:, sl], _silu(rg)))
    return jnp.concatenate(heads, axis=1)


def _short_conv_tile(u, buf, convw_ref, tail_ref):
    buf[CONV_PAD:CONV_PAD + TM, :] = u
    acc = buf[CONV_PAD - 3:CONV_PAD - 3 + TM, :] * convw_ref[0:1, :]
    for i in range(1, CONV_W):
        acc = acc + buf[CONV_PAD - 3 + i:CONV_PAD - 3 + i + TM, :] * convw_ref[i:i + 1, :]
    tail = buf[CONV_PAD + TM - 3:CONV_PAD + TM, :]
    tail_ref[...] = tail
    buf[CONV_PAD - 3:CONV_PAD, :] = tail
    return _silu(acc)


def _gdn_decay_beta(ab, alog_ref, dtb_ref):
    g_all = -jnp.exp(alog_ref[...]) * _softplus(ab + dtb_ref[...])
    beta_all = jax.nn.sigmoid(ab)
    return g_all, beta_all


def _gdn_tiles(cqs, abs_, gzs, ltri, alog_ref, dtb_ref, gdng_ref, s_refs):
    seqs = range(len(cqs))
    dd = functools.partial(jnp.dot, preferred_element_type=F32)
    gcum, gcum_t, beta_all = [], [], []
    for b in seqs:
        g_all, beta = _gdn_decay_beta(abs_[b], alog_ref, dtb_ref)
        beta_all.append(beta)
        g_hi, g_mid, g_lo = _split3(g_all)
        gcum.append(dd(ltri, g_hi) + (dd(ltri, g_mid) + dd(ltri, g_lo)))
        gcum_t.append(gcum[b].T)

    r64 = lax.broadcasted_iota(jnp.int32, (CHUNK, CHUNK), 0)
    c64 = lax.broadcasted_iota(jnp.int32, (CHUNK, CHUNK), 1)
    causal64 = r64 >= c64
    strict64 = r64 > c64
    masks = _inverse_masks()

    n_ch = TM // CHUNK
    rows = [slice(c * CHUNK, (c + 1) * CHUNK) for c in range(n_ch)]
    bh = [(b, hh) for b in seqs for hh in range(HEADS)]
    probs = [(b, hh, c) for b, hh in bh for c in range(n_ch)]
    gq = {(b, hh): _l2norm(cqs[b][:, hh * HD:(hh + 1) * HD], HD ** -0.5) for b, hh in bh}
    gk = {(b, hh): _l2norm(cqs[b][:, QK + hh * HD:QK + (hh + 1) * HD], 1.0) for b, hh in bh}
    gv = {(b, hh): cqs[b][:, 2 * QK + hh * HD:2 * QK + (hh + 1) * HD] for b, hh in bh}
    qc = {(b, hh, c): gq[b, hh][rows[c]] for b, hh, c in probs}
    kc = {(b, hh, c): gk[b, hh][rows[c]] for b, hh, c in probs}
    vc = {(b, hh, c): gv[b, hh][rows[c]] for b, hh, c in probs}
    gc = {(b, hh, c): gcum[b][rows[c], hh:hh + 1] for b, hh, c in probs}
    bcol = {(b, hh, c): beta_all[b][rows[c], HEADS + hh:HEADS + hh + 1] for b, hh, c in probs}
    dm = {(b, hh, c): jnp.where(
        causal64, jnp.exp(jnp.minimum(gc[b, hh, c] - gcum_t[b][hh:hh + 1, rows[c]], 0.0)), 0.0)
        for b, hh, c in probs}
    kk = {p: _dot_nt(kc[p], kc[p]) for p in probs}
    qk = {p: _dot_nt(qc[p], kc[p]) * dm[p] for p in probs}
    a_mats = [jnp.where(strict64, bcol[p] * dm[p] * kk[p], 0.0) for p in probs]
    tinv = dict(zip(probs, _unit_lower_inverse_many(a_mats, masks)))
    eg = {p: jnp.exp(gc[p]) for p in probs}
    uw = {p: _dot(tinv[p], jnp.concatenate([bcol[p] * vc[p], (bcol[p] * eg[p]) * kc[p]], axis=1))
          for p in probs}
    gl = {p: gc[p][CHUNK - 1:CHUNK, :] for p in probs}
    kdw = {p: _dot_tn(kc[p] * jnp.exp(gl[p] - gc[p]), uw[p]) for p in probs}
    qkw = {p: _dot(qk[p], uw[p]) for p in probs}
    lhs = {p: jnp.concatenate([kdw[p][:, HD:], eg[p] * qc[p] - qkw[p][:, HD:]], axis=0).astype(BF16)
           for p in probs}

    s = {(b, hh): s_refs[b][hh] for b, hh in bh}
    outs = {k: [] for k in bh}
    for c in range(n_ch):
        for b, hh in bh:
            p = (b, hh, c)
            ps = _dot(lhs[p], s[b, hh])
            outs[b, hh].append(ps[HD:, :] + qkw[p][:, :HD])
            s[b, hh] = jnp.exp(gl[p]) * s[b, hh] + (kdw[p][:, :HD] - ps[:HD, :])
    o_gs = []
    for b in seqs:
        heads = []
        for hh in range(HEADS):
            s_refs[b][hh] = s[b, hh]
            o = jnp.concatenate(outs[b, hh], axis=0)
            heads.append(_gdn_out_norm(o, gdng_ref[...], _silu(gzs[b][:, hh * HD:(hh + 1) * HD])))
        o_gs.append(jnp.concatenate(heads, axis=1))
    return o_gs


def _mixer_prompt_body(x_ref, cos_ref, sin_ref, rd_ref, ku_ref, ltri_ref,
                       pre_ref, post_ref, wmain_ref, wab_ref, wgates_ref, convw_ref, alog_ref, dtb_ref,
                       retg_ref, gdng_ref, wrb_ref, wgb_ref, wout_ref,
                       y_ref, sret_ref, sgdn_ref, conv_ref,
                       cbuf, *, ret_tile_decay):
    t = pl.program_id(1)
    nb = x_ref.shape[0]

    @pl.when(t == 0)
    def _():
        sret_ref[...] = jnp.zeros_like(sret_ref)
        sgdn_ref[...] = jnp.zeros_like(sgdn_ref)
        cbuf[:, 0:CONV_PAD, :] = jnp.zeros((nb, CONV_PAD, CONV_DIM), F32)

    cos = cos_ref[...]
    sin = sin_ref[...]
    ltri = ltri_ref[...]
    seqs = range(nb)
    xs = [x_ref[b] for b in seqs]
    hs = [_rms(xs[b], pre_ref[...]).astype(BF16) for b in seqs]
    pgs = [jnp.dot(hs[b], wmain_ref[:, OFF_CONV:OFF_AB], preferred_element_type=F32) for b in seqs]
    abs_ = [jnp.dot(hs[b], wab_ref[...], preferred_element_type=F32) for b in seqs]
    cqs = [_short_conv_tile(pgs[b][:, 0:CONV_DIM], cbuf.at[b], convw_ref, conv_ref.at[b]) for b in seqs]
    pms = [jnp.dot(hs[b], wmain_ref[:, 0:OFF_CONV], preferred_element_type=F32) for b in seqs]
    o_rs = [_retention_tile(pms[b], cos, sin, rd_ref, ku_ref, retg_ref, sret_ref.at[b], ret_tile_decay)
            for b in seqs]
    o_gs = _gdn_tiles(cqs, abs_, [pgs[b][:, CONV_DIM:] for b in seqs], ltri, alog_ref, dtb_ref, gdng_ref,
                      [sgdn_ref.at[b] for b in seqs])
    for b in seqs:
        gates = jnp.dot(hs[b], wgates_ref[...], preferred_element_type=F32)
        y_ref[b] = _merge(xs[b], o_rs[b], o_gs[b], jax.nn.sigmoid(gates), wrb_ref, wgb_ref, wout_ref, post_ref)


def _ret_gammas():
    return 1.0 - 2.0 ** (-5.0 - np.arange(HEADS, dtype=np.float64))


def _rope_tables(pos):
    inv = ROPE_BASE ** (-np.arange(0, HD, 2, dtype=np.float64) / HD)
    ang = np.asarray(pos, np.float64)[:, None] * inv[None, :]
    cos = np.concatenate([np.cos(ang), np.cos(ang)], axis=1)
    sin = np.concatenate([-np.sin(ang), np.sin(ang)], axis=1)
    return jnp.asarray(cos, F32), jnp.asarray(sin, F32)


def _mixer_weight_specs():
    return [
        _resident((1, D_MODEL)), _resident((1, D_MODEL)),
        _resident((D_MODEL, OFF_AB)), _resident((D_MODEL, HD)), _resident((D_MODEL, 2 * D_MODEL)),
        _resident((CONV_W, CONV_DIM)), _resident((1, HD)), _resident((1, HD)),
        _resident((1, QK)), _resident((1, HD)),
        _resident((QK, D_MODEL)), _resident((QK, D_MODEL)), _resident((D_MODEL, D_MODEL)),
    ]


def _mixer_weight_args(w):
    return (w["mix_pre_g"], w["mix_post_g"], w["w_main"], w["w_ab"], w["w_gates"], w["conv_w"],
            w["a_log"], w["dt_bias"], w["ret_norm_g"], w["gdn_norm_g"], w["w_rb"], w["w_gb"], w["w_out"])


def _mixer_prompt(x, w):
    b, t, _ = x.shape
    assert t % TM == 0 and b % NB == 0
    cos, sin = _rope_tables(np.arange(t))
    gam = _ret_gammas()
    i1 = np.arange(1, TM + 1, dtype=np.float64)
    rd = np.broadcast_to((gam[:, None] ** i1[None, :])[:, :, None], (HEADS, TM, HD))
    ku = np.broadcast_to((HD ** -0.5 * gam[:, None] ** (-i1[None, :]))[:, :, None], (HEADS, TM, HD))
    tile_decay = tuple(float(v) for v in gam ** TM)
    r = np.arange(TM)
    ltri = ((r[:, None] >= r[None, :]) & (r[:, None] // CHUNK == r[None, :] // CHUNK))

    body = functools.partial(_mixer_prompt_body, ret_tile_decay=tile_decay)
    state_spec = pl.BlockSpec((NB, HEADS, HD, HD), lambda i, j: (i, 0, 0, 0))
    return pl.pallas_call(
        body,
        grid=(b // NB, t // TM),
        in_specs=[
            pl.BlockSpec((NB, TM, D_MODEL), lambda i, j: (i, j, 0)),
            pl.BlockSpec((TM, HD), lambda i, j: (j, 0)),
            pl.BlockSpec((TM, HD), lambda i, j: (j, 0)),
            _resident((HEADS, TM, HD)), _resident((HEADS, TM, HD)), _resident((TM, TM)),
        ] + _mixer_weight_specs(),
        out_specs=[
            pl.BlockSpec((NB, TM, D_MODEL), lambda i, j: (i, j, 0)),
            state_spec, state_spec,
            pl.BlockSpec((NB, CONV_W - 1, CONV_DIM), lambda i, j: (i, 0, 0)),
        ],
        out_shape=[
            jax.ShapeDtypeStruct((b, t, D_MODEL), F32),
            jax.ShapeDtypeStruct((b, HEADS, HD, HD), F32),
            jax.ShapeDtypeStruct((b, HEADS, HD, HD), F32),
            jax.ShapeDtypeStruct((b, CONV_W - 1, CONV_DIM), F32),
        ],
        scratch_shapes=[pltpu.VMEM((NB, CONV_PAD + TM, CONV_DIM), F32)],
        compiler_params=pltpu.CompilerParams(dimension_semantics=("arbitrary", "arbitrary"),
                                             vmem_limit_bytes=VMEM_LIMIT),
        name="mixer_prompt",
    )(x, cos, sin, jnp.asarray(rd, F32), jnp.asarray(ku, F32), jnp.asarray(ltri, BF16),
      *_mixer_weight_args(w))


def _pick_rows(rows):
    ri = lax.broadcasted_iota(jnp.int32, rows[0].shape, 0)
    out = rows[0]
    for j in range(1, len(rows)):
        out = jnp.where(ri == j, rows[j], out)
    return out


def _mixer_sample_body(x_ref, cos_ref, sin_ref,
                       pre_ref, post_ref, wmain_ref, wab_ref, wgates_ref, convw_ref, alog_ref, dtb_ref,
                       retg_ref, gdng_ref, wrb_ref, wgb_ref, wout_ref,
                       sret_in, sgdn_in, conv_in,
                       y_ref, sret_out, sgdn_out, conv_out,
                       pm_s, gates_s, rq_s, rk_s, gq_s, gk_s, gv_s, eg_s, beta_s, or_s, og_s, *, ret_gamma):
    i = pl.program_id(0)

    @pl.when(i == 0)
    def _():
        h = _rms(x_ref[...], pre_ref[...]).astype(BF16)
        pm = jnp.dot(h, wmain_ref[...], preferred_element_type=F32)
        ab = jnp.dot(h, wab_ref[...], preferred_element_type=F32)
        pm_s[...] = pm
        gates_s[...] = jnp.dot(h, wgates_ref[...], preferred_element_type=F32)
        cos = cos_ref[...]
        sin = sin_ref[...]
        cin = pm[:, OFF_CONV:OFF_CONV + CONV_DIM]
        acc = cin * convw_ref[CONV_W - 1:CONV_W, :]
        for r in range(CONV_W - 1):
            acc = acc + conv_in[:, r * CONV_DIM:(r + 1) * CONV_DIM] * convw_ref[r:r + 1, :]
        conv_out[:, 0:2 * CONV_DIM] = conv_in[:, CONV_DIM:3 * CONV_DIM]
        conv_out[:, 2 * CONV_DIM:3 * CONV_DIM] = cin
        cq = _silu(acc)
        g_all, beta_all = _gdn_decay_beta(ab, alog_ref, dtb_ref)
        eg_s[...] = jnp.exp(g_all)
        beta_s[...] = beta_all
        for hh in range(HEADS):
            sl = slice(hh * HD, (hh + 1) * HD)
            rq_s[:, sl] = _rope(pm[:, OFF_RQ + hh * HD:OFF_RQ + (hh + 1) * HD], cos, sin)
            rk_s[:, sl] = _rope(pm[:, OFF_RK + hh * HD:OFF_RK + (hh + 1) * HD], cos, sin) * (HD ** -0.5)
            gq_s[:, sl] = _l2norm(cq[:, hh * HD:(hh + 1) * HD], HD ** -0.5)
            gk_s[:, sl] = _l2norm(cq[:, QK + hh * HD:QK + (hh + 1) * HD], 1.0)
        gv_s[...] = cq[:, 2 * QK:3 * QK]

    rows = pl.ds(pl.multiple_of(i * TB, TB), TB)
    eg_all = eg_s[rows, :]
    beta_all = beta_s[rows, :]
    for hh in range(HEADS):
        sl = slice(hh * HD, (hh + 1) * HD)
        q = rq_s[rows, sl]
        k_t = rk_s[rows, sl].T
        v = pm_s[rows, OFF_RV + hh * HD:OFF_RV + (hh + 1) * HD]
        o_rows = []
        for j in range(TB):
            s = ret_gamma[hh] * sret_in[j, hh] + k_t[:, j:j + 1] * v[j:j + 1, :]
            sret_out[j, hh] = s
            o_rows.append(_dot(q, s))
        or_s[rows, sl] = _pick_rows(o_rows)

        gq = gq_s[rows, sl]
        gk = gk_s[rows, sl]
        gk_t = gk.T
        gv = gv_s[rows, sl]
        o_rows = []
        for j in range(TB):
            s = sgdn_in[j, hh]
            eg = eg_all[j:j + 1, hh:hh + 1]
            beta = beta_all[j:j + 1, HEADS + hh:HEADS + hh + 1]
            ks = _dot(gk, s)[j:j + 1, :]
            u = beta * gv[j:j + 1, :] - (beta * eg) * ks
            s = eg * s + gk_t[:, j:j + 1] * u
            sgdn_out[j, hh] = s
            o_rows.append(_dot(gq, s))
        og_s[rows, sl] = _pick_rows(o_rows)

    @pl.when(i == pl.num_programs(0) - 1)
    def _():
        o_r = []
        o_g = []
        for hh in range(HEADS):
            sl = slice(hh * HD, (hh + 1) * HD)
            o_r.append(_ret_out_norm(or_s[:, sl], retg_ref[:, sl],
                                     _silu(pm_s[:, OFF_RG + hh * HD:OFF_RG + (hh + 1) * HD])))
            o_g.append(_gdn_out_norm(og_s[:, sl], gdng_ref[...],
                                     _silu(pm_s[:, OFF_GZ + hh * HD:OFF_GZ + (hh + 1) * HD])))
        y_ref[...] = _merge(x_ref[...], jnp.concatenate(o_r, axis=1), jnp.concatenate(o_g, axis=1),
                            jax.nn.sigmoid(gates_s[...]), wrb_ref, wgb_ref, wout_ref, post_ref)


def _mixer_sample(x, s_ret, s_gdn, s_conv, w, pos):
    n = x.shape[0]
    assert n % TB == 0
    cos, sin = _rope_tables([pos])
    body = functools.partial(_mixer_sample_body, ret_gamma=tuple(float(v) for v in _ret_gammas()))
    state_spec = pl.BlockSpec((TB, HEADS, HD, HD), lambda i: (i, 0, 0, 0))
    conv_cols = (CONV_W - 1) * CONV_DIM
    return pl.pallas_call(
        body,
        grid=(n // TB,),
        in_specs=[_resident((n, D_MODEL)), _resident((1, HD)), _resident((1, HD))]
        + _mixer_weight_specs()
        + [state_spec, state_spec, _resident((n, conv_cols))],
        out_specs=[pl.BlockSpec((n, D_MODEL), lambda i: (0, 0)), state_spec, state_spec,
                   pl.BlockSpec((n, conv_cols), lambda i: (0, 0))],
        out_shape=[
            jax.ShapeDtypeStruct((n, D_MODEL), F32),
            jax.ShapeDtypeStruct((n, HEADS, HD, HD), F32),
            jax.ShapeDtypeStruct((n, HEADS, HD, HD), F32),
            jax.ShapeDtypeStruct((n, conv_cols), F32),
        ],
        scratch_shapes=[pltpu.VMEM((n, OFF_AB), F32), pltpu.VMEM((n, 2 * D_MODEL), F32)]
        + [pltpu.VMEM((n, QK), F32) for _ in range(5)]
        + [pltpu.VMEM((n, HD), F32), pltpu.VMEM((n, HD), F32)]
        + [pltpu.VMEM((n, QK), F32), pltpu.VMEM((n, QK), F32)],
        compiler_params=pltpu.CompilerParams(dimension_semantics=("arbitrary",),
                                             vmem_limit_bytes=VMEM_LIMIT),
        name="mixer_sample",
    )(x, cos, sin, *_mixer_weight_args(w), s_ret, s_gdn, s_conv)


def _pad_lanes(v, n):
    return jnp.pad(v, ((0, 0), (0, n - v.shape[1])))


def kernel(x_prompt, x_sample, state_ret, state_gdn, state_conv, ffn1_pre_g, ffn1_post_g, ffn1_w_gate,
           ffn1_w_up, ffn1_w_down, mix_pre_g, mix_post_g, w_in, ret_norm_g, gdn_conv_w, gdn_a_log,
           gdn_dt_bias, gdn_norm_g, w_ret_branch, w_gdn_branch, w_out, ffn2_pre_g, ffn2_post_g,
           ffn2_w_gate, ffn2_w_up, ffn2_w_down):
    depth = w_in.shape[0]
    b, t, _ = x_prompt.shape
    n_s, t_s, _ = x_sample.shape
    assert t_s == 1
    yp = x_prompt.reshape(b * t, D_MODEL)
    ys = x_sample.reshape(n_s, D_MODEL)
    outs = [[] for _ in range(6)]
    for l in range(depth):
        row = lambda a: a[l][None, :]
        f1 = (row(ffn1_pre_g), row(ffn1_post_g), ffn1_w_gate[l].astype(BF16),
              ffn1_w_up[l].astype(BF16), ffn1_w_down[l].astype(BF16))

        w_in_t = w_in[l].T
        casts = (
            (w_in_t, (0, OFF_AB)), (w_in_t, (OFF_GATES, 2 * D_MODEL)),
            (w_ret_branch[l], None), (w_gdn_branch[l], None), (w_out[l], None),
            (ffn2_w_gate[l], None), (ffn2_w_up[l], None), (ffn2_w_down[l], None),
        )
        yp, (w_main, w_gates, w_rb, w_gb, w_o, g2, u2, d2) = _ffn(yp, *f1, tm=FFN_TM, casts=casts)
        w = {
            "mix_pre_g": row(mix_pre_g), "mix_post_g": row(mix_post_g),
            "w_main": w_main,
            "w_ab": _pad_lanes(w_in[l, :, OFF_AB:OFF_GATES], HD).astype(BF16),
            "w_gates": w_gates,
            "conv_w": gdn_conv_w[l],
            "a_log": _pad_lanes(row(gdn_a_log), HD), "dt_bias": _pad_lanes(row(gdn_dt_bias), HD),
            "ret_norm_g": row(ret_norm_g), "gdn_norm_g": row(gdn_norm_g),
            "w_rb": w_rb, "w_gb": w_gb, "w_out": w_o,
        }
        f2 = (row(ffn2_pre_g), row(ffn2_post_g), g2, u2, d2)

        yp, r1, g1, c1 = _mixer_prompt(yp.reshape(b, t, D_MODEL), w)
        yp, _ = _ffn(yp.reshape(b * t, D_MODEL), *f2, tm=FFN_TM)

        ys, _ = _ffn(ys, *f1, tm=n_s)
        ys, r2, g2s, c2 = _mixer_sample(ys, state_ret[l], state_gdn[l],
                                        state_conv[l].reshape(n_s, (CONV_W - 1) * CONV_DIM), w,
                                        PAST_LEN)
        ys, _ = _ffn(ys, *f2, tm=n_s)
        c2 = c2.reshape(n_s, CONV_W - 1, CONV_DIM)
        for lst, val in zip(outs, (r1, g1, c1, r2, g2s, c2)):
            lst.append(val)
    stacked = [v[0][None] if depth == 1 else jnp.stack(v) for v in outs]
    return (yp.reshape(b, t, D_MODEL), ys.reshape(n_s, t_s, D_MODEL), *stacked)
```

```python
import functools

import numpy as np
import jax
import jax.numpy as jnp
from jax import lax
from jax.experimental import pallas as pl
from jax.experimental.pallas import tpu as pltpu

F32 = jnp.float32
BF16 = jnp.bfloat16

D_MODEL = 1024
D_FF = 2816
HEADS = 4
HD = 128
QK = HEADS * HD
CONV_W = 4
CONV_DIM = 3 * QK
CHUNK = 64
ROPE_BASE = 10000.0
EPS = 1e-6
PAST_LEN = 16384

OFF_RQ, OFF_RK, OFF_RV, OFF_RG = 0, QK, 2 * QK, 3 * QK
OFF_CONV = 4 * QK
OFF_GZ = OFF_CONV + CONV_DIM
OFF_AB = OFF_GZ + QK
OFF_GATES = OFF_AB + 2 * HEADS
D_IN = OFF_GATES + 2 * D_MODEL

TM = 256
NB = 2
TB = 8
FFN_TM = 1024
FFN_SUB = 256
FF_CHUNK = 256
CONV_PAD = 8
BF16_ROWS = 16
F32_ROWS = 8

VMEM_LIMIT = 56 * 1024 * 1024


def _silu(x):
    return x * jax.nn.sigmoid(x)


def _rms(x, g):
    return x * lax.rsqrt(jnp.mean(x * x, axis=-1, keepdims=True) + EPS) * g


def _dot(a, b):
    return jnp.dot(a.astype(BF16), b.astype(BF16), preferred_element_type=F32)


def _dot_nt(a, b):
    return lax.dot_general(a.astype(BF16), b.astype(BF16), (((1,), (1,)), ((), ())),
                           preferred_element_type=F32)


def _dot_tn(a, b):
    return lax.dot_general(a.astype(BF16), b.astype(BF16), (((0,), (0,)), ((), ())),
                           preferred_element_type=F32)


def _split3(a):
    hi = a.astype(BF16)
    r = a - hi.astype(F32)
    mid = r.astype(BF16)
    lo = (r - mid.astype(F32)).astype(BF16)
    return hi, mid, lo


def _resident(shape):
    nd = len(shape)
    return pl.BlockSpec(shape, lambda *_: (0,) * nd, pipeline_mode=pl.Buffered(1))


def _ffn_rows(x_ref, o_ref, pre_ref, post_ref, wg_ref, wu_ref, wd_ref):
    tm = x_ref.shape[0]
    sub = min(tm, FFN_SUB)
    parts = [slice(r, r + sub) for r in range(0, tm, sub)]
    xs = [x_ref[p, :] for p in parts]
    hs = [_rms(x, pre_ref[...]).astype(BF16) for x in xs]
    acts = []
    for h in hs:
        g = jnp.dot(h, wg_ref[...], preferred_element_type=F32)
        u = jnp.dot(h, wu_ref[...], preferred_element_type=F32)
        acts.append((_silu(g) * u).astype(BF16))
    for p, x, a in zip(parts, xs, acts):
        y = jnp.dot(a, wd_ref[...], preferred_element_type=F32)
        o_ref[p, :] = x + 0.5 * _rms(y, post_ref[...])


def _ffn_body(x_ref, pre_ref, post_ref, wg_ref, wu_ref, wd_ref, *rest, cast_transposed, has_extra):
    rest = list(rest)
    xe_ref = rest.pop(0) if has_extra else None
    n_cast = len(cast_transposed)
    cast_in, o_ref = rest[:n_cast], rest[n_cast]
    rest = rest[n_cast + 1:]
    oe_ref = rest.pop(0) if has_extra else None
    cast_out = rest
    weights = (pre_ref, post_ref, wg_ref, wu_ref, wd_ref)
    _ffn_rows(x_ref, o_ref, *weights)
    for src, dst, transposed in zip(cast_in, cast_out, cast_transposed):
        blk = src[...]
        dst[...] = (blk.T if transposed else blk).astype(BF16)
    if has_extra:
        @pl.when(pl.program_id(0) == pl.num_programs(0) - 1)
        def _():
            _ffn_rows(xe_ref, oe_ref, *weights)


def _cast_row_block(rows, steps):
    rb = BF16_ROWS
    while rows % rb or rows // rb > steps:
        rb += BF16_ROWS
    return rb


def _ffn(x, pre_g, post_g, wg, wu, wd, tm, casts=(), extra=None):
    n = x.shape[0]
    assert n % tm == 0
    steps = n // tm
    in_specs = [
        pl.BlockSpec((tm, D_MODEL), lambda i: (i, 0)),
        _resident((1, D_MODEL)), _resident((1, D_MODEL)),
        _resident((D_MODEL, D_FF)), _resident((D_MODEL, D_FF)), _resident((D_FF, D_MODEL)),
    ]
    out_specs = [pl.BlockSpec((tm, D_MODEL), lambda i: (i, 0))]
    out_shape = [jax.ShapeDtypeStruct((n, D_MODEL), F32)]
    operands = [x, pre_g, post_g, wg, wu, wd]
    if extra is not None:
        in_specs.append(_resident(extra.shape))
        operands.append(extra)
    n_fixed_in = len(in_specs)
    for arr, region in casts:
        rows, width = arr.shape
        if region is None:
            rb = _cast_row_block(rows, steps)
            last = rows // rb - 1
            spec = pl.BlockSpec((rb, width), lambda i, last=last: (jnp.minimum(i, last), 0))
            in_specs.append(spec)
            out_specs.append(spec)
            out_shape.append(jax.ShapeDtypeStruct((rows, width), BF16))
        else:
            row0, n_rows = region
            cb = HD
            while n_rows % cb or n_rows // cb > steps:
                cb += HD
            last = n_rows // cb - 1
            in_specs.append(pl.BlockSpec(
                (pl.Element(cb), pl.Element(width)),
                lambda i, row0=row0, last=last, cb=cb: (
                    pl.multiple_of(row0 + cb * jnp.minimum(i, last), F32_ROWS), 0)))
            out_specs.append(pl.BlockSpec((width, cb), lambda i, last=last: (0, jnp.minimum(i, last))))
            out_shape.append(jax.ShapeDtypeStruct((width, n_rows), BF16))
    assert len(in_specs) == n_fixed_in + len(casts)
    if extra is not None:
        out_specs.insert(1, pl.BlockSpec(extra.shape, lambda i: (0, 0)))
        out_shape.insert(1, jax.ShapeDtypeStruct(extra.shape, F32))
    outs = pl.pallas_call(
        functools.partial(_ffn_body, cast_transposed=tuple(c[1] is not None for c in casts),
                          has_extra=extra is not None),
        grid=(steps,),
        in_specs=in_specs,
        out_specs=out_specs,
        out_shape=out_shape,
        compiler_params=pltpu.CompilerParams(dimension_semantics=("arbitrary",),
                                             vmem_limit_bytes=VMEM_LIMIT),
        name="ffn",
    )(*operands, *[c[0] for c in casts])
    if extra is not None:
        return outs[0], outs[1], list(outs[2:])
    return outs[0], list(outs[1:])


def _ffn_stream_body(x_ref, pre_ref, post_ref, wg_ref, wu_ref, wd_ref,
                     o_ref, wg16_ref, wu16_ref, wd16_ref, h_s, acc_s):
    c = pl.program_id(0)

    @pl.when(c == 0)
    def _():
        h_s[...] = _rms(x_ref[...], pre_ref[...]).astype(BF16)
        acc_s[...] = jnp.zeros_like(acc_s)

    wg = wg_ref[...].astype(BF16)
    wu = wu_ref[...].astype(BF16)
    wd = wd_ref[...].astype(BF16)
    wg16_ref[...] = wg
    wu16_ref[...] = wu
    wd16_ref[...] = wd
    h = h_s[...]
    g = jnp.dot(h, wg, preferred_element_type=F32)
    u = jnp.dot(h, wu, preferred_element_type=F32)
    a = (_silu(g) * u).astype(BF16)
    acc_s[...] += jnp.dot(a, wd, preferred_element_type=F32)

    @pl.when(c == pl.num_programs(0) - 1)
    def _():
        o_ref[...] = x_ref[...] + 0.5 * _rms(acc_s[...], post_ref[...])


def _ffn_stream(x, pre_g, post_g, wg, wu, wd):
    n = x.shape[0]
    assert D_FF % FF_CHUNK == 0
    col = pl.BlockSpec((D_MODEL, FF_CHUNK), lambda c: (0, c))
    row = pl.BlockSpec((FF_CHUNK, D_MODEL), lambda c: (c, 0))
    return pl.pallas_call(
        _ffn_stream_body,
        grid=(D_FF // FF_CHUNK,),
        in_specs=[_resident((n, D_MODEL)), _resident((1, D_MODEL)), _resident((1, D_MODEL)), col, col, row],
        out_specs=[pl.BlockSpec((n, D_MODEL), lambda c: (0, 0)), col, col, row],
        out_shape=[
            jax.ShapeDtypeStruct((n, D_MODEL), F32),
            jax.ShapeDtypeStruct((D_MODEL, D_FF), BF16),
            jax.ShapeDtypeStruct((D_MODEL, D_FF), BF16),
            jax.ShapeDtypeStruct((D_FF, D_MODEL), BF16),
        ],
        scratch_shapes=[pltpu.VMEM((n, D_MODEL), BF16), pltpu.VMEM((n, D_MODEL), F32)],
        compiler_params=pltpu.CompilerParams(dimension_semantics=("arbitrary",),
                                             vmem_limit_bytes=VMEM_LIMIT),
        name="ffn_stream",
    )(x, pre_g, post_g, wg, wu, wd)


def _rope(x, cos, sin_signed):
    return x * cos + pltpu.roll(x, HD // 2, 1) * sin_signed


def _ret_out_norm(o, g_row, gate_act):
    mu = jnp.mean(o, axis=-1, keepdims=True)
    d = o - mu
    var = jnp.mean(d * d, axis=-1, keepdims=True)
    return gate_act * (d * lax.rsqrt(var + EPS) * g_row)


def _gdn_out_norm(o, g_row, gate_act):
    return o * lax.rsqrt(jnp.mean(o * o, axis=-1, keepdims=True) + EPS) * g_row * gate_act


def _l2norm(x, scale):
    return x * (lax.rsqrt(jnp.sum(x * x, axis=-1, keepdims=True) + EPS) * scale)


def _softplus(x):
    return jnp.maximum(x, 0.0) + jnp.log(1.0 + jnp.exp(-jnp.abs(x)))


def _merge(x, o_r, o_g, sig_gates, wrb_ref, wgb_ref, wout_ref, post_ref):
    y = (sig_gates[:, :D_MODEL] * _dot(o_r, wrb_ref[...])
         + sig_gates[:, D_MODEL:] * _dot(o_g, wgb_ref[...]))
    m = _dot(y, wout_ref[...])
    return x + _rms(m, post_ref[...])


def _unit_lower_inverse_many(a_list, masks):
    eye, m16, off32, off64 = masks
    ad = [a * m16 for a in a_list]
    x = [eye - v for v in ad]
    p = [_dot(v, v) for v in ad]
    for level in range(3):
        x = [xi + _dot(xi, pi) for xi, pi in zip(x, p)]
        if level < 2:
            p = [_dot(pi, pi) for pi in p]
    for m in (off32, off64):
        t = [_dot(xi, a * m) for xi, a in zip(x, a_list)]
        x = [xi - _dot(ti, xi) for xi, ti in zip(x, t)]
    return x


def _inverse_masks():
    r = lax.broadcasted_iota(jnp.int32, (CHUNK, CHUNK), 0)
    c = lax.broadcasted_iota(jnp.int32, (CHUNK, CHUNK), 1)
    one, zero = jnp.float32(1.0), jnp.float32(0.0)
    eye = jnp.where(r == c, one, zero)
    same16 = (r >> 4) == (c >> 4)
    same32 = (r >> 5) == (c >> 5)
    m16 = jnp.where(same16, one, zero)
    m32 = jnp.where(same32, one, zero)
    return eye, m16, m32 - m16, 1.0 - m32


def _retention_tile(pm, cos, sin, rd_ref, ku_ref, retg_ref, s_ref, tile_decay):
    ri = lax.broadcasted_iota(jnp.int32, (TM, TM), 0)
    ci = lax.broadcasted_iota(jnp.int32, (TM, TM), 1)
    causal = ri >= ci
    heads = []
    for hh in range(HEADS):
        sl = slice(hh * HD, (hh + 1) * HD)
        rq = pm[:, OFF_RQ + hh * HD:OFF_RQ + (hh + 1) * HD]
        rk = pm[:, OFF_RK + hh * HD:OFF_RK + (hh + 1) * HD]
        v = pm[:, OFF_RV + hh * HD:OFF_RV + (hh + 1) * HD]
        rg = pm[:, OFF_RG + hh * HD:OFF_RG + (hh + 1) * HD]
        qs = _rope(rq, cos, sin) * rd_ref[hh]
        ku = _rope(rk, cos, sin) * ku_ref[hh]
        s = s_ref[hh]
        sc = jnp.where(causal, _dot_nt(qs, ku), 0.0)
        o = _dot(sc, v) + _dot(qs, s)
        s_ref[hh] = tile_decay[hh] * (s + _dot_tn(ku, v))
        heads.append(_ret_out_norm(o, retg_ref[:, sl], _silu(rg)))
    return jnp.concatenate(heads, axis=1)


def _short_conv_tile(u, buf, convw_ref, tail_ref):
    buf[CONV_PAD:CONV_PAD + TM, :] = u
    acc = buf[CONV_PAD - 3:CONV_PAD - 3 + TM, :] * convw_ref[0:1, :]
    for i in range(1, CONV_W):
        acc = acc + buf[CONV_PAD - 3 + i:CONV_PAD - 3 + i + TM, :] * convw_ref[i:i + 1, :]
    tail = buf[CONV_PAD + TM - 3:CONV_PAD + TM, :]
    tail_ref[...] = tail
    buf[CONV_PAD - 3:CONV_PAD, :] = tail
    return _silu(acc)


def _gdn_decay_beta(ab, alog_ref, dtb_ref):
    g_all = -jnp.exp(alog_ref[...]) * _softplus(ab + dtb_ref[...])
    beta_all = jax.nn.sigmoid(ab)
    return g_all, beta_all


def _gdn_tiles(cqs, abs_, gzs, ltri, alog_ref, dtb_ref, gdng_ref, s_refs):
    seqs = range(len(cqs))
    dd = functools.partial(jnp.dot, preferred_element_type=F32)
    gcum, gcum_t, beta_all = [], [], []
    for b in seqs:
        g_all, beta = _gdn_decay_beta(abs_[b], alog_ref, dtb_ref)
        beta_all.append(beta)
        g_hi, g_mid, g_lo = _split3(g_all)
        gcum.append(dd(ltri, g_hi) + (dd(ltri, g_mid) + dd(ltri, g_lo)))
        gcum_t.append(gcum[b].T)

    r64 = lax.broadcasted_iota(jnp.int32, (CHUNK, CHUNK), 0)
    c64 = lax.broadcasted_iota(jnp.int32, (CHUNK, CHUNK), 1)
    causal64 = r64 >= c64
    strict64 = r64 > c64
    masks = _inverse_masks()

    n_ch = TM // CHUNK
    rows = [slice(c * CHUNK, (c + 1) * CHUNK) for c in range(n_ch)]
    bh = [(b, hh) for b in seqs for hh in range(HEADS)]
    probs = [(b, hh, c) for b, hh in bh for c in range(n_ch)]
    gq = {(b, hh): _l2norm(cqs[b][:, hh * HD:(hh + 1) * HD], HD ** -0.5) for b, hh in bh}
    gk = {(b, hh): _l2norm(cqs[b][:, QK + hh * HD:QK + (hh + 1) * HD], 1.0) for b, hh in bh}
    gv = {(b, hh): cqs[b][:, 2 * QK + hh * HD:2 * QK + (hh + 1) * HD] for b, hh in bh}
    qc = {(b, hh, c): gq[b, hh][rows[c]] for b, hh, c in probs}
    kc = {(b, hh, c): gk[b, hh][rows[c]] for b, hh, c in probs}
    vc = {(b, hh, c): gv[b, hh][rows[c]] for b, hh, c in probs}
    gc = {(b, hh, c): gcum[b][rows[c], hh:hh + 1] for b, hh, c in probs}
    bcol = {(b, hh, c): beta_all[b][rows[c], HEADS + hh:HEADS + hh + 1] for b, hh, c in probs}
    dm = {(b, hh, c): jnp.where(
        causal64, jnp.exp(jnp.minimum(gc[b, hh, c] - gcum_t[b][hh:hh + 1, rows[c]], 0.0)), 0.0)
        for b, hh, c in probs}
    kk = {p: _dot_nt(kc[p], kc[p]) for p in probs}
    qk = {p: _dot_nt(qc[p], kc[p]) * dm[p] for p in probs}
    a_mats = [jnp.where(strict64, bcol[p] * dm[p] * kk[p], 0.0) for p in probs]
    tinv = dict(zip(probs, _unit_lower_inverse_many(a_mats, masks)))
    eg = {p: jnp.exp(gc[p]) for p in probs}
    uw = {p: _dot(tinv[p], jnp.concatenate([bcol[p] * vc[p], (bcol[p] * eg[p]) * kc[p]], axis=1))
          for p in probs}
    gl = {p: gc[p][CHUNK - 1:CHUNK, :] for p in probs}
    kdw = {p: _dot_tn(kc[p] * jnp.exp(gl[p] - gc[p]), uw[p]) for p in probs}
    qkw = {p: _dot(qk[p], uw[p]) for p in probs}
    lhs = {p: jnp.concatenate([kdw[p][:, HD:], eg[p] * qc[p] - qkw[p][:, HD:]], axis=0).astype(BF16)
           for p in probs}

    s = {(b, hh): s_refs[b][hh] for b, hh in bh}
    outs = {k: [] for k in bh}
    for c in range(n_ch):
        for b, hh in bh:
            p = (b, hh, c)
            ps = _dot(lhs[p], s[b, hh])
            outs[b, hh].append(ps[HD:, :] + qkw[p][:, :HD])
            s[b, hh] = jnp.exp(gl[p]) * s[b, hh] + (kdw[p][:, :HD] - ps[:HD, :])
    o_gs = []
    for b in seqs:
        heads = []
        for hh in range(HEADS):
            s_refs[b][hh] = s[b, hh]
            o = jnp.concatenate(outs[b, hh], axis=0)
            heads.append(_gdn_out_norm(o, gdng_ref[...], _silu(gzs[b][:, hh * HD:(hh + 1) * HD])))
        o_gs.append(jnp.concatenate(heads, axis=1))
    return o_gs


def _mixer_prompt_body(x_ref, cos_ref, sin_ref, rd_ref, ku_ref, ltri_ref,
                       pre_ref, post_ref, wmain_ref, wab_ref, wgates_ref, convw_ref, alog_ref, dtb_ref,
                       retg_ref, gdng_ref, wrb_ref, wgb_ref, wout_ref,
                       y_ref, sret_ref, sgdn_ref, conv_ref,
                       cbuf, *, ret_tile_decay):
    t = pl.program_id(1)
    nb = x_ref.shape[0]

    @pl.when(t == 0)
    def _():
        sret_ref[...] = jnp.zeros_like(sret_ref)
        sgdn_ref[...] = jnp.zeros_like(sgdn_ref)
        cbuf[:, 0:CONV_PAD, :] = jnp.zeros((nb, CONV_PAD, CONV_DIM), F32)

    cos = cos_ref[...]
    sin = sin_ref[...]
    ltri = ltri_ref[...]
    seqs = range(nb)
    xs = [x_ref[b] for b in seqs]
    hs = [_rms(xs[b], pre_ref[...]).astype(BF16) for b in seqs]
    pgs = [jnp.dot(hs[b], wmain_ref[:, OFF_CONV:OFF_AB], preferred_element_type=F32) for b in seqs]
    abs_ = [jnp.dot(hs[b], wab_ref[...], preferred_element_type=F32) for b in seqs]
    cqs = [_short_conv_tile(pgs[b][:, 0:CONV_DIM], cbuf.at[b], convw_ref, conv_ref.at[b]) for b in seqs]
    pms = [jnp.dot(hs[b], wmain_ref[:, 0:OFF_CONV], preferred_element_type=F32) for b in seqs]
    o_rs = [_retention_tile(pms[b], cos, sin, rd_ref, ku_ref, retg_ref, sret_ref.at[b], ret_tile_decay)
            for b in seqs]
    o_gs = _gdn_tiles(cqs, abs_, [pgs[b][:, CONV_DIM:] for b in seqs], ltri, alog_ref, dtb_ref, gdng_ref,
                      [sgdn_ref.at[b] for b in seqs])
    for b in seqs:
        gates = jnp.dot(hs[b], wgates_ref[...], preferred_element_type=F32)
        y_ref[b] = _merge(xs[b], o_rs[b], o_gs[b], jax.nn.sigmoid(gates), wrb_ref, wgb_ref, wout_ref, post_ref)


def _ret_gammas():
    return 1.0 - 2.0 ** (-5.0 - np.arange(HEADS, dtype=np.float64))


def _rope_tables(pos):
    inv = ROPE_BASE ** (-np.arange(0, HD, 2, dtype=np.float64) / HD)
    ang = np.asarray(pos, np.float64)[:, None] * inv[None, :]
    cos = np.concatenate([np.cos(ang), np.cos(ang)], axis=1)
    sin = np.concatenate([-np.sin(ang), np.sin(ang)], axis=1)
    return jnp.asarray(cos, F32), jnp.asarray(sin, F32)


def _mixer_weight_specs():
    return [
        _resident((1, D_MODEL)), _resident((1, D_MODEL)),
        _resident((D_MODEL, OFF_AB)), _resident((D_MODEL, HD)), _resident((D_MODEL, 2 * D_MODEL)),
        _resident((CONV_W, CONV_DIM)), _resident((1, HD)), _resident((1, HD)),
        _resident((1, QK)), _resident((1, HD)),
        _resident((QK, D_MODEL)), _resident((QK, D_MODEL)), _resident((D_MODEL, D_MODEL)),
    ]


def _mixer_weight_args(w):
    return (w["mix_pre_g"], w["mix_post_g"], w["w_main"], w["w_ab"], w["w_gates"], w["conv_w"],
            w["a_log"], w["dt_bias"], w["ret_norm_g"], w["gdn_norm_g"], w["w_rb"], w["w_gb"], w["w_out"])


def _mixer_prompt(x, w):
    b, t, _ = x.shape
    assert t % TM == 0 and b % NB == 0
    cos, sin = _rope_tables(np.arange(t))
    gam = _ret_gammas()
    i1 = np.arange(1, TM + 1, dtype=np.float64)
    rd = np.broadcast_to((gam[:, None] ** i1[None, :])[:, :, None], (HEADS, TM, HD))
    ku = np.broadcast_to((HD ** -0.5 * gam[:, None] ** (-i1[None, :]))[:, :, None], (HEADS, TM, HD))
    tile_decay = tuple(float(v) for v in gam ** TM)
    r = np.arange(TM)
    ltri = ((r[:, None] >= r[None, :]) & (r[:, None] // CHUNK == r[None, :] // CHUNK))

    body = functools.partial(_mixer_prompt_body, ret_tile_decay=tile_decay)
    state_spec = pl.BlockSpec((NB, HEADS, HD, HD), lambda i, j: (i, 0, 0, 0))
    return pl.pallas_call(
        body,
        grid=(b // NB, t // TM),
        in_specs=[
            pl.BlockSpec((NB, TM, D_MODEL), lambda i, j: (i, j, 0)),
            pl.BlockSpec((TM, HD), lambda i, j: (j, 0)),
            pl.BlockSpec((TM, HD), lambda i, j: (j, 0)),
            _resident((HEADS, TM, HD)), _resident((HEADS, TM, HD)), _resident((TM, TM)),
        ] + _mixer_weight_specs(),
        out_specs=[
            pl.BlockSpec((NB, TM, D_MODEL), lambda i, j: (i, j, 0)),
            state_spec, state_spec,
            pl.BlockSpec((NB, CONV_W - 1, CONV_DIM), lambda i, j: (i, 0, 0)),
        ],
        out_shape=[
            jax.ShapeDtypeStruct((b, t, D_MODEL), F32),
            jax.ShapeDtypeStruct((b, HEADS, HD, HD), F32),
            jax.ShapeDtypeStruct((b, HEADS, HD, HD), F32),
            jax.ShapeDtypeStruct((b, CONV_W - 1, CONV_DIM), F32),
        ],
        scratch_shapes=[pltpu.VMEM((NB, CONV_PAD + TM, CONV_DIM), F32)],
        compiler_params=pltpu.CompilerParams(dimension_semantics=("arbitrary", "arbitrary"),
                                             vmem_limit_bytes=VMEM_LIMIT),
        name="mixer_prompt",
    )(x, cos, sin, jnp.asarray(rd, F32), jnp.asarray(ku, F32), jnp.asarray(ltri, BF16),
      *_mixer_weight_args(w))


def _pick_rows(rows):
    ri = lax.broadcasted_iota(jnp.int32, rows[0].shape, 0)
    out = rows[0]
    for j in range(1, len(rows)):
        out = jnp.where(ri == j, rows[j], out)
    return out


def _mixer_sample_body(x_ref, cos_ref, sin_ref,
                       pre_ref, post_ref, wmain_ref, wab_ref, wgates_ref, convw_ref, alog_ref, dtb_ref,
                       retg_ref, gdng_ref, wrb_ref, wgb_ref, wout_ref,
                       sret_in, sgdn_in, conv_in,
                       y_ref, sret_out, sgdn_out, conv_out,
                       pm_s, gates_s, rq_s, rk_s, gq_s, gk_s, gv_s, eg_s, beta_s, or_s, og_s, *, ret_gamma):
    i = pl.program_id(0)

    @pl.when(i == 0)
    def _():
        h = _rms(x_ref[...], pre_ref[...]).astype(BF16)
        pm = jnp.dot(h, wmain_ref[...], preferred_element_type=F32)
        ab = jnp.dot(h, wab_ref[...], preferred_element_type=F32)
        pm_s[...] = pm
        gates_s[...] = jnp.dot(h, wgates_ref[...], preferred_element_type=F32)
        cos = cos_ref[...]
        sin = sin_ref[...]
        cin = pm[:, OFF_CONV:OFF_CONV + CONV_DIM]
        acc = cin * convw_ref[CONV_W - 1:CONV_W, :]
        for r in range(CONV_W - 1):
            acc = acc + conv_in[:, r, :] * convw_ref[r:r + 1, :]
        for r in range(CONV_W - 2):
            conv_out[:, r, :] = conv_in[:, r + 1, :]
        conv_out[:, CONV_W - 2, :] = cin
        cq = _silu(acc)
        g_all, beta_all = _gdn_decay_beta(ab, alog_ref, dtb_ref)
        eg_s[...] = jnp.exp(g_all)
        beta_s[...] = beta_all
        for hh in range(HEADS):
            sl = slice(hh * HD, (hh + 1) * HD)
            rq_s[:, sl] = _rope(pm[:, OFF_RQ + hh * HD:OFF_RQ + (hh + 1) * HD], cos, sin)
            rk_s[:, sl] = _rope(pm[:, OFF_RK + hh * HD:OFF_RK + (hh + 1) * HD], cos, sin) * (HD ** -0.5)
            gq_s[:, sl] = _l2norm(cq[:, hh * HD:(hh + 1) * HD], HD ** -0.5)
            gk_s[:, sl] = _l2norm(cq[:, QK + hh * HD:QK + (hh + 1) * HD], 1.0)
        gv_s[...] = cq[:, 2 * QK:3 * QK]

    rows = pl.ds(pl.multiple_of(i * TB, TB), TB)
    eg_all = eg_s[rows, :]
    beta_all = beta_s[rows, :]
    for hh in range(HEADS):
        sl = slice(hh * HD, (hh + 1) * HD)
        q = rq_s[rows, sl]
        k_t = rk_s[rows, sl].T
        v = pm_s[rows, OFF_RV + hh * HD:OFF_RV + (hh + 1) * HD]
        o_rows = []
        for j in range(TB):
            s = ret_gamma[hh] * sret_in[j, hh] + k_t[:, j:j + 1] * v[j:j + 1, :]
            sret_out[j, hh] = s
            o_rows.append(_dot(q, s))
        or_s[rows, sl] = _pick_rows(o_rows)

        gq = gq_s[rows, sl]
        gk = gk_s[rows, sl]
        gk_t = gk.T
        gv = gv_s[rows, sl]
        o_rows = []
        for j in range(TB):
            s = sgdn_in[j, hh]
            eg = eg_all[j:j + 1, hh:hh + 1]
            beta = beta_all[j:j + 1, HEADS + hh:HEADS + hh + 1]
            ks = _dot(gk, s)[j:j + 1, :]
            u = beta * gv[j:j + 1, :] - (beta * eg) * ks
            s = eg * s + gk_t[:, j:j + 1] * u
            sgdn_out[j, hh] = s
            o_rows.append(_dot(gq, s))
        og_s[rows, sl] = _pick_rows(o_rows)

    @pl.when(i == pl.num_programs(0) - 1)
    def _():
        o_r = []
        o_g = []
        for hh in range(HEADS):
            sl = slice(hh * HD, (hh + 1) * HD)
            o_r.append(_ret_out_norm(or_s[:, sl], retg_ref[:, sl],
                                     _silu(pm_s[:, OFF_RG + hh * HD:OFF_RG + (hh + 1) * HD])))
            o_g.append(_gdn_out_norm(og_s[:, sl], gdng_ref[...],
                                     _silu(pm_s[:, OFF_GZ + hh * HD:OFF_GZ + (hh + 1) * HD])))
        y_ref[...] = _merge(x_ref[...], jnp.concatenate(o_r, axis=1), jnp.concatenate(o_g, axis=1),
                            jax.nn.sigmoid(gates_s[...]), wrb_ref, wgb_ref, wout_ref, post_ref)


def _mixer_sample(x, s_ret, s_gdn, s_conv, w, pos):
    n = x.shape[0]
    assert n % TB == 0
    cos, sin = _rope_tables([pos])
    body = functools.partial(_mixer_sample_body, ret_gamma=tuple(float(v) for v in _ret_gammas()))
    state_spec = pl.BlockSpec((TB, HEADS, HD, HD), lambda i: (i, 0, 0, 0))
    conv_shape = (n, CONV_W - 1, CONV_DIM)
    return pl.pallas_call(
        body,
        grid=(n // TB,),
        in_specs=[_resident((n, D_MODEL)), _resident((1, HD)), _resident((1, HD))]
        + _mixer_weight_specs()
        + [state_spec, state_spec, _resident(conv_shape)],
        out_specs=[pl.BlockSpec((n, D_MODEL), lambda i: (0, 0)), state_spec, state_spec,
                   pl.BlockSpec(conv_shape, lambda i: (0, 0, 0))],
        out_shape=[
            jax.ShapeDtypeStruct((n, D_MODEL), F32),
            jax.ShapeDtypeStruct((n, HEADS, HD, HD), F32),
            jax.ShapeDtypeStruct((n, HEADS, HD, HD), F32),
            jax.ShapeDtypeStruct(conv_shape, F32),
        ],
        scratch_shapes=[pltpu.VMEM((n, OFF_AB), F32), pltpu.VMEM((n, 2 * D_MODEL), F32)]
        + [pltpu.VMEM((n, QK), F32) for _ in range(5)]
        + [pltpu.VMEM((n, HD), F32), pltpu.VMEM((n, HD), F32)]
        + [pltpu.VMEM((n, QK), F32), pltpu.VMEM((n, QK), F32)],
        compiler_params=pltpu.CompilerParams(dimension_semantics=("arbitrary",),
                                             vmem_limit_bytes=VMEM_LIMIT),
        name="mixer_sample",
    )(x, cos, sin, *_mixer_weight_args(w), s_ret, s_gdn, s_conv)


def _pad_lanes(v, n):
    return jnp.pad(v, ((0, 0), (0, n - v.shape[1])))


def kernel(x_prompt, x_sample, state_ret, state_gdn, state_conv, ffn1_pre_g, ffn1_post_g, ffn1_w_gate,
           ffn1_w_up, ffn1_w_down, mix_pre_g, mix_post_g, w_in, ret_norm_g, gdn_conv_w, gdn_a_log,
           gdn_dt_bias, gdn_norm_g, w_ret_branch, w_gdn_branch, w_out, ffn2_pre_g, ffn2_post_g,
           ffn2_w_gate, ffn2_w_up, ffn2_w_down):
    depth = w_in.shape[0]
    b, t, _ = x_prompt.shape
    n_s, t_s, _ = x_sample.shape
    assert t_s == 1
    yp = x_prompt.reshape(b * t, D_MODEL)
    ys = x_sample.reshape(n_s, D_MODEL)
    outs = [[] for _ in range(6)]
    for l in range(depth):
        row = lambda a: a[l][None, :]
        ys, wg1, wu1, wd1 = _ffn_stream(ys, row(ffn1_pre_g), row(ffn1_post_g),
                                        ffn1_w_gate[l], ffn1_w_up[l], ffn1_w_down[l])
        f1 = (row(ffn1_pre_g), row(ffn1_post_g), wg1, wu1, wd1)

        w_in_t = w_in[l].T
        casts = (
            (w_in_t, (0, OFF_AB)), (w_in_t, (OFF_GATES, 2 * D_MODEL)),
            (w_ret_branch[l], None), (w_gdn_branch[l], None), (w_out[l], None),
            (ffn2_w_gate[l], None), (ffn2_w_up[l], None), (ffn2_w_down[l], None),
        )
        yp, (w_main, w_gates, w_rb, w_gb, w_o, g2, u2, d2) = _ffn(yp, *f1, tm=FFN_TM, casts=casts)
        w = {
            "mix_pre_g": row(mix_pre_g), "mix_post_g": row(mix_post_g),
            "w_main": w_main,
            "w_ab": _pad_lanes(w_in[l, :, OFF_AB:OFF_GATES], HD).astype(BF16),
            "w_gates": w_gates,
            "conv_w": gdn_conv_w[l],
            "a_log": _pad_lanes(row(gdn_a_log), HD), "dt_bias": _pad_lanes(row(gdn_dt_bias), HD),
            "ret_norm_g": row(ret_norm_g), "gdn_norm_g": row(gdn_norm_g),
            "w_rb": w_rb, "w_gb": w_gb, "w_out": w_o,
        }
        f2 = (row(ffn2_pre_g), row(ffn2_post_g), g2, u2, d2)

        yp, r1, g1, c1 = _mixer_prompt(yp.reshape(b, t, D_MODEL), w)
        ys, r2, g2s, c2 = _mixer_sample(ys, state_ret[l], state_gdn[l], state_conv[l], w, PAST_LEN)
        yp, ys, _ = _ffn(yp.reshape(b * t, D_MODEL), *f2, tm=FFN_TM, extra=ys)
        for lst, val in zip(outs, (r1, g1, c1, r2, g2s, c2)):
            lst.append(val)
    stacked = [v[0][None] if depth == 1 else jnp.stack(v) for v in outs]
    return (yp.reshape(b, t, D_MODEL), ys.reshape(n_s, t_s, D_MODEL), *stacked)
```

```python
import functools

import numpy as np
import jax
import jax.numpy as jnp
from jax import lax
from jax.experimental import pallas as pl
from jax.experimental.pallas import tpu as pltpu

F32 = jnp.float32
BF16 = jnp.bfloat16

D_MODEL = 1024
D_FF = 2816
HEADS = 4
HD = 128
QK = HEADS * HD
CONV_W = 4
CONV_DIM = 3 * QK
CHUNK = 64
ROPE_BASE = 10000.0
EPS = 1e-6
PAST_LEN = 16384

OFF_RQ, OFF_RK, OFF_RV, OFF_RG = 0, QK, 2 * QK, 3 * QK
OFF_CONV = 4 * QK
OFF_GZ = OFF_CONV + CONV_DIM
OFF_AB = OFF_GZ + QK
OFF_GATES = OFF_AB + 2 * HEADS
D_IN = OFF_GATES + 2 * D_MODEL

TM = 256
NB = 2
TB = 8
FFN_TM = 1024
FFN_SUB = 256
FF_CHUNK = 256
CONV_PAD = 8
BF16_ROWS = 16
F32_ROWS = 8

VMEM_LIMIT = 56 * 1024 * 1024


def _silu(x):
    return x * jax.nn.sigmoid(x)


def _rms(x, g):
    return x * lax.rsqrt(jnp.mean(x * x, axis=-1, keepdims=True) + EPS) * g


def _dot(a, b):
    return jnp.dot(a.astype(BF16), b.astype(BF16), preferred_element_type=F32)


def _dot_nt(a, b):
    return lax.dot_general(a.astype(BF16), b.astype(BF16), (((1,), (1,)), ((), ())),
                           preferred_element_type=F32)


def _dot_tn(a, b):
    return lax.dot_general(a.astype(BF16), b.astype(BF16), (((0,), (0,)), ((), ())),
                           preferred_element_type=F32)


def _split3(a):
    hi = a.astype(BF16)
    r = a - hi.astype(F32)
    mid = r.astype(BF16)
    lo = (r - mid.astype(F32)).astype(BF16)
    return hi, mid, lo


def _resident(shape):
    nd = len(shape)
    return pl.BlockSpec(shape, lambda *_: (0,) * nd, pipeline_mode=pl.Buffered(1))


def _ffn_rows(x_ref, o_ref, pre_ref, post_ref, wg_ref, wu_ref, wd_ref):
    tm = x_ref.shape[0]
    sub = min(tm, FFN_SUB)
    parts = [slice(r, r + sub) for r in range(0, tm, sub)]
    xs = [x_ref[p, :] for p in parts]
    hs = [_rms(x, pre_ref[...]).astype(BF16) for x in xs]
    acts = []
    for h in hs:
        g = jnp.dot(h, wg_ref[...], preferred_element_type=F32)
        u = jnp.dot(h, wu_ref[...], preferred_element_type=F32)
        acts.append((_silu(g) * u).astype(BF16))
    for p, x, a in zip(parts, xs, acts):
        y = jnp.dot(a, wd_ref[...], preferred_element_type=F32)
        o_ref[p, :] = x + 0.5 * _rms(y, post_ref[...])


def _ffn_body(x_ref, pre_ref, post_ref, wg_ref, wu_ref, wd_ref, *rest, cast_transposed, has_extra):
    rest = list(rest)
    xe_ref = rest.pop(0) if has_extra else None
    n_cast = len(cast_transposed)
    cast_in, o_ref = rest[:n_cast], rest[n_cast]
    rest = rest[n_cast + 1:]
    oe_ref = rest.pop(0) if has_extra else None
    cast_out = rest
    weights = (pre_ref, post_ref, wg_ref, wu_ref, wd_ref)
    _ffn_rows(x_ref, o_ref, *weights)
    for src, dst, transposed in zip(cast_in, cast_out, cast_transposed):
        blk = src[...]
        dst[...] = (blk.T if transposed else blk).astype(BF16)
    if has_extra:
        @pl.when(pl.program_id(0) == pl.num_programs(0) - 1)
        def _():
            _ffn_rows(xe_ref, oe_ref, *weights)


def _cast_row_block(rows, steps):
    rb = BF16_ROWS
    while rows % rb or rows // rb > steps:
        rb += BF16_ROWS
    return rb


def _ffn(x, pre_g, post_g, wg, wu, wd, tm, casts=(), extra=None):
    n = x.shape[0]
    assert n % tm == 0
    steps = n // tm
    in_specs = [
        pl.BlockSpec((tm, D_MODEL), lambda i: (i, 0)),
        _resident((1, D_MODEL)), _resident((1, D_MODEL)),
        _resident((D_MODEL, D_FF)), _resident((D_MODEL, D_FF)), _resident((D_FF, D_MODEL)),
    ]
    out_specs = [pl.BlockSpec((tm, D_MODEL), lambda i: (i, 0))]
    out_shape = [jax.ShapeDtypeStruct((n, D_MODEL), F32)]
    operands = [x, pre_g, post_g, wg, wu, wd]
    if extra is not None:
        in_specs.append(_resident(extra.shape))
        operands.append(extra)
    n_fixed_in = len(in_specs)
    for arr, region in casts:
        rows, width = arr.shape
        if region is None:
            rb = _cast_row_block(rows, steps)
            last = rows // rb - 1
            spec = pl.BlockSpec((rb, width), lambda i, last=last: (jnp.minimum(i, last), 0))
            in_specs.append(spec)
            out_specs.append(spec)
            out_shape.append(jax.ShapeDtypeStruct((rows, width), BF16))
        else:
            row0, n_rows = region
            cb = HD
            while n_rows % cb or n_rows // cb > steps:
                cb += HD
            last = n_rows // cb - 1
            in_specs.append(pl.BlockSpec(
                (pl.Element(cb), pl.Element(width)),
                lambda i, row0=row0, last=last, cb=cb: (
                    pl.multiple_of(row0 + cb * jnp.minimum(i, last), F32_ROWS), 0)))
            out_specs.append(pl.BlockSpec((width, cb), lambda i, last=last: (0, jnp.minimum(i, last))))
            out_shape.append(jax.ShapeDtypeStruct((width, n_rows), BF16))
    assert len(in_specs) == n_fixed_in + len(casts)
    if extra is not None:
        out_specs.insert(1, pl.BlockSpec(extra.shape, lambda i: (0, 0)))
        out_shape.insert(1, jax.ShapeDtypeStruct(extra.shape, F32))
    outs = pl.pallas_call(
        functools.partial(_ffn_body, cast_transposed=tuple(c[1] is not None for c in casts),
                          has_extra=extra is not None),
        grid=(steps,),
        in_specs=in_specs,
        out_specs=out_specs,
        out_shape=out_shape,
        compiler_params=pltpu.CompilerParams(dimension_semantics=("arbitrary",),
                                             vmem_limit_bytes=VMEM_LIMIT),
        name="ffn",
    )(*operands, *[c[0] for c in casts])
    if extra is not None:
        return outs[0], outs[1], list(outs[2:])
    return outs[0], list(outs[1:])


def _ffn_stream_body(x_ref, pre_ref, post_ref, wg_ref, wu_ref, wd_ref,
                     o_ref, wg16_ref, wu16_ref, wd16_ref, h_s, acc_s):
    c = pl.program_id(0)

    @pl.when(c == 0)
    def _():
        h_s[...] = _rms(x_ref[...], pre_ref[...]).astype(BF16)
        acc_s[...] = jnp.zeros_like(acc_s)

    wg = wg_ref[...].astype(BF16)
    wu = wu_ref[...].astype(BF16)
    wd = wd_ref[...].astype(BF16)
    wg16_ref[...] = wg
    wu16_ref[...] = wu
    wd16_ref[...] = wd
    h = h_s[...]
    g = jnp.dot(h, wg, preferred_element_type=F32)
    u = jnp.dot(h, wu, preferred_element_type=F32)
    a = (_silu(g) * u).astype(BF16)
    acc_s[...] += jnp.dot(a, wd, preferred_element_type=F32)

    @pl.when(c == pl.num_programs(0) - 1)
    def _():
        o_ref[...] = x_ref[...] + 0.5 * _rms(acc_s[...], post_ref[...])


def _ffn_stream(x, pre_g, post_g, wg, wu, wd):
    n = x.shape[0]
    assert D_FF % FF_CHUNK == 0
    col = pl.BlockSpec((D_MODEL, FF_CHUNK), lambda c: (0, c))
    row = pl.BlockSpec((FF_CHUNK, D_MODEL), lambda c: (c, 0))
    return pl.pallas_call(
        _ffn_stream_body,
        grid=(D_FF // FF_CHUNK,),
        in_specs=[_resident((n, D_MODEL)), _resident((1, D_MODEL)), _resident((1, D_MODEL)), col, col, row],
        out_specs=[pl.BlockSpec((n, D_MODEL), lambda c: (0, 0)), col, col, row],
        out_shape=[
            jax.ShapeDtypeStruct((n, D_MODEL), F32),
            jax.ShapeDtypeStruct((D_MODEL, D_FF), BF16),
            jax.ShapeDtypeStruct((D_MODEL, D_FF), BF16),
            jax.ShapeDtypeStruct((D_FF, D_MODEL), BF16),
        ],
        scratch_shapes=[pltpu.VMEM((n, D_MODEL), BF16), pltpu.VMEM((n, D_MODEL), F32)],
        compiler_params=pltpu.CompilerParams(dimension_semantics=("arbitrary",),
                                             vmem_limit_bytes=VMEM_LIMIT),
        name="ffn_stream",
    )(x, pre_g, post_g, wg, wu, wd)


def _rope(x, cos, sin_signed):
    return x * cos + pltpu.roll(x, HD // 2, 1) * sin_signed


def _ret_out_norm(o, g_row, gate_act):
    mu = jnp.mean(o, axis=-1, keepdims=True)
    d = o - mu
    var = jnp.mean(d * d, axis=-1, keepdims=True)
    return gate_act * (d * lax.rsqrt(var + EPS) * g_row)


def _gdn_out_norm(o, g_row, gate_act):
    return o * lax.rsqrt(jnp.mean(o * o, axis=-1, keepdims=True) + EPS) * g_row * gate_act


def _l2norm(x, scale):
    return x * (lax.rsqrt(jnp.sum(x * x, axis=-1, keepdims=True) + EPS) * scale)


def _softplus(x):
    return jnp.maximum(x, 0.0) + jnp.log(1.0 + jnp.exp(-jnp.abs(x)))


def _merge(x, o_r, o_g, sig_gates, wrb_ref, wgb_ref, wout_ref, post_ref):
    y = (sig_gates[:, :D_MODEL] * _dot(o_r, wrb_ref[...])
         + sig_gates[:, D_MODEL:] * _dot(o_g, wgb_ref[...]))
    m = _dot(y, wout_ref[...])
    return x + _rms(m, post_ref[...])


def _unit_lower_inverse_many(a_list, masks):
    eye, m16, off32, off64 = masks
    ad = [a * m16 for a in a_list]
    x = [eye - v for v in ad]
    p = [_dot(v, v) for v in ad]
    for level in range(3):
        x = [xi + _dot(xi, pi) for xi, pi in zip(x, p)]
        if level < 2:
            p = [_dot(pi, pi) for pi in p]
    for m in (off32, off64):
        t = [_dot(xi, a * m) for xi, a in zip(x, a_list)]
        x = [xi - _dot(ti, xi) for xi, ti in zip(x, t)]
    return x


def _inverse_masks():
    r = lax.broadcasted_iota(jnp.int32, (CHUNK, CHUNK), 0)
    c = lax.broadcasted_iota(jnp.int32, (CHUNK, CHUNK), 1)
    one, zero = jnp.float32(1.0), jnp.float32(0.0)
    eye = jnp.where(r == c, one, zero)
    same16 = (r >> 4) == (c >> 4)
    same32 = (r >> 5) == (c >> 5)
    m16 = jnp.where(same16, one, zero)
    m32 = jnp.where(same32, one, zero)
    return eye, m16, m32 - m16, 1.0 - m32


def _retention_tile(pm, cos, sin, rd_ref, ku_ref, retg_ref, s_ref, tile_decay):
    ri = lax.broadcasted_iota(jnp.int32, (TM, TM), 0)
    ci = lax.broadcasted_iota(jnp.int32, (TM, TM), 1)
    causal = ri >= ci
    heads = []
    for hh in range(HEADS):
        sl = slice(hh * HD, (hh + 1) * HD)
        rq = pm[:, OFF_RQ + hh * HD:OFF_RQ + (hh + 1) * HD]
        rk = pm[:, OFF_RK + hh * HD:OFF_RK + (hh + 1) * HD]
        v = pm[:, OFF_RV + hh * HD:OFF_RV + (hh + 1) * HD]
        rg = pm[:, OFF_RG + hh * HD:OFF_RG + (hh + 1) * HD]
        qs = _rope(rq, cos, sin) * rd_ref[hh]
        ku = _rope(rk, cos, sin) * ku_ref[hh]
        s = s_ref[hh]
        sc = jnp.where(causal, _dot_nt(qs, ku), 0.0)
        o = _dot(sc, v) + _dot(qs, s)
        s_ref[hh] = tile_decay[hh] * (s + _dot_tn(ku, v))
        heads.append(_ret_out_norm(o, retg_ref[:, sl], _silu(rg)))
    return jnp.concatenate(heads, axis=1)


def _short_conv_tile(u, buf, convw_ref, tail_ref):
    buf[CONV_PAD:CONV_PAD + TM, :] = u
    acc = buf[CONV_PAD - 3:CONV_PAD - 3 + TM, :] * convw_ref[0:1, :]
    for i in range(1, CONV_W):
        acc = acc + buf[CONV_PAD - 3 + i:CONV_PAD - 3 + i + TM, :] * convw_ref[i:i + 1, :]
    tail = buf[CONV_PAD + TM - 3:CONV_PAD + TM, :]
    tail_ref[...] = tail
    buf[CONV_PAD - 3:CONV_PAD, :] = tail
    return _silu(acc)


def _gdn_decay_beta(ab, alog_ref, dtb_ref):
    g_all = -jnp.exp(alog_ref[...]) * _softplus(ab + dtb_ref[...])
    beta_all = jax.nn.sigmoid(ab)
    return g_all, beta_all


def _gdn_tiles(cqs, abs_, gzs, ltri, alog_ref, dtb_ref, gdng_ref, s_refs):
    seqs = range(len(cqs))
    dd = functools.partial(jnp.dot, preferred_element_type=F32)
    gcum, gcum_t, beta_all = [], [], []
    for b in seqs:
        g_all, beta = _gdn_decay_beta(abs_[b], alog_ref, dtb_ref)
        beta_all.append(beta)
        g_hi, g_mid, g_lo = _split3(g_all)
        gcum.append(dd(ltri, g_hi) + (dd(ltri, g_mid) + dd(ltri, g_lo)))
        gcum_t.append(gcum[b].T)

    r64 = lax.broadcasted_iota(jnp.int32, (CHUNK, CHUNK), 0)
    c64 = lax.broadcasted_iota(jnp.int32, (CHUNK, CHUNK), 1)
    causal64 = r64 >= c64
    strict64 = r64 > c64
    masks = _inverse_masks()

    n_ch = TM // CHUNK
    rows = [slice(c * CHUNK, (c + 1) * CHUNK) for c in range(n_ch)]
    bh = [(b, hh) for b in seqs for hh in range(HEADS)]
    probs = [(b, hh, c) for b, hh in bh for c in range(n_ch)]
    gq = {(b, hh): _l2norm(cqs[b][:, hh * HD:(hh + 1) * HD], HD ** -0.5) for b, hh in bh}
    gk = {(b, hh): _l2norm(cqs[b][:, QK + hh * HD:QK + (hh + 1) * HD], 1.0) for b, hh in bh}
    gv = {(b, hh): cqs[b][:, 2 * QK + hh * HD:2 * QK + (hh + 1) * HD] for b, hh in bh}
    qc = {(b, hh, c): gq[b, hh][rows[c]] for b, hh, c in probs}
    kc = {(b, hh, c): gk[b, hh][rows[c]] for b, hh, c in probs}
    vc = {(b, hh, c): gv[b, hh][rows[c]] for b, hh, c in probs}
    gc = {(b, hh, c): gcum[b][rows[c], hh:hh + 1] for b, hh, c in probs}
    bcol = {(b, hh, c): beta_all[b][rows[c], HEADS + hh:HEADS + hh + 1] for b, hh, c in probs}
    dm = {(b, hh, c): jnp.where(
        causal64, jnp.exp(jnp.minimum(gc[b, hh, c] - gcum_t[b][hh:hh + 1, rows[c]], 0.0)), 0.0)
        for b, hh, c in probs}
    kk = {p: _dot_nt(kc[p], kc[p]) for p in probs}
    qk = {p: _dot_nt(qc[p], kc[p]) * dm[p] for p in probs}
    a_mats = [jnp.where(strict64, bcol[p] * dm[p] * kk[p], 0.0) for p in probs]
    tinv = dict(zip(probs, _unit_lower_inverse_many(a_mats, masks)))
    eg = {p: jnp.exp(gc[p]) for p in probs}
    uw = {p: _dot(tinv[p], jnp.concatenate([bcol[p] * vc[p], (bcol[p] * eg[p]) * kc[p]], axis=1))
          for p in probs}
    gl = {p: gc[p][CHUNK - 1:CHUNK, :] for p in probs}
    kdw = {p: _dot_tn(kc[p] * jnp.exp(gl[p] - gc[p]), uw[p]) for p in probs}
    qkw = {p: _dot(qk[p], uw[p]) for p in probs}
    lhs = {p: jnp.concatenate([kdw[p][:, HD:], eg[p] * qc[p] - qkw[p][:, HD:]], axis=0).astype(BF16)
           for p in probs}

    s = {(b, hh): s_refs[b][hh] for b, hh in bh}
    outs = {k: [] for k in bh}
    for c in range(n_ch):
        for b, hh in bh:
            p = (b, hh, c)
            ps = _dot(lhs[p], s[b, hh])
            outs[b, hh].append(ps[HD:, :] + qkw[p][:, :HD])
            s[b, hh] = jnp.exp(gl[p]) * s[b, hh] + (kdw[p][:, :HD] - ps[:HD, :])
    o_gs = []
    for b in seqs:
        heads = []
        for hh in range(HEADS):
            s_refs[b][hh] = s[b, hh]
            o = jnp.concatenate(outs[b, hh], axis=0)
            heads.append(_gdn_out_norm(o, gdng_ref[...], _silu(gzs[b][:, hh * HD:(hh + 1) * HD])))
        o_gs.append(jnp.concatenate(heads, axis=1))
    return o_gs


def _mixer_prompt_body(x_ref, cos_ref, sin_ref, rd_ref, ku_ref, ltri_ref,
                       pre_ref, post_ref, wmain_ref, wab_ref, wgates_ref, convw_ref, alog_ref, dtb_ref,
                       retg_ref, gdng_ref, wrb_ref, wgb_ref, wout_ref,
                       y_ref, sret_ref, sgdn_ref, conv_ref,
                       cbuf, *, ret_tile_decay):
    t = pl.program_id(1)
    nb = x_ref.shape[0]

    @pl.when(t == 0)
    def _():
        sret_ref[...] = jnp.zeros_like(sret_ref)
        sgdn_ref[...] = jnp.zeros_like(sgdn_ref)
        cbuf[:, 0:CONV_PAD, :] = jnp.zeros((nb, CONV_PAD, CONV_DIM), F32)

    cos = cos_ref[...]
    sin = sin_ref[...]
    ltri = ltri_ref[...]
    seqs = range(nb)
    xs = [x_ref[b] for b in seqs]
    hs = [_rms(xs[b], pre_ref[...]).astype(BF16) for b in seqs]
    pgs = [jnp.dot(hs[b], wmain_ref[:, OFF_CONV:OFF_AB], preferred_element_type=F32) for b in seqs]
    abs_ = [jnp.dot(hs[b], wab_ref[...], preferred_element_type=F32) for b in seqs]
    cqs = [_short_conv_tile(pgs[b][:, 0:CONV_DIM], cbuf.at[b], convw_ref, conv_ref.at[b]) for b in seqs]
    pms = [jnp.dot(hs[b], wmain_ref[:, 0:OFF_CONV], preferred_element_type=F32) for b in seqs]
    o_rs = [_retention_tile(pms[b], cos, sin, rd_ref, ku_ref, retg_ref, sret_ref.at[b], ret_tile_decay)
            for b in seqs]
    o_gs = _gdn_tiles(cqs, abs_, [pgs[b][:, CONV_DIM:] for b in seqs], ltri, alog_ref, dtb_ref, gdng_ref,
                      [sgdn_ref.at[b] for b in seqs])
    for b in seqs:
        gates = jnp.dot(hs[b], wgates_ref[...], preferred_element_type=F32)
        y_ref[b] = _merge(xs[b], o_rs[b], o_gs[b], jax.nn.sigmoid(gates), wrb_ref, wgb_ref, wout_ref, post_ref)


def _ret_gammas():
    return 1.0 - 2.0 ** (-5.0 - np.arange(HEADS, dtype=np.float64))


def _rope_tables(pos):
    inv = ROPE_BASE ** (-np.arange(0, HD, 2, dtype=np.float64) / HD)
    ang = np.asarray(pos, np.float64)[:, None] * inv[None, :]
    cos = np.concatenate([np.cos(ang), np.cos(ang)], axis=1)
    sin = np.concatenate([-np.sin(ang), np.sin(ang)], axis=1)
    return jnp.asarray(cos, F32), jnp.asarray(sin, F32)


def _mixer_weight_specs():
    return [
        _resident((1, D_MODEL)), _resident((1, D_MODEL)),
        _resident((D_MODEL, OFF_AB)), _resident((D_MODEL, HD)), _resident((D_MODEL, 2 * D_MODEL)),
        _resident((CONV_W, CONV_DIM)), _resident((1, HD)), _resident((1, HD)),
        _resident((1, QK)), _resident((1, HD)),
        _resident((QK, D_MODEL)), _resident((QK, D_MODEL)), _resident((D_MODEL, D_MODEL)),
    ]


def _mixer_weight_args(w):
    return (w["mix_pre_g"], w["mix_post_g"], w["w_main"], w["w_ab"], w["w_gates"], w["conv_w"],
            w["a_log"], w["dt_bias"], w["ret_norm_g"], w["gdn_norm_g"], w["w_rb"], w["w_gb"], w["w_out"])


def _mixer_prompt(x, w):
    b, t, _ = x.shape
    assert t % TM == 0 and b % NB == 0
    cos, sin = _rope_tables(np.arange(t))
    gam = _ret_gammas()
    i1 = np.arange(1, TM + 1, dtype=np.float64)
    rd = np.broadcast_to((gam[:, None] ** i1[None, :])[:, :, None], (HEADS, TM, HD))
    ku = np.broadcast_to((HD ** -0.5 * gam[:, None] ** (-i1[None, :]))[:, :, None], (HEADS, TM, HD))
    tile_decay = tuple(float(v) for v in gam ** TM)
    r = np.arange(TM)
    ltri = ((r[:, None] >= r[None, :]) & (r[:, None] // CHUNK == r[None, :] // CHUNK))

    body = functools.partial(_mixer_prompt_body, ret_tile_decay=tile_decay)
    state_spec = pl.BlockSpec((NB, HEADS, HD, HD), lambda i, j: (i, 0, 0, 0))
    return pl.pallas_call(
        body,
        grid=(b // NB, t // TM),
        in_specs=[
            pl.BlockSpec((NB, TM, D_MODEL), lambda i, j: (i, j, 0)),
            pl.BlockSpec((TM, HD), lambda i, j: (j, 0)),
            pl.BlockSpec((TM, HD), lambda i, j: (j, 0)),
            _resident((HEADS, TM, HD)), _resident((HEADS, TM, HD)), _resident((TM, TM)),
        ] + _mixer_weight_specs(),
        out_specs=[
            pl.BlockSpec((NB, TM, D_MODEL), lambda i, j: (i, j, 0)),
            state_spec, state_spec,
            pl.BlockSpec((NB, CONV_W - 1, CONV_DIM), lambda i, j: (i, 0, 0)),
        ],
        out_shape=[
            jax.ShapeDtypeStruct((b, t, D_MODEL), F32),
            jax.ShapeDtypeStruct((b, HEADS, HD, HD), F32),
            jax.ShapeDtypeStruct((b, HEADS, HD, HD), F32),
            jax.ShapeDtypeStruct((b, CONV_W - 1, CONV_DIM), F32),
        ],
        scratch_shapes=[pltpu.VMEM((NB, CONV_PAD + TM, CONV_DIM), F32)],
        compiler_params=pltpu.CompilerParams(dimension_semantics=("arbitrary", "arbitrary"),
                                             vmem_limit_bytes=VMEM_LIMIT),
        name="mixer_prompt",
    )(x, cos, sin, jnp.asarray(rd, F32), jnp.asarray(ku, F32), jnp.asarray(ltri, BF16),
      *_mixer_weight_args(w))


def _pick_rows(rows):
    ri = lax.broadcasted_iota(jnp.int32, rows[0].shape, 0)
    out = rows[0]
    for j in range(1, len(rows)):
        out = jnp.where(ri == j, rows[j], out)
    return out


def _mixer_sample_body(x_ref, cos_ref, sin_ref,
                       pre_ref, post_ref, wmain_ref, wab_ref, wgates_ref, convw_ref, alog_ref, dtb_ref,
                       retg_ref, gdng_ref, wrb_ref, wgb_ref, wout_ref,
                       sret_in, sgdn_in, conv_in,
                       y_ref, sret_out, sgdn_out, conv_out,
                       pm_s, gates_s, rq_s, rk_s, gq_s, gk_s, gv_s, eg_s, beta_s, or_s, og_s, *, ret_gamma):
    i = pl.program_id(0)

    @pl.when(i == 0)
    def _():
        h = _rms(x_ref[...], pre_ref[...]).astype(BF16)
        pm = jnp.dot(h, wmain_ref[...], preferred_element_type=F32)
        ab = jnp.dot(h, wab_ref[...], preferred_element_type=F32)
        pm_s[...] = pm
        gates_s[...] = jnp.dot(h, wgates_ref[...], preferred_element_type=F32)
        cos = cos_ref[...]
        sin = sin_ref[...]
        cin = pm[:, OFF_CONV:OFF_CONV + CONV_DIM]
        acc = cin * convw_ref[CONV_W - 1:CONV_W, :]
        for r in range(CONV_W - 1):
            acc = acc + conv_in[r] * convw_ref[r:r + 1, :]
        for r in range(CONV_W - 2):
            conv_out[r] = conv_in[r + 1]
        conv_out[CONV_W - 2] = cin
        cq = _silu(acc)
        g_all, beta_all = _gdn_decay_beta(ab, alog_ref, dtb_ref)
        eg_s[...] = jnp.exp(g_all)
        beta_s[...] = beta_all
        for hh in range(HEADS):
            sl = slice(hh * HD, (hh + 1) * HD)
            rq_s[:, sl] = _rope(pm[:, OFF_RQ + hh * HD:OFF_RQ + (hh + 1) * HD], cos, sin)
            rk_s[:, sl] = _rope(pm[:, OFF_RK + hh * HD:OFF_RK + (hh + 1) * HD], cos, sin) * (HD ** -0.5)
            gq_s[:, sl] = _l2norm(cq[:, hh * HD:(hh + 1) * HD], HD ** -0.5)
            gk_s[:, sl] = _l2norm(cq[:, QK + hh * HD:QK + (hh + 1) * HD], 1.0)
        gv_s[...] = cq[:, 2 * QK:3 * QK]

    rows = pl.ds(pl.multiple_of(i * TB, TB), TB)
    eg_all = eg_s[rows, :]
    beta_all = beta_s[rows, :]
    for hh in range(HEADS):
        sl = slice(hh * HD, (hh + 1) * HD)
        q = rq_s[rows, sl]
        k_t = rk_s[rows, sl].T
        v = pm_s[rows, OFF_RV + hh * HD:OFF_RV + (hh + 1) * HD]
        o_rows = []
        for j in range(TB):
            s = ret_gamma[hh] * sret_in[j, hh] + k_t[:, j:j + 1] * v[j:j + 1, :]
            sret_out[j, hh] = s
            o_rows.append(_dot(q, s))
        or_s[rows, sl] = _pick_rows(o_rows)

        gq = gq_s[rows, sl]
        gk = gk_s[rows, sl]
        gk_t = gk.T
        gv = gv_s[rows, sl]
        o_rows = []
        for j in range(TB):
            s = sgdn_in[j, hh]
            eg = eg_all[j:j + 1, hh:hh + 1]
            beta = beta_all[j:j + 1, HEADS + hh:HEADS + hh + 1]
            ks = _dot(gk, s)[j:j + 1, :]
            u = beta * gv[j:j + 1, :] - (beta * eg) * ks
            s = eg * s + gk_t[:, j:j + 1] * u
            sgdn_out[j, hh] = s
            o_rows.append(_dot(gq, s))
        og_s[rows, sl] = _pick_rows(o_rows)

    @pl.when(i == pl.num_programs(0) - 1)
    def _():
        o_r = []
        o_g = []
        for hh in range(HEADS):
            sl = slice(hh * HD, (hh + 1) * HD)
            o_r.append(_ret_out_norm(or_s[:, sl], retg_ref[:, sl],
                                     _silu(pm_s[:, OFF_RG + hh * HD:OFF_RG + (hh + 1) * HD])))
            o_g.append(_gdn_out_norm(og_s[:, sl], gdng_ref[...],
                                     _silu(pm_s[:, OFF_GZ + hh * HD:OFF_GZ + (hh + 1) * HD])))
        y_ref[...] = _merge(x_ref[...], jnp.concatenate(o_r, axis=1), jnp.concatenate(o_g, axis=1),
                            jax.nn.sigmoid(gates_s[...]), wrb_ref, wgb_ref, wout_ref, post_ref)


def _mixer_sample(x, s_ret, s_gdn, s_conv, w, pos):
    n = x.shape[0]
    assert n % TB == 0
    cos, sin = _rope_tables([pos])
    body = functools.partial(_mixer_sample_body, ret_gamma=tuple(float(v) for v in _ret_gammas()))
    state_spec = pl.BlockSpec((TB, HEADS, HD, HD), lambda i: (i, 0, 0, 0))
    conv_shape = (CONV_W - 1, n, CONV_DIM)
    return pl.pallas_call(
        body,
        grid=(n // TB,),
        in_specs=[_resident((n, D_MODEL)), _resident((1, HD)), _resident((1, HD))]
        + _mixer_weight_specs()
        + [state_spec, state_spec, _resident(conv_shape)],
        out_specs=[pl.BlockSpec((n, D_MODEL), lambda i: (0, 0)), state_spec, state_spec,
                   pl.BlockSpec(conv_shape, lambda i: (0, 0, 0))],
        out_shape=[
            jax.ShapeDtypeStruct((n, D_MODEL), F32),
            jax.ShapeDtypeStruct((n, HEADS, HD, HD), F32),
            jax.ShapeDtypeStruct((n, HEADS, HD, HD), F32),
            jax.ShapeDtypeStruct(conv_shape, F32),
        ],
        scratch_shapes=[pltpu.VMEM((n, OFF_AB), F32), pltpu.VMEM((n, 2 * D_MODEL), F32)]
        + [pltpu.VMEM((n, QK), F32) for _ in range(5)]
        + [pltpu.VMEM((n, HD), F32), pltpu.VMEM((n, HD), F32)]
        + [pltpu.VMEM((n, QK), F32), pltpu.VMEM((n, QK), F32)],
        compiler_params=pltpu.CompilerParams(dimension_semantics=("arbitrary",),
                                             vmem_limit_bytes=VMEM_LIMIT),
        name="mixer_sample",
    )(x, cos, sin, *_mixer_weight_args(w), s_ret, s_gdn, s_conv)


def _pad_lanes(v, n):
    return jnp.pad(v, ((0, 0), (0, n - v.shape[1])))


def kernel(x_prompt, x_sample, state_ret, state_gdn, state_conv, ffn1_pre_g, ffn1_post_g, ffn1_w_gate,
           ffn1_w_up, ffn1_w_down, mix_pre_g, mix_post_g, w_in, ret_norm_g, gdn_conv_w, gdn_a_log,
           gdn_dt_bias, gdn_norm_g, w_ret_branch, w_gdn_branch, w_out, ffn2_pre_g, ffn2_post_g,
           ffn2_w_gate, ffn2_w_up, ffn2_w_down):
    depth = w_in.shape[0]
    b, t, _ = x_prompt.shape
    n_s, t_s, _ = x_sample.shape
    assert t_s == 1
    yp = x_prompt.reshape(b * t, D_MODEL)
    ys = x_sample.reshape(n_s, D_MODEL)
    outs = [[] for _ in range(6)]
    for l in range(depth):
        row = lambda a: a[l][None, :]
        ys, wg1, wu1, wd1 = _ffn_stream(ys, row(ffn1_pre_g), row(ffn1_post_g),
                                        ffn1_w_gate[l], ffn1_w_up[l], ffn1_w_down[l])
        f1 = (row(ffn1_pre_g), row(ffn1_post_g), wg1, wu1, wd1)

        w_in_t = w_in[l].T
        casts = (
            (w_in_t, (0, OFF_AB)), (w_in_t, (OFF_GATES, 2 * D_MODEL)),
            (w_ret_branch[l], None), (w_gdn_branch[l], None), (w_out[l], None),
            (ffn2_w_gate[l], None), (ffn2_w_up[l], None), (ffn2_w_down[l], None),
        )
        yp, (w_main, w_gates, w_rb, w_gb, w_o, g2, u2, d2) = _ffn(yp, *f1, tm=FFN_TM, casts=casts)
        w = {
            "mix_pre_g": row(mix_pre_g), "mix_post_g": row(mix_post_g),
            "w_main": w_main,
            "w_ab": _pad_lanes(w_in[l, :, OFF_AB:OFF_GATES], HD).astype(BF16),
            "w_gates": w_gates,
            "conv_w": gdn_conv_w[l],
            "a_log": _pad_lanes(row(gdn_a_log), HD), "dt_bias": _pad_lanes(row(gdn_dt_bias), HD),
            "ret_norm_g": row(ret_norm_g), "gdn_norm_g": row(gdn_norm_g),
            "w_rb": w_rb, "w_gb": w_gb, "w_out": w_o,
        }
        f2 = (row(ffn2_pre_g), row(ffn2_post_g), g2, u2, d2)

        yp, r1, g1, c1 = _mixer_prompt(yp.reshape(b, t, D_MODEL), w)
        ys, r2, g2s, c2 = _mixer_sample(ys, state_ret[l], state_gdn[l],
                                        jnp.swapaxes(state_conv[l], 0, 1), w, PAST_LEN)
        c2 = jnp.swapaxes(c2, 0, 1)
        yp, ys, _ = _ffn(yp.reshape(b * t, D_MODEL), *f2, tm=FFN_TM, extra=ys)
        for lst, val in zip(outs, (r1, g1, c1, r2, g2s, c2)):
            lst.append(val)
    stacked = [v[0][None] if depth == 1 else jnp.stack(v) for v in outs]
    return (yp.reshape(b, t, D_MODEL), ys.reshape(n_s, t_s, D_MODEL), *stacked)
```

```python
import functools

import numpy as np
import jax
import jax.numpy as jnp
from jax import lax
from jax.experimental import pallas as pl
from jax.experimental.pallas import tpu as pltpu

F32 = jnp.float32
BF16 = jnp.bfloat16

D_MODEL = 1024
D_FF = 2816
HEADS = 4
HD = 128
QK = HEADS * HD
CONV_W = 4
CONV_DIM = 3 * QK
CHUNK = 64
ROPE_BASE = 10000.0
EPS = 1e-6
PAST_LEN = 16384

OFF_RQ, OFF_RK, OFF_RV, OFF_RG = 0, QK, 2 * QK, 3 * QK
OFF_CONV = 4 * QK
OFF_GZ = OFF_CONV + CONV_DIM
OFF_AB = OFF_GZ + QK
OFF_GATES = OFF_AB + 2 * HEADS
D_IN = OFF_GATES + 2 * D_MODEL

TM = 256
NB = 2
TB = 8
FFN_TM = 1024
FFN_SUB = 256
FF_CHUNK = 256
CONV_PAD = 8
BF16_ROWS = 16
F32_ROWS = 8

VMEM_LIMIT = 56 * 1024 * 1024


def _silu(x):
    return x * jax.nn.sigmoid(x)


def _rms(x, g):
    return x * lax.rsqrt(jnp.mean(x * x, axis=-1, keepdims=True) + EPS) * g


def _dot(a, b):
    return jnp.dot(a.astype(BF16), b.astype(BF16), preferred_element_type=F32)


def _dot_nt(a, b):
    return lax.dot_general(a.astype(BF16), b.astype(BF16), (((1,), (1,)), ((), ())),
                           preferred_element_type=F32)


def _dot_tn(a, b):
    return lax.dot_general(a.astype(BF16), b.astype(BF16), (((0,), (0,)), ((), ())),
                           preferred_element_type=F32)


def _split3(a):
    hi = a.astype(BF16)
    r = a - hi.astype(F32)
    mid = r.astype(BF16)
    lo = (r - mid.astype(F32)).astype(BF16)
    return hi, mid, lo


def _resident(shape):
    nd = len(shape)
    return pl.BlockSpec(shape, lambda *_: (0,) * nd, pipeline_mode=pl.Buffered(1))


def _ffn_rows(x_ref, o_ref, pre_ref, post_ref, wg_ref, wu_ref, wd_ref):
    tm = x_ref.shape[0]
    sub = min(tm, FFN_SUB)
    parts = [slice(r, r + sub) for r in range(0, tm, sub)]
    xs = [x_ref[p, :] for p in parts]
    hs = [_rms(x, pre_ref[...]).astype(BF16) for x in xs]
    acts = []
    for h in hs:
        g = jnp.dot(h, wg_ref[...], preferred_element_type=F32)
        u = jnp.dot(h, wu_ref[...], preferred_element_type=F32)
        acts.append((_silu(g) * u).astype(BF16))
    for p, x, a in zip(parts, xs, acts):
        y = jnp.dot(a, wd_ref[...], preferred_element_type=F32)
        o_ref[p, :] = x + 0.5 * _rms(y, post_ref[...])


def _ffn_body(x_ref, pre_ref, post_ref, wg_ref, wu_ref, wd_ref, *rest, cast_transposed, has_extra):
    rest = list(rest)
    xe_ref = rest.pop(0) if has_extra else None
    n_cast = len(cast_transposed)
    cast_in, o_ref = rest[:n_cast], rest[n_cast]
    rest = rest[n_cast + 1:]
    oe_ref = rest.pop(0) if has_extra else None
    cast_out = rest
    weights = (pre_ref, post_ref, wg_ref, wu_ref, wd_ref)
    _ffn_rows(x_ref, o_ref, *weights)
    for src, dst, transposed in zip(cast_in, cast_out, cast_transposed):
        blk = src[...]
        dst[...] = (blk.T if transposed else blk).astype(BF16)
    if has_extra:
        @pl.when(pl.program_id(0) == pl.num_programs(0) - 1)
        def _():
            _ffn_rows(xe_ref, oe_ref, *weights)


def _cast_row_block(rows, steps):
    rb = BF16_ROWS
    while rows % rb or rows // rb > steps:
        rb += BF16_ROWS
    return rb


def _ffn(x, pre_g, post_g, wg, wu, wd, tm, casts=(), extra=None):
    n = x.shape[0]
    assert n % tm == 0
    steps = n // tm
    in_specs = [
        pl.BlockSpec((tm, D_MODEL), lambda i: (i, 0)),
        _resident((1, D_MODEL)), _resident((1, D_MODEL)),
        _resident((D_MODEL, D_FF)), _resident((D_MODEL, D_FF)), _resident((D_FF, D_MODEL)),
    ]
    out_specs = [pl.BlockSpec((tm, D_MODEL), lambda i: (i, 0))]
    out_shape = [jax.ShapeDtypeStruct((n, D_MODEL), F32)]
    operands = [x, pre_g, post_g, wg, wu, wd]
    if extra is not None:
        in_specs.append(_resident(extra.shape))
        operands.append(extra)
    n_fixed_in = len(in_specs)
    for arr, region in casts:
        rows, width = arr.shape
        if region is None:
            rb = _cast_row_block(rows, steps)
            last = rows // rb - 1
            spec = pl.BlockSpec((rb, width), lambda i, last=last: (jnp.minimum(i, last), 0))
            in_specs.append(spec)
            out_specs.append(spec)
            out_shape.append(jax.ShapeDtypeStruct((rows, width), BF16))
        else:
            row0, n_rows = region
            cb = HD
            while n_rows % cb or n_rows // cb > steps:
                cb += HD
            last = n_rows // cb - 1
            in_specs.append(pl.BlockSpec(
                (pl.Element(cb), pl.Element(width)),
                lambda i, row0=row0, last=last, cb=cb: (
                    pl.multiple_of(row0 + cb * jnp.minimum(i, last), F32_ROWS), 0)))
            out_specs.append(pl.BlockSpec((width, cb), lambda i, last=last: (0, jnp.minimum(i, last))))
            out_shape.append(jax.ShapeDtypeStruct((width, n_rows), BF16))
    assert len(in_specs) == n_fixed_in + len(casts)
    if extra is not None:
        out_specs.insert(1, pl.BlockSpec(extra.shape, lambda i: (0, 0)))
        out_shape.insert(1, jax.ShapeDtypeStruct(extra.shape, F32))
    outs = pl.pallas_call(
        functools.partial(_ffn_body, cast_transposed=tuple(c[1] is not None for c in casts),
                          has_extra=extra is not None),
        grid=(steps,),
        in_specs=in_specs,
        out_specs=out_specs,
        out_shape=out_shape,
        compiler_params=pltpu.CompilerParams(dimension_semantics=("arbitrary",),
                                             vmem_limit_bytes=VMEM_LIMIT),
        name="ffn",
    )(*operands, *[c[0] for c in casts])
    if extra is not None:
        return outs[0], outs[1], list(outs[2:])
    return outs[0], list(outs[1:])


def _ffn_stream_body(x_ref, pre_ref, post_ref, wg_ref, wu_ref, wd_ref,
                     o_ref, wg16_ref, wu16_ref, wd16_ref, h_s, acc_s):
    c = pl.program_id(0)

    @pl.when(c == 0)
    def _():
        h_s[...] = _rms(x_ref[...], pre_ref[...]).astype(BF16)
        acc_s[...] = jnp.zeros_like(acc_s)

    wg = wg_ref[...].astype(BF16)
    wu = wu_ref[...].astype(BF16)
    wd = wd_ref[...].astype(BF16)
    wg16_ref[...] = wg
    wu16_ref[...] = wu
    wd16_ref[...] = wd
    h = h_s[...]
    g = jnp.dot(h, wg, preferred_element_type=F32)
    u = jnp.dot(h, wu, preferred_element_type=F32)
    a = (_silu(g) * u).astype(BF16)
    acc_s[...] += jnp.dot(a, wd, preferred_element_type=F32)

    @pl.when(c == pl.num_programs(0) - 1)
    def _():
        o_ref[...] = x_ref[...] + 0.5 * _rms(acc_s[...], post_ref[...])


def _ffn_stream(x, pre_g, post_g, wg, wu, wd):
    n = x.shape[0]
    assert D_FF % FF_CHUNK == 0
    col = pl.BlockSpec((D_MODEL, FF_CHUNK), lambda c: (0, c))
    row = pl.BlockSpec((FF_CHUNK, D_MODEL), lambda c: (c, 0))
    return pl.pallas_call(
        _ffn_stream_body,
        grid=(D_FF // FF_CHUNK,),
        in_specs=[_resident((n, D_MODEL)), _resident((1, D_MODEL)), _resident((1, D_MODEL)), col, col, row],
        out_specs=[pl.BlockSpec((n, D_MODEL), lambda c: (0, 0)), col, col, row],
        out_shape=[
            jax.ShapeDtypeStruct((n, D_MODEL), F32),
            jax.ShapeDtypeStruct((D_MODEL, D_FF), BF16),
            jax.ShapeDtypeStruct((D_MODEL, D_FF), BF16),
            jax.ShapeDtypeStruct((D_FF, D_MODEL), BF16),
        ],
        scratch_shapes=[pltpu.VMEM((n, D_MODEL), BF16), pltpu.VMEM((n, D_MODEL), F32)],
        compiler_params=pltpu.CompilerParams(dimension_semantics=("arbitrary",),
                                             vmem_limit_bytes=VMEM_LIMIT),
        name="ffn_stream",
    )(x, pre_g, post_g, wg, wu, wd)


def _rope(x, cos, sin_signed):
    return x * cos + pltpu.roll(x, HD // 2, 1) * sin_signed


def _ret_out_norm(o, g_row, gate_act):
    mu = jnp.mean(o, axis=-1, keepdims=True)
    d = o - mu
    var = jnp.mean(d * d, axis=-1, keepdims=True)
    return gate_act * (d * lax.rsqrt(var + EPS) * g_row)


def _gdn_out_norm(o, g_row, gate_act):
    return o * lax.rsqrt(jnp.mean(o * o, axis=-1, keepdims=True) + EPS) * g_row * gate_act


def _l2norm(x, scale):
    return x * (lax.rsqrt(jnp.sum(x * x, axis=-1, keepdims=True) + EPS) * scale)


def _softplus(x):
    return jnp.maximum(x, 0.0) + jnp.log(1.0 + jnp.exp(-jnp.abs(x)))


def _merge(x, o_r, o_g, sig_gates, wrb_ref, wgb_ref, wout_ref, post_ref):
    y = (sig_gates[:, :D_MODEL] * _dot(o_r, wrb_ref[...])
         + sig_gates[:, D_MODEL:] * _dot(o_g, wgb_ref[...]))
    m = _dot(y, wout_ref[...])
    return x + _rms(m, post_ref[...])


def _pad_rows(a):
    return jnp.concatenate([a, jnp.zeros_like(a)], axis=0)


def _dot64(a, b):
    return jnp.dot(a.astype(BF16), _pad_rows(b.astype(BF16)), preferred_element_type=F32)


def _unit_lower_inverse_many(a_list, masks):
    eye, m16, off32, off64 = masks
    ad = [a * m16 for a in a_list]
    x = [eye - v for v in ad]
    p = [_dot64(v, v) for v in ad]
    for level in range(3):
        x = [xi + _dot64(xi, pi) for xi, pi in zip(x, p)]
        if level < 2:
            p = [_dot64(pi, pi) for pi in p]
    for m in (off32, off64):
        t = [_dot64(xi, a * m) for xi, a in zip(x, a_list)]
        x = [xi - _dot64(ti, xi) for xi, ti in zip(x, t)]
    return x


def _chunk_masks():
    r = lax.broadcasted_iota(jnp.int32, (CHUNK, 2 * CHUNK), 0)
    c = lax.broadcasted_iota(jnp.int32, (CHUNK, 2 * CHUNK), 1)
    one, zero = jnp.float32(1.0), jnp.float32(0.0)
    live = c < CHUNK
    eye = jnp.where(r == c, one, zero)
    m16 = jnp.where(live, jnp.where((r >> 4) == (c >> 4), one, zero), zero)
    m32 = jnp.where(live, jnp.where((r >> 5) == (c >> 5), one, zero), zero)
    m64 = jnp.where(live, one, zero)
    return r >= c, r > c, (eye, m16, m32 - m16, m64 - m32)


def _retention_tile(pm, cos, sin, rd_ref, ku_ref, retg_ref, s_ref, tile_decay):
    ri = lax.broadcasted_iota(jnp.int32, (TM, TM), 0)
    ci = lax.broadcasted_iota(jnp.int32, (TM, TM), 1)
    causal = ri >= ci
    heads = []
    for hh in range(HEADS):
        sl = slice(hh * HD, (hh + 1) * HD)
        rq = pm[:, OFF_RQ + hh * HD:OFF_RQ + (hh + 1) * HD]
        rk = pm[:, OFF_RK + hh * HD:OFF_RK + (hh + 1) * HD]
        v = pm[:, OFF_RV + hh * HD:OFF_RV + (hh + 1) * HD]
        rg = pm[:, OFF_RG + hh * HD:OFF_RG + (hh + 1) * HD]
        qs = _rope(rq, cos, sin) * rd_ref[hh]
        ku = _rope(rk, cos, sin) * ku_ref[hh]
        s = s_ref[hh]
        sc = jnp.where(causal, _dot_nt(qs, ku), 0.0)
        o = _dot(sc, v) + _dot(qs, s)
        s_ref[hh] = tile_decay[hh] * (s + _dot_tn(ku, v))
        heads.append(_ret_out_norm(o, retg_ref[:, sl], _silu(rg)))
    return jnp.concatenate(heads, axis=1)


def _short_conv_tile(u, buf, convw_ref, tail_ref):
    buf[CONV_PAD:CONV_PAD + TM, :] = u
    acc = buf[CONV_PAD - 3:CONV_PAD - 3 + TM, :] * convw_ref[0:1, :]
    for i in range(1, CONV_W):
        acc = acc + buf[CONV_PAD - 3 + i:CONV_PAD - 3 + i + TM, :] * convw_ref[i:i + 1, :]
    tail = buf[CONV_PAD + TM - 3:CONV_PAD + TM, :]
    tail_ref[...] = tail
    buf[CONV_PAD - 3:CONV_PAD, :] = tail
    return _silu(acc)


def _gdn_decay_beta(ab, alog_ref, dtb_ref):
    g_all = -jnp.exp(alog_ref[...]) * _softplus(ab + dtb_ref[...])
    beta_all = jax.nn.sigmoid(ab)
    return g_all, beta_all


def _gdn_tiles(cqs, abs_, gzs, ltri, alog_ref, dtb_ref, gdng_ref, s_refs):
    seqs = range(len(cqs))
    dd = functools.partial(jnp.dot, preferred_element_type=F32)
    gcum, gcum_t, beta_all = [], [], []
    for b in seqs:
        g_all, beta = _gdn_decay_beta(abs_[b], alog_ref, dtb_ref)
        beta_all.append(beta)
        g_hi, g_mid, g_lo = _split3(g_all)
        gcum.append(dd(ltri, g_hi) + (dd(ltri, g_mid) + dd(ltri, g_lo)))
        gcum_t.append(gcum[b].T)

    causal64, strict64, masks = _chunk_masks()

    n_ch = TM // CHUNK
    rows = [slice(c * CHUNK, (c + 1) * CHUNK) for c in range(n_ch)]
    bh = [(b, hh) for b in seqs for hh in range(HEADS)]
    probs = [(b, hh, c) for b, hh in bh for c in range(n_ch)]
    gq = {(b, hh): _l2norm(cqs[b][:, hh * HD:(hh + 1) * HD], HD ** -0.5) for b, hh in bh}
    gk = {(b, hh): _l2norm(cqs[b][:, QK + hh * HD:QK + (hh + 1) * HD], 1.0) for b, hh in bh}
    gv = {(b, hh): cqs[b][:, 2 * QK + hh * HD:2 * QK + (hh + 1) * HD] for b, hh in bh}
    qc = {(b, hh, c): gq[b, hh][rows[c]] for b, hh, c in probs}
    kc = {(b, hh, c): gk[b, hh][rows[c]] for b, hh, c in probs}
    vc = {(b, hh, c): gv[b, hh][rows[c]] for b, hh, c in probs}
    gc = {(b, hh, c): gcum[b][rows[c], hh:hh + 1] for b, hh, c in probs}
    bcol = {(b, hh, c): beta_all[b][rows[c], HEADS + hh:HEADS + hh + 1] for b, hh, c in probs}
    def g_row(b, hh, c):
        gr = gcum_t[b][hh:hh + 1, rows[c]]
        return jnp.concatenate([gr, gr], axis=1)

    dm = {(b, hh, c): jnp.where(
        causal64, jnp.exp(jnp.minimum(gc[b, hh, c] - g_row(b, hh, c), 0.0)), 0.0)
        for b, hh, c in probs}
    kpad = {p: _pad_rows(kc[p].astype(BF16)) for p in probs}
    kk = {p: _dot_nt(kc[p], kpad[p]) for p in probs}
    qk = {p: _dot_nt(qc[p], kpad[p]) * dm[p] for p in probs}
    a_mats = [jnp.where(strict64, bcol[p] * dm[p] * kk[p], 0.0) for p in probs]
    tinv = dict(zip(probs, _unit_lower_inverse_many(a_mats, masks)))
    eg = {p: jnp.exp(gc[p]) for p in probs}
    uw = {p: _dot64(tinv[p], jnp.concatenate([bcol[p] * vc[p], (bcol[p] * eg[p]) * kc[p]], axis=1))
          for p in probs}
    gl = {p: gc[p][CHUNK - 1:CHUNK, :] for p in probs}
    uwp = {p: _pad_rows(uw[p].astype(BF16)) for p in probs}
    kdw = {p: _dot_tn(_pad_rows((kc[p] * jnp.exp(gl[p] - gc[p])).astype(BF16)), uwp[p])
           for p in probs}
    qkw = {p: jnp.dot(qk[p].astype(BF16), uwp[p], preferred_element_type=F32)
           for p in probs}
    lhs = {p: jnp.concatenate([kdw[p][:, HD:], eg[p] * qc[p] - qkw[p][:, HD:]], axis=0).astype(BF16)
           for p in probs}

    s = {(b, hh): s_refs[b][hh] for b, hh in bh}
    outs = {k: [] for k in bh}
    for c in range(n_ch):
        for b, hh in bh:
            p = (b, hh, c)
            ps = _dot(lhs[p], s[b, hh])
            outs[b, hh].append(ps[HD:, :] + qkw[p][:, :HD])
            s[b, hh] = jnp.exp(gl[p]) * s[b, hh] + (kdw[p][:, :HD] - ps[:HD, :])
    o_gs = []
    for b in seqs:
        heads = []
        for hh in range(HEADS):
            s_refs[b][hh] = s[b, hh]
            o = jnp.concatenate(outs[b, hh], axis=0)
            heads.append(_gdn_out_norm(o, gdng_ref[...], _silu(gzs[b][:, hh * HD:(hh + 1) * HD])))
        o_gs.append(jnp.concatenate(heads, axis=1))
    return o_gs


def _mixer_prompt_body(x_ref, cos_ref, sin_ref, rd_ref, ku_ref, ltri_ref,
                       pre_ref, post_ref, wmain_ref, wab_ref, wgates_ref, convw_ref, alog_ref, dtb_ref,
                       retg_ref, gdng_ref, wrb_ref, wgb_ref, wout_ref,
                       y_ref, sret_ref, sgdn_ref, conv_ref,
                       cbuf, *, ret_tile_decay):
    t = pl.program_id(1)
    nb = x_ref.shape[0]

    @pl.when(t == 0)
    def _():
        sret_ref[...] = jnp.zeros_like(sret_ref)
        sgdn_ref[...] = jnp.zeros_like(sgdn_ref)
        cbuf[:, 0:CONV_PAD, :] = jnp.zeros((nb, CONV_PAD, CONV_DIM), F32)

    cos = cos_ref[...]
    sin = sin_ref[...]
    ltri = ltri_ref[...]
    seqs = range(nb)
    xs = [x_ref[b] for b in seqs]
    hs = [_rms(xs[b], pre_ref[...]).astype(BF16) for b in seqs]
    pgs = [jnp.dot(hs[b], wmain_ref[:, OFF_CONV:OFF_AB], preferred_element_type=F32) for b in seqs]
    abs_ = [jnp.dot(hs[b], wab_ref[...], preferred_element_type=F32) for b in seqs]
    cqs = [_short_conv_tile(pgs[b][:, 0:CONV_DIM], cbuf.at[b], convw_ref, conv_ref.at[b]) for b in seqs]
    pms = [jnp.dot(hs[b], wmain_ref[:, 0:OFF_CONV], preferred_element_type=F32) for b in seqs]
    o_rs = [_retention_tile(pms[b], cos, sin, rd_ref, ku_ref, retg_ref, sret_ref.at[b], ret_tile_decay)
            for b in seqs]
    o_gs = _gdn_tiles(cqs, abs_, [pgs[b][:, CONV_DIM:] for b in seqs], ltri, alog_ref, dtb_ref, gdng_ref,
                      [sgdn_ref.at[b] for b in seqs])
    for b in seqs:
        gates = jnp.dot(hs[b], wgates_ref[...], preferred_element_type=F32)
        y_ref[b] = _merge(xs[b], o_rs[b], o_gs[b], jax.nn.sigmoid(gates), wrb_ref, wgb_ref, wout_ref, post_ref)


def _ret_gammas():
    return 1.0 - 2.0 ** (-5.0 - np.arange(HEADS, dtype=np.float64))


def _rope_tables(pos):
    inv = ROPE_BASE ** (-np.arange(0, HD, 2, dtype=np.float64) / HD)
    ang = np.asarray(pos, np.float64)[:, None] * inv[None, :]
    cos = np.concatenate([np.cos(ang), np.cos(ang)], axis=1)
    sin = np.concatenate([-np.sin(ang), np.sin(ang)], axis=1)
    return jnp.asarray(cos, F32), jnp.asarray(sin, F32)


def _mixer_weight_specs():
    return [
        _resident((1, D_MODEL)), _resident((1, D_MODEL)),
        _resident((D_MODEL, OFF_AB)), _resident((D_MODEL, HD)), _resident((D_MODEL, 2 * D_MODEL)),
        _resident((CONV_W, CONV_DIM)), _resident((1, HD)), _resident((1, HD)),
        _resident((1, QK)), _resident((1, HD)),
        _resident((QK, D_MODEL)), _resident((QK, D_MODEL)), _resident((D_MODEL, D_MODEL)),
    ]


def _mixer_weight_args(w):
    return (w["mix_pre_g"], w["mix_post_g"], w["w_main"], w["w_ab"], w["w_gates"], w["conv_w"],
            w["a_log"], w["dt_bias"], w["ret_norm_g"], w["gdn_norm_g"], w["w_rb"], w["w_gb"], w["w_out"])


def _mixer_prompt(x, w):
    b, t, _ = x.shape
    assert t % TM == 0 and b % NB == 0
    cos, sin = _rope_tables(np.arange(t))
    gam = _ret_gammas()
    i1 = np.arange(1, TM + 1, dtype=np.float64)
    rd = np.broadcast_to((gam[:, None] ** i1[None, :])[:, :, None], (HEADS, TM, HD))
    ku = np.broadcast_to((HD ** -0.5 * gam[:, None] ** (-i1[None, :]))[:, :, None], (HEADS, TM, HD))
    tile_decay = tuple(float(v) for v in gam ** TM)
    r = np.arange(TM)
    ltri = ((r[:, None] >= r[None, :]) & (r[:, None] // CHUNK == r[None, :] // CHUNK))

    body = functools.partial(_mixer_prompt_body, ret_tile_decay=tile_decay)
    state_spec = pl.BlockSpec((NB, HEADS, HD, HD), lambda i, j: (i, 0, 0, 0))
    return pl.pallas_call(
        body,
        grid=(b // NB, t // TM),
        in_specs=[
            pl.BlockSpec((NB, TM, D_MODEL), lambda i, j: (i, j, 0)),
            pl.BlockSpec((TM, HD), lambda i, j: (j, 0)),
            pl.BlockSpec((TM, HD), lambda i, j: (j, 0)),
            _resident((HEADS, TM, HD)), _resident((HEADS, TM, HD)), _resident((TM, TM)),
        ] + _mixer_weight_specs(),
        out_specs=[
            pl.BlockSpec((NB, TM, D_MODEL), lambda i, j: (i, j, 0)),
            state_spec, state_spec,
            pl.BlockSpec((NB, CONV_W - 1, CONV_DIM), lambda i, j: (i, 0, 0)),
        ],
        out_shape=[
            jax.ShapeDtypeStruct((b, t, D_MODEL), F32),
            jax.ShapeDtypeStruct((b, HEADS, HD, HD), F32),
            jax.ShapeDtypeStruct((b, HEADS, HD, HD), F32),
            jax.ShapeDtypeStruct((b, CONV_W - 1, CONV_DIM), F32),
        ],
        scratch_shapes=[pltpu.VMEM((NB, CONV_PAD + TM, CONV_DIM), F32)],
        compiler_params=pltpu.CompilerParams(dimension_semantics=("arbitrary", "arbitrary"),
                                             vmem_limit_bytes=VMEM_LIMIT),
        name="mixer_prompt",
    )(x, cos, sin, jnp.asarray(rd, F32), jnp.asarray(ku, F32), jnp.asarray(ltri, BF16),
      *_mixer_weight_args(w))


def _pick_rows(rows):
    ri = lax.broadcasted_iota(jnp.int32, rows[0].shape, 0)
    out = rows[0]
    for j in range(1, len(rows)):
        out = jnp.where(ri == j, rows[j], out)
    return out


def _mixer_sample_body(x_ref, cos_ref, sin_ref,
                       pre_ref, post_ref, wmain_ref, wab_ref, wgates_ref, convw_ref, alog_ref, dtb_ref,
                       retg_ref, gdng_ref, wrb_ref, wgb_ref, wout_ref,
                       sret_in, sgdn_in, conv_in,
                       y_ref, sret_out, sgdn_out, conv_out,
                       pm_s, gates_s, rq_s, rk_s, gq_s, gk_s, gv_s, eg_s, beta_s, or_s, og_s, *, ret_gamma):
    i = pl.program_id(0)

    @pl.when(i == 0)
    def _():
        h = _rms(x_ref[...], pre_ref[...]).astype(BF16)
        pm = jnp.dot(h, wmain_ref[...], preferred_element_type=F32)
        ab = jnp.dot(h, wab_ref[...], preferred_element_type=F32)
        pm_s[...] = pm
        gates_s[...] = jnp.dot(h, wgates_ref[...], preferred_element_type=F32)
        cos = cos_ref[...]
        sin = sin_ref[...]
        cin = pm[:, OFF_CONV:OFF_CONV + CONV_DIM]
        acc = cin * convw_ref[CONV_W - 1:CONV_W, :]
        for r in range(CONV_W - 1):
            acc = acc + conv_in[r] * convw_ref[r:r + 1, :]
        for r in range(CONV_W - 2):
            conv_out[r] = conv_in[r + 1]
        conv_out[CONV_W - 2] = cin
        cq = _silu(acc)
        g_all, beta_all = _gdn_decay_beta(ab, alog_ref, dtb_ref)
        eg_s[...] = jnp.exp(g_all)
        beta_s[...] = beta_all
        for hh in range(HEADS):
            sl = slice(hh * HD, (hh + 1) * HD)
            rq_s[:, sl] = _rope(pm[:, OFF_RQ + hh * HD:OFF_RQ + (hh + 1) * HD], cos, sin)
            rk_s[:, sl] = _rope(pm[:, OFF_RK + hh * HD:OFF_RK + (hh + 1) * HD], cos, sin) * (HD ** -0.5)
            gq_s[:, sl] = _l2norm(cq[:, hh * HD:(hh + 1) * HD], HD ** -0.5)
            gk_s[:, sl] = _l2norm(cq[:, QK + hh * HD:QK + (hh + 1) * HD], 1.0)
        gv_s[...] = cq[:, 2 * QK:3 * QK]

    rows = pl.ds(pl.multiple_of(i * TB, TB), TB)
    eg_all = eg_s[rows, :]
    beta_all = beta_s[rows, :]
    for hh in range(HEADS):
        sl = slice(hh * HD, (hh + 1) * HD)
        q = rq_s[rows, sl]
        k_t = rk_s[rows, sl].T
        v = pm_s[rows, OFF_RV + hh * HD:OFF_RV + (hh + 1) * HD]
        o_rows = []
        for j in range(TB):
            s = ret_gamma[hh] * sret_in[j, hh] + k_t[:, j:j + 1] * v[j:j + 1, :]
            sret_out[j, hh] = s
            o_rows.append(_dot(q, s))
        or_s[rows, sl] = _pick_rows(o_rows)

        gq = gq_s[rows, sl]
        gk = gk_s[rows, sl]
        gk_t = gk.T
        gv = gv_s[rows, sl]
        o_rows = []
        for j in range(TB):
            s = sgdn_in[j, hh]
            eg = eg_all[j:j + 1, hh:hh + 1]
            beta = beta_all[j:j + 1, HEADS + hh:HEADS + hh + 1]
            ks = _dot(gk, s)[j:j + 1, :]
            u = beta * gv[j:j + 1, :] - (beta * eg) * ks
            s = eg * s + gk_t[:, j:j + 1] * u
            sgdn_out[j, hh] = s
            o_rows.append(_dot(gq, s))
        og_s[rows, sl] = _pick_rows(o_rows)

    @pl.when(i == pl.num_programs(0) - 1)
    def _():
        o_r = []
        o_g = []
        for hh in range(HEADS):
            sl = slice(hh * HD, (hh + 1) * HD)
            o_r.append(_ret_out_norm(or_s[:, sl], retg_ref[:, sl],
                                     _silu(pm_s[:, OFF_RG + hh * HD:OFF_RG + (hh + 1) * HD])))
            o_g.append(_gdn_out_norm(og_s[:, sl], gdng_ref[...],
                                     _silu(pm_s[:, OFF_GZ + hh * HD:OFF_GZ + (hh + 1) * HD])))
        y_ref[...] = _merge(x_ref[...], jnp.concatenate(o_r, axis=1), jnp.concatenate(o_g, axis=1),
                            jax.nn.sigmoid(gates_s[...]), wrb_ref, wgb_ref, wout_ref, post_ref)


def _mixer_sample(x, s_ret, s_gdn, s_conv, w, pos):
    n = x.shape[0]
    assert n % TB == 0
    cos, sin = _rope_tables([pos])
    body = functools.partial(_mixer_sample_body, ret_gamma=tuple(float(v) for v in _ret_gammas()))
    state_spec = pl.BlockSpec((TB, HEADS, HD, HD), lambda i: (i, 0, 0, 0))
    conv_shape = (CONV_W - 1, n, CONV_DIM)
    return pl.pallas_call(
        body,
        grid=(n // TB,),
        in_specs=[_resident((n, D_MODEL)), _resident((1, HD)), _resident((1, HD))]
        + _mixer_weight_specs()
        + [state_spec, state_spec, _resident(conv_shape)],
        out_specs=[pl.BlockSpec((n, D_MODEL), lambda i: (0, 0)), state_spec, state_spec,
                   pl.BlockSpec(conv_shape, lambda i: (0, 0, 0))],
        out_shape=[
            jax.ShapeDtypeStruct((n, D_MODEL), F32),
            jax.ShapeDtypeStruct((n, HEADS, HD, HD), F32),
            jax.ShapeDtypeStruct((n, HEADS, HD, HD), F32),
            jax.ShapeDtypeStruct(conv_shape, F32),
        ],
        scratch_shapes=[pltpu.VMEM((n, OFF_AB), F32), pltpu.VMEM((n, 2 * D_MODEL), F32)]
        + [pltpu.VMEM((n, QK), F32) for _ in range(5)]
        + [pltpu.VMEM((n, HD), F32), pltpu.VMEM((n, HD), F32)]
        + [pltpu.VMEM((n, QK), F32), pltpu.VMEM((n, QK), F32)],
        compiler_params=pltpu.CompilerParams(dimension_semantics=("arbitrary",),
                                             vmem_limit_bytes=VMEM_LIMIT),
        name="mixer_sample",
    )(x, cos, sin, *_mixer_weight_args(w), s_ret, s_gdn, s_conv)


def _pad_lanes(v, n):
    return jnp.pad(v, ((0, 0), (0, n - v.shape[1])))


def kernel(x_prompt, x_sample, state_ret, state_gdn, state_conv, ffn1_pre_g, ffn1_post_g, ffn1_w_gate,
           ffn1_w_up, ffn1_w_down, mix_pre_g, mix_post_g, w_in, ret_norm_g, gdn_conv_w, gdn_a_log,
           gdn_dt_bias, gdn_norm_g, w_ret_branch, w_gdn_branch, w_out, ffn2_pre_g, ffn2_post_g,
           ffn2_w_gate, ffn2_w_up, ffn2_w_down):
    depth = w_in.shape[0]
    b, t, _ = x_prompt.shape
    n_s, t_s, _ = x_sample.shape
    assert t_s == 1
    yp = x_prompt.reshape(b * t, D_MODEL)
    ys = x_sample.reshape(n_s, D_MODEL)
    outs = [[] for _ in range(6)]
    for l in range(depth):
        row = lambda a: a[l][None, :]
        ys, wg1, wu1, wd1 = _ffn_stream(ys, row(ffn1_pre_g), row(ffn1_post_g),
                                        ffn1_w_gate[l], ffn1_w_up[l], ffn1_w_down[l])
        f1 = (row(ffn1_pre_g), row(ffn1_post_g), wg1, wu1, wd1)

        w_in_t = w_in[l].T
        casts = (
            (w_in_t, (0, OFF_AB)), (w_in_t, (OFF_GATES, 2 * D_MODEL)),
            (w_ret_branch[l], None), (w_gdn_branch[l], None), (w_out[l], None),
            (ffn2_w_gate[l], None), (ffn2_w_up[l], None), (ffn2_w_down[l], None),
        )
        yp, (w_main, w_gates, w_rb, w_gb, w_o, g2, u2, d2) = _ffn(yp, *f1, tm=FFN_TM, casts=casts)
        w = {
            "mix_pre_g": row(mix_pre_g), "mix_post_g": row(mix_post_g),
            "w_main": w_main,
            "w_ab": _pad_lanes(w_in[l, :, OFF_AB:OFF_GATES], HD).astype(BF16),
            "w_gates": w_gates,
            "conv_w": gdn_conv_w[l],
            "a_log": _pad_lanes(row(gdn_a_log), HD), "dt_bias": _pad_lanes(row(gdn_dt_bias), HD),
            "ret_norm_g": row(ret_norm_g), "gdn_norm_g": row(gdn_norm_g),
            "w_rb": w_rb, "w_gb": w_gb, "w_out": w_o,
        }
        f2 = (row(ffn2_pre_g), row(ffn2_post_g), g2, u2, d2)

        yp, r1, g1, c1 = _mixer_prompt(yp.reshape(b, t, D_MODEL), w)
        ys, r2, g2s, c2 = _mixer_sample(ys, state_ret[l], state_gdn[l],
                                        jnp.swapaxes(state_conv[l], 0, 1), w, PAST_LEN)
        c2 = jnp.swapaxes(c2, 0, 1)
        yp, ys, _ = _ffn(yp.reshape(b * t, D_MODEL), *f2, tm=FFN_TM, extra=ys)
        for lst, val in zip(outs, (r1, g1, c1, r2, g2s, c2)):
            lst.append(val)
    stacked = [v[0][None] if depth == 1 else jnp.stack(v) for v in outs]
    return (yp.reshape(b, t, D_MODEL), ys.reshape(n_s, t_s, D_MODEL), *stacked)
```

```python
import functools

import numpy as np
import jax
import jax.numpy as jnp
from jax import lax
from jax.experimental import pallas as pl
from jax.experimental.pallas import tpu as pltpu

F32 = jnp.float32
BF16 = jnp.bfloat16

D_MODEL = 1024
D_FF = 2816
HEADS = 4
HD = 128
QK = HEADS * HD
CONV_W = 4
CONV_DIM = 3 * QK
CHUNK = 64
ROPE_BASE = 10000.0
EPS = 1e-6
PAST_LEN = 16384

OFF_RQ, OFF_RK, OFF_RV, OFF_RG = 0, QK, 2 * QK, 3 * QK
OFF_CONV = 4 * QK
OFF_GZ = OFF_CONV + CONV_DIM
OFF_AB = OFF_GZ + QK
OFF_GATES = OFF_AB + 2 * HEADS
D_IN = OFF_GATES + 2 * D_MODEL

TM = 256
NB = 2
TB = 8
FFN_TM = 1024
FFN_SUB = 256
FF_CHUNK = 256
CONV_PAD = 8
BF16_ROWS = 16
F32_ROWS = 8

VMEM_LIMIT = 56 * 1024 * 1024


def _silu(x):
    return x * jax.nn.sigmoid(x)


def _rms(x, g):
    return x * lax.rsqrt(jnp.mean(x * x, axis=-1, keepdims=True) + EPS) * g


def _dot(a, b):
    return jnp.dot(a.astype(BF16), b.astype(BF16), preferred_element_type=F32)


def _dot_nt(a, b):
    return lax.dot_general(a.astype(BF16), b.astype(BF16), (((1,), (1,)), ((), ())),
                           preferred_element_type=F32)


def _dot_tn(a, b):
    return lax.dot_general(a.astype(BF16), b.astype(BF16), (((0,), (0,)), ((), ())),
                           preferred_element_type=F32)


def _split3(a):
    hi = a.astype(BF16)
    r = a - hi.astype(F32)
    mid = r.astype(BF16)
    lo = (r - mid.astype(F32)).astype(BF16)
    return hi, mid, lo


def _resident(shape):
    nd = len(shape)
    return pl.BlockSpec(shape, lambda *_: (0,) * nd, pipeline_mode=pl.Buffered(1))


def _ffn_rows(x_ref, o_ref, pre_ref, post_ref, wg_ref, wu_ref, wd_ref):
    tm = x_ref.shape[0]
    sub = min(tm, FFN_SUB)
    parts = [slice(r, r + sub) for r in range(0, tm, sub)]
    xs = [x_ref[p, :] for p in parts]
    hs = [_rms(x, pre_ref[...]).astype(BF16) for x in xs]
    acts = []
    for h in hs:
        g = jnp.dot(h, wg_ref[...], preferred_element_type=F32)
        u = jnp.dot(h, wu_ref[...], preferred_element_type=F32)
        acts.append((_silu(g) * u).astype(BF16))
    for p, x, a in zip(parts, xs, acts):
        y = jnp.dot(a, wd_ref[...], preferred_element_type=F32)
        o_ref[p, :] = x + 0.5 * _rms(y, post_ref[...])


def _ffn_body(x_ref, pre_ref, post_ref, wg_ref, wu_ref, wd_ref, *rest, cast_transposed, has_extra):
    rest = list(rest)
    xe_ref = rest.pop(0) if has_extra else None
    n_cast = len(cast_transposed)
    cast_in, o_ref = rest[:n_cast], rest[n_cast]
    rest = rest[n_cast + 1:]
    oe_ref = rest.pop(0) if has_extra else None
    cast_out = rest
    weights = (pre_ref, post_ref, wg_ref, wu_ref, wd_ref)
    _ffn_rows(x_ref, o_ref, *weights)
    for src, dst, transposed in zip(cast_in, cast_out, cast_transposed):
        blk = src[...]
        dst[...] = (blk.T if transposed else blk).astype(BF16)
    if has_extra:
        @pl.when(pl.program_id(0) == pl.num_programs(0) - 1)
        def _():
            _ffn_rows(xe_ref, oe_ref, *weights)


def _cast_row_block(rows, steps):
    rb = BF16_ROWS
    while rows % rb or rows // rb > steps:
        rb += BF16_ROWS
    return rb


def _ffn(x, pre_g, post_g, wg, wu, wd, tm, casts=(), extra=None):
    n = x.shape[0]
    assert n % tm == 0
    steps = n // tm
    in_specs = [
        pl.BlockSpec((tm, D_MODEL), lambda i: (i, 0)),
        _resident((1, D_MODEL)), _resident((1, D_MODEL)),
        _resident((D_MODEL, D_FF)), _resident((D_MODEL, D_FF)), _resident((D_FF, D_MODEL)),
    ]
    out_specs = [pl.BlockSpec((tm, D_MODEL), lambda i: (i, 0))]
    out_shape = [jax.ShapeDtypeStruct((n, D_MODEL), F32)]
    operands = [x, pre_g, post_g, wg, wu, wd]
    if extra is not None:
        in_specs.append(_resident(extra.shape))
        operands.append(extra)
    n_fixed_in = len(in_specs)
    for arr, region in casts:
        rows, width = arr.shape
        if region is None:
            rb = _cast_row_block(rows, steps)
            last = rows // rb - 1
            spec = pl.BlockSpec((rb, width), lambda i, last=last: (jnp.minimum(i, last), 0))
            in_specs.append(spec)
            out_specs.append(spec)
            out_shape.append(jax.ShapeDtypeStruct((rows, width), BF16))
        else:
            row0, n_rows = region
            cb = HD
            while n_rows % cb or n_rows // cb > steps:
                cb += HD
            last = n_rows // cb - 1
            in_specs.append(pl.BlockSpec(
                (pl.Element(cb), pl.Element(width)),
                lambda i, row0=row0, last=last, cb=cb: (
                    pl.multiple_of(row0 + cb * jnp.minimum(i, last), F32_ROWS), 0)))
            out_specs.append(pl.BlockSpec((width, cb), lambda i, last=last: (0, jnp.minimum(i, last))))
            out_shape.append(jax.ShapeDtypeStruct((width, n_rows), BF16))
    assert len(in_specs) == n_fixed_in + len(casts)
    if extra is not None:
        out_specs.insert(1, pl.BlockSpec(extra.shape, lambda i: (0, 0)))
        out_shape.insert(1, jax.ShapeDtypeStruct(extra.shape, F32))
    outs = pl.pallas_call(
        functools.partial(_ffn_body, cast_transposed=tuple(c[1] is not None for c in casts),
                          has_extra=extra is not None),
        grid=(steps,),
        in_specs=in_specs,
        out_specs=out_specs,
        out_shape=out_shape,
        compiler_params=pltpu.CompilerParams(dimension_semantics=("arbitrary",),
                                             vmem_limit_bytes=VMEM_LIMIT),
        name="ffn",
    )(*operands, *[c[0] for c in casts])
    if extra is not None:
        return outs[0], outs[1], list(outs[2:])
    return outs[0], list(outs[1:])


def _ffn_stream_body(x_ref, pre_ref, post_ref, wg_ref, wu_ref, wd_ref,
                     o_ref, wg16_ref, wu16_ref, wd16_ref, h_s, acc_s):
    c = pl.program_id(0)

    @pl.when(c == 0)
    def _():
        h_s[...] = _rms(x_ref[...], pre_ref[...]).astype(BF16)
        acc_s[...] = jnp.zeros_like(acc_s)

    wg = wg_ref[...].astype(BF16)
    wu = wu_ref[...].astype(BF16)
    wd = wd_ref[...].astype(BF16)
    wg16_ref[...] = wg
    wu16_ref[...] = wu
    wd16_ref[...] = wd
    h = h_s[...]
    g = jnp.dot(h, wg, preferred_element_type=F32)
    u = jnp.dot(h, wu, preferred_element_type=F32)
    a = (_silu(g) * u).astype(BF16)
    acc_s[...] += jnp.dot(a, wd, preferred_element_type=F32)

    @pl.when(c == pl.num_programs(0) - 1)
    def _():
        o_ref[...] = x_ref[...] + 0.5 * _rms(acc_s[...], post_ref[...])


def _ffn_stream(x, pre_g, post_g, wg, wu, wd):
    n = x.shape[0]
    assert D_FF % FF_CHUNK == 0
    col = pl.BlockSpec((D_MODEL, FF_CHUNK), lambda c: (0, c))
    row = pl.BlockSpec((FF_CHUNK, D_MODEL), lambda c: (c, 0))
    return pl.pallas_call(
        _ffn_stream_body,
        grid=(D_FF // FF_CHUNK,),
        in_specs=[_resident((n, D_MODEL)), _resident((1, D_MODEL)), _resident((1, D_MODEL)), col, col, row],
        out_specs=[pl.BlockSpec((n, D_MODEL), lambda c: (0, 0)), col, col, row],
        out_shape=[
            jax.ShapeDtypeStruct((n, D_MODEL), F32),
            jax.ShapeDtypeStruct((D_MODEL, D_FF), BF16),
            jax.ShapeDtypeStruct((D_MODEL, D_FF), BF16),
            jax.ShapeDtypeStruct((D_FF, D_MODEL), BF16),
        ],
        scratch_shapes=[pltpu.VMEM((n, D_MODEL), BF16), pltpu.VMEM((n, D_MODEL), F32)],
        compiler_params=pltpu.CompilerParams(dimension_semantics=("arbitrary",),
                                             vmem_limit_bytes=VMEM_LIMIT),
        name="ffn_stream",
    )(x, pre_g, post_g, wg, wu, wd)


def _rope(x, cos, sin_signed):
    return x * cos + pltpu.roll(x, HD // 2, 1) * sin_signed


def _ret_out_norm(o, g_row, gate_act):
    mu = jnp.mean(o, axis=-1, keepdims=True)
    d = o - mu
    var = jnp.mean(d * d, axis=-1, keepdims=True)
    return gate_act * (d * lax.rsqrt(var + EPS) * g_row)


def _gdn_out_norm(o, g_row, gate_act):
    return o * lax.rsqrt(jnp.mean(o * o, axis=-1, keepdims=True) + EPS) * g_row * gate_act


def _l2norm(x, scale):
    return x * (lax.rsqrt(jnp.sum(x * x, axis=-1, keepdims=True) + EPS) * scale)


def _softplus(x):
    return jnp.maximum(x, 0.0) + jnp.log(1.0 + jnp.exp(-jnp.abs(x)))


def _merge(x, o_r, o_g, sig_gates, wrb_ref, wgb_ref, wout_ref, post_ref):
    y = (sig_gates[:, :D_MODEL] * _dot(o_r, wrb_ref[...])
         + sig_gates[:, D_MODEL:] * _dot(o_g, wgb_ref[...]))
    m = _dot(y, wout_ref[...])
    return x + _rms(m, post_ref[...])


def _pad_rows(a):
    return jnp.concatenate([a, jnp.zeros_like(a)], axis=0)


def _dot64(a, b):
    return jnp.dot(a.astype(BF16), _pad_rows(b.astype(BF16)), preferred_element_type=F32)


def _unit_lower_inverse_many(a_list, masks):
    eye, m16, off32, off64 = masks
    ad = [a * m16 for a in a_list]
    x = [eye - v for v in ad]
    p = [_dot64(v, v) for v in ad]
    for level in range(3):
        x = [xi + _dot64(xi, pi) for xi, pi in zip(x, p)]
        if level < 2:
            p = [_dot64(pi, pi) for pi in p]
    for m in (off32, off64):
        t = [_dot64(xi, a * m) for xi, a in zip(x, a_list)]
        x = [xi - _dot64(ti, xi) for xi, ti in zip(x, t)]
    return x


def _chunk_masks():
    r = lax.broadcasted_iota(jnp.int32, (CHUNK, 2 * CHUNK), 0)
    c = lax.broadcasted_iota(jnp.int32, (CHUNK, 2 * CHUNK), 1)
    one, zero = jnp.float32(1.0), jnp.float32(0.0)
    live = c < CHUNK
    eye = jnp.where(r == c, one, zero)
    m16 = jnp.where(live, jnp.where((r >> 4) == (c >> 4), one, zero), zero)
    m32 = jnp.where(live, jnp.where((r >> 5) == (c >> 5), one, zero), zero)
    m64 = jnp.where(live, one, zero)
    return r >= c, r > c, (eye, m16, m32 - m16, m64 - m32)


def _retention_tile(pm, cos, sin, rd_ref, ku_ref, retg_ref, s_ref, tile_decay):
    ri = lax.broadcasted_iota(jnp.int32, (TM, TM), 0)
    ci = lax.broadcasted_iota(jnp.int32, (TM, TM), 1)
    causal = ri >= ci
    heads = []
    for hh in range(HEADS):
        sl = slice(hh * HD, (hh + 1) * HD)
        rq = pm[:, OFF_RQ + hh * HD:OFF_RQ + (hh + 1) * HD]
        rk = pm[:, OFF_RK + hh * HD:OFF_RK + (hh + 1) * HD]
        v = pm[:, OFF_RV + hh * HD:OFF_RV + (hh + 1) * HD]
        rg = pm[:, OFF_RG + hh * HD:OFF_RG + (hh + 1) * HD]
        qs = _rope(rq, cos, sin) * rd_ref[hh]
        ku = _rope(rk, cos, sin) * ku_ref[hh]
        s = s_ref[hh]
        sc = jnp.where(causal, _dot_nt(qs, ku), 0.0)
        o = _dot(sc, v) + _dot(qs, s)
        s_ref[hh] = tile_decay[hh] * (s + _dot_tn(ku, v))
        heads.append(_ret_out_norm(o, retg_ref[:, sl], _silu(rg)))
    return jnp.concatenate(heads, axis=1)


def _short_conv_tile(u, buf, convw_ref, tail_ref):
    prev = buf[...]
    row = lax.broadcasted_iota(jnp.int32, (CONV_PAD, u.shape[1]), 0)
    acc = u * convw_ref[CONV_W - 1:CONV_W, :]
    for k in range(1, CONV_W):
        shifted = pltpu.roll(u, k, 0)
        head = jnp.where(row < k, pltpu.roll(prev, k, 0), shifted[0:CONV_PAD, :])
        shifted = jnp.concatenate([head, shifted[CONV_PAD:, :]], axis=0)
        acc = acc + shifted * convw_ref[CONV_W - 1 - k:CONV_W - k, :]
    buf[...] = u[TM - CONV_PAD:TM, :]
    tail_ref[...] = u[TM - (CONV_W - 1):TM, :]
    return _silu(acc)


def _gdn_decay_beta(ab, alog_ref, dtb_ref):
    g_all = -jnp.exp(alog_ref[...]) * _softplus(ab + dtb_ref[...])
    beta_all = jax.nn.sigmoid(ab)
    return g_all, beta_all


def _gdn_tiles(cqs, abs_, gzs, ltri, alog_ref, dtb_ref, gdng_ref, s_refs):
    seqs = range(len(cqs))
    dd = functools.partial(jnp.dot, preferred_element_type=F32)
    gcum, gcum_t, beta_all = [], [], []
    for b in seqs:
        g_all, beta = _gdn_decay_beta(abs_[b], alog_ref, dtb_ref)
        beta_all.append(beta)
        g_hi, g_mid, g_lo = _split3(g_all)
        gcum.append(dd(ltri, g_hi) + (dd(ltri, g_mid) + dd(ltri, g_lo)))
        gcum_t.append(gcum[b].T)

    causal64, strict64, masks = _chunk_masks()

    n_ch = TM // CHUNK
    rows = [slice(c * CHUNK, (c + 1) * CHUNK) for c in range(n_ch)]
    bh = [(b, hh) for b in seqs for hh in range(HEADS)]
    probs = [(b, hh, c) for b, hh in bh for c in range(n_ch)]
    gq = {(b, hh): _l2norm(cqs[b][:, hh * HD:(hh + 1) * HD], HD ** -0.5) for b, hh in bh}
    gk = {(b, hh): _l2norm(cqs[b][:, QK + hh * HD:QK + (hh + 1) * HD], 1.0) for b, hh in bh}
    gv = {(b, hh): cqs[b][:, 2 * QK + hh * HD:2 * QK + (hh + 1) * HD] for b, hh in bh}
    qc = {(b, hh, c): gq[b, hh][rows[c]] for b, hh, c in probs}
    kc = {(b, hh, c): gk[b, hh][rows[c]] for b, hh, c in probs}
    vc = {(b, hh, c): gv[b, hh][rows[c]] for b, hh, c in probs}
    gc = {(b, hh, c): gcum[b][rows[c], hh:hh + 1] for b, hh, c in probs}
    bcol = {(b, hh, c): beta_all[b][rows[c], HEADS + hh:HEADS + hh + 1] for b, hh, c in probs}
    def g_row(b, hh, c):
        gr = gcum_t[b][hh:hh + 1, rows[c]]
        return jnp.concatenate([gr, gr], axis=1)

    dm = {(b, hh, c): jnp.where(
        causal64, jnp.exp(jnp.minimum(gc[b, hh, c] - g_row(b, hh, c), 0.0)), 0.0)
        for b, hh, c in probs}
    kpad = {p: _pad_rows(kc[p].astype(BF16)) for p in probs}
    kk = {p: _dot_nt(kc[p], kpad[p]) for p in probs}
    qk = {p: _dot_nt(qc[p], kpad[p]) * dm[p] for p in probs}
    a_mats = [jnp.where(strict64, bcol[p] * dm[p] * kk[p], 0.0) for p in probs]
    tinv = dict(zip(probs, _unit_lower_inverse_many(a_mats, masks)))
    eg = {p: jnp.exp(gc[p]) for p in probs}
    uw = {p: _dot64(tinv[p], jnp.concatenate([bcol[p] * vc[p], (bcol[p] * eg[p]) * kc[p]], axis=1))
          for p in probs}
    gl = {p: gc[p][CHUNK - 1:CHUNK, :] for p in probs}
    uwp = {p: _pad_rows(uw[p].astype(BF16)) for p in probs}
    kdw = {p: _dot_tn(_pad_rows((kc[p] * jnp.exp(gl[p] - gc[p])).astype(BF16)), uwp[p])
           for p in probs}
    qkw = {p: jnp.dot(qk[p].astype(BF16), uwp[p], preferred_element_type=F32)
           for p in probs}
    lhs = {p: jnp.concatenate([kdw[p][:, HD:], eg[p] * qc[p] - qkw[p][:, HD:]], axis=0).astype(BF16)
           for p in probs}

    s = {(b, hh): s_refs[b][hh] for b, hh in bh}
    outs = {k: [] for k in bh}
    for c in range(n_ch):
        for b, hh in bh:
            p = (b, hh, c)
            ps = _dot(lhs[p], s[b, hh])
            outs[b, hh].append(ps[HD:, :] + qkw[p][:, :HD])
            s[b, hh] = jnp.exp(gl[p]) * s[b, hh] + (kdw[p][:, :HD] - ps[:HD, :])
    o_gs = []
    for b in seqs:
        heads = []
        for hh in range(HEADS):
            s_refs[b][hh] = s[b, hh]
            o = jnp.concatenate(outs[b, hh], axis=0)
            heads.append(_gdn_out_norm(o, gdng_ref[...], _silu(gzs[b][:, hh * HD:(hh + 1) * HD])))
        o_gs.append(jnp.concatenate(heads, axis=1))
    return o_gs


def _mixer_prompt_body(x_ref, cos_ref, sin_ref, rd_ref, ku_ref, ltri_ref,
                       pre_ref, post_ref, wmain_ref, wab_ref, wgates_ref, convw_ref, alog_ref, dtb_ref,
                       retg_ref, gdng_ref, wrb_ref, wgb_ref, wout_ref,
                       y_ref, sret_ref, sgdn_ref, conv_ref,
                       cbuf, *, ret_tile_decay):
    t = pl.program_id(1)
    nb = x_ref.shape[0]

    @pl.when(t == 0)
    def _():
        sret_ref[...] = jnp.zeros_like(sret_ref)
        sgdn_ref[...] = jnp.zeros_like(sgdn_ref)
        cbuf[:, 0:CONV_PAD, :] = jnp.zeros((nb, CONV_PAD, CONV_DIM), F32)

    cos = cos_ref[...]
    sin = sin_ref[...]
    ltri = ltri_ref[...]
    seqs = range(nb)
    xs = [x_ref[b] for b in seqs]
    hs = [_rms(xs[b], pre_ref[...]).astype(BF16) for b in seqs]
    pgs = [jnp.dot(hs[b], wmain_ref[:, OFF_CONV:OFF_AB], preferred_element_type=F32) for b in seqs]
    abs_ = [jnp.dot(hs[b], wab_ref[...], preferred_element_type=F32) for b in seqs]
    cqs = [_short_conv_tile(pgs[b][:, 0:CONV_DIM], cbuf.at[b], convw_ref, conv_ref.at[b]) for b in seqs]
    pms = [jnp.dot(hs[b], wmain_ref[:, 0:OFF_CONV], preferred_element_type=F32) for b in seqs]
    o_rs = [_retention_tile(pms[b], cos, sin, rd_ref, ku_ref, retg_ref, sret_ref.at[b], ret_tile_decay)
            for b in seqs]
    o_gs = _gdn_tiles(cqs, abs_, [pgs[b][:, CONV_DIM:] for b in seqs], ltri, alog_ref, dtb_ref, gdng_ref,
                      [sgdn_ref.at[b] for b in seqs])
    for b in seqs:
        gates = jnp.dot(hs[b], wgates_ref[...], preferred_element_type=F32)
        y_ref[b] = _merge(xs[b], o_rs[b], o_gs[b], jax.nn.sigmoid(gates), wrb_ref, wgb_ref, wout_ref, post_ref)


def _ret_gammas():
    return 1.0 - 2.0 ** (-5.0 - np.arange(HEADS, dtype=np.float64))


def _rope_tables(pos):
    inv = ROPE_BASE ** (-np.arange(0, HD, 2, dtype=np.float64) / HD)
    ang = np.asarray(pos, np.float64)[:, None] * inv[None, :]
    cos = np.concatenate([np.cos(ang), np.cos(ang)], axis=1)
    sin = np.concatenate([-np.sin(ang), np.sin(ang)], axis=1)
    return jnp.asarray(cos, F32), jnp.asarray(sin, F32)


def _mixer_weight_specs():
    return [
        _resident((1, D_MODEL)), _resident((1, D_MODEL)),
        _resident((D_MODEL, OFF_AB)), _resident((D_MODEL, HD)), _resident((D_MODEL, 2 * D_MODEL)),
        _resident((CONV_W, CONV_DIM)), _resident((1, HD)), _resident((1, HD)),
        _resident((1, QK)), _resident((1, HD)),
        _resident((QK, D_MODEL)), _resident((QK, D_MODEL)), _resident((D_MODEL, D_MODEL)),
    ]


def _mixer_weight_args(w):
    return (w["mix_pre_g"], w["mix_post_g"], w["w_main"], w["w_ab"], w["w_gates"], w["conv_w"],
            w["a_log"], w["dt_bias"], w["ret_norm_g"], w["gdn_norm_g"], w["w_rb"], w["w_gb"], w["w_out"])


def _mixer_prompt(x, w):
    b, t, _ = x.shape
    assert t % TM == 0 and b % NB == 0
    cos, sin = _rope_tables(np.arange(t))
    gam = _ret_gammas()
    i1 = np.arange(1, TM + 1, dtype=np.float64)
    rd = np.broadcast_to((gam[:, None] ** i1[None, :])[:, :, None], (HEADS, TM, HD))
    ku = np.broadcast_to((HD ** -0.5 * gam[:, None] ** (-i1[None, :]))[:, :, None], (HEADS, TM, HD))
    tile_decay = tuple(float(v) for v in gam ** TM)
    r = np.arange(TM)
    ltri = ((r[:, None] >= r[None, :]) & (r[:, None] // CHUNK == r[None, :] // CHUNK))

    body = functools.partial(_mixer_prompt_body, ret_tile_decay=tile_decay)
    state_spec = pl.BlockSpec((NB, HEADS, HD, HD), lambda i, j: (i, 0, 0, 0))
    return pl.pallas_call(
        body,
        grid=(b // NB, t // TM),
        in_specs=[
            pl.BlockSpec((NB, TM, D_MODEL), lambda i, j: (i, j, 0)),
            pl.BlockSpec((TM, HD), lambda i, j: (j, 0)),
            pl.BlockSpec((TM, HD), lambda i, j: (j, 0)),
            _resident((HEADS, TM, HD)), _resident((HEADS, TM, HD)), _resident((TM, TM)),
        ] + _mixer_weight_specs(),
        out_specs=[
            pl.BlockSpec((NB, TM, D_MODEL), lambda i, j: (i, j, 0)),
            state_spec, state_spec,
            pl.BlockSpec((NB, CONV_W - 1, CONV_DIM), lambda i, j: (i, 0, 0)),
        ],
        out_shape=[
            jax.ShapeDtypeStruct((b, t, D_MODEL), F32),
            jax.ShapeDtypeStruct((b, HEADS, HD, HD), F32),
            jax.ShapeDtypeStruct((b, HEADS, HD, HD), F32),
            jax.ShapeDtypeStruct((b, CONV_W - 1, CONV_DIM), F32),
        ],
        scratch_shapes=[pltpu.VMEM((NB, CONV_PAD, CONV_DIM), F32)],
        compiler_params=pltpu.CompilerParams(dimension_semantics=("arbitrary", "arbitrary"),
                                             vmem_limit_bytes=VMEM_LIMIT),
        name="mixer_prompt",
    )(x, cos, sin, jnp.asarray(rd, F32), jnp.asarray(ku, F32), jnp.asarray(ltri, BF16),
      *_mixer_weight_args(w))


def _pick_rows(rows):
    ri = lax.broadcasted_iota(jnp.int32, rows[0].shape, 0)
    out = rows[0]
    for j in range(1, len(rows)):
        out = jnp.where(ri == j, rows[j], out)
    return out


def _mixer_sample_body(x_ref, cos_ref, sin_ref,
                       pre_ref, post_ref, wmain_ref, wab_ref, wgates_ref, convw_ref, alog_ref, dtb_ref,
                       retg_ref, gdng_ref, wrb_ref, wgb_ref, wout_ref,
                       sret_in, sgdn_in, conv_in,
                       y_ref, sret_out, sgdn_out, conv_out,
                       pm_s, gates_s, rq_s, rk_s, gq_s, gk_s, gv_s, eg_s, beta_s, or_s, og_s, *, ret_gamma):
    i = pl.program_id(0)

    @pl.when(i == 0)
    def _():
        h = _rms(x_ref[...], pre_ref[...]).astype(BF16)
        pm = jnp.dot(h, wmain_ref[...], preferred_element_type=F32)
        ab = jnp.dot(h, wab_ref[...], preferred_element_type=F32)
        pm_s[...] = pm
        gates_s[...] = jnp.dot(h, wgates_ref[...], preferred_element_type=F32)
        cos = cos_ref[...]
        sin = sin_ref[...]
        cin = pm[:, OFF_CONV:OFF_CONV + CONV_DIM]
        acc = cin * convw_ref[CONV_W - 1:CONV_W, :]
        for r in range(CONV_W - 1):
            acc = acc + conv_in[r] * convw_ref[r:r + 1, :]
        for r in range(CONV_W - 2):
            conv_out[r] = conv_in[r + 1]
        conv_out[CONV_W - 2] = cin
        cq = _silu(acc)
        g_all, beta_all = _gdn_decay_beta(ab, alog_ref, dtb_ref)
        eg_s[...] = jnp.exp(g_all)
        beta_s[...] = beta_all
        for hh in range(HEADS):
            sl = slice(hh * HD, (hh + 1) * HD)
            rq_s[:, sl] = _rope(pm[:, OFF_RQ + hh * HD:OFF_RQ + (hh + 1) * HD], cos, sin)
            rk_s[:, sl] = _rope(pm[:, OFF_RK + hh * HD:OFF_RK + (hh + 1) * HD], cos, sin) * (HD ** -0.5)
            gq_s[:, sl] = _l2norm(cq[:, hh * HD:(hh + 1) * HD], HD ** -0.5)
            gk_s[:, sl] = _l2norm(cq[:, QK + hh * HD:QK + (hh + 1) * HD], 1.0)
        gv_s[...] = cq[:, 2 * QK:3 * QK]

    rows = pl.ds(pl.multiple_of(i * TB, TB), TB)
    eg_all = eg_s[rows, :]
    beta_all = beta_s[rows, :]
    for hh in range(HEADS):
        sl = slice(hh * HD, (hh + 1) * HD)
        q = rq_s[rows, sl]
        k_t = rk_s[rows, sl].T
        v = pm_s[rows, OFF_RV + hh * HD:OFF_RV + (hh + 1) * HD]
        o_rows = []
        for j in range(TB):
            s = ret_gamma[hh] * sret_in[j, hh] + k_t[:, j:j + 1] * v[j:j + 1, :]
            sret_out[j, hh] = s
            o_rows.append(_dot(q, s))
        or_s[rows, sl] = _pick_rows(o_rows)

        gq = gq_s[rows, sl]
        gk = gk_s[rows, sl]
        gk_t = gk.T
        gv = gv_s[rows, sl]
        o_rows = []
        for j in range(TB):
            s = sgdn_in[j, hh]
            eg = eg_all[j:j + 1, hh:hh + 1]
            beta = beta_all[j:j + 1, HEADS + hh:HEADS + hh + 1]
            ks = _dot(gk, s)[j:j + 1, :]
            u = beta * gv[j:j + 1, :] - (beta * eg) * ks
            s = eg * s + gk_t[:, j:j + 1] * u
            sgdn_out[j, hh] = s
            o_rows.append(_dot(gq, s))
        og_s[rows, sl] = _pick_rows(o_rows)

    @pl.when(i == pl.num_programs(0) - 1)
    def _():
        o_r = []
        o_g = []
        for hh in range(HEADS):
            sl = slice(hh * HD, (hh + 1) * HD)
            o_r.append(_ret_out_norm(or_s[:, sl], retg_ref[:, sl],
                                     _silu(pm_s[:, OFF_RG + hh * HD:OFF_RG + (hh + 1) * HD])))
            o_g.append(_gdn_out_norm(og_s[:, sl], gdng_ref[...],
                                     _silu(pm_s[:, OFF_GZ + hh * HD:OFF_GZ + (hh + 1) * HD])))
        y_ref[...] = _merge(x_ref[...], jnp.concatenate(o_r, axis=1), jnp.concatenate(o_g, axis=1),
                            jax.nn.sigmoid(gates_s[...]), wrb_ref, wgb_ref, wout_ref, post_ref)


def _mixer_sample(x, s_ret, s_gdn, s_conv, w, pos):
    n = x.shape[0]
    assert n % TB == 0
    cos, sin = _rope_tables([pos])
    body = functools.partial(_mixer_sample_body, ret_gamma=tuple(float(v) for v in _ret_gammas()))
    state_spec = pl.BlockSpec((TB, HEADS, HD, HD), lambda i: (i, 0, 0, 0))
    conv_shape = (CONV_W - 1, n, CONV_DIM)
    return pl.pallas_call(
        body,
        grid=(n // TB,),
        in_specs=[_resident((n, D_MODEL)), _resident((1, HD)), _resident((1, HD))]
        + _mixer_weight_specs()
        + [state_spec, state_spec, _resident(conv_shape)],
        out_specs=[pl.BlockSpec((n, D_MODEL), lambda i: (0, 0)), state_spec, state_spec,
                   pl.BlockSpec(conv_shape, lambda i: (0, 0, 0))],
        out_shape=[
            jax.ShapeDtypeStruct((n, D_MODEL), F32),
            jax.ShapeDtypeStruct((n, HEADS, HD, HD), F32),
            jax.ShapeDtypeStruct((n, HEADS, HD, HD), F32),
            jax.ShapeDtypeStruct(conv_shape, F32),
        ],
        scratch_shapes=[pltpu.VMEM((n, OFF_AB), F32), pltpu.VMEM((n, 2 * D_MODEL), F32)]
        + [pltpu.VMEM((n, QK), F32) for _ in range(5)]
        + [pltpu.VMEM((n, HD), F32), pltpu.VMEM((n, HD), F32)]
        + [pltpu.VMEM((n, QK), F32), pltpu.VMEM((n, QK), F32)],
        compiler_params=pltpu.CompilerParams(dimension_semantics=("arbitrary",),
                                             vmem_limit_bytes=VMEM_LIMIT),
        name="mixer_sample",
    )(x, cos, sin, *_mixer_weight_args(w), s_ret, s_gdn, s_conv)


def _pad_lanes(v, n):
    return jnp.pad(v, ((0, 0), (0, n - v.shape[1])))


def kernel(x_prompt, x_sample, state_ret, state_gdn, state_conv, ffn1_pre_g, ffn1_post_g, ffn1_w_gate,
           ffn1_w_up, ffn1_w_down, mix_pre_g, mix_post_g, w_in, ret_norm_g, gdn_conv_w, gdn_a_log,
           gdn_dt_bias, gdn_norm_g, w_ret_branch, w_gdn_branch, w_out, ffn2_pre_g, ffn2_post_g,
           ffn2_w_gate, ffn2_w_up, ffn2_w_down):
    depth = w_in.shape[0]
    b, t, _ = x_prompt.shape
    n_s, t_s, _ = x_sample.shape
    assert t_s == 1
    yp = x_prompt.reshape(b * t, D_MODEL)
    ys = x_sample.reshape(n_s, D_MODEL)
    outs = [[] for _ in range(6)]
    for l in range(depth):
        row = lambda a: a[l][None, :]
        ys, wg1, wu1, wd1 = _ffn_stream(ys, row(ffn1_pre_g), row(ffn1_post_g),
                                        ffn1_w_gate[l], ffn1_w_up[l], ffn1_w_down[l])
        f1 = (row(ffn1_pre_g), row(ffn1_post_g), wg1, wu1, wd1)

        w_in_t = w_in[l].T
        casts = (
            (w_in_t, (0, OFF_AB)), (w_in_t, (OFF_GATES, 2 * D_MODEL)),
            (w_ret_branch[l], None), (w_gdn_branch[l], None), (w_out[l], None),
            (ffn2_w_gate[l], None), (ffn2_w_up[l], None), (ffn2_w_down[l], None),
        )
        yp, (w_main, w_gates, w_rb, w_gb, w_o, g2, u2, d2) = _ffn(yp, *f1, tm=FFN_TM, casts=casts)
        w = {
            "mix_pre_g": row(mix_pre_g), "mix_post_g": row(mix_post_g),
            "w_main": w_main,
            "w_ab": _pad_lanes(w_in[l, :, OFF_AB:OFF_GATES], HD).astype(BF16),
            "w_gates": w_gates,
            "conv_w": gdn_conv_w[l],
            "a_log": _pad_lanes(row(gdn_a_log), HD), "dt_bias": _pad_lanes(row(gdn_dt_bias), HD),
            "ret_norm_g": row(ret_norm_g), "gdn_norm_g": row(gdn_norm_g),
            "w_rb": w_rb, "w_gb": w_gb, "w_out": w_o,
        }
        f2 = (row(ffn2_pre_g), row(ffn2_post_g), g2, u2, d2)

        yp, r1, g1, c1 = _mixer_prompt(yp.reshape(b, t, D_MODEL), w)
        ys, r2, g2s, c2 = _mixer_sample(ys, state_ret[l], state_gdn[l],
                                        jnp.swapaxes(state_conv[l], 0, 1), w, PAST_LEN)
        c2 = jnp.swapaxes(c2, 0, 1)
        yp, ys, _ = _ffn(yp.reshape(b * t, D_MODEL), *f2, tm=FFN_TM, extra=ys)
        for lst, val in zip(outs, (r1, g1, c1, r2, g2s, c2)):
            lst.append(val)
    stacked = [v[0][None] if depth == 1 else jnp.stack(v) for v in outs]
    return (yp.reshape(b, t, D_MODEL), ys.reshape(n_s, t_s, D_MODEL), *stacked)
```

```python
import functools

import numpy as np
import jax
import jax.numpy as jnp
from jax import lax
from jax.experimental import pallas as pl
from jax.experimental.pallas import tpu as pltpu

F32 = jnp.float32
BF16 = jnp.bfloat16

D_MODEL = 1024
D_FF = 2816
HEADS = 4
HD = 128
QK = HEADS * HD
CONV_W = 4
CONV_DIM = 3 * QK
CHUNK = 64
ROPE_BASE = 10000.0
EPS = 1e-6
PAST_LEN = 16384

OFF_RQ, OFF_RK, OFF_RV, OFF_RG = 0, QK, 2 * QK, 3 * QK
OFF_CONV = 4 * QK
OFF_GZ = OFF_CONV + CONV_DIM
OFF_AB = OFF_GZ + QK
OFF_GATES = OFF_AB + 2 * HEADS
D_IN = OFF_GATES + 2 * D_MODEL

TM = 256
NB = 2
TB = 8
FFN_TM = 1024
FFN_SUB = 256
FF_CHUNK = 256
CONV_PAD = 8
BF16_ROWS = 16
F32_ROWS = 8

VMEM_LIMIT = 56 * 1024 * 1024


def _silu(x):
    return x * jax.nn.sigmoid(x)


def _rms(x, g):
    return x * lax.rsqrt(jnp.mean(x * x, axis=-1, keepdims=True) + EPS) * g


def _dot(a, b):
    return jnp.dot(a.astype(BF16), b.astype(BF16), preferred_element_type=F32)


def _dot_nt(a, b):
    return lax.dot_general(a.astype(BF16), b.astype(BF16), (((1,), (1,)), ((), ())),
                           preferred_element_type=F32)


def _dot_tn(a, b):
    return lax.dot_general(a.astype(BF16), b.astype(BF16), (((0,), (0,)), ((), ())),
                           preferred_element_type=F32)


def _split3(a):
    hi = a.astype(BF16)
    r = a - hi.astype(F32)
    mid = r.astype(BF16)
    lo = (r - mid.astype(F32)).astype(BF16)
    return hi, mid, lo


def _resident(shape):
    nd = len(shape)
    return pl.BlockSpec(shape, lambda *_: (0,) * nd, pipeline_mode=pl.Buffered(1))


def _ffn_rows(x_ref, o_ref, pre_ref, post_ref, wg_ref, wu_ref, wd_ref):
    tm = x_ref.shape[0]
    sub = min(tm, FFN_SUB)
    parts = [slice(r, r + sub) for r in range(0, tm, sub)]
    xs = [x_ref[p, :] for p in parts]
    hs = [_rms(x, pre_ref[...]).astype(BF16) for x in xs]
    acts = []
    for h in hs:
        g = jnp.dot(h, wg_ref[...], preferred_element_type=F32)
        u = jnp.dot(h, wu_ref[...], preferred_element_type=F32)
        acts.append((_silu(g) * u).astype(BF16))
    for p, x, a in zip(parts, xs, acts):
        y = jnp.dot(a, wd_ref[...], preferred_element_type=F32)
        o_ref[p, :] = x + 0.5 * _rms(y, post_ref[...])


def _ffn_body(x_ref, pre_ref, post_ref, wg_ref, wu_ref, wd_ref, *rest, cast_transposed, has_extra):
    rest = list(rest)
    xe_ref = rest.pop(0) if has_extra else None
    n_cast = len(cast_transposed)
    cast_in, o_ref = rest[:n_cast], rest[n_cast]
    rest = rest[n_cast + 1:]
    oe_ref = rest.pop(0) if has_extra else None
    cast_out = rest
    weights = (pre_ref, post_ref, wg_ref, wu_ref, wd_ref)
    _ffn_rows(x_ref, o_ref, *weights)
    for src, dst, transposed in zip(cast_in, cast_out, cast_transposed):
        blk = src[...]
        dst[...] = (blk.T if transposed else blk).astype(BF16)
    if has_extra:
        @pl.when(pl.program_id(0) == pl.num_programs(0) - 1)
        def _():
            _ffn_rows(xe_ref, oe_ref, *weights)


def _cast_row_block(rows, steps):
    rb = BF16_ROWS
    while rows % rb or rows // rb > steps:
        rb += BF16_ROWS
    return rb


def _ffn(x, pre_g, post_g, wg, wu, wd, tm, casts=(), extra=None):
    n = x.shape[0]
    assert n % tm == 0
    steps = n // tm
    in_specs = [
        pl.BlockSpec((tm, D_MODEL), lambda i: (i, 0)),
        _resident((1, D_MODEL)), _resident((1, D_MODEL)),
        _resident((D_MODEL, D_FF)), _resident((D_MODEL, D_FF)), _resident((D_FF, D_MODEL)),
    ]
    out_specs = [pl.BlockSpec((tm, D_MODEL), lambda i: (i, 0))]
    out_shape = [jax.ShapeDtypeStruct((n, D_MODEL), F32)]
    operands = [x, pre_g, post_g, wg, wu, wd]
    if extra is not None:
        in_specs.append(_resident(extra.shape))
        operands.append(extra)
    n_fixed_in = len(in_specs)
    for arr, region in casts:
        rows, width = arr.shape
        if region is None:
            rb = _cast_row_block(rows, steps)
            last = rows // rb - 1
            spec = pl.BlockSpec((rb, width), lambda i, last=last: (jnp.minimum(i, last), 0))
            in_specs.append(spec)
            out_specs.append(spec)
            out_shape.append(jax.ShapeDtypeStruct((rows, width), BF16))
        else:
            row0, n_rows = region
            cb = HD
            while n_rows % cb or n_rows // cb > steps:
                cb += HD
            last = n_rows // cb - 1
            in_specs.append(pl.BlockSpec(
                (pl.Element(cb), pl.Element(width)),
                lambda i, row0=row0, last=last, cb=cb: (
                    pl.multiple_of(row0 + cb * jnp.minimum(i, last), F32_ROWS), 0)))
            out_specs.append(pl.BlockSpec((width, cb), lambda i, last=last: (0, jnp.minimum(i, last))))
            out_shape.append(jax.ShapeDtypeStruct((width, n_rows), BF16))
    assert len(in_specs) == n_fixed_in + len(casts)
    if extra is not None:
        out_specs.insert(1, pl.BlockSpec(extra.shape, lambda i: (0, 0)))
        out_shape.insert(1, jax.ShapeDtypeStruct(extra.shape, F32))
    outs = pl.pallas_call(
        functools.partial(_ffn_body, cast_transposed=tuple(c[1] is not None for c in casts),
                          has_extra=extra is not None),
        grid=(steps,),
        in_specs=in_specs,
        out_specs=out_specs,
        out_shape=out_shape,
        compiler_params=pltpu.CompilerParams(dimension_semantics=("arbitrary",),
                                             vmem_limit_bytes=VMEM_LIMIT),
        name="ffn",
    )(*operands, *[c[0] for c in casts])
    if extra is not None:
        return outs[0], outs[1], list(outs[2:])
    return outs[0], list(outs[1:])


def _ffn_stream_body(x_ref, pre_ref, post_ref, wg_ref, wu_ref, wd_ref,
                     o_ref, wg16_ref, wu16_ref, wd16_ref, h_s, acc_s):
    c = pl.program_id(0)

    @pl.when(c == 0)
    def _():
        h_s[...] = _rms(x_ref[...], pre_ref[...]).astype(BF16)
        acc_s[...] = jnp.zeros_like(acc_s)

    wg = wg_ref[...].astype(BF16)
    wu = wu_ref[...].astype(BF16)
    wd = wd_ref[...].astype(BF16)
    wg16_ref[...] = wg
    wu16_ref[...] = wu
    wd16_ref[...] = wd
    h = h_s[...]
    g = jnp.dot(h, wg, preferred_element_type=F32)
    u = jnp.dot(h, wu, preferred_element_type=F32)
    a = (_silu(g) * u).astype(BF16)
    acc_s[...] += jnp.dot(a, wd, preferred_element_type=F32)

    @pl.when(c == pl.num_programs(0) - 1)
    def _():
        o_ref[...] = x_ref[...] + 0.5 * _rms(acc_s[...], post_ref[...])


def _ffn_stream(x, pre_g, post_g, wg, wu, wd):
    n = x.shape[0]
    assert D_FF % FF_CHUNK == 0
    col = pl.BlockSpec((D_MODEL, FF_CHUNK), lambda c: (0, c))
    row = pl.BlockSpec((FF_CHUNK, D_MODEL), lambda c: (c, 0))
    return pl.pallas_call(
        _ffn_stream_body,
        grid=(D_FF // FF_CHUNK,),
        in_specs=[_resident((n, D_MODEL)), _resident((1, D_MODEL)), _resident((1, D_MODEL)), col, col, row],
        out_specs=[pl.BlockSpec((n, D_MODEL), lambda c: (0, 0)), col, col, row],
        out_shape=[
            jax.ShapeDtypeStruct((n, D_MODEL), F32),
            jax.ShapeDtypeStruct((D_MODEL, D_FF), BF16),
            jax.ShapeDtypeStruct((D_MODEL, D_FF), BF16),
            jax.ShapeDtypeStruct((D_FF, D_MODEL), BF16),
        ],
        scratch_shapes=[pltpu.VMEM((n, D_MODEL), BF16), pltpu.VMEM((n, D_MODEL), F32)],
        compiler_params=pltpu.CompilerParams(dimension_semantics=("arbitrary",),
                                             vmem_limit_bytes=VMEM_LIMIT),
        name="ffn_stream",
    )(x, pre_g, post_g, wg, wu, wd)


def _rope(x, cos, sin_signed):
    return x * cos + pltpu.roll(x, HD // 2, 1) * sin_signed


def _ret_out_norm(o, g_row, gate_act):
    mu = jnp.mean(o, axis=-1, keepdims=True)
    d = o - mu
    var = jnp.mean(d * d, axis=-1, keepdims=True)
    return gate_act * (d * lax.rsqrt(var + EPS) * g_row)


def _gdn_out_norm(o, g_row, gate_act):
    return o * lax.rsqrt(jnp.mean(o * o, axis=-1, keepdims=True) + EPS) * g_row * gate_act


def _l2norm(x, scale):
    return x * (lax.rsqrt(jnp.sum(x * x, axis=-1, keepdims=True) + EPS) * scale)


def _softplus(x):
    return jnp.maximum(x, 0.0) + jnp.log(1.0 + jnp.exp(-jnp.abs(x)))


def _merge(x, o_r, o_g, sig_gates, wrb_ref, wgb_ref, wout_ref, post_ref):
    y = (sig_gates[:, :D_MODEL] * _dot(o_r, wrb_ref[...])
         + sig_gates[:, D_MODEL:] * _dot(o_g, wgb_ref[...]))
    m = _dot(y, wout_ref[...])
    return x + _rms(m, post_ref[...])


def _pad_rows(a):
    return jnp.concatenate([a, jnp.zeros_like(a)], axis=0)


def _dot64(a, b):
    return jnp.dot(a.astype(BF16), _pad_rows(b.astype(BF16)), preferred_element_type=F32)


def _unit_lower_inverse_many(a_list, masks):
    eye, m16, off32, off64 = masks
    ad = [a * m16 for a in a_list]
    x = [eye - v for v in ad]
    p = [_dot64(v, v) for v in ad]
    for level in range(3):
        x = [xi + _dot64(xi, pi) for xi, pi in zip(x, p)]
        if level < 2:
            p = [_dot64(pi, pi) for pi in p]
    for m in (off32, off64):
        t = [_dot64(xi, a * m) for xi, a in zip(x, a_list)]
        x = [xi - _dot64(ti, xi) for xi, ti in zip(x, t)]
    return x


def _chunk_masks():
    r = lax.broadcasted_iota(jnp.int32, (CHUNK, 2 * CHUNK), 0)
    c = lax.broadcasted_iota(jnp.int32, (CHUNK, 2 * CHUNK), 1)
    one, zero = jnp.float32(1.0), jnp.float32(0.0)
    live = c < CHUNK
    eye = jnp.where(r == c, one, zero)
    m16 = jnp.where(live, jnp.where((r >> 4) == (c >> 4), one, zero), zero)
    m32 = jnp.where(live, jnp.where((r >> 5) == (c >> 5), one, zero), zero)
    m64 = jnp.where(live, one, zero)
    return r >= c, r > c, (eye, m16, m32 - m16, m64 - m32)


def _retention_tile(pm, cos, sin, rd_ref, ku_ref, retg_ref, s_ref, tile_decay):
    ri = lax.broadcasted_iota(jnp.int32, (TM, TM), 0)
    ci = lax.broadcasted_iota(jnp.int32, (TM, TM), 1)
    causal = ri >= ci
    heads = []
    for hh in range(HEADS):
        sl = slice(hh * HD, (hh + 1) * HD)
        rq = pm[:, OFF_RQ + hh * HD:OFF_RQ + (hh + 1) * HD]
        rk = pm[:, OFF_RK + hh * HD:OFF_RK + (hh + 1) * HD]
        v = pm[:, OFF_RV + hh * HD:OFF_RV + (hh + 1) * HD]
        rg = pm[:, OFF_RG + hh * HD:OFF_RG + (hh + 1) * HD]
        qs = _rope(rq, cos, sin) * rd_ref[hh]
        ku = _rope(rk, cos, sin) * ku_ref[hh]
        s = s_ref[hh]
        sc = jnp.where(causal, _dot_nt(qs, ku), 0.0)
        o = _dot(sc, v) + _dot(qs, s)
        s_ref[hh] = tile_decay[hh] * (s + _dot_tn(ku, v))
        heads.append(_ret_out_norm(o, retg_ref[:, sl], _silu(rg)))
    return jnp.concatenate(heads, axis=1)


def _short_conv_tile(u, buf, convw_ref, tail_ref):
    prev = buf[...]
    row = lax.broadcasted_iota(jnp.int32, (CONV_PAD, u.shape[1]), 0)
    acc = u * convw_ref[CONV_W - 1:CONV_W, :]
    for k in range(1, CONV_W):
        shifted = pltpu.roll(u, k, 0)
        head = jnp.where(row < k, pltpu.roll(prev, k, 0), shifted[0:CONV_PAD, :])
        shifted = jnp.concatenate([head, shifted[CONV_PAD:, :]], axis=0)
        acc = acc + shifted * convw_ref[CONV_W - 1 - k:CONV_W - k, :]
    buf[...] = u[TM - CONV_PAD:TM, :]
    tail_ref[...] = u[TM - (CONV_W - 1):TM, :]
    return _silu(acc)


def _gdn_decay_beta(ab, alog_ref, dtb_ref):
    g_all = -jnp.exp(alog_ref[...]) * _softplus(ab + dtb_ref[...])
    beta_all = jax.nn.sigmoid(ab)
    return g_all, beta_all


def _gdn_tiles(cqs, abs_, gzs, ltri, alog_ref, dtb_ref, gdng_ref, s_refs):
    seqs = range(len(cqs))
    dd = functools.partial(jnp.dot, preferred_element_type=F32)
    gcum, gcum_t, beta_all = [], [], []
    for b in seqs:
        g_all, beta = _gdn_decay_beta(abs_[b], alog_ref, dtb_ref)
        beta_all.append(beta)
        g_hi, g_mid, g_lo = _split3(g_all)
        gcum.append(dd(ltri, g_hi) + (dd(ltri, g_mid) + dd(ltri, g_lo)))
        gcum_t.append(gcum[b].T)

    causal64, strict64, masks = _chunk_masks()

    n_ch = TM // CHUNK
    rows = [slice(c * CHUNK, (c + 1) * CHUNK) for c in range(n_ch)]
    bh = [(b, hh) for b in seqs for hh in range(HEADS)]
    probs = [(b, hh, c) for b, hh in bh for c in range(n_ch)]
    gq = {(b, hh): _l2norm(cqs[b][:, hh * HD:(hh + 1) * HD], HD ** -0.5) for b, hh in bh}
    gk = {(b, hh): _l2norm(cqs[b][:, QK + hh * HD:QK + (hh + 1) * HD], 1.0) for b, hh in bh}
    gv = {(b, hh): cqs[b][:, 2 * QK + hh * HD:2 * QK + (hh + 1) * HD] for b, hh in bh}
    qc = {(b, hh, c): gq[b, hh][rows[c]] for b, hh, c in probs}
    kc = {(b, hh, c): gk[b, hh][rows[c]] for b, hh, c in probs}
    vc = {(b, hh, c): gv[b, hh][rows[c]] for b, hh, c in probs}
    gc = {(b, hh, c): gcum[b][rows[c], hh:hh + 1] for b, hh, c in probs}
    bcol = {(b, hh, c): beta_all[b][rows[c], HEADS + hh:HEADS + hh + 1] for b, hh, c in probs}
    def g_row(b, hh, c):
        gr = gcum_t[b][hh:hh + 1, rows[c]]
        return jnp.concatenate([gr, gr], axis=1)

    dm = {(b, hh, c): jnp.where(
        causal64, jnp.exp(jnp.minimum(gc[b, hh, c] - g_row(b, hh, c), 0.0)), 0.0)
        for b, hh, c in probs}
    kpad = {p: _pad_rows(kc[p].astype(BF16)) for p in probs}
    kk = {p: _dot_nt(kc[p], kpad[p]) for p in probs}
    qk = {p: _dot_nt(qc[p], kpad[p]) * dm[p] for p in probs}
    a_mats = [jnp.where(strict64, bcol[p] * dm[p] * kk[p], 0.0) for p in probs]
    tinv = dict(zip(probs, _unit_lower_inverse_many(a_mats, masks)))
    eg = {p: jnp.exp(gc[p]) for p in probs}
    uw = {p: _dot64(tinv[p], jnp.concatenate([bcol[p] * vc[p], (bcol[p] * eg[p]) * kc[p]], axis=1))
          for p in probs}
    gl = {p: gc[p][CHUNK - 1:CHUNK, :] for p in probs}
    uwp = {p: _pad_rows(uw[p].astype(BF16)) for p in probs}
    kdw = {p: _dot_tn(_pad_rows((kc[p] * jnp.exp(gl[p] - gc[p])).astype(BF16)), uwp[p])
           for p in probs}
    qkw = {p: jnp.dot(qk[p].astype(BF16), uwp[p], preferred_element_type=F32)
           for p in probs}
    lhs = {p: jnp.concatenate([kdw[p][:, HD:], eg[p] * qc[p] - qkw[p][:, HD:]], axis=0).astype(BF16)
           for p in probs}

    s = {(b, hh): s_refs[b][hh] for b, hh in bh}
    outs = {k: [] for k in bh}
    for c in range(n_ch):
        for b, hh in bh:
            p = (b, hh, c)
            ps = _dot(lhs[p], s[b, hh])
            outs[b, hh].append(ps[HD:, :] + qkw[p][:, :HD])
            s[b, hh] = jnp.exp(gl[p]) * s[b, hh] + (kdw[p][:, :HD] - ps[:HD, :])
    o_gs = []
    for b in seqs:
        heads = []
        for hh in range(HEADS):
            s_refs[b][hh] = s[b, hh]
            o = jnp.concatenate(outs[b, hh], axis=0)
            heads.append(_gdn_out_norm(o, gdng_ref[...], _silu(gzs[b][:, hh * HD:(hh + 1) * HD])))
        o_gs.append(jnp.concatenate(heads, axis=1))
    return o_gs


def _mixer_prompt_body(x_ref, cos_ref, sin_ref, rd_ref, ku_ref, ltri_ref,
                       pre_ref, post_ref, wmain_ref, wab_ref, wgates_ref, convw_ref, alog_ref, dtb_ref,
                       retg_ref, gdng_ref, wrb_ref, wgb_ref, wout_ref,
                       y_ref, sret_ref, sgdn_ref, conv_ref,
                       cbuf, *, ret_tile_decay):
    t = pl.program_id(1)
    nb = x_ref.shape[0]

    @pl.when(t == 0)
    def _():
        sret_ref[...] = jnp.zeros_like(sret_ref)
        sgdn_ref[...] = jnp.zeros_like(sgdn_ref)
        cbuf[:, 0:CONV_PAD, :] = jnp.zeros((nb, CONV_PAD, CONV_DIM), F32)

    cos = cos_ref[...]
    sin = sin_ref[...]
    ltri = ltri_ref[...]
    seqs = range(nb)
    rows = [slice(b * TM, (b + 1) * TM) for b in seqs]

    def stack(parts):
        return jnp.concatenate(list(parts), axis=0)

    def dot_all(lhs, w):
        return jnp.dot(lhs, w, preferred_element_type=F32)

    x_all = stack(x_ref[b] for b in seqs)
    h_all = _rms(x_all, pre_ref[...]).astype(BF16)
    pg_all = dot_all(h_all, wmain_ref[:, OFF_CONV:OFF_AB])
    ab_all = dot_all(h_all, wab_ref[...])
    cqs = [_short_conv_tile(pg_all[rows[b], 0:CONV_DIM], cbuf.at[b], convw_ref, conv_ref.at[b]) for b in seqs]
    pm_all = dot_all(h_all, wmain_ref[:, 0:OFF_CONV])
    o_rs = [_retention_tile(pm_all[rows[b]], cos, sin, rd_ref, ku_ref, retg_ref, sret_ref.at[b],
                            ret_tile_decay) for b in seqs]
    o_gs = _gdn_tiles(cqs, [ab_all[rows[b]] for b in seqs], [pg_all[rows[b], CONV_DIM:] for b in seqs],
                      ltri, alog_ref, dtb_ref, gdng_ref, [sgdn_ref.at[b] for b in seqs])
    gates = dot_all(h_all, wgates_ref[...])
    y_all = _merge(x_all, stack(o_rs), stack(o_gs), jax.nn.sigmoid(gates), wrb_ref, wgb_ref, wout_ref,
                   post_ref)
    for b in seqs:
        y_ref[b] = y_all[rows[b]]


def _ret_gammas():
    return 1.0 - 2.0 ** (-5.0 - np.arange(HEADS, dtype=np.float64))


def _rope_tables(pos):
    inv = ROPE_BASE ** (-np.arange(0, HD, 2, dtype=np.float64) / HD)
    ang = np.asarray(pos, np.float64)[:, None] * inv[None, :]
    cos = np.concatenate([np.cos(ang), np.cos(ang)], axis=1)
    sin = np.concatenate([-np.sin(ang), np.sin(ang)], axis=1)
    return jnp.asarray(cos, F32), jnp.asarray(sin, F32)


def _mixer_weight_specs():
    return [
        _resident((1, D_MODEL)), _resident((1, D_MODEL)),
        _resident((D_MODEL, OFF_AB)), _resident((D_MODEL, HD)), _resident((D_MODEL, 2 * D_MODEL)),
        _resident((CONV_W, CONV_DIM)), _resident((1, HD)), _resident((1, HD)),
        _resident((1, QK)), _resident((1, HD)),
        _resident((QK, D_MODEL)), _resident((QK, D_MODEL)), _resident((D_MODEL, D_MODEL)),
    ]


def _mixer_weight_args(w):
    return (w["mix_pre_g"], w["mix_post_g"], w["w_main"], w["w_ab"], w["w_gates"], w["conv_w"],
            w["a_log"], w["dt_bias"], w["ret_norm_g"], w["gdn_norm_g"], w["w_rb"], w["w_gb"], w["w_out"])


def _mixer_prompt(x, w):
    b, t, _ = x.shape
    assert t % TM == 0 and b % NB == 0
    cos, sin = _rope_tables(np.arange(t))
    gam = _ret_gammas()
    i1 = np.arange(1, TM + 1, dtype=np.float64)
    rd = np.broadcast_to((gam[:, None] ** i1[None, :])[:, :, None], (HEADS, TM, HD))
    ku = np.broadcast_to((HD ** -0.5 * gam[:, None] ** (-i1[None, :]))[:, :, None], (HEADS, TM, HD))
    tile_decay = tuple(float(v) for v in gam ** TM)
    r = np.arange(TM)
    ltri = ((r[:, None] >= r[None, :]) & (r[:, None] // CHUNK == r[None, :] // CHUNK))

    body = functools.partial(_mixer_prompt_body, ret_tile_decay=tile_decay)
    state_spec = pl.BlockSpec((NB, HEADS, HD, HD), lambda i, j: (i, 0, 0, 0))
    return pl.pallas_call(
        body,
        grid=(b // NB, t // TM),
        in_specs=[
            pl.BlockSpec((NB, TM, D_MODEL), lambda i, j: (i, j, 0)),
            pl.BlockSpec((TM, HD), lambda i, j: (j, 0)),
            pl.BlockSpec((TM, HD), lambda i, j: (j, 0)),
            _resident((HEADS, TM, HD)), _resident((HEADS, TM, HD)), _resident((TM, TM)),
        ] + _mixer_weight_specs(),
        out_specs=[
            pl.BlockSpec((NB, TM, D_MODEL), lambda i, j: (i, j, 0)),
            state_spec, state_spec,
            pl.BlockSpec((NB, CONV_W - 1, CONV_DIM), lambda i, j: (i, 0, 0)),
        ],
        out_shape=[
            jax.ShapeDtypeStruct((b, t, D_MODEL), F32),
            jax.ShapeDtypeStruct((b, HEADS, HD, HD), F32),
            jax.ShapeDtypeStruct((b, HEADS, HD, HD), F32),
            jax.ShapeDtypeStruct((b, CONV_W - 1, CONV_DIM), F32),
        ],
        scratch_shapes=[pltpu.VMEM((NB, CONV_PAD, CONV_DIM), F32)],
        compiler_params=pltpu.CompilerParams(dimension_semantics=("arbitrary", "arbitrary"),
                                             vmem_limit_bytes=VMEM_LIMIT),
        name="mixer_prompt",
    )(x, cos, sin, jnp.asarray(rd, F32), jnp.asarray(ku, F32), jnp.asarray(ltri, BF16),
      *_mixer_weight_args(w))


def _pick_rows(rows):
    ri = lax.broadcasted_iota(jnp.int32, rows[0].shape, 0)
    out = rows[0]
    for j in range(1, len(rows)):
        out = jnp.where(ri == j, rows[j], out)
    return out


def _mixer_sample_body(x_ref, cos_ref, sin_ref,
                       pre_ref, post_ref, wmain_ref, wab_ref, wgates_ref, convw_ref, alog_ref, dtb_ref,
                       retg_ref, gdng_ref, wrb_ref, wgb_ref, wout_ref,
                       sret_in, sgdn_in, conv_in,
                       y_ref, sret_out, sgdn_out, conv_out,
                       pm_s, gates_s, rq_s, rk_s, gq_s, gk_s, gv_s, eg_s, beta_s, or_s, og_s, *, ret_gamma):
    i = pl.program_id(0)

    @pl.when(i == 0)
    def _():
        h = _rms(x_ref[...], pre_ref[...]).astype(BF16)
        pm = jnp.dot(h, wmain_ref[...], preferred_element_type=F32)
        ab = jnp.dot(h, wab_ref[...], preferred_element_type=F32)
        pm_s[...] = pm
        gates_s[...] = jnp.dot(h, wgates_ref[...], preferred_element_type=F32)
        cos = cos_ref[...]
        sin = sin_ref[...]
        cin = pm[:, OFF_CONV:OFF_CONV + CONV_DIM]
        acc = cin * convw_ref[CONV_W - 1:CONV_W, :]
        for r in range(CONV_W - 1):
            acc = acc + conv_in[r] * convw_ref[r:r + 1, :]
        for r in range(CONV_W - 2):
            conv_out[r] = conv_in[r + 1]
        conv_out[CONV_W - 2] = cin
        cq = _silu(acc)
        g_all, beta_all = _gdn_decay_beta(ab, alog_ref, dtb_ref)
        eg_s[...] = jnp.exp(g_all)
        beta_s[...] = beta_all
        for hh in range(HEADS):
            sl = slice(hh * HD, (hh + 1) * HD)
            rq_s[:, sl] = _rope(pm[:, OFF_RQ + hh * HD:OFF_RQ + (hh + 1) * HD], cos, sin)
            rk_s[:, sl] = _rope(pm[:, OFF_RK + hh * HD:OFF_RK + (hh + 1) * HD], cos, sin) * (HD ** -0.5)
            gq_s[:, sl] = _l2norm(cq[:, hh * HD:(hh + 1) * HD], HD ** -0.5)
            gk_s[:, sl] = _l2norm(cq[:, QK + hh * HD:QK + (hh + 1) * HD], 1.0)
        gv_s[...] = cq[:, 2 * QK:3 * QK]

    rows = pl.ds(pl.multiple_of(i * TB, TB), TB)
    eg_all = eg_s[rows, :]
    beta_all = beta_s[rows, :]
    for hh in range(HEADS):
        sl = slice(hh * HD, (hh + 1) * HD)
        q = rq_s[rows, sl]
        k_t = rk_s[rows, sl].T
        v = pm_s[rows, OFF_RV + hh * HD:OFF_RV + (hh + 1) * HD]
        o_rows = []
        for j in range(TB):
            s = ret_gamma[hh] * sret_in[j, hh] + k_t[:, j:j + 1] * v[j:j + 1, :]
            sret_out[j, hh] = s
            o_rows.append(_dot(q, s))
        or_s[rows, sl] = _pick_rows(o_rows)

        gq = gq_s[rows, sl]
        gk = gk_s[rows, sl]
        gk_t = gk.T
        gv = gv_s[rows, sl]
        o_rows = []
        for j in range(TB):
            s = sgdn_in[j, hh]
            eg = eg_all[j:j + 1, hh:hh + 1]
            beta = beta_all[j:j + 1, HEADS + hh:HEADS + hh + 1]
            ks = _dot(gk, s)[j:j + 1, :]
            u = beta * gv[j:j + 1, :] - (beta * eg) * ks
            s = eg * s + gk_t[:, j:j + 1] * u
            sgdn_out[j, hh] = s
            o_rows.append(_dot(gq, s))
        og_s[rows, sl] = _pick_rows(o_rows)

    @pl.when(i == pl.num_programs(0) - 1)
    def _():
        o_r = []
        o_g = []
        for hh in range(HEADS):
            sl = slice(hh * HD, (hh + 1) * HD)
            o_r.append(_ret_out_norm(or_s[:, sl], retg_ref[:, sl],
                                     _silu(pm_s[:, OFF_RG + hh * HD:OFF_RG + (hh + 1) * HD])))
            o_g.append(_gdn_out_norm(og_s[:, sl], gdng_ref[...],
                                     _silu(pm_s[:, OFF_GZ + hh * HD:OFF_GZ + (hh + 1) * HD])))
        y_ref[...] = _merge(x_ref[...], jnp.concatenate(o_r, axis=1), jnp.concatenate(o_g, axis=1),
                            jax.nn.sigmoid(gates_s[...]), wrb_ref, wgb_ref, wout_ref, post_ref)


def _mixer_sample(x, s_ret, s_gdn, s_conv, w, pos):
    n = x.shape[0]
    assert n % TB == 0
    cos, sin = _rope_tables([pos])
    body = functools.partial(_mixer_sample_body, ret_gamma=tuple(float(v) for v in _ret_gammas()))
    state_spec = pl.BlockSpec((TB, HEADS, HD, HD), lambda i: (i, 0, 0, 0))
    conv_shape = (CONV_W - 1, n, CONV_DIM)
    return pl.pallas_call(
        body,
        grid=(n // TB,),
        in_specs=[_resident((n, D_MODEL)), _resident((1, HD)), _resident((1, HD))]
        + _mixer_weight_specs()
        + [state_spec, state_spec, _resident(conv_shape)],
        out_specs=[pl.BlockSpec((n, D_MODEL), lambda i: (0, 0)), state_spec, state_spec,
                   pl.BlockSpec(conv_shape, lambda i: (0, 0, 0))],
        out_shape=[
            jax.ShapeDtypeStruct((n, D_MODEL), F32),
            jax.ShapeDtypeStruct((n, HEADS, HD, HD), F32),
            jax.ShapeDtypeStruct((n, HEADS, HD, HD), F32),
            jax.ShapeDtypeStruct(conv_shape, F32),
        ],
        scratch_shapes=[pltpu.VMEM((n, OFF_AB), F32), pltpu.VMEM((n, 2 * D_MODEL), F32)]
        + [pltpu.VMEM((n, QK), F32) for _ in range(5)]
        + [pltpu.VMEM((n, HD), F32), pltpu.VMEM((n, HD), F32)]
        + [pltpu.VMEM((n, QK), F32), pltpu.VMEM((n, QK), F32)],
        compiler_params=pltpu.CompilerParams(dimension_semantics=("arbitrary",),
                                             vmem_limit_bytes=VMEM_LIMIT),
        name="mixer_sample",
    )(x, cos, sin, *_mixer_weight_args(w), s_ret, s_gdn, s_conv)


def _pad_lanes(v, n):
    return jnp.pad(v, ((0, 0), (0, n - v.shape[1])))


def kernel(x_prompt, x_sample, state_ret, state_gdn, state_conv, ffn1_pre_g, ffn1_post_g, ffn1_w_gate,
           ffn1_w_up, ffn1_w_down, mix_pre_g, mix_post_g, w_in, ret_norm_g, gdn_conv_w, gdn_a_log,
           gdn_dt_bias, gdn_norm_g, w_ret_branch, w_gdn_branch, w_out, ffn2_pre_g, ffn2_post_g,
           ffn2_w_gate, ffn2_w_up, ffn2_w_down):
    depth = w_in.shape[0]
    b, t, _ = x_prompt.shape
    n_s, t_s, _ = x_sample.shape
    assert t_s == 1
    yp = x_prompt.reshape(b * t, D_MODEL)
    ys = x_sample.reshape(n_s, D_MODEL)
    outs = [[] for _ in range(6)]
    for l in range(depth):
        row = lambda a: a[l][None, :]
        ys, wg1, wu1, wd1 = _ffn_stream(ys, row(ffn1_pre_g), row(ffn1_post_g),
                                        ffn1_w_gate[l], ffn1_w_up[l], ffn1_w_down[l])
        f1 = (row(ffn1_pre_g), row(ffn1_post_g), wg1, wu1, wd1)

        w_in_t = w_in[l].T
        casts = (
            (w_in_t, (0, OFF_AB)), (w_in_t, (OFF_GATES, 2 * D_MODEL)),
            (w_ret_branch[l], None), (w_gdn_branch[l], None), (w_out[l], None),
            (ffn2_w_gate[l], None), (ffn2_w_up[l], None), (ffn2_w_down[l], None),
        )
        yp, (w_main, w_gates, w_rb, w_gb, w_o, g2, u2, d2) = _ffn(yp, *f1, tm=FFN_TM, casts=casts)
        w = {
            "mix_pre_g": row(mix_pre_g), "mix_post_g": row(mix_post_g),
            "w_main": w_main,
            "w_ab": _pad_lanes(w_in[l, :, OFF_AB:OFF_GATES], HD).astype(BF16),
            "w_gates": w_gates,
            "conv_w": gdn_conv_w[l],
            "a_log": _pad_lanes(row(gdn_a_log), HD), "dt_bias": _pad_lanes(row(gdn_dt_bias), HD),
            "ret_norm_g": row(ret_norm_g), "gdn_norm_g": row(gdn_norm_g),
            "w_rb": w_rb, "w_gb": w_gb, "w_out": w_o,
        }
        f2 = (row(ffn2_pre_g), row(ffn2_post_g), g2, u2, d2)

        yp, r1, g1, c1 = _mixer_prompt(yp.reshape(b, t, D_MODEL), w)
        ys, r2, g2s, c2 = _mixer_sample(ys, state_ret[l], state_gdn[l],
                                        jnp.swapaxes(state_conv[l], 0, 1), w, PAST_LEN)
        c2 = jnp.swapaxes(c2, 0, 1)
        yp, ys, _ = _ffn(yp.reshape(b * t, D_MODEL), *f2, tm=FFN_TM, extra=ys)
        for lst, val in zip(outs, (r1, g1, c1, r2, g2s, c2)):
            lst.append(val)
    stacked = [v[0][None] if depth == 1 else jnp.stack(v) for v in outs]
    return (yp.reshape(b, t, D_MODEL), ys.reshape(n_s, t_s, D_MODEL), *stacked)
```

```python
import functools

import numpy as np
import jax
import jax.numpy as jnp
from jax import lax
from jax.experimental import pallas as pl
from jax.experimental.pallas import tpu as pltpu

F32 = jnp.float32
BF16 = jnp.bfloat16

D_MODEL = 1024
D_FF = 2816
HEADS = 4
HD = 128
QK = HEADS * HD
CONV_W = 4
CONV_DIM = 3 * QK
CHUNK = 64
ROPE_BASE = 10000.0
EPS = 1e-6
PAST_LEN = 16384

OFF_RQ, OFF_RK, OFF_RV, OFF_RG = 0, QK, 2 * QK, 3 * QK
OFF_CONV = 4 * QK
OFF_GZ = OFF_CONV + CONV_DIM
OFF_AB = OFF_GZ + QK
OFF_GATES = OFF_AB + 2 * HEADS
D_IN = OFF_GATES + 2 * D_MODEL

TM = 256
NB = 2
TB = 8
FFN_TM = 1024
FFN_SUB = 256
FF_CHUNK = 256
CONV_PAD = 8
BF16_ROWS = 16
F32_ROWS = 8

VMEM_LIMIT = 56 * 1024 * 1024


def _silu(x):
    return x * jax.nn.sigmoid(x)


def _rms(x, g):
    return x * lax.rsqrt(jnp.mean(x * x, axis=-1, keepdims=True) + EPS) * g


def _dot(a, b):
    return jnp.dot(a.astype(BF16), b.astype(BF16), preferred_element_type=F32)


def _dot_nt(a, b):
    return lax.dot_general(a.astype(BF16), b.astype(BF16), (((1,), (1,)), ((), ())),
                           preferred_element_type=F32)


def _dot_tn(a, b):
    return lax.dot_general(a.astype(BF16), b.astype(BF16), (((0,), (0,)), ((), ())),
                           preferred_element_type=F32)


def _split3(a):
    hi = a.astype(BF16)
    r = a - hi.astype(F32)
    mid = r.astype(BF16)
    lo = (r - mid.astype(F32)).astype(BF16)
    return hi, mid, lo


def _resident(shape):
    nd = len(shape)
    return pl.BlockSpec(shape, lambda *_: (0,) * nd, pipeline_mode=pl.Buffered(1))


def _ffn_rows(x_ref, o_ref, pre_ref, post_ref, wg_ref, wu_ref, wd_ref):
    tm = x_ref.shape[0]
    sub = min(tm, FFN_SUB)
    parts = [slice(r, r + sub) for r in range(0, tm, sub)]
    xs = [x_ref[p, :] for p in parts]
    hs = [_rms(x, pre_ref[...]).astype(BF16) for x in xs]
    acts = []
    for h in hs:
        g = jnp.dot(h, wg_ref[...], preferred_element_type=F32)
        u = jnp.dot(h, wu_ref[...], preferred_element_type=F32)
        acts.append((_silu(g) * u).astype(BF16))
    for p, x, a in zip(parts, xs, acts):
        y = jnp.dot(a, wd_ref[...], preferred_element_type=F32)
        o_ref[p, :] = x + 0.5 * _rms(y, post_ref[...])


def _ffn_body(x_ref, pre_ref, post_ref, wg_ref, wu_ref, wd_ref, *rest, cast_transposed, has_extra):
    rest = list(rest)
    xe_ref = rest.pop(0) if has_extra else None
    n_cast = len(cast_transposed)
    cast_in, o_ref = rest[:n_cast], rest[n_cast]
    rest = rest[n_cast + 1:]
    oe_ref = rest.pop(0) if has_extra else None
    cast_out = rest
    weights = (pre_ref, post_ref, wg_ref, wu_ref, wd_ref)
    _ffn_rows(x_ref, o_ref, *weights)
    for src, dst, transposed in zip(cast_in, cast_out, cast_transposed):
        blk = src[...]
        dst[...] = (blk.T if transposed else blk).astype(BF16)
    if has_extra:
        @pl.when(pl.program_id(0) == pl.num_programs(0) - 1)
        def _():
            _ffn_rows(xe_ref, oe_ref, *weights)


def _cast_row_block(rows, steps):
    rb = BF16_ROWS
    while rows % rb or rows // rb > steps:
        rb += BF16_ROWS
    return rb


def _ffn(x, pre_g, post_g, wg, wu, wd, tm, casts=(), extra=None):
    n = x.shape[0]
    assert n % tm == 0
    steps = n // tm
    in_specs = [
        pl.BlockSpec((tm, D_MODEL), lambda i: (i, 0)),
        _resident((1, D_MODEL)), _resident((1, D_MODEL)),
        _resident((D_MODEL, D_FF)), _resident((D_MODEL, D_FF)), _resident((D_FF, D_MODEL)),
    ]
    out_specs = [pl.BlockSpec((tm, D_MODEL), lambda i: (i, 0))]
    out_shape = [jax.ShapeDtypeStruct((n, D_MODEL), F32)]
    operands = [x, pre_g, post_g, wg, wu, wd]
    if extra is not None:
        in_specs.append(_resident(extra.shape))
        operands.append(extra)
    n_fixed_in = len(in_specs)
    for arr, region in casts:
        rows, width = arr.shape
        if region is None:
            rb = _cast_row_block(rows, steps)
            last = rows // rb - 1
            spec = pl.BlockSpec((rb, width), lambda i, last=last: (jnp.minimum(i, last), 0))
            in_specs.append(spec)
            out_specs.append(spec)
            out_shape.append(jax.ShapeDtypeStruct((rows, width), BF16))
        else:
            row0, n_rows = region
            cb = HD
            while n_rows % cb or n_rows // cb > steps:
                cb += HD
            last = n_rows // cb - 1
            in_specs.append(pl.BlockSpec(
                (pl.Element(cb), pl.Element(width)),
                lambda i, row0=row0, last=last, cb=cb: (
                    pl.multiple_of(row0 + cb * jnp.minimum(i, last), F32_ROWS), 0)))
            out_specs.append(pl.BlockSpec((width, cb), lambda i, last=last: (0, jnp.minimum(i, last))))
            out_shape.append(jax.ShapeDtypeStruct((width, n_rows), BF16))
    assert len(in_specs) == n_fixed_in + len(casts)
    if extra is not None:
        out_specs.insert(1, pl.BlockSpec(extra.shape, lambda i: (0, 0)))
        out_shape.insert(1, jax.ShapeDtypeStruct(extra.shape, F32))
    outs = pl.pallas_call(
        functools.partial(_ffn_body, cast_transposed=tuple(c[1] is not None for c in casts),
                          has_extra=extra is not None),
        grid=(steps,),
        in_specs=in_specs,
        out_specs=out_specs,
        out_shape=out_shape,
        compiler_params=pltpu.CompilerParams(dimension_semantics=("arbitrary",),
                                             vmem_limit_bytes=VMEM_LIMIT),
        name="ffn",
    )(*operands, *[c[0] for c in casts])
    if extra is not None:
        return outs[0], outs[1], list(outs[2:])
    return outs[0], list(outs[1:])


def _ffn_stream_body(x_ref, pre_ref, post_ref, wg_ref, wu_ref, wd_ref,
                     o_ref, wg16_ref, wu16_ref, wd16_ref, h_s, acc_s):
    c = pl.program_id(0)

    @pl.when(c == 0)
    def _():
        h_s[...] = _rms(x_ref[...], pre_ref[...]).astype(BF16)
        acc_s[...] = jnp.zeros_like(acc_s)

    wg = wg_ref[...].astype(BF16)
    wu = wu_ref[...].astype(BF16)
    wd = wd_ref[...].astype(BF16)
    wg16_ref[...] = wg
    wu16_ref[...] = wu
    wd16_ref[...] = wd
    h = h_s[...]
    g = jnp.dot(h, wg, preferred_element_type=F32)
    u = jnp.dot(h, wu, preferred_element_type=F32)
    a = (_silu(g) * u).astype(BF16)
    acc_s[...] += jnp.dot(a, wd, preferred_element_type=F32)

    @pl.when(c == pl.num_programs(0) - 1)
    def _():
        o_ref[...] = x_ref[...] + 0.5 * _rms(acc_s[...], post_ref[...])


def _ffn_stream(x, pre_g, post_g, wg, wu, wd):
    n = x.shape[0]
    assert D_FF % FF_CHUNK == 0
    col = pl.BlockSpec((D_MODEL, FF_CHUNK), lambda c: (0, c))
    row = pl.BlockSpec((FF_CHUNK, D_MODEL), lambda c: (c, 0))
    return pl.pallas_call(
        _ffn_stream_body,
        grid=(D_FF // FF_CHUNK,),
        in_specs=[_resident((n, D_MODEL)), _resident((1, D_MODEL)), _resident((1, D_MODEL)), col, col, row],
        out_specs=[pl.BlockSpec((n, D_MODEL), lambda c: (0, 0)), col, col, row],
        out_shape=[
            jax.ShapeDtypeStruct((n, D_MODEL), F32),
            jax.ShapeDtypeStruct((D_MODEL, D_FF), BF16),
            jax.ShapeDtypeStruct((D_MODEL, D_FF), BF16),
            jax.ShapeDtypeStruct((D_FF, D_MODEL), BF16),
        ],
        scratch_shapes=[pltpu.VMEM((n, D_MODEL), BF16), pltpu.VMEM((n, D_MODEL), F32)],
        compiler_params=pltpu.CompilerParams(dimension_semantics=("arbitrary",),
                                             vmem_limit_bytes=VMEM_LIMIT),
        name="ffn_stream",
    )(x, pre_g, post_g, wg, wu, wd)


def _rope(x, cos, sin_signed):
    return x * cos + pltpu.roll(x, HD // 2, 1) * sin_signed


def _ret_out_norm(o, g_row, gate_act):
    mu = jnp.mean(o, axis=-1, keepdims=True)
    d = o - mu
    var = jnp.mean(d * d, axis=-1, keepdims=True)
    return gate_act * (d * lax.rsqrt(var + EPS) * g_row)


def _gdn_out_norm(o, g_row, gate_act):
    return o * lax.rsqrt(jnp.mean(o * o, axis=-1, keepdims=True) + EPS) * g_row * gate_act


def _l2norm(x, scale):
    return x * (lax.rsqrt(jnp.sum(x * x, axis=-1, keepdims=True) + EPS) * scale)


def _softplus(x):
    return jnp.maximum(x, 0.0) + jnp.log(1.0 + jnp.exp(-jnp.abs(x)))


def _merge(x, o_r, o_g, sig_gates, wrb_ref, wgb_ref, wout_ref, post_ref):
    y = (sig_gates[:, :D_MODEL] * _dot(o_r, wrb_ref[...])
         + sig_gates[:, D_MODEL:] * _dot(o_g, wgb_ref[...]))
    m = _dot(y, wout_ref[...])
    return x + _rms(m, post_ref[...])


def _pad_rows(a):
    return jnp.concatenate([a, jnp.zeros_like(a)], axis=0)


def _dot64(a, b):
    return jnp.dot(a.astype(BF16), _pad_rows(b.astype(BF16)), preferred_element_type=F32)


def _unit_lower_inverse_many(a_list, masks):
    eye, m16, off32, off64 = masks
    ad = [a * m16 for a in a_list]
    x = [eye - v for v in ad]
    p = [_dot64(v, v) for v in ad]
    for level in range(3):
        x = [xi + _dot64(xi, pi) for xi, pi in zip(x, p)]
        if level < 2:
            p = [_dot64(pi, pi) for pi in p]
    for m in (off32, off64):
        t = [_dot64(xi, a * m) for xi, a in zip(x, a_list)]
        x = [xi - _dot64(ti, xi) for xi, ti in zip(x, t)]
    return x


def _chunk_masks():
    r = lax.broadcasted_iota(jnp.int32, (CHUNK, 2 * CHUNK), 0)
    c = lax.broadcasted_iota(jnp.int32, (CHUNK, 2 * CHUNK), 1)
    one, zero = jnp.float32(1.0), jnp.float32(0.0)
    live = c < CHUNK
    eye = jnp.where(r == c, one, zero)
    m16 = jnp.where(live, jnp.where((r >> 4) == (c >> 4), one, zero), zero)
    m32 = jnp.where(live, jnp.where((r >> 5) == (c >> 5), one, zero), zero)
    m64 = jnp.where(live, one, zero)
    return r >= c, r > c, (eye, m16, m32 - m16, m64 - m32)


def _retention_tile(pm, cos, sin, rd_ref, ku_ref, retg_ref, s_ref, tile_decay):
    ri = lax.broadcasted_iota(jnp.int32, (TM, TM), 0)
    ci = lax.broadcasted_iota(jnp.int32, (TM, TM), 1)
    causal = ri >= ci
    heads = []
    for hh in range(HEADS):
        sl = slice(hh * HD, (hh + 1) * HD)
        rq = pm[:, OFF_RQ + hh * HD:OFF_RQ + (hh + 1) * HD]
        rk = pm[:, OFF_RK + hh * HD:OFF_RK + (hh + 1) * HD]
        v = pm[:, OFF_RV + hh * HD:OFF_RV + (hh + 1) * HD]
        rg = pm[:, OFF_RG + hh * HD:OFF_RG + (hh + 1) * HD]
        qs = _rope(rq, cos, sin) * rd_ref[hh]
        ku = _rope(rk, cos, sin) * ku_ref[hh]
        s = s_ref[hh]
        sc = jnp.where(causal, _dot_nt(qs, ku), 0.0)
        o = _dot(sc, v) + _dot(qs, s)
        s_ref[hh] = tile_decay[hh] * (s + _dot_tn(ku, v))
        heads.append(_ret_out_norm(o, retg_ref[:, sl], _silu(rg)))
    return jnp.concatenate(heads, axis=1)


def _short_conv_tile(u, buf, convw_ref, tail_ref):
    prev = buf[...]
    row = lax.broadcasted_iota(jnp.int32, (CONV_PAD, u.shape[1]), 0)
    acc = u * convw_ref[CONV_W - 1:CONV_W, :]
    for k in range(1, CONV_W):
        shifted = pltpu.roll(u, k, 0)
        head = jnp.where(row < k, pltpu.roll(prev, k, 0), shifted[0:CONV_PAD, :])
        shifted = jnp.concatenate([head, shifted[CONV_PAD:, :]], axis=0)
        acc = acc + shifted * convw_ref[CONV_W - 1 - k:CONV_W - k, :]
    buf[...] = u[TM - CONV_PAD:TM, :]
    tail_ref[...] = u[TM - (CONV_W - 1):TM, :]
    return _silu(acc)


def _gdn_decay_beta(ab, alog_ref, dtb_ref):
    g_all = -jnp.exp(alog_ref[...]) * _softplus(ab + dtb_ref[...])
    beta_all = jax.nn.sigmoid(ab)
    return g_all, beta_all


def _gdn_tiles(cqs, abs_, gzs, ltri, alog_ref, dtb_ref, gdng_ref, s_refs):
    seqs = range(len(cqs))
    dd = functools.partial(jnp.dot, preferred_element_type=F32)
    gcum, gcum_t, beta_all = [], [], []
    for b in seqs:
        g_all, beta = _gdn_decay_beta(abs_[b], alog_ref, dtb_ref)
        beta_all.append(beta)
        g_hi, g_mid, g_lo = _split3(g_all)
        gcum.append(dd(ltri, g_hi) + (dd(ltri, g_mid) + dd(ltri, g_lo)))
        gcum_t.append(gcum[b].T)

    causal64, strict64, masks = _chunk_masks()

    n_ch = TM // CHUNK
    rows = [slice(c * CHUNK, (c + 1) * CHUNK) for c in range(n_ch)]
    bh = [(b, hh) for b in seqs for hh in range(HEADS)]
    probs = [(b, hh, c) for b, hh in bh for c in range(n_ch)]
    gq = {(b, hh): _l2norm(cqs[b][:, hh * HD:(hh + 1) * HD], HD ** -0.5) for b, hh in bh}
    gk = {(b, hh): _l2norm(cqs[b][:, QK + hh * HD:QK + (hh + 1) * HD], 1.0) for b, hh in bh}
    gv = {(b, hh): cqs[b][:, 2 * QK + hh * HD:2 * QK + (hh + 1) * HD] for b, hh in bh}
    qc = {(b, hh, c): gq[b, hh][rows[c]] for b, hh, c in probs}
    kc = {(b, hh, c): gk[b, hh][rows[c]] for b, hh, c in probs}
    vc = {(b, hh, c): gv[b, hh][rows[c]] for b, hh, c in probs}
    gc = {(b, hh, c): gcum[b][rows[c], hh:hh + 1] for b, hh, c in probs}
    bcol = {(b, hh, c): beta_all[b][rows[c], HEADS + hh:HEADS + hh + 1] for b, hh, c in probs}
    def g_row(b, hh, c):
        gr = gcum_t[b][hh:hh + 1, rows[c]]
        return jnp.concatenate([gr, gr], axis=1)

    dm = {(b, hh, c): jnp.where(
        causal64, jnp.exp(jnp.minimum(gc[b, hh, c] - g_row(b, hh, c), 0.0)), 0.0)
        for b, hh, c in probs}
    kpad = {p: _pad_rows(kc[p].astype(BF16)) for p in probs}
    kq = {p: _dot_nt(jnp.concatenate([kc[p], qc[p]], axis=0), kpad[p]) for p in probs}
    kk = {p: kq[p][:CHUNK] for p in probs}
    qk = {p: kq[p][CHUNK:] * dm[p] for p in probs}
    a_mats = [jnp.where(strict64, bcol[p] * dm[p] * kk[p], 0.0) for p in probs]
    tinv = dict(zip(probs, _unit_lower_inverse_many(a_mats, masks)))
    eg = {p: jnp.exp(gc[p]) for p in probs}
    uw = {p: _dot64(tinv[p], jnp.concatenate([bcol[p] * vc[p], (bcol[p] * eg[p]) * kc[p]], axis=1))
          for p in probs}
    gl = {p: gc[p][CHUNK - 1:CHUNK, :] for p in probs}
    uwp = {p: _pad_rows(uw[p].astype(BF16)) for p in probs}
    kdw = {p: _dot_tn(_pad_rows((kc[p] * jnp.exp(gl[p] - gc[p])).astype(BF16)), uwp[p])
           for p in probs}
    qkw = {p: jnp.dot(qk[p].astype(BF16), uwp[p], preferred_element_type=F32)
           for p in probs}
    lhs = {p: jnp.concatenate([kdw[p][:, HD:], eg[p] * qc[p] - qkw[p][:, HD:]], axis=0).astype(BF16)
           for p in probs}

    s = {(b, hh): s_refs[b][hh] for b, hh in bh}
    outs = {k: [] for k in bh}
    for c in range(n_ch):
        for b, hh in bh:
            p = (b, hh, c)
            ps = _dot(lhs[p], s[b, hh])
            outs[b, hh].append(ps[HD:, :] + qkw[p][:, :HD])
            s[b, hh] = jnp.exp(gl[p]) * s[b, hh] + (kdw[p][:, :HD] - ps[:HD, :])
    o_gs = []
    for b in seqs:
        heads = []
        for hh in range(HEADS):
            s_refs[b][hh] = s[b, hh]
            o = jnp.concatenate(outs[b, hh], axis=0)
            heads.append(_gdn_out_norm(o, gdng_ref[...], _silu(gzs[b][:, hh * HD:(hh + 1) * HD])))
        o_gs.append(jnp.concatenate(heads, axis=1))
    return o_gs


def _mixer_prompt_body(x_ref, cos_ref, sin_ref, rd_ref, ku_ref, ltri_ref,
                       pre_ref, post_ref, wmain_ref, wab_ref, wgates_ref, convw_ref, alog_ref, dtb_ref,
                       retg_ref, gdng_ref, wrb_ref, wgb_ref, wout_ref,
                       y_ref, sret_ref, sgdn_ref, conv_ref,
                       cbuf, *, ret_tile_decay):
    t = pl.program_id(1)
    nb = x_ref.shape[0]

    @pl.when(t == 0)
    def _():
        sret_ref[...] = jnp.zeros_like(sret_ref)
        sgdn_ref[...] = jnp.zeros_like(sgdn_ref)
        cbuf[:, 0:CONV_PAD, :] = jnp.zeros((nb, CONV_PAD, CONV_DIM), F32)

    cos = cos_ref[...]
    sin = sin_ref[...]
    ltri = ltri_ref[...]
    seqs = range(nb)
    rows = [slice(b * TM, (b + 1) * TM) for b in seqs]

    def stack(parts):
        return jnp.concatenate(list(parts), axis=0)

    def dot_all(lhs, w):
        return jnp.dot(lhs, w, preferred_element_type=F32)

    x_all = stack(x_ref[b] for b in seqs)
    h_all = _rms(x_all, pre_ref[...]).astype(BF16)
    pg_all = dot_all(h_all, wmain_ref[:, OFF_CONV:OFF_AB])
    ab_all = dot_all(h_all, wab_ref[...])
    cqs = [_short_conv_tile(pg_all[rows[b], 0:CONV_DIM], cbuf.at[b], convw_ref, conv_ref.at[b]) for b in seqs]
    pm_all = dot_all(h_all, wmain_ref[:, 0:OFF_CONV])
    o_rs = [_retention_tile(pm_all[rows[b]], cos, sin, rd_ref, ku_ref, retg_ref, sret_ref.at[b],
                            ret_tile_decay) for b in seqs]
    o_gs = _gdn_tiles(cqs, [ab_all[rows[b]] for b in seqs], [pg_all[rows[b], CONV_DIM:] for b in seqs],
                      ltri, alog_ref, dtb_ref, gdng_ref, [sgdn_ref.at[b] for b in seqs])
    gates = dot_all(h_all, wgates_ref[...])
    y_all = _merge(x_all, stack(o_rs), stack(o_gs), jax.nn.sigmoid(gates), wrb_ref, wgb_ref, wout_ref,
                   post_ref)
    for b in seqs:
        y_ref[b] = y_all[rows[b]]


def _ret_gammas():
    return 1.0 - 2.0 ** (-5.0 - np.arange(HEADS, dtype=np.float64))


def _rope_tables(pos):
    inv = ROPE_BASE ** (-np.arange(0, HD, 2, dtype=np.float64) / HD)
    ang = np.asarray(pos, np.float64)[:, None] * inv[None, :]
    cos = np.concatenate([np.cos(ang), np.cos(ang)], axis=1)
    sin = np.concatenate([-np.sin(ang), np.sin(ang)], axis=1)
    return jnp.asarray(cos, F32), jnp.asarray(sin, F32)


def _mixer_weight_specs():
    return [
        _resident((1, D_MODEL)), _resident((1, D_MODEL)),
        _resident((D_MODEL, OFF_AB)), _resident((D_MODEL, HD)), _resident((D_MODEL, 2 * D_MODEL)),
        _resident((CONV_W, CONV_DIM)), _resident((1, HD)), _resident((1, HD)),
        _resident((1, QK)), _resident((1, HD)),
        _resident((QK, D_MODEL)), _resident((QK, D_MODEL)), _resident((D_MODEL, D_MODEL)),
    ]


def _mixer_weight_args(w):
    return (w["mix_pre_g"], w["mix_post_g"], w["w_main"], w["w_ab"], w["w_gates"], w["conv_w"],
            w["a_log"], w["dt_bias"], w["ret_norm_g"], w["gdn_norm_g"], w["w_rb"], w["w_gb"], w["w_out"])


def _mixer_prompt(x, w):
    b, t, _ = x.shape
    assert t % TM == 0 and b % NB == 0
    cos, sin = _rope_tables(np.arange(t))
    gam = _ret_gammas()
    i1 = np.arange(1, TM + 1, dtype=np.float64)
    rd = np.broadcast_to((gam[:, None] ** i1[None, :])[:, :, None], (HEADS, TM, HD))
    ku = np.broadcast_to((HD ** -0.5 * gam[:, None] ** (-i1[None, :]))[:, :, None], (HEADS, TM, HD))
    tile_decay = tuple(float(v) for v in gam ** TM)
    r = np.arange(TM)
    ltri = ((r[:, None] >= r[None, :]) & (r[:, None] // CHUNK == r[None, :] // CHUNK))

    body = functools.partial(_mixer_prompt_body, ret_tile_decay=tile_decay)
    state_spec = pl.BlockSpec((NB, HEADS, HD, HD), lambda i, j: (i, 0, 0, 0))
    return pl.pallas_call(
        body,
        grid=(b // NB, t // TM),
        in_specs=[
            pl.BlockSpec((NB, TM, D_MODEL), lambda i, j: (i, j, 0)),
            pl.BlockSpec((TM, HD), lambda i, j: (j, 0)),
            pl.BlockSpec((TM, HD), lambda i, j: (j, 0)),
            _resident((HEADS, TM, HD)), _resident((HEADS, TM, HD)), _resident((TM, TM)),
        ] + _mixer_weight_specs(),
        out_specs=[
            pl.BlockSpec((NB, TM, D_MODEL), lambda i, j: (i, j, 0)),
            state_spec, state_spec,
            pl.BlockSpec((NB, CONV_W - 1, CONV_DIM), lambda i, j: (i, 0, 0)),
        ],
        out_shape=[
            jax.ShapeDtypeStruct((b, t, D_MODEL), F32),
            jax.ShapeDtypeStruct((b, HEADS, HD, HD), F32),
            jax.ShapeDtypeStruct((b, HEADS, HD, HD), F32),
            jax.ShapeDtypeStruct((b, CONV_W - 1, CONV_DIM), F32),
        ],
        scratch_shapes=[pltpu.VMEM((NB, CONV_PAD, CONV_DIM), F32)],
        compiler_params=pltpu.CompilerParams(dimension_semantics=("arbitrary", "arbitrary"),
                                             vmem_limit_bytes=VMEM_LIMIT),
        name="mixer_prompt",
    )(x, cos, sin, jnp.asarray(rd, F32), jnp.asarray(ku, F32), jnp.asarray(ltri, BF16),
      *_mixer_weight_args(w))


def _pick_rows(rows):
    ri = lax.broadcasted_iota(jnp.int32, rows[0].shape, 0)
    out = rows[0]
    for j in range(1, len(rows)):
        out = jnp.where(ri == j, rows[j], out)
    return out


def _mixer_sample_body(x_ref, cos_ref, sin_ref,
                       pre_ref, post_ref, wmain_ref, wab_ref, wgates_ref, convw_ref, alog_ref, dtb_ref,
                       retg_ref, gdng_ref, wrb_ref, wgb_ref, wout_ref,
                       sret_in, sgdn_in, conv_in,
                       y_ref, sret_out, sgdn_out, conv_out,
                       pm_s, gates_s, rq_s, rk_s, gq_s, gk_s, gv_s, eg_s, beta_s, or_s, og_s, *, ret_gamma):
    i = pl.program_id(0)

    @pl.when(i == 0)
    def _():
        h = _rms(x_ref[...], pre_ref[...]).astype(BF16)
        pm = jnp.dot(h, wmain_ref[...], preferred_element_type=F32)
        ab = jnp.dot(h, wab_ref[...], preferred_element_type=F32)
        pm_s[...] = pm
        gates_s[...] = jnp.dot(h, wgates_ref[...], preferred_element_type=F32)
        cos = cos_ref[...]
        sin = sin_ref[...]
        cin = pm[:, OFF_CONV:OFF_CONV + CONV_DIM]
        acc = cin * convw_ref[CONV_W - 1:CONV_W, :]
        for r in range(CONV_W - 1):
            acc = acc + conv_in[r] * convw_ref[r:r + 1, :]
        for r in range(CONV_W - 2):
            conv_out[r] = conv_in[r + 1]
        conv_out[CONV_W - 2] = cin
        cq = _silu(acc)
        g_all, beta_all = _gdn_decay_beta(ab, alog_ref, dtb_ref)
        eg_s[...] = jnp.exp(g_all)
        beta_s[...] = beta_all
        for hh in range(HEADS):
            sl = slice(hh * HD, (hh + 1) * HD)
            rq_s[:, sl] = _rope(pm[:, OFF_RQ + hh * HD:OFF_RQ + (hh + 1) * HD], cos, sin)
            rk_s[:, sl] = _rope(pm[:, OFF_RK + hh * HD:OFF_RK + (hh + 1) * HD], cos, sin) * (HD ** -0.5)
            gq_s[:, sl] = _l2norm(cq[:, hh * HD:(hh + 1) * HD], HD ** -0.5)
            gk_s[:, sl] = _l2norm(cq[:, QK + hh * HD:QK + (hh + 1) * HD], 1.0)
        gv_s[...] = cq[:, 2 * QK:3 * QK]

    rows = pl.ds(pl.multiple_of(i * TB, TB), TB)
    eg_all = eg_s[rows, :]
    beta_all = beta_s[rows, :]
    for hh in range(HEADS):
        sl = slice(hh * HD, (hh + 1) * HD)
        q = rq_s[rows, sl]
        k_t = rk_s[rows, sl].T
        v = pm_s[rows, OFF_RV + hh * HD:OFF_RV + (hh + 1) * HD]
        o_rows = []
        for j in range(TB):
            s = ret_gamma[hh] * sret_in[j, hh] + k_t[:, j:j + 1] * v[j:j + 1, :]
            sret_out[j, hh] = s
            o_rows.append(_dot(q, s))
        or_s[rows, sl] = _pick_rows(o_rows)

        gq = gq_s[rows, sl]
        gk = gk_s[rows, sl]
        gk_t = gk.T
        gv = gv_s[rows, sl]
        o_rows = []
        for j in range(TB):
            s = sgdn_in[j, hh]
            eg = eg_all[j:j + 1, hh:hh + 1]
            beta = beta_all[j:j + 1, HEADS + hh:HEADS + hh + 1]
            ks = _dot(gk, s)[j:j + 1, :]
            u = beta * gv[j:j + 1, :] - (beta * eg) * ks
            s = eg * s + gk_t[:, j:j + 1] * u
            sgdn_out[j, hh] = s
            o_rows.append(_dot(gq, s))
        og_s[rows, sl] = _pick_rows(o_rows)

    @pl.when(i == pl.num_programs(0) - 1)
    def _():
        o_r = []
        o_g = []
        for hh in range(HEADS):
            sl = slice(hh * HD, (hh + 1) * HD)
            o_r.append(_ret_out_norm(or_s[:, sl], retg_ref[:, sl],
                                     _silu(pm_s[:, OFF_RG + hh * HD:OFF_RG + (hh + 1) * HD])))
            o_g.append(_gdn_out_norm(og_s[:, sl], gdng_ref[...],
                                     _silu(pm_s[:, OFF_GZ + hh * HD:OFF_GZ + (hh + 1) * HD])))
        y_ref[...] = _merge(x_ref[...], jnp.concatenate(o_r, axis=1), jnp.concatenate(o_g, axis=1),
                            jax.nn.sigmoid(gates_s[...]), wrb_ref, wgb_ref, wout_ref, post_ref)


def _mixer_sample(x, s_ret, s_gdn, s_conv, w, pos):
    n = x.shape[0]
    assert n % TB == 0
    cos, sin = _rope_tables([pos])
    body = functools.partial(_mixer_sample_body, ret_gamma=tuple(float(v) for v in _ret_gammas()))
    state_spec = pl.BlockSpec((TB, HEADS, HD, HD), lambda i: (i, 0, 0, 0))
    conv_shape = (CONV_W - 1, n, CONV_DIM)
    return pl.pallas_call(
        body,
        grid=(n // TB,),
        in_specs=[_resident((n, D_MODEL)), _resident((1, HD)), _resident((1, HD))]
        + _mixer_weight_specs()
        + [state_spec, state_spec, _resident(conv_shape)],
        out_specs=[pl.BlockSpec((n, D_MODEL), lambda i: (0, 0)), state_spec, state_spec,
                   pl.BlockSpec(conv_shape, lambda i: (0, 0, 0))],
        out_shape=[
            jax.ShapeDtypeStruct((n, D_MODEL), F32),
            jax.ShapeDtypeStruct((n, HEADS, HD, HD), F32),
            jax.ShapeDtypeStruct((n, HEADS, HD, HD), F32),
            jax.ShapeDtypeStruct(conv_shape, F32),
        ],
        scratch_shapes=[pltpu.VMEM((n, OFF_AB), F32), pltpu.VMEM((n, 2 * D_MODEL), F32)]
        + [pltpu.VMEM((n, QK), F32) for _ in range(5)]
        + [pltpu.VMEM((n, HD), F32), pltpu.VMEM((n, HD), F32)]
        + [pltpu.VMEM((n, QK), F32), pltpu.VMEM((n, QK), F32)],
        compiler_params=pltpu.CompilerParams(dimension_semantics=("arbitrary",),
                                             vmem_limit_bytes=VMEM_LIMIT),
        name="mixer_sample",
    )(x, cos, sin, *_mixer_weight_args(w), s_ret, s_gdn, s_conv)


def _pad_lanes(v, n):
    return jnp.pad(v, ((0, 0), (0, n - v.shape[1])))


def kernel(x_prompt, x_sample, state_ret, state_gdn, state_conv, ffn1_pre_g, ffn1_post_g, ffn1_w_gate,
           ffn1_w_up, ffn1_w_down, mix_pre_g, mix_post_g, w_in, ret_norm_g, gdn_conv_w, gdn_a_log,
           gdn_dt_bias, gdn_norm_g, w_ret_branch, w_gdn_branch, w_out, ffn2_pre_g, ffn2_post_g,
           ffn2_w_gate, ffn2_w_up, ffn2_w_down):
    depth = w_in.shape[0]
    b, t, _ = x_prompt.shape
    n_s, t_s, _ = x_sample.shape
    assert t_s == 1
    yp = x_prompt.reshape(b * t, D_MODEL)
    ys = x_sample.reshape(n_s, D_MODEL)
    outs = [[] for _ in range(6)]
    for l in range(depth):
        row = lambda a: a[l][None, :]
        ys, wg1, wu1, wd1 = _ffn_stream(ys, row(ffn1_pre_g), row(ffn1_post_g),
                                        ffn1_w_gate[l], ffn1_w_up[l], ffn1_w_down[l])
        f1 = (row(ffn1_pre_g), row(ffn1_post_g), wg1, wu1, wd1)

        w_in_t = w_in[l].T
        casts = (
            (w_in_t, (0, OFF_AB)), (w_in_t, (OFF_GATES, 2 * D_MODEL)),
            (w_ret_branch[l], None), (w_gdn_branch[l], None), (w_out[l], None),
            (ffn2_w_gate[l], None), (ffn2_w_up[l], None), (ffn2_w_down[l], None),
        )
        yp, (w_main, w_gates, w_rb, w_gb, w_o, g2, u2, d2) = _ffn(yp, *f1, tm=FFN_TM, casts=casts)
        w = {
            "mix_pre_g": row(mix_pre_g), "mix_post_g": row(mix_post_g),
            "w_main": w_main,
            "w_ab": _pad_lanes(w_in[l, :, OFF_AB:OFF_GATES], HD).astype(BF16),
            "w_gates": w_gates,
            "conv_w": gdn_conv_w[l],
            "a_log": _pad_lanes(row(gdn_a_log), HD), "dt_bias": _pad_lanes(row(gdn_dt_bias), HD),
            "ret_norm_g": row(ret_norm_g), "gdn_norm_g": row(gdn_norm_g),
            "w_rb": w_rb, "w_gb": w_gb, "w_out": w_o,
        }
        f2 = (row(ffn2_pre_g), row(ffn2_post_g), g2, u2, d2)

        yp, r1, g1, c1 = _mixer_prompt(yp.reshape(b, t, D_MODEL), w)
        ys, r2, g2s, c2 = _mixer_sample(ys, state_ret[l], state_gdn[l],
                                        jnp.swapaxes(state_conv[l], 0, 1), w, PAST_LEN)
        c2 = jnp.swapaxes(c2, 0, 1)
        yp, ys, _ = _ffn(yp.reshape(b * t, D_MODEL), *f2, tm=FFN_TM, extra=ys)
        for lst, val in zip(outs, (r1, g1, c1, r2, g2s, c2)):
            lst.append(val)
    stacked = [v[0][None] if depth == 1 else jnp.stack(v) for v in outs]
    return (yp.reshape(b, t, D_MODEL), ys.reshape(n_s, t_s, D_MODEL), *stacked)
```

```python
import functools

import numpy as np
import jax
import jax.numpy as jnp
from jax import lax
from jax.experimental import pallas as pl
from jax.experimental.pallas import tpu as pltpu

F32 = jnp.float32
BF16 = jnp.bfloat16

D_MODEL = 1024
D_FF = 2816
HEADS = 4
HD = 128
QK = HEADS * HD
CONV_W = 4
CONV_DIM = 3 * QK
CHUNK = 64
ROPE_BASE = 10000.0
EPS = 1e-6
PAST_LEN = 16384

OFF_RQ, OFF_RK, OFF_RV, OFF_RG = 0, QK, 2 * QK, 3 * QK
OFF_CONV = 4 * QK
OFF_GZ = OFF_CONV + CONV_DIM
OFF_AB = OFF_GZ + QK
OFF_GATES = OFF_AB + 2 * HEADS
D_IN = OFF_GATES + 2 * D_MODEL

TM = 256
NB = 2
TB = 8
STATE_SLOTS = 3
FFN_TM = 1024
FFN_SUB = 256
FF_CHUNK = 256
CONV_PAD = 8
BF16_ROWS = 16
F32_ROWS = 8

VMEM_LIMIT = 56 * 1024 * 1024


def _silu(x):
    return x * jax.nn.sigmoid(x)


def _rms(x, g):
    return x * lax.rsqrt(jnp.mean(x * x, axis=-1, keepdims=True) + EPS) * g


def _dot(a, b):
    return jnp.dot(a.astype(BF16), b.astype(BF16), preferred_element_type=F32)


def _dot_nt(a, b):
    return lax.dot_general(a.astype(BF16), b.astype(BF16), (((1,), (1,)), ((), ())),
                           preferred_element_type=F32)


def _dot_tn(a, b):
    return lax.dot_general(a.astype(BF16), b.astype(BF16), (((0,), (0,)), ((), ())),
                           preferred_element_type=F32)


def _split3(a):
    hi = a.astype(BF16)
    r = a - hi.astype(F32)
    mid = r.astype(BF16)
    lo = (r - mid.astype(F32)).astype(BF16)
    return hi, mid, lo


def _resident(shape):
    nd = len(shape)
    return pl.BlockSpec(shape, lambda *_: (0,) * nd, pipeline_mode=pl.Buffered(1))


def _ffn_rows(x_ref, o_ref, pre_ref, post_ref, wg_ref, wu_ref, wd_ref):
    tm = x_ref.shape[0]
    sub = min(tm, FFN_SUB)
    parts = [slice(r, r + sub) for r in range(0, tm, sub)]
    xs = [x_ref[p, :] for p in parts]
    hs = [_rms(x, pre_ref[...]).astype(BF16) for x in xs]
    acts = []
    for h in hs:
        g = jnp.dot(h, wg_ref[...], preferred_element_type=F32)
        u = jnp.dot(h, wu_ref[...], preferred_element_type=F32)
        acts.append((_silu(g) * u).astype(BF16))
    for p, x, a in zip(parts, xs, acts):
        y = jnp.dot(a, wd_ref[...], preferred_element_type=F32)
        o_ref[p, :] = x + 0.5 * _rms(y, post_ref[...])


def _ffn_body(x_ref, pre_ref, post_ref, wg_ref, wu_ref, wd_ref, *rest, cast_transposed, has_extra):
    rest = list(rest)
    xe_ref = rest.pop(0) if has_extra else None
    n_cast = len(cast_transposed)
    cast_in, o_ref = rest[:n_cast], rest[n_cast]
    rest = rest[n_cast + 1:]
    oe_ref = rest.pop(0) if has_extra else None
    cast_out = rest
    weights = (pre_ref, post_ref, wg_ref, wu_ref, wd_ref)
    _ffn_rows(x_ref, o_ref, *weights)
    for src, dst, transposed in zip(cast_in, cast_out, cast_transposed):
        blk = src[...]
        dst[...] = (blk.T if transposed else blk).astype(BF16)
    if has_extra:
        @pl.when(pl.program_id(0) == pl.num_programs(0) - 1)
        def _():
            _ffn_rows(xe_ref, oe_ref, *weights)


def _cast_row_block(rows, steps):
    rb = BF16_ROWS
    while rows % rb or rows // rb > steps:
        rb += BF16_ROWS
    return rb


def _ffn(x, pre_g, post_g, wg, wu, wd, tm, casts=(), extra=None):
    n = x.shape[0]
    assert n % tm == 0
    steps = n // tm
    in_specs = [
        pl.BlockSpec((tm, D_MODEL), lambda i: (i, 0)),
        _resident((1, D_MODEL)), _resident((1, D_MODEL)),
        _resident((D_MODEL, D_FF)), _resident((D_MODEL, D_FF)), _resident((D_FF, D_MODEL)),
    ]
    out_specs = [pl.BlockSpec((tm, D_MODEL), lambda i: (i, 0))]
    out_shape = [jax.ShapeDtypeStruct((n, D_MODEL), F32)]
    operands = [x, pre_g, post_g, wg, wu, wd]
    if extra is not None:
        in_specs.append(_resident(extra.shape))
        operands.append(extra)
    n_fixed_in = len(in_specs)
    for arr, region in casts:
        rows, width = arr.shape
        if region is None:
            rb = _cast_row_block(rows, steps)
            last = rows // rb - 1
            spec = pl.BlockSpec((rb, width), lambda i, last=last: (jnp.minimum(i, last), 0))
            in_specs.append(spec)
            out_specs.append(spec)
            out_shape.append(jax.ShapeDtypeStruct((rows, width), BF16))
        else:
            row0, n_rows = region
            cb = HD
            while n_rows % cb or n_rows // cb > steps:
                cb += HD
            last = n_rows // cb - 1
            in_specs.append(pl.BlockSpec(
                (pl.Element(cb), pl.Element(width)),
                lambda i, row0=row0, last=last, cb=cb: (
                    pl.multiple_of(row0 + cb * jnp.minimum(i, last), F32_ROWS), 0)))
            out_specs.append(pl.BlockSpec((width, cb), lambda i, last=last: (0, jnp.minimum(i, last))))
            out_shape.append(jax.ShapeDtypeStruct((width, n_rows), BF16))
    assert len(in_specs) == n_fixed_in + len(casts)
    if extra is not None:
        out_specs.insert(1, pl.BlockSpec(extra.shape, lambda i: (0, 0)))
        out_shape.insert(1, jax.ShapeDtypeStruct(extra.shape, F32))
    outs = pl.pallas_call(
        functools.partial(_ffn_body, cast_transposed=tuple(c[1] is not None for c in casts),
                          has_extra=extra is not None),
        grid=(steps,),
        in_specs=in_specs,
        out_specs=out_specs,
        out_shape=out_shape,
        compiler_params=pltpu.CompilerParams(dimension_semantics=("arbitrary",),
                                             vmem_limit_bytes=VMEM_LIMIT),
        name="ffn",
    )(*operands, *[c[0] for c in casts])
    if extra is not None:
        return outs[0], outs[1], list(outs[2:])
    return outs[0], list(outs[1:])


def _ffn_stream_body(x_ref, pre_ref, post_ref, wg_ref, wu_ref, wd_ref,
                     o_ref, wg16_ref, wu16_ref, wd16_ref, h_s, acc_s):
    c = pl.program_id(0)

    @pl.when(c == 0)
    def _():
        h_s[...] = _rms(x_ref[...], pre_ref[...]).astype(BF16)
        acc_s[...] = jnp.zeros_like(acc_s)

    wg = wg_ref[...].astype(BF16)
    wu = wu_ref[...].astype(BF16)
    wd = wd_ref[...].astype(BF16)
    wg16_ref[...] = wg
    wu16_ref[...] = wu
    wd16_ref[...] = wd
    h = h_s[...]
    g = jnp.dot(h, wg, preferred_element_type=F32)
    u = jnp.dot(h, wu, preferred_element_type=F32)
    a = (_silu(g) * u).astype(BF16)
    acc_s[...] += jnp.dot(a, wd, preferred_element_type=F32)

    @pl.when(c == pl.num_programs(0) - 1)
    def _():
        o_ref[...] = x_ref[...] + 0.5 * _rms(acc_s[...], post_ref[...])


def _ffn_stream(x, pre_g, post_g, wg, wu, wd):
    n = x.shape[0]
    assert D_FF % FF_CHUNK == 0
    col = pl.BlockSpec((D_MODEL, FF_CHUNK), lambda c: (0, c))
    row = pl.BlockSpec((FF_CHUNK, D_MODEL), lambda c: (c, 0))
    return pl.pallas_call(
        _ffn_stream_body,
        grid=(D_FF // FF_CHUNK,),
        in_specs=[_resident((n, D_MODEL)), _resident((1, D_MODEL)), _resident((1, D_MODEL)), col, col, row],
        out_specs=[pl.BlockSpec((n, D_MODEL), lambda c: (0, 0)), col, col, row],
        out_shape=[
            jax.ShapeDtypeStruct((n, D_MODEL), F32),
            jax.ShapeDtypeStruct((D_MODEL, D_FF), BF16),
            jax.ShapeDtypeStruct((D_MODEL, D_FF), BF16),
            jax.ShapeDtypeStruct((D_FF, D_MODEL), BF16),
        ],
        scratch_shapes=[pltpu.VMEM((n, D_MODEL), BF16), pltpu.VMEM((n, D_MODEL), F32)],
        compiler_params=pltpu.CompilerParams(dimension_semantics=("arbitrary",),
                                             vmem_limit_bytes=VMEM_LIMIT),
        name="ffn_stream",
    )(x, pre_g, post_g, wg, wu, wd)


def _rope(x, cos, sin_signed):
    return x * cos + pltpu.roll(x, HD // 2, 1) * sin_signed


def _ret_out_norm(o, g_row, gate_act):
    mu = jnp.mean(o, axis=-1, keepdims=True)
    d = o - mu
    var = jnp.mean(d * d, axis=-1, keepdims=True)
    return gate_act * (d * lax.rsqrt(var + EPS) * g_row)


def _gdn_out_norm(o, g_row, gate_act):
    return o * lax.rsqrt(jnp.mean(o * o, axis=-1, keepdims=True) + EPS) * g_row * gate_act


def _l2norm(x, scale):
    return x * (lax.rsqrt(jnp.sum(x * x, axis=-1, keepdims=True) + EPS) * scale)


def _softplus(x):
    return jnp.maximum(x, 0.0) + jnp.log(1.0 + jnp.exp(-jnp.abs(x)))


def _merge(x, o_r, o_g, sig_gates, wrb_ref, wgb_ref, wout_ref, post_ref):
    y = (sig_gates[:, :D_MODEL] * _dot(o_r, wrb_ref[...])
         + sig_gates[:, D_MODEL:] * _dot(o_g, wgb_ref[...]))
    m = _dot(y, wout_ref[...])
    return x + _rms(m, post_ref[...])


def _pad_rows(a):
    return jnp.concatenate([a, jnp.zeros_like(a)], axis=0)


def _dot64(a, b):
    return jnp.dot(a.astype(BF16), _pad_rows(b.astype(BF16)), preferred_element_type=F32)


def _unit_lower_inverse_many(a_list, masks):
    eye, m16, off32, off64 = masks
    ad = [a * m16 for a in a_list]
    x = [eye - v for v in ad]
    p = [_dot64(v, v) for v in ad]
    for level in range(3):
        x = [xi + _dot64(xi, pi) for xi, pi in zip(x, p)]
        if level < 2:
            p = [_dot64(pi, pi) for pi in p]
    for m in (off32, off64):
        t = [_dot64(xi, a * m) for xi, a in zip(x, a_list)]
        x = [xi - _dot64(ti, xi) for xi, ti in zip(x, t)]
    return x


def _chunk_masks():
    r = lax.broadcasted_iota(jnp.int32, (CHUNK, 2 * CHUNK), 0)
    c = lax.broadcasted_iota(jnp.int32, (CHUNK, 2 * CHUNK), 1)
    one, zero = jnp.float32(1.0), jnp.float32(0.0)
    live = c < CHUNK
    eye = jnp.where(r == c, one, zero)
    m16 = jnp.where(live, jnp.where((r >> 4) == (c >> 4), one, zero), zero)
    m32 = jnp.where(live, jnp.where((r >> 5) == (c >> 5), one, zero), zero)
    m64 = jnp.where(live, one, zero)
    return r >= c, r > c, (eye, m16, m32 - m16, m64 - m32)


def _retention_tile(pm, cos, sin, rd_ref, ku_ref, retg_ref, s_ref, tile_decay):
    ri = lax.broadcasted_iota(jnp.int32, (TM, TM), 0)
    ci = lax.broadcasted_iota(jnp.int32, (TM, TM), 1)
    causal = ri >= ci
    heads = []
    for hh in range(HEADS):
        sl = slice(hh * HD, (hh + 1) * HD)
        rq = pm[:, OFF_RQ + hh * HD:OFF_RQ + (hh + 1) * HD]
        rk = pm[:, OFF_RK + hh * HD:OFF_RK + (hh + 1) * HD]
        v = pm[:, OFF_RV + hh * HD:OFF_RV + (hh + 1) * HD]
        rg = pm[:, OFF_RG + hh * HD:OFF_RG + (hh + 1) * HD]
        qs = _rope(rq, cos, sin) * rd_ref[hh]
        ku = _rope(rk, cos, sin) * ku_ref[hh]
        s = s_ref[hh]
        sc = jnp.where(causal, _dot_nt(qs, ku), 0.0)
        o = _dot(sc, v) + _dot(qs, s)
        s_ref[hh] = tile_decay[hh] * (s + _dot_tn(ku, v))
        heads.append(_ret_out_norm(o, retg_ref[:, sl], _silu(rg)))
    return jnp.concatenate(heads, axis=1)


def _short_conv_tile(u, buf, convw_ref, tail_ref):
    prev = buf[...]
    row = lax.broadcasted_iota(jnp.int32, (CONV_PAD, u.shape[1]), 0)
    acc = u * convw_ref[CONV_W - 1:CONV_W, :]
    for k in range(1, CONV_W):
        shifted = pltpu.roll(u, k, 0)
        head = jnp.where(row < k, pltpu.roll(prev, k, 0), shifted[0:CONV_PAD, :])
        shifted = jnp.concatenate([head, shifted[CONV_PAD:, :]], axis=0)
        acc = acc + shifted * convw_ref[CONV_W - 1 - k:CONV_W - k, :]
    buf[...] = u[TM - CONV_PAD:TM, :]
    tail_ref[...] = u[TM - (CONV_W - 1):TM, :]
    return _silu(acc)


def _gdn_decay_beta(ab, alog_ref, dtb_ref):
    g_all = -jnp.exp(alog_ref[...]) * _softplus(ab + dtb_ref[...])
    beta_all = jax.nn.sigmoid(ab)
    return g_all, beta_all


def _gdn_tiles(cqs, abs_, gzs, ltri, alog_ref, dtb_ref, gdng_ref, s_refs):
    seqs = range(len(cqs))
    dd = functools.partial(jnp.dot, preferred_element_type=F32)
    gcum, gcum_t, beta_all = [], [], []
    for b in seqs:
        g_all, beta = _gdn_decay_beta(abs_[b], alog_ref, dtb_ref)
        beta_all.append(beta)
        g_hi, g_mid, g_lo = _split3(g_all)
        gcum.append(dd(ltri, g_hi) + (dd(ltri, g_mid) + dd(ltri, g_lo)))
        gcum_t.append(gcum[b].T)

    causal64, strict64, masks = _chunk_masks()

    n_ch = TM // CHUNK
    rows = [slice(c * CHUNK, (c + 1) * CHUNK) for c in range(n_ch)]
    bh = [(b, hh) for b in seqs for hh in range(HEADS)]
    probs = [(b, hh, c) for b, hh in bh for c in range(n_ch)]
    gq = {(b, hh): _l2norm(cqs[b][:, hh * HD:(hh + 1) * HD], HD ** -0.5) for b, hh in bh}
    gk = {(b, hh): _l2norm(cqs[b][:, QK + hh * HD:QK + (hh + 1) * HD], 1.0) for b, hh in bh}
    gv = {(b, hh): cqs[b][:, 2 * QK + hh * HD:2 * QK + (hh + 1) * HD] for b, hh in bh}
    qc = {(b, hh, c): gq[b, hh][rows[c]] for b, hh, c in probs}
    kc = {(b, hh, c): gk[b, hh][rows[c]] for b, hh, c in probs}
    vc = {(b, hh, c): gv[b, hh][rows[c]] for b, hh, c in probs}
    gc = {(b, hh, c): gcum[b][rows[c], hh:hh + 1] for b, hh, c in probs}
    bcol = {(b, hh, c): beta_all[b][rows[c], HEADS + hh:HEADS + hh + 1] for b, hh, c in probs}
    def g_row(b, hh, c):
        gr = gcum_t[b][hh:hh + 1, rows[c]]
        return jnp.concatenate([gr, gr], axis=1)

    dm = {(b, hh, c): jnp.where(
        causal64, jnp.exp(jnp.minimum(gc[b, hh, c] - g_row(b, hh, c), 0.0)), 0.0)
        for b, hh, c in probs}
    kpad = {p: _pad_rows(kc[p].astype(BF16)) for p in probs}
    kk = {p: _dot_nt(kc[p], kpad[p]) for p in probs}
    qk = {p: _dot_nt(qc[p], kpad[p]) * dm[p] for p in probs}
    a_mats = [jnp.where(strict64, bcol[p] * dm[p] * kk[p], 0.0) for p in probs]
    tinv = dict(zip(probs, _unit_lower_inverse_many(a_mats, masks)))
    eg = {p: jnp.exp(gc[p]) for p in probs}
    uw = {p: _dot64(tinv[p], jnp.concatenate([bcol[p] * vc[p], (bcol[p] * eg[p]) * kc[p]], axis=1))
          for p in probs}
    gl = {p: gc[p][CHUNK - 1:CHUNK, :] for p in probs}
    uwp = {p: _pad_rows(uw[p].astype(BF16)) for p in probs}
    kdw = {p: _dot_tn(_pad_rows((kc[p] * jnp.exp(gl[p] - gc[p])).astype(BF16)), uwp[p])
           for p in probs}
    qkw = {p: jnp.dot(qk[p].astype(BF16), uwp[p], preferred_element_type=F32)
           for p in probs}
    lhs = {p: jnp.concatenate([kdw[p][:, HD:], eg[p] * qc[p] - qkw[p][:, HD:]], axis=0).astype(BF16)
           for p in probs}

    s = {(b, hh): s_refs[b][hh] for b, hh in bh}
    outs = {k: [] for k in bh}
    for c in range(n_ch):
        for b, hh in bh:
            p = (b, hh, c)
            ps = _dot(lhs[p], s[b, hh])
            outs[b, hh].append(ps[HD:, :] + qkw[p][:, :HD])
            s[b, hh] = jnp.exp(gl[p]) * s[b, hh] + (kdw[p][:, :HD] - ps[:HD, :])
    o_gs = []
    for b in seqs:
        heads = []
        for hh in range(HEADS):
            s_refs[b][hh] = s[b, hh]
            o = jnp.concatenate(outs[b, hh], axis=0)
            heads.append(_gdn_out_norm(o, gdng_ref[...], _silu(gzs[b][:, hh * HD:(hh + 1) * HD])))
        o_gs.append(jnp.concatenate(heads, axis=1))
    return o_gs


def _mixer_prompt_body(x_ref, cos_ref, sin_ref, rd_ref, ku_ref, ltri_ref,
                       pre_ref, post_ref, wmain_ref, wab_ref, wgates_ref, convw_ref, alog_ref, dtb_ref,
                       retg_ref, gdng_ref, wrb_ref, wgb_ref, wout_ref,
                       y_ref, sret_ref, sgdn_ref, conv_ref,
                       cbuf, *, ret_tile_decay):
    t = pl.program_id(1)
    nb = x_ref.shape[0]

    @pl.when(t == 0)
    def _():
        sret_ref[...] = jnp.zeros_like(sret_ref)
        sgdn_ref[...] = jnp.zeros_like(sgdn_ref)
        cbuf[:, 0:CONV_PAD, :] = jnp.zeros((nb, CONV_PAD, CONV_DIM), F32)

    cos = cos_ref[...]
    sin = sin_ref[...]
    ltri = ltri_ref[...]
    seqs = range(nb)
    rows = [slice(b * TM, (b + 1) * TM) for b in seqs]

    def stack(parts):
        return jnp.concatenate(list(parts), axis=0)

    def dot_all(lhs, w):
        return jnp.dot(lhs, w, preferred_element_type=F32)

    x_all = stack(x_ref[b] for b in seqs)
    h_all = _rms(x_all, pre_ref[...]).astype(BF16)
    pg_all = dot_all(h_all, wmain_ref[:, OFF_CONV:OFF_AB])
    ab_all = dot_all(h_all, wab_ref[...])
    cqs = [_short_conv_tile(pg_all[rows[b], 0:CONV_DIM], cbuf.at[b], convw_ref, conv_ref.at[b]) for b in seqs]
    pm_all = dot_all(h_all, wmain_ref[:, 0:OFF_CONV])
    o_rs = [_retention_tile(pm_all[rows[b]], cos, sin, rd_ref, ku_ref, retg_ref, sret_ref.at[b],
                            ret_tile_decay) for b in seqs]
    o_gs = _gdn_tiles(cqs, [ab_all[rows[b]] for b in seqs], [pg_all[rows[b], CONV_DIM:] for b in seqs],
                      ltri, alog_ref, dtb_ref, gdng_ref, [sgdn_ref.at[b] for b in seqs])
    gates = dot_all(h_all, wgates_ref[...])
    y_all = _merge(x_all, stack(o_rs), stack(o_gs), jax.nn.sigmoid(gates), wrb_ref, wgb_ref, wout_ref,
                   post_ref)
    for b in seqs:
        y_ref[b] = y_all[rows[b]]


def _ret_gammas():
    return 1.0 - 2.0 ** (-5.0 - np.arange(HEADS, dtype=np.float64))


def _rope_tables(pos):
    inv = ROPE_BASE ** (-np.arange(0, HD, 2, dtype=np.float64) / HD)
    ang = np.asarray(pos, np.float64)[:, None] * inv[None, :]
    cos = np.concatenate([np.cos(ang), np.cos(ang)], axis=1)
    sin = np.concatenate([-np.sin(ang), np.sin(ang)], axis=1)
    return jnp.asarray(cos, F32), jnp.asarray(sin, F32)


def _mixer_weight_specs():
    return [
        _resident((1, D_MODEL)), _resident((1, D_MODEL)),
        _resident((D_MODEL, OFF_AB)), _resident((D_MODEL, HD)), _resident((D_MODEL, 2 * D_MODEL)),
        _resident((CONV_W, CONV_DIM)), _resident((1, HD)), _resident((1, HD)),
        _resident((1, QK)), _resident((1, HD)),
        _resident((QK, D_MODEL)), _resident((QK, D_MODEL)), _resident((D_MODEL, D_MODEL)),
    ]


def _mixer_weight_args(w):
    return (w["mix_pre_g"], w["mix_post_g"], w["w_main"], w["w_ab"], w["w_gates"], w["conv_w"],
            w["a_log"], w["dt_bias"], w["ret_norm_g"], w["gdn_norm_g"], w["w_rb"], w["w_gb"], w["w_out"])


def _mixer_prompt(x, w):
    b, t, _ = x.shape
    assert t % TM == 0 and b % NB == 0
    cos, sin = _rope_tables(np.arange(t))
    gam = _ret_gammas()
    i1 = np.arange(1, TM + 1, dtype=np.float64)
    rd = np.broadcast_to((gam[:, None] ** i1[None, :])[:, :, None], (HEADS, TM, HD))
    ku = np.broadcast_to((HD ** -0.5 * gam[:, None] ** (-i1[None, :]))[:, :, None], (HEADS, TM, HD))
    tile_decay = tuple(float(v) for v in gam ** TM)
    r = np.arange(TM)
    ltri = ((r[:, None] >= r[None, :]) & (r[:, None] // CHUNK == r[None, :] // CHUNK))

    body = functools.partial(_mixer_prompt_body, ret_tile_decay=tile_decay)
    state_spec = pl.BlockSpec((NB, HEADS, HD, HD), lambda i, j: (i, 0, 0, 0))
    return pl.pallas_call(
        body,
        grid=(b // NB, t // TM),
        in_specs=[
            pl.BlockSpec((NB, TM, D_MODEL), lambda i, j: (i, j, 0)),
            pl.BlockSpec((TM, HD), lambda i, j: (j, 0)),
            pl.BlockSpec((TM, HD), lambda i, j: (j, 0)),
            _resident((HEADS, TM, HD)), _resident((HEADS, TM, HD)), _resident((TM, TM)),
        ] + _mixer_weight_specs(),
        out_specs=[
            pl.BlockSpec((NB, TM, D_MODEL), lambda i, j: (i, j, 0)),
            state_spec, state_spec,
            pl.BlockSpec((NB, CONV_W - 1, CONV_DIM), lambda i, j: (i, 0, 0)),
        ],
        out_shape=[
            jax.ShapeDtypeStruct((b, t, D_MODEL), F32),
            jax.ShapeDtypeStruct((b, HEADS, HD, HD), F32),
            jax.ShapeDtypeStruct((b, HEADS, HD, HD), F32),
            jax.ShapeDtypeStruct((b, CONV_W - 1, CONV_DIM), F32),
        ],
        scratch_shapes=[pltpu.VMEM((NB, CONV_PAD, CONV_DIM), F32)],
        compiler_params=pltpu.CompilerParams(dimension_semantics=("arbitrary", "arbitrary"),
                                             vmem_limit_bytes=VMEM_LIMIT),
        name="mixer_prompt",
    )(x, cos, sin, jnp.asarray(rd, F32), jnp.asarray(ku, F32), jnp.asarray(ltri, BF16),
      *_mixer_weight_args(w))


def _pick_rows(rows):
    ri = lax.broadcasted_iota(jnp.int32, rows[0].shape, 0)
    out = rows[0]
    for j in range(1, len(rows)):
        out = jnp.where(ri == j, rows[j], out)
    return out


def _mixer_sample_body(x_ref, cos_ref, sin_ref,
                       pre_ref, post_ref, wmain_ref, wab_ref, wgates_ref, convw_ref, alog_ref, dtb_ref,
                       retg_ref, gdng_ref, wrb_ref, wgb_ref, wout_ref,
                       sret_hbm, sgdn_hbm, conv_in,
                       y_ref, sret_out, sgdn_out, conv_out,
                       pm_s, gates_s, rq_s, rk_s, gq_s, gk_s, gv_s, eg_s, beta_s, or_s, og_s,
                       sret_buf, sgdn_buf, sems, *, ret_gamma):
    i = pl.program_id(0)
    n_steps = pl.num_programs(0)

    def state_copy(which, step):
        hbm, buf = ((sret_hbm, sret_buf), (sgdn_hbm, sgdn_buf))[which]
        slot = step % STATE_SLOTS
        return pltpu.make_async_copy(hbm.at[pl.ds(step * TB, TB)], buf.at[slot], sems.at[which, slot])

    @pl.when(i == 0)
    def _():
        for step in range(STATE_SLOTS - 1):
            state_copy(0, step).start()
            state_copy(1, step).start()

    @pl.when(i + (STATE_SLOTS - 1) < n_steps)
    def _():
        state_copy(0, i + (STATE_SLOTS - 1)).start()
        state_copy(1, i + (STATE_SLOTS - 1)).start()

    @pl.when(i == 0)
    def _():
        h = _rms(x_ref[...], pre_ref[...]).astype(BF16)
        pm = jnp.dot(h, wmain_ref[...], preferred_element_type=F32)
        ab = jnp.dot(h, wab_ref[...], preferred_element_type=F32)
        pm_s[...] = pm
        gates_s[...] = jnp.dot(h, wgates_ref[...], preferred_element_type=F32)
        cos = cos_ref[...]
        sin = sin_ref[...]
        cin = pm[:, OFF_CONV:OFF_CONV + CONV_DIM]
        acc = cin * convw_ref[CONV_W - 1:CONV_W, :]
        for r in range(CONV_W - 1):
            acc = acc + conv_in[r] * convw_ref[r:r + 1, :]
        for r in range(CONV_W - 2):
            conv_out[r] = conv_in[r + 1]
        conv_out[CONV_W - 2] = cin
        cq = _silu(acc)
        g_all, beta_all = _gdn_decay_beta(ab, alog_ref, dtb_ref)
        eg_s[...] = jnp.exp(g_all)
        beta_s[...] = beta_all
        for hh in range(HEADS):
            sl = slice(hh * HD, (hh + 1) * HD)
            rq_s[:, sl] = _rope(pm[:, OFF_RQ + hh * HD:OFF_RQ + (hh + 1) * HD], cos, sin)
            rk_s[:, sl] = _rope(pm[:, OFF_RK + hh * HD:OFF_RK + (hh + 1) * HD], cos, sin) * (HD ** -0.5)
            gq_s[:, sl] = _l2norm(cq[:, hh * HD:(hh + 1) * HD], HD ** -0.5)
            gk_s[:, sl] = _l2norm(cq[:, QK + hh * HD:QK + (hh + 1) * HD], 1.0)
        gv_s[...] = cq[:, 2 * QK:3 * QK]

    state_copy(0, i).wait()
    state_copy(1, i).wait()
    sret_in = sret_buf.at[i % STATE_SLOTS]
    sgdn_in = sgdn_buf.at[i % STATE_SLOTS]

    rows = pl.ds(pl.multiple_of(i * TB, TB), TB)
    eg_all = eg_s[rows, :]
    beta_all = beta_s[rows, :]
    for hh in range(HEADS):
        sl = slice(hh * HD, (hh + 1) * HD)
        q = rq_s[rows, sl]
        k_t = rk_s[rows, sl].T
        v = pm_s[rows, OFF_RV + hh * HD:OFF_RV + (hh + 1) * HD]
        o_rows = []
        for j in range(TB):
            s = ret_gamma[hh] * sret_in[j, hh] + k_t[:, j:j + 1] * v[j:j + 1, :]
            sret_out[j, hh] = s
            o_rows.append(_dot(q, s))
        or_s[rows, sl] = _pick_rows(o_rows)

        gq = gq_s[rows, sl]
        gk = gk_s[rows, sl]
        gk_t = gk.T
        gv = gv_s[rows, sl]
        o_rows = []
        for j in range(TB):
            s = sgdn_in[j, hh]
            eg = eg_all[j:j + 1, hh:hh + 1]
            beta = beta_all[j:j + 1, HEADS + hh:HEADS + hh + 1]
            ks = _dot(gk, s)[j:j + 1, :]
            u = beta * gv[j:j + 1, :] - (beta * eg) * ks
            s = eg * s + gk_t[:, j:j + 1] * u
            sgdn_out[j, hh] = s
            o_rows.append(_dot(gq, s))
        og_s[rows, sl] = _pick_rows(o_rows)

    @pl.when(i == pl.num_programs(0) - 1)
    def _():
        o_r = []
        o_g = []
        for hh in range(HEADS):
            sl = slice(hh * HD, (hh + 1) * HD)
            o_r.append(_ret_out_norm(or_s[:, sl], retg_ref[:, sl],
                                     _silu(pm_s[:, OFF_RG + hh * HD:OFF_RG + (hh + 1) * HD])))
            o_g.append(_gdn_out_norm(og_s[:, sl], gdng_ref[...],
                                     _silu(pm_s[:, OFF_GZ + hh * HD:OFF_GZ + (hh + 1) * HD])))
        y_ref[...] = _merge(x_ref[...], jnp.concatenate(o_r, axis=1), jnp.concatenate(o_g, axis=1),
                            jax.nn.sigmoid(gates_s[...]), wrb_ref, wgb_ref, wout_ref, post_ref)


def _mixer_sample(x, s_ret, s_gdn, s_conv, w, pos):
    n = x.shape[0]
    assert n % TB == 0 and n // TB >= STATE_SLOTS
    cos, sin = _rope_tables([pos])
    body = functools.partial(_mixer_sample_body, ret_gamma=tuple(float(v) for v in _ret_gammas()))
    state_spec = pl.BlockSpec((TB, HEADS, HD, HD), lambda i: (i, 0, 0, 0))
    state_hbm = pl.BlockSpec(memory_space=pl.ANY)
    conv_shape = (CONV_W - 1, n, CONV_DIM)
    return pl.pallas_call(
        body,
        grid=(n // TB,),
        in_specs=[_resident((n, D_MODEL)), _resident((1, HD)), _resident((1, HD))]
        + _mixer_weight_specs()
        + [state_hbm, state_hbm, _resident(conv_shape)],
        out_specs=[pl.BlockSpec((n, D_MODEL), lambda i: (0, 0)), state_spec, state_spec,
                   pl.BlockSpec(conv_shape, lambda i: (0, 0, 0))],
        out_shape=[
            jax.ShapeDtypeStruct((n, D_MODEL), F32),
            jax.ShapeDtypeStruct((n, HEADS, HD, HD), F32),
            jax.ShapeDtypeStruct((n, HEADS, HD, HD), F32),
            jax.ShapeDtypeStruct(conv_shape, F32),
        ],
        scratch_shapes=[pltpu.VMEM((n, OFF_AB), F32), pltpu.VMEM((n, 2 * D_MODEL), F32)]
        + [pltpu.VMEM((n, QK), F32) for _ in range(5)]
        + [pltpu.VMEM((n, HD), F32), pltpu.VMEM((n, HD), F32)]
        + [pltpu.VMEM((n, QK), F32), pltpu.VMEM((n, QK), F32)]
        + [pltpu.VMEM((STATE_SLOTS, TB, HEADS, HD, HD), F32), pltpu.VMEM((STATE_SLOTS, TB, HEADS, HD, HD), F32),
           pltpu.SemaphoreType.DMA((2, STATE_SLOTS))],
        compiler_params=pltpu.CompilerParams(dimension_semantics=("arbitrary",),
                                             vmem_limit_bytes=VMEM_LIMIT),
        name="mixer_sample",
    )(x, cos, sin, *_mixer_weight_args(w), s_ret, s_gdn, s_conv)


def _pad_lanes(v, n):
    return jnp.pad(v, ((0, 0), (0, n - v.shape[1])))


def kernel(x_prompt, x_sample, state_ret, state_gdn, state_conv, ffn1_pre_g, ffn1_post_g, ffn1_w_gate,
           ffn1_w_up, ffn1_w_down, mix_pre_g, mix_post_g, w_in, ret_norm_g, gdn_conv_w, gdn_a_log,
           gdn_dt_bias, gdn_norm_g, w_ret_branch, w_gdn_branch, w_out, ffn2_pre_g, ffn2_post_g,
           ffn2_w_gate, ffn2_w_up, ffn2_w_down):
    depth = w_in.shape[0]
    b, t, _ = x_prompt.shape
    n_s, t_s, _ = x_sample.shape
    assert t_s == 1
    yp = x_prompt.reshape(b * t, D_MODEL)
    ys = x_sample.reshape(n_s, D_MODEL)
    outs = [[] for _ in range(6)]
    for l in range(depth):
        row = lambda a: a[l][None, :]
        ys, wg1, wu1, wd1 = _ffn_stream(ys, row(ffn1_pre_g), row(ffn1_post_g),
                                        ffn1_w_gate[l], ffn1_w_up[l], ffn1_w_down[l])
        f1 = (row(ffn1_pre_g), row(ffn1_post_g), wg1, wu1, wd1)

        w_in_t = w_in[l].T
        casts = (
            (w_in_t, (0, OFF_AB)), (w_in_t, (OFF_GATES, 2 * D_MODEL)),
            (w_ret_branch[l], None), (w_gdn_branch[l], None), (w_out[l], None),
            (ffn2_w_gate[l], None), (ffn2_w_up[l], None), (ffn2_w_down[l], None),
        )
        yp, (w_main, w_gates, w_rb, w_gb, w_o, g2, u2, d2) = _ffn(yp, *f1, tm=FFN_TM, casts=casts)
        w = {
            "mix_pre_g": row(mix_pre_g), "mix_post_g": row(mix_post_g),
            "w_main": w_main,
            "w_ab": _pad_lanes(w_in[l, :, OFF_AB:OFF_GATES], HD).astype(BF16),
            "w_gates": w_gates,
            "conv_w": gdn_conv_w[l],
            "a_log": _pad_lanes(row(gdn_a_log), HD), "dt_bias": _pad_lanes(row(gdn_dt_bias), HD),
            "ret_norm_g": row(ret_norm_g), "gdn_norm_g": row(gdn_norm_g),
            "w_rb": w_rb, "w_gb": w_gb, "w_out": w_o,
        }
        f2 = (row(ffn2_pre_g), row(ffn2_post_g), g2, u2, d2)

        yp, r1, g1, c1 = _mixer_prompt(yp.reshape(b, t, D_MODEL), w)
        ys, r2, g2s, c2 = _mixer_sample(ys, state_ret[l], state_gdn[l],
                                        jnp.swapaxes(state_conv[l], 0, 1), w, PAST_LEN)
        c2 = jnp.swapaxes(c2, 0, 1)
        yp, ys, _ = _ffn(yp.reshape(b * t, D_MODEL), *f2, tm=FFN_TM, extra=ys)
        for lst, val in zip(outs, (r1, g1, c1, r2, g2s, c2)):
            lst.append(val)
    stacked = [v[0][None] if depth == 1 else jnp.stack(v) for v in outs]
    return (yp.reshape(b, t, D_MODEL), ys.reshape(n_s, t_s, D_MODEL), *stacked)
```

```python
import functools

import numpy as np
import jax
import jax.numpy as jnp
from jax import lax
from jax.experimental import pallas as pl
from jax.experimental.pallas import tpu as pltpu

F32 = jnp.float32
BF16 = jnp.bfloat16

D_MODEL = 1024
D_FF = 2816
HEADS = 4
HD = 128
QK = HEADS * HD
CONV_W = 4
CONV_DIM = 3 * QK
CHUNK = 64
ROPE_BASE = 10000.0
EPS = 1e-6
PAST_LEN = 16384

OFF_RQ, OFF_RK, OFF_RV, OFF_RG = 0, QK, 2 * QK, 3 * QK
OFF_CONV = 4 * QK
OFF_GZ = OFF_CONV + CONV_DIM
OFF_AB = OFF_GZ + QK
OFF_GATES = OFF_AB + 2 * HEADS
D_IN = OFF_GATES + 2 * D_MODEL

TM = 256
NB = 2
TB = 8
STATE_SLOTS = 3
FFN_TM = 1024
FFN_SUB = 256
FF_CHUNK = 256
CONV_PAD = 8
BF16_ROWS = 16
F32_ROWS = 8

VMEM_LIMIT = 56 * 1024 * 1024


def _silu(x):
    return x * jax.nn.sigmoid(x)


def _rms(x, g):
    return x * lax.rsqrt(jnp.mean(x * x, axis=-1, keepdims=True) + EPS) * g


def _dot(a, b):
    return jnp.dot(a.astype(BF16), b.astype(BF16), preferred_element_type=F32)


def _dot_nt(a, b):
    return lax.dot_general(a.astype(BF16), b.astype(BF16), (((1,), (1,)), ((), ())),
                           preferred_element_type=F32)


def _dot_tn(a, b):
    return lax.dot_general(a.astype(BF16), b.astype(BF16), (((0,), (0,)), ((), ())),
                           preferred_element_type=F32)


def _split3(a):
    hi = a.astype(BF16)
    r = a - hi.astype(F32)
    mid = r.astype(BF16)
    lo = (r - mid.astype(F32)).astype(BF16)
    return hi, mid, lo


def _resident(shape):
    nd = len(shape)
    return pl.BlockSpec(shape, lambda *_: (0,) * nd, pipeline_mode=pl.Buffered(1))


def _ffn_rows(x_ref, o_ref, pre_ref, post_ref, wg_ref, wu_ref, wd_ref):
    tm = x_ref.shape[0]
    sub = min(tm, FFN_SUB)
    parts = [slice(r, r + sub) for r in range(0, tm, sub)]
    xs = [x_ref[p, :] for p in parts]
    hs = [_rms(x, pre_ref[...]).astype(BF16) for x in xs]
    acts = []
    for h in hs:
        g = jnp.dot(h, wg_ref[...], preferred_element_type=F32)
        u = jnp.dot(h, wu_ref[...], preferred_element_type=F32)
        acts.append((_silu(g) * u).astype(BF16))
    for p, x, a in zip(parts, xs, acts):
        y = jnp.dot(a, wd_ref[...], preferred_element_type=F32)
        o_ref[p, :] = x + 0.5 * _rms(y, post_ref[...])


def _ffn_body(x_ref, pre_ref, post_ref, wg_ref, wu_ref, wd_ref, *rest, cast_transposed, has_extra):
    rest = list(rest)
    xe_ref = rest.pop(0) if has_extra else None
    n_cast = len(cast_transposed)
    cast_in, o_ref = rest[:n_cast], rest[n_cast]
    rest = rest[n_cast + 1:]
    oe_ref = rest.pop(0) if has_extra else None
    cast_out = rest
    weights = (pre_ref, post_ref, wg_ref, wu_ref, wd_ref)
    _ffn_rows(x_ref, o_ref, *weights)
    for src, dst, transposed in zip(cast_in, cast_out, cast_transposed):
        blk = src[...]
        dst[...] = (blk.T if transposed else blk).astype(BF16)
    if has_extra:
        @pl.when(pl.program_id(0) == pl.num_programs(0) - 1)
        def _():
            _ffn_rows(xe_ref, oe_ref, *weights)


def _cast_row_block(rows, steps):
    rb = BF16_ROWS
    while rows % rb or rows // rb > steps:
        rb += BF16_ROWS
    return rb


def _ffn(x, pre_g, post_g, wg, wu, wd, tm, casts=(), extra=None):
    n = x.shape[0]
    assert n % tm == 0
    steps = n // tm
    in_specs = [
        pl.BlockSpec((tm, D_MODEL), lambda i: (i, 0)),
        _resident((1, D_MODEL)), _resident((1, D_MODEL)),
        _resident((D_MODEL, D_FF)), _resident((D_MODEL, D_FF)), _resident((D_FF, D_MODEL)),
    ]
    out_specs = [pl.BlockSpec((tm, D_MODEL), lambda i: (i, 0))]
    out_shape = [jax.ShapeDtypeStruct((n, D_MODEL), F32)]
    operands = [x, pre_g, post_g, wg, wu, wd]
    if extra is not None:
        in_specs.append(_resident(extra.shape))
        operands.append(extra)
    n_fixed_in = len(in_specs)
    for arr, region in casts:
        rows, width = arr.shape
        if region is None:
            rb = _cast_row_block(rows, steps)
            last = rows // rb - 1
            spec = pl.BlockSpec((rb, width), lambda i, last=last: (jnp.minimum(i, last), 0))
            in_specs.append(spec)
            out_specs.append(spec)
            out_shape.append(jax.ShapeDtypeStruct((rows, width), BF16))
        else:
            row0, n_rows = region
            cb = HD
            while n_rows % cb or n_rows // cb > steps:
                cb += HD
            last = n_rows // cb - 1
            in_specs.append(pl.BlockSpec(
                (pl.Element(cb), pl.Element(width)),
                lambda i, row0=row0, last=last, cb=cb: (
                    pl.multiple_of(row0 + cb * jnp.minimum(i, last), F32_ROWS), 0)))
            out_specs.append(pl.BlockSpec((width, cb), lambda i, last=last: (0, jnp.minimum(i, last))))
            out_shape.append(jax.ShapeDtypeStruct((width, n_rows), BF16))
    assert len(in_specs) == n_fixed_in + len(casts)
    if extra is not None:
        out_specs.insert(1, pl.BlockSpec(extra.shape, lambda i: (0, 0)))
        out_shape.insert(1, jax.ShapeDtypeStruct(extra.shape, F32))
    outs = pl.pallas_call(
        functools.partial(_ffn_body, cast_transposed=tuple(c[1] is not None for c in casts),
                          has_extra=extra is not None),
        grid=(steps,),
        in_specs=in_specs,
        out_specs=out_specs,
        out_shape=out_shape,
        compiler_params=pltpu.CompilerParams(dimension_semantics=("arbitrary",),
                                             vmem_limit_bytes=VMEM_LIMIT),
        name="ffn",
    )(*operands, *[c[0] for c in casts])
    if extra is not None:
        return outs[0], outs[1], list(outs[2:])
    return outs[0], list(outs[1:])


def _ffn_stream_body(x_ref, pre_ref, post_ref, wg_hbm, wu_hbm, wd_hbm,
                     o_ref, wg16_ref, wu16_ref, wd16_ref, h_s, acc_s, wg_buf, wu_buf, wd_buf, sems):
    c = pl.program_id(0)
    n_steps = pl.num_programs(0)

    def chunk_copy(which, step):
        slot = step % STATE_SLOTS
        cols = pl.ds(step * FF_CHUNK, FF_CHUNK)
        src = (wg_hbm.at[:, cols], wu_hbm.at[:, cols], wd_hbm.at[cols, :])[which]
        dst = (wg_buf, wu_buf, wd_buf)[which].at[slot]
        return pltpu.make_async_copy(src, dst, sems.at[which, slot])

    @pl.when(c == 0)
    def _():
        for step in range(STATE_SLOTS - 1):
            for which in range(3):
                chunk_copy(which, step).start()

    @pl.when(c + (STATE_SLOTS - 1) < n_steps)
    def _():
        for which in range(3):
            chunk_copy(which, c + (STATE_SLOTS - 1)).start()

    @pl.when(c == 0)
    def _():
        h_s[...] = _rms(x_ref[...], pre_ref[...]).astype(BF16)
        acc_s[...] = jnp.zeros_like(acc_s)

    for which in range(3):
        chunk_copy(which, c).wait()
    slot = c % STATE_SLOTS
    wg = wg_buf[slot].astype(BF16)
    wu = wu_buf[slot].astype(BF16)
    wd = wd_buf[slot].astype(BF16)
    wg16_ref[...] = wg
    wu16_ref[...] = wu
    wd16_ref[...] = wd
    h = h_s[...]
    g = jnp.dot(h, wg, preferred_element_type=F32)
    u = jnp.dot(h, wu, preferred_element_type=F32)
    a = (_silu(g) * u).astype(BF16)
    acc_s[...] += jnp.dot(a, wd, preferred_element_type=F32)

    @pl.when(c == pl.num_programs(0) - 1)
    def _():
        o_ref[...] = x_ref[...] + 0.5 * _rms(acc_s[...], post_ref[...])


def _ffn_stream(x, pre_g, post_g, wg, wu, wd):
    n = x.shape[0]
    assert D_FF % FF_CHUNK == 0 and D_FF // FF_CHUNK >= STATE_SLOTS
    col = pl.BlockSpec((D_MODEL, FF_CHUNK), lambda c: (0, c))
    row = pl.BlockSpec((FF_CHUNK, D_MODEL), lambda c: (c, 0))
    hbm = pl.BlockSpec(memory_space=pl.ANY)
    return pl.pallas_call(
        _ffn_stream_body,
        grid=(D_FF // FF_CHUNK,),
        in_specs=[_resident((n, D_MODEL)), _resident((1, D_MODEL)), _resident((1, D_MODEL)), hbm, hbm, hbm],
        out_specs=[pl.BlockSpec((n, D_MODEL), lambda c: (0, 0)), col, col, row],
        out_shape=[
            jax.ShapeDtypeStruct((n, D_MODEL), F32),
            jax.ShapeDtypeStruct((D_MODEL, D_FF), BF16),
            jax.ShapeDtypeStruct((D_MODEL, D_FF), BF16),
            jax.ShapeDtypeStruct((D_FF, D_MODEL), BF16),
        ],
        scratch_shapes=[pltpu.VMEM((n, D_MODEL), BF16), pltpu.VMEM((n, D_MODEL), F32),
                        pltpu.VMEM((STATE_SLOTS, D_MODEL, FF_CHUNK), F32),
                        pltpu.VMEM((STATE_SLOTS, D_MODEL, FF_CHUNK), F32),
                        pltpu.VMEM((STATE_SLOTS, FF_CHUNK, D_MODEL), F32),
                        pltpu.SemaphoreType.DMA((3, STATE_SLOTS))],
        compiler_params=pltpu.CompilerParams(dimension_semantics=("arbitrary",),
                                             vmem_limit_bytes=VMEM_LIMIT),
        name="ffn_stream",
    )(x, pre_g, post_g, wg, wu, wd)


def _rope(x, cos, sin_signed):
    return x * cos + pltpu.roll(x, HD // 2, 1) * sin_signed


def _ret_out_norm(o, g_row, gate_act):
    mu = jnp.mean(o, axis=-1, keepdims=True)
    d = o - mu
    var = jnp.mean(d * d, axis=-1, keepdims=True)
    return gate_act * (d * lax.rsqrt(var + EPS) * g_row)


def _gdn_out_norm(o, g_row, gate_act):
    return o * lax.rsqrt(jnp.mean(o * o, axis=-1, keepdims=True) + EPS) * g_row * gate_act


def _l2norm(x, scale):
    return x * (lax.rsqrt(jnp.sum(x * x, axis=-1, keepdims=True) + EPS) * scale)


def _softplus(x):
    return jnp.maximum(x, 0.0) + jnp.log(1.0 + jnp.exp(-jnp.abs(x)))


def _merge(x, o_r, o_g, sig_gates, wrb_ref, wgb_ref, wout_ref, post_ref):
    y = (sig_gates[:, :D_MODEL] * _dot(o_r, wrb_ref[...])
         + sig_gates[:, D_MODEL:] * _dot(o_g, wgb_ref[...]))
    m = _dot(y, wout_ref[...])
    return x + _rms(m, post_ref[...])


def _pad_rows(a):
    return jnp.concatenate([a, jnp.zeros_like(a)], axis=0)


def _dot64(a, b):
    return jnp.dot(a.astype(BF16), _pad_rows(b.astype(BF16)), preferred_element_type=F32)


def _unit_lower_inverse_many(a_list, masks):
    eye, m16, off32, off64 = masks
    ad = [a * m16 for a in a_list]
    x = [eye - v for v in ad]
    p = [_dot64(v, v) for v in ad]
    for level in range(3):
        x = [xi + _dot64(xi, pi) for xi, pi in zip(x, p)]
        if level < 2:
            p = [_dot64(pi, pi) for pi in p]
    for m in (off32, off64):
        t = [_dot64(xi, a * m) for xi, a in zip(x, a_list)]
        x = [xi - _dot64(ti, xi) for xi, ti in zip(x, t)]
    return x


def _chunk_masks():
    r = lax.broadcasted_iota(jnp.int32, (CHUNK, 2 * CHUNK), 0)
    c = lax.broadcasted_iota(jnp.int32, (CHUNK, 2 * CHUNK), 1)
    one, zero = jnp.float32(1.0), jnp.float32(0.0)
    live = c < CHUNK
    eye = jnp.where(r == c, one, zero)
    m16 = jnp.where(live, jnp.where((r >> 4) == (c >> 4), one, zero), zero)
    m32 = jnp.where(live, jnp.where((r >> 5) == (c >> 5), one, zero), zero)
    m64 = jnp.where(live, one, zero)
    return r >= c, r > c, (eye, m16, m32 - m16, m64 - m32)


def _retention_tile(pm, cos, sin, rd_ref, ku_ref, retg_ref, s_ref, tile_decay):
    ri = lax.broadcasted_iota(jnp.int32, (TM, TM), 0)
    ci = lax.broadcasted_iota(jnp.int32, (TM, TM), 1)
    causal = ri >= ci
    heads = []
    for hh in range(HEADS):
        sl = slice(hh * HD, (hh + 1) * HD)
        rq = pm[:, OFF_RQ + hh * HD:OFF_RQ + (hh + 1) * HD]
        rk = pm[:, OFF_RK + hh * HD:OFF_RK + (hh + 1) * HD]
        v = pm[:, OFF_RV + hh * HD:OFF_RV + (hh + 1) * HD]
        rg = pm[:, OFF_RG + hh * HD:OFF_RG + (hh + 1) * HD]
        qs = _rope(rq, cos, sin) * rd_ref[hh]
        ku = _rope(rk, cos, sin) * ku_ref[hh]
        s = s_ref[hh]
        sc = jnp.where(causal, _dot_nt(qs, ku), 0.0)
        o = _dot(sc, v) + _dot(qs, s)
        s_ref[hh] = tile_decay[hh] * (s + _dot_tn(ku, v))
        heads.append(_ret_out_norm(o, retg_ref[:, sl], _silu(rg)))
    return jnp.concatenate(heads, axis=1)


def _short_conv_tile(u, buf, convw_ref, tail_ref):
    prev = buf[...]
    row = lax.broadcasted_iota(jnp.int32, (CONV_PAD, u.shape[1]), 0)
    acc = u * convw_ref[CONV_W - 1:CONV_W, :]
    for k in range(1, CONV_W):
        shifted = pltpu.roll(u, k, 0)
        head = jnp.where(row < k, pltpu.roll(prev, k, 0), shifted[0:CONV_PAD, :])
        shifted = jnp.concatenate([head, shifted[CONV_PAD:, :]], axis=0)
        acc = acc + shifted * convw_ref[CONV_W - 1 - k:CONV_W - k, :]
    buf[...] = u[TM - CONV_PAD:TM, :]
    tail_ref[...] = u[TM - (CONV_W - 1):TM, :]
    return _silu(acc)


def _gdn_decay_beta(ab, alog_ref, dtb_ref):
    g_all = -jnp.exp(alog_ref[...]) * _softplus(ab + dtb_ref[...])
    beta_all = jax.nn.sigmoid(ab)
    return g_all, beta_all


def _gdn_tiles(cqs, abs_, gzs, ltri, alog_ref, dtb_ref, gdng_ref, s_refs):
    seqs = range(len(cqs))
    dd = functools.partial(jnp.dot, preferred_element_type=F32)
    gcum, gcum_t, beta_all = [], [], []
    for b in seqs:
        g_all, beta = _gdn_decay_beta(abs_[b], alog_ref, dtb_ref)
        beta_all.append(beta)
        g_hi, g_mid, g_lo = _split3(g_all)
        gcum.append(dd(ltri, g_hi) + (dd(ltri, g_mid) + dd(ltri, g_lo)))
        gcum_t.append(gcum[b].T)

    causal64, strict64, masks = _chunk_masks()

    n_ch = TM // CHUNK
    rows = [slice(c * CHUNK, (c + 1) * CHUNK) for c in range(n_ch)]
    bh = [(b, hh) for b in seqs for hh in range(HEADS)]
    probs = [(b, hh, c) for b, hh in bh for c in range(n_ch)]
    gq = {(b, hh): _l2norm(cqs[b][:, hh * HD:(hh + 1) * HD], HD ** -0.5) for b, hh in bh}
    gk = {(b, hh): _l2norm(cqs[b][:, QK + hh * HD:QK + (hh + 1) * HD], 1.0) for b, hh in bh}
    gv = {(b, hh): cqs[b][:, 2 * QK + hh * HD:2 * QK + (hh + 1) * HD] for b, hh in bh}
    qc = {(b, hh, c): gq[b, hh][rows[c]] for b, hh, c in probs}
    kc = {(b, hh, c): gk[b, hh][rows[c]] for b, hh, c in probs}
    vc = {(b, hh, c): gv[b, hh][rows[c]] for b, hh, c in probs}
    gc = {(b, hh, c): gcum[b][rows[c], hh:hh + 1] for b, hh, c in probs}
    bcol = {(b, hh, c): beta_all[b][rows[c], HEADS + hh:HEADS + hh + 1] for b, hh, c in probs}
    def g_row(b, hh, c):
        gr = gcum_t[b][hh:hh + 1, rows[c]]
        return jnp.concatenate([gr, gr], axis=1)

    dm = {(b, hh, c): jnp.where(
        causal64, jnp.exp(jnp.minimum(gc[b, hh, c] - g_row(b, hh, c), 0.0)), 0.0)
        for b, hh, c in probs}
    kpad = {p: _pad_rows(kc[p].astype(BF16)) for p in probs}
    kk = {p: _dot_nt(kc[p], kpad[p]) for p in probs}
    qk = {p: _dot_nt(qc[p], kpad[p]) * dm[p] for p in probs}
    a_mats = [jnp.where(strict64, bcol[p] * dm[p] * kk[p], 0.0) for p in probs]
    tinv = dict(zip(probs, _unit_lower_inverse_many(a_mats, masks)))
    eg = {p: jnp.exp(gc[p]) for p in probs}
    uw = {p: _dot64(tinv[p], jnp.concatenate([bcol[p] * vc[p], (bcol[p] * eg[p]) * kc[p]], axis=1))
          for p in probs}
    gl = {p: gc[p][CHUNK - 1:CHUNK, :] for p in probs}
    uwp = {p: _pad_rows(uw[p].astype(BF16)) for p in probs}
    kdw = {p: _dot_tn(_pad_rows((kc[p] * jnp.exp(gl[p] - gc[p])).astype(BF16)), uwp[p])
           for p in probs}
    qkw = {p: jnp.dot(qk[p].astype(BF16), uwp[p], preferred_element_type=F32)
           for p in probs}
    lhs = {p: jnp.concatenate([kdw[p][:, HD:], eg[p] * qc[p] - qkw[p][:, HD:]], axis=0).astype(BF16)
           for p in probs}

    s = {(b, hh): s_refs[b][hh] for b, hh in bh}
    outs = {k: [] for k in bh}
    for c in range(n_ch):
        for b, hh in bh:
            p = (b, hh, c)
            ps = _dot(lhs[p], s[b, hh])
            outs[b, hh].append(ps[HD:, :] + qkw[p][:, :HD])
            s[b, hh] = jnp.exp(gl[p]) * s[b, hh] + (kdw[p][:, :HD] - ps[:HD, :])
    o_gs = []
    for b in seqs:
        heads = []
        for hh in range(HEADS):
            s_refs[b][hh] = s[b, hh]
            o = jnp.concatenate(outs[b, hh], axis=0)
            heads.append(_gdn_out_norm(o, gdng_ref[...], _silu(gzs[b][:, hh * HD:(hh + 1) * HD])))
        o_gs.append(jnp.concatenate(heads, axis=1))
    return o_gs


def _mixer_prompt_body(x_ref, cos_ref, sin_ref, rd_ref, ku_ref, ltri_ref,
                       pre_ref, post_ref, wmain_ref, wab_ref, wgates_ref, convw_ref, alog_ref, dtb_ref,
                       retg_ref, gdng_ref, wrb_ref, wgb_ref, wout_ref,
                       y_ref, sret_ref, sgdn_ref, conv_ref,
                       cbuf, *, ret_tile_decay):
    t = pl.program_id(1)
    nb = x_ref.shape[0]

    @pl.when(t == 0)
    def _():
        sret_ref[...] = jnp.zeros_like(sret_ref)
        sgdn_ref[...] = jnp.zeros_like(sgdn_ref)
        cbuf[:, 0:CONV_PAD, :] = jnp.zeros((nb, CONV_PAD, CONV_DIM), F32)

    cos = cos_ref[...]
    sin = sin_ref[...]
    ltri = ltri_ref[...]
    seqs = range(nb)
    rows = [slice(b * TM, (b + 1) * TM) for b in seqs]

    def stack(parts):
        return jnp.concatenate(list(parts), axis=0)

    def dot_all(lhs, w):
        return jnp.dot(lhs, w, preferred_element_type=F32)

    x_all = stack(x_ref[b] for b in seqs)
    h_all = _rms(x_all, pre_ref[...]).astype(BF16)
    pg_all = dot_all(h_all, wmain_ref[:, OFF_CONV:OFF_AB])
    ab_all = dot_all(h_all, wab_ref[...])
    cqs = [_short_conv_tile(pg_all[rows[b], 0:CONV_DIM], cbuf.at[b], convw_ref, conv_ref.at[b]) for b in seqs]
    pm_all = dot_all(h_all, wmain_ref[:, 0:OFF_CONV])
    o_rs = [_retention_tile(pm_all[rows[b]], cos, sin, rd_ref, ku_ref, retg_ref, sret_ref.at[b],
                            ret_tile_decay) for b in seqs]
    o_gs = _gdn_tiles(cqs, [ab_all[rows[b]] for b in seqs], [pg_all[rows[b], CONV_DIM:] for b in seqs],
                      ltri, alog_ref, dtb_ref, gdng_ref, [sgdn_ref.at[b] for b in seqs])
    gates = dot_all(h_all, wgates_ref[...])
    y_all = _merge(x_all, stack(o_rs), stack(o_gs), jax.nn.sigmoid(gates), wrb_ref, wgb_ref, wout_ref,
                   post_ref)
    for b in seqs:
        y_ref[b] = y_all[rows[b]]


def _ret_gammas():
    return 1.0 - 2.0 ** (-5.0 - np.arange(HEADS, dtype=np.float64))


def _rope_tables(pos):
    inv = ROPE_BASE ** (-np.arange(0, HD, 2, dtype=np.float64) / HD)
    ang = np.asarray(pos, np.float64)[:, None] * inv[None, :]
    cos = np.concatenate([np.cos(ang), np.cos(ang)], axis=1)
    sin = np.concatenate([-np.sin(ang), np.sin(ang)], axis=1)
    return jnp.asarray(cos, F32), jnp.asarray(sin, F32)


def _mixer_weight_specs():
    return [
        _resident((1, D_MODEL)), _resident((1, D_MODEL)),
        _resident((D_MODEL, OFF_AB)), _resident((D_MODEL, HD)), _resident((D_MODEL, 2 * D_MODEL)),
        _resident((CONV_W, CONV_DIM)), _resident((1, HD)), _resident((1, HD)),
        _resident((1, QK)), _resident((1, HD)),
        _resident((QK, D_MODEL)), _resident((QK, D_MODEL)), _resident((D_MODEL, D_MODEL)),
    ]


def _mixer_weight_args(w):
    return (w["mix_pre_g"], w["mix_post_g"], w["w_main"], w["w_ab"], w["w_gates"], w["conv_w"],
            w["a_log"], w["dt_bias"], w["ret_norm_g"], w["gdn_norm_g"], w["w_rb"], w["w_gb"], w["w_out"])


def _mixer_prompt(x, w):
    b, t, _ = x.shape
    assert t % TM == 0 and b % NB == 0
    cos, sin = _rope_tables(np.arange(t))
    gam = _ret_gammas()
    i1 = np.arange(1, TM + 1, dtype=np.float64)
    rd = np.broadcast_to((gam[:, None] ** i1[None, :])[:, :, None], (HEADS, TM, HD))
    ku = np.broadcast_to((HD ** -0.5 * gam[:, None] ** (-i1[None, :]))[:, :, None], (HEADS, TM, HD))
    tile_decay = tuple(float(v) for v in gam ** TM)
    r = np.arange(TM)
    ltri = ((r[:, None] >= r[None, :]) & (r[:, None] // CHUNK == r[None, :] // CHUNK))

    body = functools.partial(_mixer_prompt_body, ret_tile_decay=tile_decay)
    state_spec = pl.BlockSpec((NB, HEADS, HD, HD), lambda i, j: (i, 0, 0, 0))
    return pl.pallas_call(
        body,
        grid=(b // NB, t // TM),
        in_specs=[
            pl.BlockSpec((NB, TM, D_MODEL), lambda i, j: (i, j, 0)),
            pl.BlockSpec((TM, HD), lambda i, j: (j, 0)),
            pl.BlockSpec((TM, HD), lambda i, j: (j, 0)),
            _resident((HEADS, TM, HD)), _resident((HEADS, TM, HD)), _resident((TM, TM)),
        ] + _mixer_weight_specs(),
        out_specs=[
            pl.BlockSpec((NB, TM, D_MODEL), lambda i, j: (i, j, 0)),
            state_spec, state_spec,
            pl.BlockSpec((NB, CONV_W - 1, CONV_DIM), lambda i, j: (i, 0, 0)),
        ],
        out_shape=[
            jax.ShapeDtypeStruct((b, t, D_MODEL), F32),
            jax.ShapeDtypeStruct((b, HEADS, HD, HD), F32),
            jax.ShapeDtypeStruct((b, HEADS, HD, HD), F32),
            jax.ShapeDtypeStruct((b, CONV_W - 1, CONV_DIM), F32),
        ],
        scratch_shapes=[pltpu.VMEM((NB, CONV_PAD, CONV_DIM), F32)],
        compiler_params=pltpu.CompilerParams(dimension_semantics=("arbitrary", "arbitrary"),
                                             vmem_limit_bytes=VMEM_LIMIT),
        name="mixer_prompt",
    )(x, cos, sin, jnp.asarray(rd, F32), jnp.asarray(ku, F32), jnp.asarray(ltri, BF16),
      *_mixer_weight_args(w))


def _pick_rows(rows):
    ri = lax.broadcasted_iota(jnp.int32, rows[0].shape, 0)
    out = rows[0]
    for j in range(1, len(rows)):
        out = jnp.where(ri == j, rows[j], out)
    return out


def _mixer_sample_body(x_ref, cos_ref, sin_ref,
                       pre_ref, post_ref, wmain_ref, wab_ref, wgates_ref, convw_ref, alog_ref, dtb_ref,
                       retg_ref, gdng_ref, wrb_ref, wgb_ref, wout_ref,
                       sret_hbm, sgdn_hbm, conv_in,
                       y_ref, sret_out, sgdn_out, conv_out,
                       pm_s, gates_s, rq_s, rk_s, gq_s, gk_s, gv_s, eg_s, beta_s, or_s, og_s,
                       sret_buf, sgdn_buf, sems, *, ret_gamma):
    i = pl.program_id(0)
    n_steps = pl.num_programs(0)

    def state_copy(which, step):
        hbm, buf = ((sret_hbm, sret_buf), (sgdn_hbm, sgdn_buf))[which]
        slot = step % STATE_SLOTS
        return pltpu.make_async_copy(hbm.at[pl.ds(step * TB, TB)], buf.at[slot], sems.at[which, slot])

    @pl.when(i == 0)
    def _():
        for step in range(STATE_SLOTS - 1):
            state_copy(0, step).start()
            state_copy(1, step).start()

    @pl.when(i + (STATE_SLOTS - 1) < n_steps)
    def _():
        state_copy(0, i + (STATE_SLOTS - 1)).start()
        state_copy(1, i + (STATE_SLOTS - 1)).start()

    @pl.when(i == 0)
    def _():
        h = _rms(x_ref[...], pre_ref[...]).astype(BF16)
        pm = jnp.dot(h, wmain_ref[...], preferred_element_type=F32)
        ab = jnp.dot(h, wab_ref[...], preferred_element_type=F32)
        pm_s[...] = pm
        gates_s[...] = jnp.dot(h, wgates_ref[...], preferred_element_type=F32)
        cos = cos_ref[...]
        sin = sin_ref[...]
        cin = pm[:, OFF_CONV:OFF_CONV + CONV_DIM]
        acc = cin * convw_ref[CONV_W - 1:CONV_W, :]
        for r in range(CONV_W - 1):
            acc = acc + conv_in[r] * convw_ref[r:r + 1, :]
        for r in range(CONV_W - 2):
            conv_out[r] = conv_in[r + 1]
        conv_out[CONV_W - 2] = cin
        cq = _silu(acc)
        g_all, beta_all = _gdn_decay_beta(ab, alog_ref, dtb_ref)
        eg_s[...] = jnp.exp(g_all)
        beta_s[...] = beta_all
        for hh in range(HEADS):
            sl = slice(hh * HD, (hh + 1) * HD)
            rq_s[:, sl] = _rope(pm[:, OFF_RQ + hh * HD:OFF_RQ + (hh + 1) * HD], cos, sin)
            rk_s[:, sl] = _rope(pm[:, OFF_RK + hh * HD:OFF_RK + (hh + 1) * HD], cos, sin) * (HD ** -0.5)
            gq_s[:, sl] = _l2norm(cq[:, hh * HD:(hh + 1) * HD], HD ** -0.5)
            gk_s[:, sl] = _l2norm(cq[:, QK + hh * HD:QK + (hh + 1) * HD], 1.0)
        gv_s[...] = cq[:, 2 * QK:3 * QK]

    state_copy(0, i).wait()
    state_copy(1, i).wait()
    sret_in = sret_buf.at[i % STATE_SLOTS]
    sgdn_in = sgdn_buf.at[i % STATE_SLOTS]

    rows = pl.ds(pl.multiple_of(i * TB, TB), TB)
    eg_all = eg_s[rows, :]
    beta_all = beta_s[rows, :]
    for hh in range(HEADS):
        sl = slice(hh * HD, (hh + 1) * HD)
        q = rq_s[rows, sl]
        k_t = rk_s[rows, sl].T
        v = pm_s[rows, OFF_RV + hh * HD:OFF_RV + (hh + 1) * HD]
        o_rows = []
        for j in range(TB):
            s = ret_gamma[hh] * sret_in[j, hh] + k_t[:, j:j + 1] * v[j:j + 1, :]
            sret_out[j, hh] = s
            o_rows.append(_dot(q, s))
        or_s[rows, sl] = _pick_rows(o_rows)

        gq = gq_s[rows, sl]
        gk = gk_s[rows, sl]
        gk_t = gk.T
        gv = gv_s[rows, sl]
        o_rows = []
        for j in range(TB):
            s = sgdn_in[j, hh]
            eg = eg_all[j:j + 1, hh:hh + 1]
            beta = beta_all[j:j + 1, HEADS + hh:HEADS + hh + 1]
            ks = _dot(gk, s)[j:j + 1, :]
            u = beta * gv[j:j + 1, :] - (beta * eg) * ks
            s = eg * s + gk_t[:, j:j + 1] * u
            sgdn_out[j, hh] = s
            o_rows.append(_dot(gq, s))
        og_s[rows, sl] = _pick_rows(o_rows)

    @pl.when(i == pl.num_programs(0) - 1)
    def _():
        o_r = []
        o_g = []
        for hh in range(HEADS):
            sl = slice(hh * HD, (hh + 1) * HD)
            o_r.append(_ret_out_norm(or_s[:, sl], retg_ref[:, sl],
                                     _silu(pm_s[:, OFF_RG + hh * HD:OFF_RG + (hh + 1) * HD])))
            o_g.append(_gdn_out_norm(og_s[:, sl], gdng_ref[...],
                                     _silu(pm_s[:, OFF_GZ + hh * HD:OFF_GZ + (hh + 1) * HD])))
        y_ref[...] = _merge(x_ref[...], jnp.concatenate(o_r, axis=1), jnp.concatenate(o_g, axis=1),
                            jax.nn.sigmoid(gates_s[...]), wrb_ref, wgb_ref, wout_ref, post_ref)


def _mixer_sample(x, s_ret, s_gdn, s_conv, w, pos):
    n = x.shape[0]
    assert n % TB == 0 and n // TB >= STATE_SLOTS
    cos, sin = _rope_tables([pos])
    body = functools.partial(_mixer_sample_body, ret_gamma=tuple(float(v) for v in _ret_gammas()))
    state_spec = pl.BlockSpec((TB, HEADS, HD, HD), lambda i: (i, 0, 0, 0))
    state_hbm = pl.BlockSpec(memory_space=pl.ANY)
    conv_shape = (CONV_W - 1, n, CONV_DIM)
    return pl.pallas_call(
        body,
        grid=(n // TB,),
        in_specs=[_resident((n, D_MODEL)), _resident((1, HD)), _resident((1, HD))]
        + _mixer_weight_specs()
        + [state_hbm, state_hbm, _resident(conv_shape)],
        out_specs=[pl.BlockSpec((n, D_MODEL), lambda i: (0, 0)), state_spec, state_spec,
                   pl.BlockSpec(conv_shape, lambda i: (0, 0, 0))],
        out_shape=[
            jax.ShapeDtypeStruct((n, D_MODEL), F32),
            jax.ShapeDtypeStruct((n, HEADS, HD, HD), F32),
            jax.ShapeDtypeStruct((n, HEADS, HD, HD), F32),
            jax.ShapeDtypeStruct(conv_shape, F32),
        ],
        scratch_shapes=[pltpu.VMEM((n, OFF_AB), F32), pltpu.VMEM((n, 2 * D_MODEL), F32)]
        + [pltpu.VMEM((n, QK), F32) for _ in range(5)]
        + [pltpu.VMEM((n, HD), F32), pltpu.VMEM((n, HD), F32)]
        + [pltpu.VMEM((n, QK), F32), pltpu.VMEM((n, QK), F32)]
        + [pltpu.VMEM((STATE_SLOTS, TB, HEADS, HD, HD), F32), pltpu.VMEM((STATE_SLOTS, TB, HEADS, HD, HD), F32),
           pltpu.SemaphoreType.DMA((2, STATE_SLOTS))],
        compiler_params=pltpu.CompilerParams(dimension_semantics=("arbitrary",),
                                             vmem_limit_bytes=VMEM_LIMIT),
        name="mixer_sample",
    )(x, cos, sin, *_mixer_weight_args(w), s_ret, s_gdn, s_conv)


def _pad_lanes(v, n):
    return jnp.pad(v, ((0, 0), (0, n - v.shape[1])))


def kernel(x_prompt, x_sample, state_ret, state_gdn, state_conv, ffn1_pre_g, ffn1_post_g, ffn1_w_gate,
           ffn1_w_up, ffn1_w_down, mix_pre_g, mix_post_g, w_in, ret_norm_g, gdn_conv_w, gdn_a_log,
           gdn_dt_bias, gdn_norm_g, w_ret_branch, w_gdn_branch, w_out, ffn2_pre_g, ffn2_post_g,
           ffn2_w_gate, ffn2_w_up, ffn2_w_down):
    depth = w_in.shape[0]
    b, t, _ = x_prompt.shape
    n_s, t_s, _ = x_sample.shape
    assert t_s == 1
    yp = x_prompt.reshape(b * t, D_MODEL)
    ys = x_sample.reshape(n_s, D_MODEL)
    outs = [[] for _ in range(6)]
    for l in range(depth):
        row = lambda a: a[l][None, :]
        ys, wg1, wu1, wd1 = _ffn_stream(ys, row(ffn1_pre_g), row(ffn1_post_g),
                                        ffn1_w_gate[l], ffn1_w_up[l], ffn1_w_down[l])
        f1 = (row(ffn1_pre_g), row(ffn1_post_g), wg1, wu1, wd1)

        w_in_t = w_in[l].T
        casts = (
            (w_in_t, (0, OFF_AB)), (w_in_t, (OFF_GATES, 2 * D_MODEL)),
            (w_ret_branch[l], None), (w_gdn_branch[l], None), (w_out[l], None),
            (ffn2_w_gate[l], None), (ffn2_w_up[l], None), (ffn2_w_down[l], None),
        )
        yp, (w_main, w_gates, w_rb, w_gb, w_o, g2, u2, d2) = _ffn(yp, *f1, tm=FFN_TM, casts=casts)
        w = {
            "mix_pre_g": row(mix_pre_g), "mix_post_g": row(mix_post_g),
            "w_main": w_main,
            "w_ab": _pad_lanes(w_in[l, :, OFF_AB:OFF_GATES], HD).astype(BF16),
            "w_gates": w_gates,
            "conv_w": gdn_conv_w[l],
            "a_log": _pad_lanes(row(gdn_a_log), HD), "dt_bias": _pad_lanes(row(gdn_dt_bias), HD),
            "ret_norm_g": row(ret_norm_g), "gdn_norm_g": row(gdn_norm_g),
            "w_rb": w_rb, "w_gb": w_gb, "w_out": w_o,
        }
        f2 = (row(ffn2_pre_g), row(ffn2_post_g), g2, u2, d2)

        yp, r1, g1, c1 = _mixer_prompt(yp.reshape(b, t, D_MODEL), w)
        ys, r2, g2s, c2 = _mixer_sample(ys, state_ret[l], state_gdn[l],
                                        jnp.swapaxes(state_conv[l], 0, 1), w, PAST_LEN)
        c2 = jnp.swapaxes(c2, 0, 1)
        yp, ys, _ = _ffn(yp.reshape(b * t, D_MODEL), *f2, tm=FFN_TM, extra=ys)
        for lst, val in zip(outs, (r1, g1, c1, r2, g2s, c2)):
            lst.append(val)
    stacked = [v[0][None] if depth == 1 else jnp.stack(v) for v in outs]
    return (yp.reshape(b, t, D_MODEL), ys.reshape(n_s, t_s, D_MODEL), *stacked)
```

```python
import functools

import numpy as np
import jax
import jax.numpy as jnp
from jax import lax
from jax.experimental import pallas as pl
from jax.experimental.pallas import tpu as pltpu

F32 = jnp.float32
BF16 = jnp.bfloat16

D_MODEL = 1024
D_FF = 2816
HEADS = 4
HD = 128
QK = HEADS * HD
CONV_W = 4
CONV_DIM = 3 * QK
CHUNK = 64
ROPE_BASE = 10000.0
EPS = 1e-6
PAST_LEN = 16384

OFF_RQ, OFF_RK, OFF_RV, OFF_RG = 0, QK, 2 * QK, 3 * QK
OFF_CONV = 4 * QK
OFF_GZ = OFF_CONV + CONV_DIM
OFF_AB = OFF_GZ + QK
OFF_GATES = OFF_AB + 2 * HEADS
D_IN = OFF_GATES + 2 * D_MODEL

TM = 256
NB = 2
TB = 8
STATE_SLOTS = 3
FFN_TM = 1024
FFN_SUB = 256
FF_CHUNK = 256
CONV_PAD = 8
BF16_ROWS = 16
F32_ROWS = 8

VMEM_LIMIT = 56 * 1024 * 1024


def _silu(x):
    return x * jax.nn.sigmoid(x)


def _rms(x, g):
    return x * lax.rsqrt(jnp.mean(x * x, axis=-1, keepdims=True) + EPS) * g


def _dot(a, b):
    return jnp.dot(a.astype(BF16), b.astype(BF16), preferred_element_type=F32)


def _dot_nt(a, b):
    return lax.dot_general(a.astype(BF16), b.astype(BF16), (((1,), (1,)), ((), ())),
                           preferred_element_type=F32)


def _dot_tn(a, b):
    return lax.dot_general(a.astype(BF16), b.astype(BF16), (((0,), (0,)), ((), ())),
                           preferred_element_type=F32)


def _split3(a):
    hi = a.astype(BF16)
    r = a - hi.astype(F32)
    mid = r.astype(BF16)
    lo = (r - mid.astype(F32)).astype(BF16)
    return hi, mid, lo


def _resident(shape):
    nd = len(shape)
    return pl.BlockSpec(shape, lambda *_: (0,) * nd, pipeline_mode=pl.Buffered(1))


def _ffn_rows(x_ref, o_ref, pre_ref, post_ref, wg_ref, wu_ref, wd_ref):
    tm = x_ref.shape[0]
    sub = min(tm, FFN_SUB)
    parts = [slice(r, r + sub) for r in range(0, tm, sub)]
    xs = [x_ref[p, :] for p in parts]
    hs = [_rms(x, pre_ref[...]).astype(BF16) for x in xs]
    acts = []
    for h in hs:
        g = jnp.dot(h, wg_ref[...], preferred_element_type=F32)
        u = jnp.dot(h, wu_ref[...], preferred_element_type=F32)
        acts.append((_silu(g) * u).astype(BF16))
    for p, x, a in zip(parts, xs, acts):
        y = jnp.dot(a, wd_ref[...], preferred_element_type=F32)
        o_ref[p, :] = x + 0.5 * _rms(y, post_ref[...])


def _ffn_body(x_ref, pre_ref, post_ref, wg_ref, wu_ref, wd_ref, *rest, cast_transposed, has_extra):
    rest = list(rest)
    xe_ref = rest.pop(0) if has_extra else None
    n_cast = len(cast_transposed)
    cast_in, o_ref = rest[:n_cast], rest[n_cast]
    rest = rest[n_cast + 1:]
    oe_ref = rest.pop(0) if has_extra else None
    cast_out = rest
    weights = (pre_ref, post_ref, wg_ref, wu_ref, wd_ref)
    _ffn_rows(x_ref, o_ref, *weights)
    for src, dst, transposed in zip(cast_in, cast_out, cast_transposed):
        blk = src[...]
        dst[...] = (blk.T if transposed else blk).astype(BF16)
    if has_extra:
        @pl.when(pl.program_id(0) == pl.num_programs(0) - 1)
        def _():
            _ffn_rows(xe_ref, oe_ref, *weights)


def _cast_row_block(rows, steps):
    rb = BF16_ROWS
    while rows % rb or rows // rb > steps:
        rb += BF16_ROWS
    return rb


def _ffn(x, pre_g, post_g, wg, wu, wd, tm, casts=(), extra=None):
    n = x.shape[0]
    assert n % tm == 0
    steps = n // tm
    in_specs = [
        pl.BlockSpec((tm, D_MODEL), lambda i: (i, 0)),
        _resident((1, D_MODEL)), _resident((1, D_MODEL)),
        _resident((D_MODEL, D_FF)), _resident((D_MODEL, D_FF)), _resident((D_FF, D_MODEL)),
    ]
    out_specs = [pl.BlockSpec((tm, D_MODEL), lambda i: (i, 0))]
    out_shape = [jax.ShapeDtypeStruct((n, D_MODEL), F32)]
    operands = [x, pre_g, post_g, wg, wu, wd]
    if extra is not None:
        in_specs.append(_resident(extra.shape))
        operands.append(extra)
    n_fixed_in = len(in_specs)
    for arr, region in casts:
        rows, width = arr.shape
        if region is None:
            rb = _cast_row_block(rows, steps)
            last = rows // rb - 1
            spec = pl.BlockSpec((rb, width), lambda i, last=last: (jnp.minimum(i, last), 0))
            in_specs.append(spec)
            out_specs.append(spec)
            out_shape.append(jax.ShapeDtypeStruct((rows, width), BF16))
        else:
            row0, n_rows = region
            cb = HD
            while n_rows % cb or n_rows // cb > steps:
                cb += HD
            last = n_rows // cb - 1
            in_specs.append(pl.BlockSpec(
                (pl.Element(cb), pl.Element(width)),
                lambda i, row0=row0, last=last, cb=cb: (
                    pl.multiple_of(row0 + cb * jnp.minimum(i, last), F32_ROWS), 0)))
            out_specs.append(pl.BlockSpec((width, cb), lambda i, last=last: (0, jnp.minimum(i, last))))
            out_shape.append(jax.ShapeDtypeStruct((width, n_rows), BF16))
    assert len(in_specs) == n_fixed_in + len(casts)
    if extra is not None:
        out_specs.insert(1, pl.BlockSpec(extra.shape, lambda i: (0, 0)))
        out_shape.insert(1, jax.ShapeDtypeStruct(extra.shape, F32))
    outs = pl.pallas_call(
        functools.partial(_ffn_body, cast_transposed=tuple(c[1] is not None for c in casts),
                          has_extra=extra is not None),
        grid=(steps,),
        in_specs=in_specs,
        out_specs=out_specs,
        out_shape=out_shape,
        compiler_params=pltpu.CompilerParams(dimension_semantics=("arbitrary",),
                                             vmem_limit_bytes=VMEM_LIMIT),
        name="ffn",
    )(*operands, *[c[0] for c in casts])
    if extra is not None:
        return outs[0], outs[1], list(outs[2:])
    return outs[0], list(outs[1:])


def _ffn_stream_body(x_ref, pre_ref, post_ref, wg_hbm, wu_hbm, wd_hbm,
                     o_ref, wg16_ref, wu16_ref, wd16_ref, h_s, acc_s, wg_buf, wu_buf, wd_buf, sems):
    c = pl.program_id(0)
    n_steps = pl.num_programs(0)

    def chunk_copy(which, step):
        slot = step % STATE_SLOTS
        cols = pl.ds(step * FF_CHUNK, FF_CHUNK)
        src = (wg_hbm.at[:, cols], wu_hbm.at[:, cols], wd_hbm.at[cols, :])[which]
        dst = (wg_buf, wu_buf, wd_buf)[which].at[slot]
        return pltpu.make_async_copy(src, dst, sems.at[which, slot])

    @pl.when(c == 0)
    def _():
        for step in range(STATE_SLOTS - 1):
            for which in range(3):
                chunk_copy(which, step).start()

    @pl.when(c + (STATE_SLOTS - 1) < n_steps)
    def _():
        for which in range(3):
            chunk_copy(which, c + (STATE_SLOTS - 1)).start()

    @pl.when(c == 0)
    def _():
        h_s[...] = _rms(x_ref[...], pre_ref[...]).astype(BF16)
        acc_s[...] = jnp.zeros_like(acc_s)

    for which in range(3):
        chunk_copy(which, c).wait()
    slot = c % STATE_SLOTS
    wg = wg_buf[slot].astype(BF16)
    wu = wu_buf[slot].astype(BF16)
    wd = wd_buf[slot].astype(BF16)
    wg16_ref[...] = wg
    wu16_ref[...] = wu
    wd16_ref[...] = wd
    h = h_s[...]
    g = jnp.dot(h, wg, preferred_element_type=F32)
    u = jnp.dot(h, wu, preferred_element_type=F32)
    a = (_silu(g) * u).astype(BF16)
    acc_s[...] += jnp.dot(a, wd, preferred_element_type=F32)

    @pl.when(c == pl.num_programs(0) - 1)
    def _():
        o_ref[...] = x_ref[...] + 0.5 * _rms(acc_s[...], post_ref[...])


def _ffn_stream(x, pre_g, post_g, wg, wu, wd):
    n = x.shape[0]
    assert D_FF % FF_CHUNK == 0 and D_FF // FF_CHUNK >= STATE_SLOTS
    col = pl.BlockSpec((D_MODEL, FF_CHUNK), lambda c: (0, c))
    row = pl.BlockSpec((FF_CHUNK, D_MODEL), lambda c: (c, 0))
    hbm = pl.BlockSpec(memory_space=pl.ANY)
    return pl.pallas_call(
        _ffn_stream_body,
        grid=(D_FF // FF_CHUNK,),
        in_specs=[_resident((n, D_MODEL)), _resident((1, D_MODEL)), _resident((1, D_MODEL)), hbm, hbm, hbm],
        out_specs=[pl.BlockSpec((n, D_MODEL), lambda c: (0, 0)), col, col, row],
        out_shape=[
            jax.ShapeDtypeStruct((n, D_MODEL), F32),
            jax.ShapeDtypeStruct((D_MODEL, D_FF), BF16),
            jax.ShapeDtypeStruct((D_MODEL, D_FF), BF16),
            jax.ShapeDtypeStruct((D_FF, D_MODEL), BF16),
        ],
        scratch_shapes=[pltpu.VMEM((n, D_MODEL), BF16), pltpu.VMEM((n, D_MODEL), F32),
                        pltpu.VMEM((STATE_SLOTS, D_MODEL, FF_CHUNK), F32),
                        pltpu.VMEM((STATE_SLOTS, D_MODEL, FF_CHUNK), F32),
                        pltpu.VMEM((STATE_SLOTS, FF_CHUNK, D_MODEL), F32),
                        pltpu.SemaphoreType.DMA((3, STATE_SLOTS))],
        compiler_params=pltpu.CompilerParams(dimension_semantics=("arbitrary",),
                                             vmem_limit_bytes=VMEM_LIMIT),
        name="ffn_stream",
    )(x, pre_g, post_g, wg, wu, wd)


def _rope(x, cos, sin_signed):
    return x * cos + pltpu.roll(x, HD // 2, 1) * sin_signed


def _ret_out_norm(o, g_row, gate_act):
    mu = jnp.mean(o, axis=-1, keepdims=True)
    d = o - mu
    var = jnp.mean(d * d, axis=-1, keepdims=True)
    return gate_act * (d * lax.rsqrt(var + EPS) * g_row)


def _gdn_out_norm(o, g_row, gate_act):
    return o * lax.rsqrt(jnp.mean(o * o, axis=-1, keepdims=True) + EPS) * g_row * gate_act


def _l2norm(x, scale):
    return x * (lax.rsqrt(jnp.sum(x * x, axis=-1, keepdims=True) + EPS) * scale)


def _softplus(x):
    return jnp.maximum(x, 0.0) + jnp.log(1.0 + jnp.exp(-jnp.abs(x)))


def _merge(x, o_r, o_g, sig_gates, wrb_ref, wgb_ref, wout_ref, post_ref):
    y = (sig_gates[:, :D_MODEL] * _dot(o_r, wrb_ref[...])
         + sig_gates[:, D_MODEL:] * _dot(o_g, wgb_ref[...]))
    m = _dot(y, wout_ref[...])
    return x + _rms(m, post_ref[...])


def _pad_rows(a):
    return jnp.concatenate([a, jnp.zeros_like(a)], axis=0)


def _dot64(a, b):
    return jnp.dot(a.astype(BF16), _pad_rows(b.astype(BF16)), preferred_element_type=F32)


def _unit_lower_inverse_many(a_list, masks):
    eye, m16, off32, off64 = masks
    ad = [a * m16 for a in a_list]
    x = [eye - v for v in ad]
    p = [_dot64(v, v) for v in ad]
    for level in range(3):
        x = [xi + _dot64(xi, pi) for xi, pi in zip(x, p)]
        if level < 2:
            p = [_dot64(pi, pi) for pi in p]
    for m in (off32, off64):
        t = [_dot64(xi, a * m) for xi, a in zip(x, a_list)]
        x = [xi - _dot64(ti, xi) for xi, ti in zip(x, t)]
    return x


def _chunk_masks():
    r = lax.broadcasted_iota(jnp.int32, (CHUNK, 2 * CHUNK), 0)
    c = lax.broadcasted_iota(jnp.int32, (CHUNK, 2 * CHUNK), 1)
    one, zero = jnp.float32(1.0), jnp.float32(0.0)
    live = c < CHUNK
    eye = jnp.where(r == c, one, zero)
    m16 = jnp.where(live, jnp.where((r >> 4) == (c >> 4), one, zero), zero)
    m32 = jnp.where(live, jnp.where((r >> 5) == (c >> 5), one, zero), zero)
    m64 = jnp.where(live, one, zero)
    return r >= c, r > c, (eye, m16, m32 - m16, m64 - m32)


def _retention_tile(pm, cos, sin, rd_ref, ku_ref, retg_ref, s_ref, tile_decay):
    ri = lax.broadcasted_iota(jnp.int32, (TM, TM), 0)
    ci = lax.broadcasted_iota(jnp.int32, (TM, TM), 1)
    causal = ri >= ci
    heads = []
    for hh in range(HEADS):
        sl = slice(hh * HD, (hh + 1) * HD)
        rq = pm[:, OFF_RQ + hh * HD:OFF_RQ + (hh + 1) * HD]
        rk = pm[:, OFF_RK + hh * HD:OFF_RK + (hh + 1) * HD]
        v = pm[:, OFF_RV + hh * HD:OFF_RV + (hh + 1) * HD]
        rg = pm[:, OFF_RG + hh * HD:OFF_RG + (hh + 1) * HD]
        qs = _rope(rq, cos, sin) * rd_ref[hh]
        ku = _rope(rk, cos, sin) * ku_ref[hh]
        s = s_ref[hh]
        sc = jnp.where(causal, _dot_nt(qs, ku), 0.0)
        o = _dot(sc, v) + _dot(qs, s)
        s_ref[hh] = tile_decay[hh] * (s + _dot_tn(ku, v))
        heads.append(_ret_out_norm(o, retg_ref[:, sl], _silu(rg)))
    return jnp.concatenate(heads, axis=1)


def _short_conv_tile(u, buf, convw_ref, tail_ref):
    prev = buf[...]
    row = lax.broadcasted_iota(jnp.int32, (CONV_PAD, u.shape[1]), 0)
    acc = u * convw_ref[CONV_W - 1:CONV_W, :]
    for k in range(1, CONV_W):
        shifted = pltpu.roll(u, k, 0)
        head = jnp.where(row < k, pltpu.roll(prev, k, 0), shifted[0:CONV_PAD, :])
        shifted = jnp.concatenate([head, shifted[CONV_PAD:, :]], axis=0)
        acc = acc + shifted * convw_ref[CONV_W - 1 - k:CONV_W - k, :]
    buf[...] = u[TM - CONV_PAD:TM, :]
    tail_ref[...] = u[TM - (CONV_W - 1):TM, :]
    return _silu(acc)


def _gdn_decay_beta(ab, alog_ref, dtb_ref):
    g_all = -jnp.exp(alog_ref[...]) * _softplus(ab + dtb_ref[...])
    beta_all = jax.nn.sigmoid(ab)
    return g_all, beta_all


def _gdn_tiles(cqs, abs_, gzs, ltri, alog_ref, dtb_ref, gdng_ref, s_refs):
    seqs = range(len(cqs))
    dd = functools.partial(jnp.dot, preferred_element_type=F32)
    gcum, gcum_t, beta_all = [], [], []
    for b in seqs:
        g_all, beta = _gdn_decay_beta(abs_[b], alog_ref, dtb_ref)
        beta_all.append(beta)
        g_hi, g_mid, g_lo = _split3(g_all)
        gcum.append(dd(ltri, g_hi) + (dd(ltri, g_mid) + dd(ltri, g_lo)))
        gcum_t.append(gcum[b].T)

    causal64, strict64, masks = _chunk_masks()

    n_ch = TM // CHUNK
    rows = [slice(c * CHUNK, (c + 1) * CHUNK) for c in range(n_ch)]
    bh = [(b, hh) for b in seqs for hh in range(HEADS)]
    probs = [(b, hh, c) for b, hh in bh for c in range(n_ch)]
    gq = {(b, hh): _l2norm(cqs[b][:, hh * HD:(hh + 1) * HD], HD ** -0.5) for b, hh in bh}
    gk = {(b, hh): _l2norm(cqs[b][:, QK + hh * HD:QK + (hh + 1) * HD], 1.0) for b, hh in bh}
    gv = {(b, hh): cqs[b][:, 2 * QK + hh * HD:2 * QK + (hh + 1) * HD] for b, hh in bh}
    qc = {(b, hh, c): gq[b, hh][rows[c]] for b, hh, c in probs}
    kc = {(b, hh, c): gk[b, hh][rows[c]] for b, hh, c in probs}
    vc = {(b, hh, c): gv[b, hh][rows[c]] for b, hh, c in probs}
    gc = {(b, hh, c): gcum[b][rows[c], hh:hh + 1] for b, hh, c in probs}
    bcol = {(b, hh, c): beta_all[b][rows[c], HEADS + hh:HEADS + hh + 1] for b, hh, c in probs}
    def g_row(b, hh, c):
        gr = gcum_t[b][hh:hh + 1, rows[c]]
        return jnp.concatenate([gr, gr], axis=1)

    dm = {(b, hh, c): jnp.where(
        causal64, jnp.exp(jnp.minimum(gc[b, hh, c] - g_row(b, hh, c), 0.0)), 0.0)
        for b, hh, c in probs}
    kpad = {p: _pad_rows(kc[p].astype(BF16)) for p in probs}
    kk = {p: _dot_nt(kc[p], kpad[p]) for p in probs}
    qk = {p: _dot_nt(qc[p], kpad[p]) * dm[p] for p in probs}
    a_mats = [jnp.where(strict64, bcol[p] * dm[p] * kk[p], 0.0) for p in probs]
    tinv = dict(zip(probs, _unit_lower_inverse_many(a_mats, masks)))
    eg = {p: jnp.exp(gc[p]) for p in probs}
    uw = {p: _dot64(tinv[p], jnp.concatenate([bcol[p] * vc[p], (bcol[p] * eg[p]) * kc[p]], axis=1))
          for p in probs}
    gl = {p: gc[p][CHUNK - 1:CHUNK, :] for p in probs}
    uwp = {p: _pad_rows(uw[p].astype(BF16)) for p in probs}
    kdw = {p: _dot_tn(_pad_rows((kc[p] * jnp.exp(gl[p] - gc[p])).astype(BF16)), uwp[p])
           for p in probs}
    qkw = {p: jnp.dot(qk[p].astype(BF16), uwp[p], preferred_element_type=F32)
           for p in probs}
    lhs = {p: jnp.concatenate([kdw[p][:, HD:], eg[p] * qc[p] - qkw[p][:, HD:]], axis=0).astype(BF16)
           for p in probs}

    s = {(b, hh): s_refs[b][hh] for b, hh in bh}
    outs = {k: [] for k in bh}
    for c in range(n_ch):
        for b, hh in bh:
            p = (b, hh, c)
            ps = _dot(lhs[p], s[b, hh])
            outs[b, hh].append(ps[HD:, :] + qkw[p][:, :HD])
            s[b, hh] = jnp.exp(gl[p]) * s[b, hh] + (kdw[p][:, :HD] - ps[:HD, :])
    o_gs = []
    for b in seqs:
        heads = []
        for hh in range(HEADS):
            s_refs[b][hh] = s[b, hh]
            o = jnp.concatenate(outs[b, hh], axis=0)
            heads.append(_gdn_out_norm(o, gdng_ref[...], _silu(gzs[b][:, hh * HD:(hh + 1) * HD])))
        o_gs.append(jnp.concatenate(heads, axis=1))
    return o_gs


def _mixer_prompt_body(x_ref, cos_ref, sin_ref, rd_ref, ku_ref, ltri_ref,
                       pre_ref, post_ref, wmain_ref, wab_ref, wgates_ref, convw_ref, alog_ref, dtb_ref,
                       retg_ref, gdng_ref, wrb_ref, wgb_ref, wout_ref,
                       y_ref, sret_ref, sgdn_ref, conv_ref,
                       cbuf, *, ret_tile_decay):
    t = pl.program_id(1)
    nb = x_ref.shape[0]

    @pl.when(t == 0)
    def _():
        sret_ref[...] = jnp.zeros_like(sret_ref)
        sgdn_ref[...] = jnp.zeros_like(sgdn_ref)
        cbuf[:, 0:CONV_PAD, :] = jnp.zeros((nb, CONV_PAD, CONV_DIM), F32)

    cos = cos_ref[...]
    sin = sin_ref[...]
    ltri = ltri_ref[...]
    seqs = range(nb)
    rows = [slice(b * TM, (b + 1) * TM) for b in seqs]

    def stack(parts):
        return jnp.concatenate(list(parts), axis=0)

    def dot_all(lhs, w):
        return jnp.dot(lhs, w, preferred_element_type=F32)

    x_all = stack(x_ref[b] for b in seqs)
    h_all = _rms(x_all, pre_ref[...]).astype(BF16)
    pg_all = dot_all(h_all, wmain_ref[:, OFF_CONV:OFF_AB])
    ab_all = dot_all(h_all, wab_ref[...])
    cqs = [_short_conv_tile(pg_all[rows[b], 0:CONV_DIM], cbuf.at[b], convw_ref, conv_ref.at[b]) for b in seqs]
    pm_all = dot_all(h_all, wmain_ref[:, 0:OFF_CONV])
    o_rs = [_retention_tile(pm_all[rows[b]], cos, sin, rd_ref, ku_ref, retg_ref, sret_ref.at[b],
                            ret_tile_decay) for b in seqs]
    o_gs = _gdn_tiles(cqs, [ab_all[rows[b]] for b in seqs], [pg_all[rows[b], CONV_DIM:] for b in seqs],
                      ltri, alog_ref, dtb_ref, gdng_ref, [sgdn_ref.at[b] for b in seqs])
    gates = dot_all(h_all, wgates_ref[...])
    y_all = _merge(x_all, stack(o_rs), stack(o_gs), jax.nn.sigmoid(gates), wrb_ref, wgb_ref, wout_ref,
                   post_ref)
    for b in seqs:
        y_ref[b] = y_all[rows[b]]


def _ret_gammas():
    return 1.0 - 2.0 ** (-5.0 - np.arange(HEADS, dtype=np.float64))


def _rope_tables(pos):
    inv = ROPE_BASE ** (-np.arange(0, HD, 2, dtype=np.float64) / HD)
    ang = np.asarray(pos, np.float64)[:, None] * inv[None, :]
    cos = np.concatenate([np.cos(ang), np.cos(ang)], axis=1)
    sin = np.concatenate([-np.sin(ang), np.sin(ang)], axis=1)
    return jnp.asarray(cos, F32), jnp.asarray(sin, F32)


def _mixer_weight_specs():
    return [
        _resident((1, D_MODEL)), _resident((1, D_MODEL)),
        _resident((D_MODEL, OFF_AB)), _resident((D_MODEL, HD)), _resident((D_MODEL, 2 * D_MODEL)),
        _resident((CONV_W, CONV_DIM)), _resident((1, HD)), _resident((1, HD)),
        _resident((1, QK)), _resident((1, HD)),
        _resident((QK, D_MODEL)), _resident((QK, D_MODEL)), _resident((D_MODEL, D_MODEL)),
    ]


def _mixer_weight_args(w):
    return (w["mix_pre_g"], w["mix_post_g"], w["w_main"], w["w_ab"], w["w_gates"], w["conv_w"],
            w["a_log"], w["dt_bias"], w["ret_norm_g"], w["gdn_norm_g"], w["w_rb"], w["w_gb"], w["w_out"])


def _mixer_prompt(x, w):
    b, t, _ = x.shape
    assert t % TM == 0 and b % NB == 0
    cos, sin = _rope_tables(np.arange(t))
    gam = _ret_gammas()
    i1 = np.arange(1, TM + 1, dtype=np.float64)
    rd = np.broadcast_to((gam[:, None] ** i1[None, :])[:, :, None], (HEADS, TM, HD))
    ku = np.broadcast_to((HD ** -0.5 * gam[:, None] ** (-i1[None, :]))[:, :, None], (HEADS, TM, HD))
    tile_decay = tuple(float(v) for v in gam ** TM)
    r = np.arange(TM)
    ltri = ((r[:, None] >= r[None, :]) & (r[:, None] // CHUNK == r[None, :] // CHUNK))

    body = functools.partial(_mixer_prompt_body, ret_tile_decay=tile_decay)
    state_spec = pl.BlockSpec((NB, HEADS, HD, HD), lambda i, j: (i, 0, 0, 0))
    return pl.pallas_call(
        body,
        grid=(b // NB, t // TM),
        in_specs=[
            pl.BlockSpec((NB, TM, D_MODEL), lambda i, j: (i, j, 0)),
            pl.BlockSpec((TM, HD), lambda i, j: (j, 0)),
            pl.BlockSpec((TM, HD), lambda i, j: (j, 0)),
            _resident((HEADS, TM, HD)), _resident((HEADS, TM, HD)), _resident((TM, TM)),
        ] + _mixer_weight_specs(),
        out_specs=[
            pl.BlockSpec((NB, TM, D_MODEL), lambda i, j: (i, j, 0)),
            state_spec, state_spec,
            pl.BlockSpec((NB, CONV_W - 1, CONV_DIM), lambda i, j: (i, 0, 0)),
        ],
        out_shape=[
            jax.ShapeDtypeStruct((b, t, D_MODEL), F32),
            jax.ShapeDtypeStruct((b, HEADS, HD, HD), F32),
            jax.ShapeDtypeStruct((b, HEADS, HD, HD), F32),
            jax.ShapeDtypeStruct((b, CONV_W - 1, CONV_DIM), F32),
        ],
        scratch_shapes=[pltpu.VMEM((NB, CONV_PAD, CONV_DIM), F32)],
        compiler_params=pltpu.CompilerParams(dimension_semantics=("arbitrary", "arbitrary"),
                                             vmem_limit_bytes=VMEM_LIMIT),
        name="mixer_prompt",
    )(x, cos, sin, jnp.asarray(rd, F32), jnp.asarray(ku, F32), jnp.asarray(ltri, BF16),
      *_mixer_weight_args(w))


def _pick_rows(rows):
    ri = lax.broadcasted_iota(jnp.int32, rows[0].shape, 0)
    out = rows[0]
    for j in range(1, len(rows)):
        out = jnp.where(ri == j, rows[j], out)
    return out


def _mixer_sample_body(x_ref, cos_ref, sin_ref,
                       pre_ref, post_ref, wmain_ref, wab_ref, wgates_ref, convw_ref, alog_ref, dtb_ref,
                       retg_ref, gdng_ref, wrb_ref, wgb_ref, wout_ref,
                       sret_hbm, sgdn_hbm, conv_in,
                       y_ref, sret_out, sgdn_out, conv_out,
                       pm_s, gates_s, rq_s, rk_s, gq_s, gk_s, gv_s, eg_s, beta_s, or_s, og_s,
                       sret_buf, sgdn_buf, sems, *, ret_gamma):
    i = pl.program_id(0)
    n_steps = pl.num_programs(0)

    def state_copy(which, step):
        hbm, buf = ((sret_hbm, sret_buf), (sgdn_hbm, sgdn_buf))[which]
        slot = step % STATE_SLOTS
        return pltpu.make_async_copy(hbm.at[pl.ds(step * TB, TB)], buf.at[slot], sems.at[which, slot])

    @pl.when(i == 0)
    def _():
        for step in range(STATE_SLOTS - 1):
            state_copy(0, step).start()
            state_copy(1, step).start()

    @pl.when(i == 0)
    def _():
        h = _rms(x_ref[...], pre_ref[...]).astype(BF16)
        pm = jnp.dot(h, wmain_ref[...], preferred_element_type=F32)
        ab = jnp.dot(h, wab_ref[...], preferred_element_type=F32)
        pm_s[...] = pm
        gates_s[...] = jnp.dot(h, wgates_ref[...], preferred_element_type=F32)
        cos = cos_ref[...]
        sin = sin_ref[...]
        cin = pm[:, OFF_CONV:OFF_CONV + CONV_DIM]
        acc = cin * convw_ref[CONV_W - 1:CONV_W, :]
        for r in range(CONV_W - 1):
            acc = acc + conv_in[r] * convw_ref[r:r + 1, :]
        for r in range(CONV_W - 2):
            conv_out[r] = conv_in[r + 1]
        conv_out[CONV_W - 2] = cin
        cq = _silu(acc)
        g_all, beta_all = _gdn_decay_beta(ab, alog_ref, dtb_ref)
        eg_s[...] = jnp.exp(g_all)
        beta_s[...] = beta_all
        for hh in range(HEADS):
            sl = slice(hh * HD, (hh + 1) * HD)
            rq_s[:, sl] = _rope(pm[:, OFF_RQ + hh * HD:OFF_RQ + (hh + 1) * HD], cos, sin)
            rk_s[:, sl] = _rope(pm[:, OFF_RK + hh * HD:OFF_RK + (hh + 1) * HD], cos, sin) * (HD ** -0.5)
            gq_s[:, sl] = _l2norm(cq[:, hh * HD:(hh + 1) * HD], HD ** -0.5)
            gk_s[:, sl] = _l2norm(cq[:, QK + hh * HD:QK + (hh + 1) * HD], 1.0)
        gv_s[...] = cq[:, 2 * QK:3 * QK]

    rows = pl.ds(pl.multiple_of(i * TB, TB), TB)
    eg_all = eg_s[rows, :]
    beta_all = beta_s[rows, :]
    feats = []
    for hh in range(HEADS):
        sl = slice(hh * HD, (hh + 1) * HD)
        gk = gk_s[rows, sl]
        feats.append((rq_s[rows, sl], rk_s[rows, sl].T, pm_s[rows, OFF_RV + hh * HD:OFF_RV + (hh + 1) * HD],
                      gq_s[rows, sl], gk, gk.T, gv_s[rows, sl]))

    @pl.when(i + (STATE_SLOTS - 1) < n_steps)
    def _():
        state_copy(0, i + (STATE_SLOTS - 1)).start()
        state_copy(1, i + (STATE_SLOTS - 1)).start()

    state_copy(0, i).wait()
    state_copy(1, i).wait()
    sret_in = sret_buf.at[i % STATE_SLOTS]
    sgdn_in = sgdn_buf.at[i % STATE_SLOTS]

    for hh in range(HEADS):
        sl = slice(hh * HD, (hh + 1) * HD)
        q, k_t, v, gq, gk, gk_t, gv = feats[hh]
        o_rows = []
        for j in range(TB):
            s = ret_gamma[hh] * sret_in[j, hh] + k_t[:, j:j + 1] * v[j:j + 1, :]
            sret_out[j, hh] = s
            o_rows.append(_dot(q, s))
        or_s[rows, sl] = _pick_rows(o_rows)

        o_rows = []
        for j in range(TB):
            s = sgdn_in[j, hh]
            eg = eg_all[j:j + 1, hh:hh + 1]
            beta = beta_all[j:j + 1, HEADS + hh:HEADS + hh + 1]
            ks = _dot(gk, s)[j:j + 1, :]
            u = beta * gv[j:j + 1, :] - (beta * eg) * ks
            s = eg * s + gk_t[:, j:j + 1] * u
            sgdn_out[j, hh] = s
            o_rows.append(_dot(gq, s))
        og_s[rows, sl] = _pick_rows(o_rows)

    @pl.when(i == pl.num_programs(0) - 1)
    def _():
        o_r = []
        o_g = []
        for hh in range(HEADS):
            sl = slice(hh * HD, (hh + 1) * HD)
            o_r.append(_ret_out_norm(or_s[:, sl], retg_ref[:, sl],
                                     _silu(pm_s[:, OFF_RG + hh * HD:OFF_RG + (hh + 1) * HD])))
            o_g.append(_gdn_out_norm(og_s[:, sl], gdng_ref[...],
                                     _silu(pm_s[:, OFF_GZ + hh * HD:OFF_GZ + (hh + 1) * HD])))
        y_ref[...] = _merge(x_ref[...], jnp.concatenate(o_r, axis=1), jnp.concatenate(o_g, axis=1),
                            jax.nn.sigmoid(gates_s[...]), wrb_ref, wgb_ref, wout_ref, post_ref)


def _mixer_sample(x, s_ret, s_gdn, s_conv, w, pos):
    n = x.shape[0]
    assert n % TB == 0 and n // TB >= STATE_SLOTS
    cos, sin = _rope_tables([pos])
    body = functools.partial(_mixer_sample_body, ret_gamma=tuple(float(v) for v in _ret_gammas()))
    state_spec = pl.BlockSpec((TB, HEADS, HD, HD), lambda i: (i, 0, 0, 0))
    state_hbm = pl.BlockSpec(memory_space=pl.ANY)
    conv_shape = (CONV_W - 1, n, CONV_DIM)
    return pl.pallas_call(
        body,
        grid=(n // TB,),
        in_specs=[_resident((n, D_MODEL)), _resident((1, HD)), _resident((1, HD))]
        + _mixer_weight_specs()
        + [state_hbm, state_hbm, _resident(conv_shape)],
        out_specs=[pl.BlockSpec((n, D_MODEL), lambda i: (0, 0)), state_spec, state_spec,
                   pl.BlockSpec(conv_shape, lambda i: (0, 0, 0))],
        out_shape=[
            jax.ShapeDtypeStruct((n, D_MODEL), F32),
            jax.ShapeDtypeStruct((n, HEADS, HD, HD), F32),
            jax.ShapeDtypeStruct((n, HEADS, HD, HD), F32),
            jax.ShapeDtypeStruct(conv_shape, F32),
        ],
        scratch_shapes=[pltpu.VMEM((n, OFF_AB), F32), pltpu.VMEM((n, 2 * D_MODEL), F32)]
        + [pltpu.VMEM((n, QK), F32) for _ in range(5)]
        + [pltpu.VMEM((n, HD), F32), pltpu.VMEM((n, HD), F32)]
        + [pltpu.VMEM((n, QK), F32), pltpu.VMEM((n, QK), F32)]
        + [pltpu.VMEM((STATE_SLOTS, TB, HEADS, HD, HD), F32), pltpu.VMEM((STATE_SLOTS, TB, HEADS, HD, HD), F32),
           pltpu.SemaphoreType.DMA((2, STATE_SLOTS))],
        compiler_params=pltpu.CompilerParams(dimension_semantics=("arbitrary",),
                                             vmem_limit_bytes=VMEM_LIMIT),
        name="mixer_sample",
    )(x, cos, sin, *_mixer_weight_args(w), s_ret, s_gdn, s_conv)


def _pad_lanes(v, n):
    return jnp.pad(v, ((0, 0), (0, n - v.shape[1])))


def kernel(x_prompt, x_sample, state_ret, state_gdn, state_conv, ffn1_pre_g, ffn1_post_g, ffn1_w_gate,
           ffn1_w_up, ffn1_w_down, mix_pre_g, mix_post_g, w_in, ret_norm_g, gdn_conv_w, gdn_a_log,
           gdn_dt_bias, gdn_norm_g, w_ret_branch, w_gdn_branch, w_out, ffn2_pre_g, ffn2_post_g,
           ffn2_w_gate, ffn2_w_up, ffn2_w_down):
    depth = w_in.shape[0]
    b, t, _ = x_prompt.shape
    n_s, t_s, _ = x_sample.shape
    assert t_s == 1
    yp = x_prompt.reshape(b * t, D_MODEL)
    ys = x_sample.reshape(n_s, D_MODEL)
    outs = [[] for _ in range(6)]
    for l in range(depth):
        row = lambda a: a[l][None, :]
        ys, wg1, wu1, wd1 = _ffn_stream(ys, row(ffn1_pre_g), row(ffn1_post_g),
                                        ffn1_w_gate[l], ffn1_w_up[l], ffn1_w_down[l])
        f1 = (row(ffn1_pre_g), row(ffn1_post_g), wg1, wu1, wd1)

        w_in_t = w_in[l].T
        casts = (
            (w_in_t, (0, OFF_AB)), (w_in_t, (OFF_GATES, 2 * D_MODEL)),
            (w_ret_branch[l], None), (w_gdn_branch[l], None), (w_out[l], None),
            (ffn2_w_gate[l], None), (ffn2_w_up[l], None), (ffn2_w_down[l], None),
        )
        yp, (w_main, w_gates, w_rb, w_gb, w_o, g2, u2, d2) = _ffn(yp, *f1, tm=FFN_TM, casts=casts)
        w = {
            "mix_pre_g": row(mix_pre_g), "mix_post_g": row(mix_post_g),
            "w_main": w_main,
            "w_ab": _pad_lanes(w_in[l, :, OFF_AB:OFF_GATES], HD).astype(BF16),
            "w_gates": w_gates,
            "conv_w": gdn_conv_w[l],
            "a_log": _pad_lanes(row(gdn_a_log), HD), "dt_bias": _pad_lanes(row(gdn_dt_bias), HD),
            "ret_norm_g": row(ret_norm_g), "gdn_norm_g": row(gdn_norm_g),
            "w_rb": w_rb, "w_gb": w_gb, "w_out": w_o,
        }
        f2 = (row(ffn2_pre_g), row(ffn2_post_g), g2, u2, d2)

        yp, r1, g1, c1 = _mixer_prompt(yp.reshape(b, t, D_MODEL), w)
        ys, r2, g2s, c2 = _mixer_sample(ys, state_ret[l], state_gdn[l],
                                        jnp.swapaxes(state_conv[l], 0, 1), w, PAST_LEN)
        c2 = jnp.swapaxes(c2, 0, 1)
        yp, ys, _ = _ffn(yp.reshape(b * t, D_MODEL), *f2, tm=FFN_TM, extra=ys)
        for lst, val in zip(outs, (r1, g1, c1, r2, g2s, c2)):
            lst.append(val)
    stacked = [v[0][None] if depth == 1 else jnp.stack(v) for v in outs]
    return (yp.reshape(b, t, D_MODEL), ys.reshape(n_s, t_s, D_MODEL), *stacked)
```

```python
import functools

import numpy as np
import jax
import jax.numpy as jnp
from jax import lax
from jax.experimental import pallas as pl
from jax.experimental.pallas import tpu as pltpu

F32 = jnp.float32
BF16 = jnp.bfloat16

D_MODEL = 1024
D_FF = 2816
HEADS = 4
HD = 128
QK = HEADS * HD
CONV_W = 4
CONV_DIM = 3 * QK
CHUNK = 64
ROPE_BASE = 10000.0
EPS = 1e-6
PAST_LEN = 16384

OFF_RQ, OFF_RK, OFF_RV, OFF_RG = 0, QK, 2 * QK, 3 * QK
OFF_CONV = 4 * QK
OFF_GZ = OFF_CONV + CONV_DIM
OFF_AB = OFF_GZ + QK
OFF_GATES = OFF_AB + 2 * HEADS
D_IN = OFF_GATES + 2 * D_MODEL

TM = 256
NB = 2
TB = 8
STATE_SLOTS = 3
FFN_TM = 1024
FFN_SUB = 256
FF_CHUNK = 256
CONV_PAD = 8
BF16_ROWS = 16
F32_ROWS = 8

VMEM_LIMIT = 56 * 1024 * 1024


def _silu(x):
    return x * jax.nn.sigmoid(x)


def _rms(x, g):
    return x * lax.rsqrt(jnp.mean(x * x, axis=-1, keepdims=True) + EPS) * g


def _dot(a, b):
    return jnp.dot(a.astype(BF16), b.astype(BF16), preferred_element_type=F32)


def _dot_nt(a, b):
    return lax.dot_general(a.astype(BF16), b.astype(BF16), (((1,), (1,)), ((), ())),
                           preferred_element_type=F32)


def _dot_tn(a, b):
    return lax.dot_general(a.astype(BF16), b.astype(BF16), (((0,), (0,)), ((), ())),
                           preferred_element_type=F32)


def _split3(a):
    hi = a.astype(BF16)
    r = a - hi.astype(F32)
    mid = r.astype(BF16)
    lo = (r - mid.astype(F32)).astype(BF16)
    return hi, mid, lo


def _resident(shape):
    nd = len(shape)
    return pl.BlockSpec(shape, lambda *_: (0,) * nd, pipeline_mode=pl.Buffered(1))


def _ffn_rows(x_ref, o_ref, pre_ref, post_ref, wg_ref, wu_ref, wd_ref):
    tm = x_ref.shape[0]
    sub = min(tm, FFN_SUB)
    parts = [slice(r, r + sub) for r in range(0, tm, sub)]
    xs = [x_ref[p, :] for p in parts]
    hs = [_rms(x, pre_ref[...]).astype(BF16) for x in xs]
    acts = []
    for h in hs:
        g = jnp.dot(h, wg_ref[...], preferred_element_type=F32)
        u = jnp.dot(h, wu_ref[...], preferred_element_type=F32)
        acts.append((_silu(g) * u).astype(BF16))
    for p, x, a in zip(parts, xs, acts):
        y = jnp.dot(a, wd_ref[...], preferred_element_type=F32)
        o_ref[p, :] = x + 0.5 * _rms(y, post_ref[...])


def _ffn_body(x_ref, pre_ref, post_ref, wg_ref, wu_ref, wd_ref, *rest, cast_transposed, has_extra):
    rest = list(rest)
    xe_ref = rest.pop(0) if has_extra else None
    n_cast = len(cast_transposed)
    cast_in, o_ref = rest[:n_cast], rest[n_cast]
    rest = rest[n_cast + 1:]
    oe_ref = rest.pop(0) if has_extra else None
    cast_out = rest
    weights = (pre_ref, post_ref, wg_ref, wu_ref, wd_ref)
    _ffn_rows(x_ref, o_ref, *weights)
    for src, dst, transposed in zip(cast_in, cast_out, cast_transposed):
        blk = src[...]
        dst[...] = (blk.T if transposed else blk).astype(BF16)
    if has_extra:
        @pl.when(pl.program_id(0) == pl.num_programs(0) - 1)
        def _():
            _ffn_rows(xe_ref, oe_ref, *weights)


def _cast_row_block(rows, steps):
    rb = BF16_ROWS
    while rows % rb or rows // rb > steps:
        rb += BF16_ROWS
    return rb


def _ffn(x, pre_g, post_g, wg, wu, wd, tm, casts=(), extra=None):
    n = x.shape[0]
    assert n % tm == 0
    steps = n // tm
    in_specs = [
        pl.BlockSpec((tm, D_MODEL), lambda i: (i, 0)),
        _resident((1, D_MODEL)), _resident((1, D_MODEL)),
        _resident((D_MODEL, D_FF)), _resident((D_MODEL, D_FF)), _resident((D_FF, D_MODEL)),
    ]
    out_specs = [pl.BlockSpec((tm, D_MODEL), lambda i: (i, 0))]
    out_shape = [jax.ShapeDtypeStruct((n, D_MODEL), F32)]
    operands = [x, pre_g, post_g, wg, wu, wd]
    if extra is not None:
        in_specs.append(_resident(extra.shape))
        operands.append(extra)
    n_fixed_in = len(in_specs)
    for arr, region in casts:
        rows, width = arr.shape
        if region is None:
            rb = _cast_row_block(rows, steps)
            last = rows // rb - 1
            spec = pl.BlockSpec((rb, width), lambda i, last=last: (jnp.minimum(i, last), 0))
            in_specs.append(spec)
            out_specs.append(spec)
            out_shape.append(jax.ShapeDtypeStruct((rows, width), BF16))
        else:
            row0, n_rows = region
            cb = HD
            while n_rows % cb or n_rows // cb > steps:
                cb += HD
            last = n_rows // cb - 1
            in_specs.append(pl.BlockSpec(
                (pl.Element(cb), pl.Element(width)),
                lambda i, row0=row0, last=last, cb=cb: (
                    pl.multiple_of(row0 + cb * jnp.minimum(i, last), F32_ROWS), 0)))
            out_specs.append(pl.BlockSpec((width, cb), lambda i, last=last: (0, jnp.minimum(i, last))))
            out_shape.append(jax.ShapeDtypeStruct((width, n_rows), BF16))
    assert len(in_specs) == n_fixed_in + len(casts)
    if extra is not None:
        out_specs.insert(1, pl.BlockSpec(extra.shape, lambda i: (0, 0)))
        out_shape.insert(1, jax.ShapeDtypeStruct(extra.shape, F32))
    outs = pl.pallas_call(
        functools.partial(_ffn_body, cast_transposed=tuple(c[1] is not None for c in casts),
                          has_extra=extra is not None),
        grid=(steps,),
        in_specs=in_specs,
        out_specs=out_specs,
        out_shape=out_shape,
        compiler_params=pltpu.CompilerParams(dimension_semantics=("arbitrary",),
                                             vmem_limit_bytes=VMEM_LIMIT),
        name="ffn",
    )(*operands, *[c[0] for c in casts])
    if extra is not None:
        return outs[0], outs[1], list(outs[2:])
    return outs[0], list(outs[1:])


def _ffn_stream_body(x_ref, pre_ref, post_ref, wg_hbm, wu_hbm, wd_hbm,
                     o_ref, wg16_ref, wu16_ref, wd16_ref, h_s, acc_s, wg_buf, wu_buf, wd_buf, sems):
    c = pl.program_id(0)
    n_steps = pl.num_programs(0)

    def chunk_copy(which, step):
        slot = step % STATE_SLOTS
        cols = pl.ds(step * FF_CHUNK, FF_CHUNK)
        src = (wg_hbm.at[:, cols], wu_hbm.at[:, cols], wd_hbm.at[cols, :])[which]
        dst = (wg_buf, wu_buf, wd_buf)[which].at[slot]
        return pltpu.make_async_copy(src, dst, sems.at[which, slot])

    @pl.when(c == 0)
    def _():
        for step in range(STATE_SLOTS - 1):
            for which in range(3):
                chunk_copy(which, step).start(priority=which % 2)

    @pl.when(c + (STATE_SLOTS - 1) < n_steps)
    def _():
        for which in range(3):
            chunk_copy(which, c + (STATE_SLOTS - 1)).start(priority=which % 2)

    @pl.when(c == 0)
    def _():
        h_s[...] = _rms(x_ref[...], pre_ref[...]).astype(BF16)
        acc_s[...] = jnp.zeros_like(acc_s)

    for which in range(3):
        chunk_copy(which, c).wait()
    slot = c % STATE_SLOTS
    wg = wg_buf[slot].astype(BF16)
    wu = wu_buf[slot].astype(BF16)
    wd = wd_buf[slot].astype(BF16)
    wg16_ref[...] = wg
    wu16_ref[...] = wu
    wd16_ref[...] = wd
    h = h_s[...]
    g = jnp.dot(h, wg, preferred_element_type=F32)
    u = jnp.dot(h, wu, preferred_element_type=F32)
    a = (_silu(g) * u).astype(BF16)
    acc_s[...] += jnp.dot(a, wd, preferred_element_type=F32)

    @pl.when(c == pl.num_programs(0) - 1)
    def _():
        o_ref[...] = x_ref[...] + 0.5 * _rms(acc_s[...], post_ref[...])


def _ffn_stream(x, pre_g, post_g, wg, wu, wd):
    n = x.shape[0]
    assert D_FF % FF_CHUNK == 0 and D_FF // FF_CHUNK >= STATE_SLOTS
    col = pl.BlockSpec((D_MODEL, FF_CHUNK), lambda c: (0, c))
    row = pl.BlockSpec((FF_CHUNK, D_MODEL), lambda c: (c, 0))
    hbm = pl.BlockSpec(memory_space=pl.ANY)
    return pl.pallas_call(
        _ffn_stream_body,
        grid=(D_FF // FF_CHUNK,),
        in_specs=[_resident((n, D_MODEL)), _resident((1, D_MODEL)), _resident((1, D_MODEL)), hbm, hbm, hbm],
        out_specs=[pl.BlockSpec((n, D_MODEL), lambda c: (0, 0)), col, col, row],
        out_shape=[
            jax.ShapeDtypeStruct((n, D_MODEL), F32),
            jax.ShapeDtypeStruct((D_MODEL, D_FF), BF16),
            jax.ShapeDtypeStruct((D_MODEL, D_FF), BF16),
            jax.ShapeDtypeStruct((D_FF, D_MODEL), BF16),
        ],
        scratch_shapes=[pltpu.VMEM((n, D_MODEL), BF16), pltpu.VMEM((n, D_MODEL), F32),
                        pltpu.VMEM((STATE_SLOTS, D_MODEL, FF_CHUNK), F32),
                        pltpu.VMEM((STATE_SLOTS, D_MODEL, FF_CHUNK), F32),
                        pltpu.VMEM((STATE_SLOTS, FF_CHUNK, D_MODEL), F32),
                        pltpu.SemaphoreType.DMA((3, STATE_SLOTS))],
        compiler_params=pltpu.CompilerParams(dimension_semantics=("arbitrary",),
                                             vmem_limit_bytes=VMEM_LIMIT),
        name="ffn_stream",
    )(x, pre_g, post_g, wg, wu, wd)


def _rope(x, cos, sin_signed):
    return x * cos + pltpu.roll(x, HD // 2, 1) * sin_signed


def _ret_out_norm(o, g_row, gate_act):
    mu = jnp.mean(o, axis=-1, keepdims=True)
    d = o - mu
    var = jnp.mean(d * d, axis=-1, keepdims=True)
    return gate_act * (d * lax.rsqrt(var + EPS) * g_row)


def _gdn_out_norm(o, g_row, gate_act):
    return o * lax.rsqrt(jnp.mean(o * o, axis=-1, keepdims=True) + EPS) * g_row * gate_act


def _l2norm(x, scale):
    return x * (lax.rsqrt(jnp.sum(x * x, axis=-1, keepdims=True) + EPS) * scale)


def _softplus(x):
    return jnp.maximum(x, 0.0) + jnp.log(1.0 + jnp.exp(-jnp.abs(x)))


def _merge(x, o_r, o_g, sig_gates, wrb_ref, wgb_ref, wout_ref, post_ref):
    y = (sig_gates[:, :D_MODEL] * _dot(o_r, wrb_ref[...])
         + sig_gates[:, D_MODEL:] * _dot(o_g, wgb_ref[...]))
    m = _dot(y, wout_ref[...])
    return x + _rms(m, post_ref[...])


def _pad_rows(a):
    return jnp.concatenate([a, jnp.zeros_like(a)], axis=0)


def _dot64(a, b):
    return jnp.dot(a.astype(BF16), _pad_rows(b.astype(BF16)), preferred_element_type=F32)


def _unit_lower_inverse_many(a_list, masks):
    eye, m16, off32, off64 = masks
    ad = [a * m16 for a in a_list]
    x = [eye - v for v in ad]
    p = [_dot64(v, v) for v in ad]
    for level in range(3):
        x = [xi + _dot64(xi, pi) for xi, pi in zip(x, p)]
        if level < 2:
            p = [_dot64(pi, pi) for pi in p]
    for m in (off32, off64):
        t = [_dot64(xi, a * m) for xi, a in zip(x, a_list)]
        x = [xi - _dot64(ti, xi) for xi, ti in zip(x, t)]
    return x


def _chunk_masks():
    r = lax.broadcasted_iota(jnp.int32, (CHUNK, 2 * CHUNK), 0)
    c = lax.broadcasted_iota(jnp.int32, (CHUNK, 2 * CHUNK), 1)
    one, zero = jnp.float32(1.0), jnp.float32(0.0)
    live = c < CHUNK
    eye = jnp.where(r == c, one, zero)
    m16 = jnp.where(live, jnp.where((r >> 4) == (c >> 4), one, zero), zero)
    m32 = jnp.where(live, jnp.where((r >> 5) == (c >> 5), one, zero), zero)
    m64 = jnp.where(live, one, zero)
    return r >= c, r > c, (eye, m16, m32 - m16, m64 - m32)


def _retention_tile(pm, cos, sin, rd_ref, ku_ref, retg_ref, s_ref, tile_decay):
    ri = lax.broadcasted_iota(jnp.int32, (TM, TM), 0)
    ci = lax.broadcasted_iota(jnp.int32, (TM, TM), 1)
    causal = ri >= ci
    heads = []
    for hh in range(HEADS):
        sl = slice(hh * HD, (hh + 1) * HD)
        rq = pm[:, OFF_RQ + hh * HD:OFF_RQ + (hh + 1) * HD]
        rk = pm[:, OFF_RK + hh * HD:OFF_RK + (hh + 1) * HD]
        v = pm[:, OFF_RV + hh * HD:OFF_RV + (hh + 1) * HD]
        rg = pm[:, OFF_RG + hh * HD:OFF_RG + (hh + 1) * HD]
        qs = _rope(rq, cos, sin) * rd_ref[hh]
        ku = _rope(rk, cos, sin) * ku_ref[hh]
        s = s_ref[hh]
        sc = jnp.where(causal, _dot_nt(qs, ku), 0.0)
        o = _dot(sc, v) + _dot(qs, s)
        s_ref[hh] = tile_decay[hh] * (s + _dot_tn(ku, v))
        heads.append(_ret_out_norm(o, retg_ref[:, sl], _silu(rg)))
    return jnp.concatenate(heads, axis=1)


def _short_conv_tile(u, buf, convw_ref, tail_ref):
    prev = buf[...]
    row = lax.broadcasted_iota(jnp.int32, (CONV_PAD, u.shape[1]), 0)
    acc = u * convw_ref[CONV_W - 1:CONV_W, :]
    for k in range(1, CONV_W):
        shifted = pltpu.roll(u, k, 0)
        head = jnp.where(row < k, pltpu.roll(prev, k, 0), shifted[0:CONV_PAD, :])
        shifted = jnp.concatenate([head, shifted[CONV_PAD:, :]], axis=0)
        acc = acc + shifted * convw_ref[CONV_W - 1 - k:CONV_W - k, :]
    buf[...] = u[TM - CONV_PAD:TM, :]
    tail_ref[...] = u[TM - (CONV_W - 1):TM, :]
    return _silu(acc)


def _gdn_decay_beta(ab, alog_ref, dtb_ref):
    g_all = -jnp.exp(alog_ref[...]) * _softplus(ab + dtb_ref[...])
    beta_all = jax.nn.sigmoid(ab)
    return g_all, beta_all


def _gdn_tiles(cqs, abs_, gzs, ltri, alog_ref, dtb_ref, gdng_ref, s_refs):
    seqs = range(len(cqs))
    dd = functools.partial(jnp.dot, preferred_element_type=F32)
    gcum, gcum_t, beta_all = [], [], []
    for b in seqs:
        g_all, beta = _gdn_decay_beta(abs_[b], alog_ref, dtb_ref)
        beta_all.append(beta)
        g_hi, g_mid, g_lo = _split3(g_all)
        gcum.append(dd(ltri, g_hi) + (dd(ltri, g_mid) + dd(ltri, g_lo)))
        gcum_t.append(gcum[b].T)

    causal64, strict64, masks = _chunk_masks()

    n_ch = TM // CHUNK
    rows = [slice(c * CHUNK, (c + 1) * CHUNK) for c in range(n_ch)]
    bh = [(b, hh) for b in seqs for hh in range(HEADS)]
    probs = [(b, hh, c) for b, hh in bh for c in range(n_ch)]
    gq = {(b, hh): _l2norm(cqs[b][:, hh * HD:(hh + 1) * HD], HD ** -0.5) for b, hh in bh}
    gk = {(b, hh): _l2norm(cqs[b][:, QK + hh * HD:QK + (hh + 1) * HD], 1.0) for b, hh in bh}
    gv = {(b, hh): cqs[b][:, 2 * QK + hh * HD:2 * QK + (hh + 1) * HD] for b, hh in bh}
    qc = {(b, hh, c): gq[b, hh][rows[c]] for b, hh, c in probs}
    kc = {(b, hh, c): gk[b, hh][rows[c]] for b, hh, c in probs}
    vc = {(b, hh, c): gv[b, hh][rows[c]] for b, hh, c in probs}
    gc = {(b, hh, c): gcum[b][rows[c], hh:hh + 1] for b, hh, c in probs}
    bcol = {(b, hh, c): beta_all[b][rows[c], HEADS + hh:HEADS + hh + 1] for b, hh, c in probs}
    def g_row(b, hh, c):
        gr = gcum_t[b][hh:hh + 1, rows[c]]
        return jnp.concatenate([gr, gr], axis=1)

    dm = {(b, hh, c): jnp.where(
        causal64, jnp.exp(jnp.minimum(gc[b, hh, c] - g_row(b, hh, c), 0.0)), 0.0)
        for b, hh, c in probs}
    kpad = {p: _pad_rows(kc[p].astype(BF16)) for p in probs}
    kk = {p: _dot_nt(kc[p], kpad[p]) for p in probs}
    qk = {p: _dot_nt(qc[p], kpad[p]) * dm[p] for p in probs}
    a_mats = [jnp.where(strict64, bcol[p] * dm[p] * kk[p], 0.0) for p in probs]
    tinv = dict(zip(probs, _unit_lower_inverse_many(a_mats, masks)))
    eg = {p: jnp.exp(gc[p]) for p in probs}
    uw = {p: _dot64(tinv[p], jnp.concatenate([bcol[p] * vc[p], (bcol[p] * eg[p]) * kc[p]], axis=1))
          for p in probs}
    gl = {p: gc[p][CHUNK - 1:CHUNK, :] for p in probs}
    uwp = {p: _pad_rows(uw[p].astype(BF16)) for p in probs}
    kdw = {p: _dot_tn(_pad_rows((kc[p] * jnp.exp(gl[p] - gc[p])).astype(BF16)), uwp[p])
           for p in probs}
    qkw = {p: jnp.dot(qk[p].astype(BF16), uwp[p], preferred_element_type=F32)
           for p in probs}
    lhs = {p: jnp.concatenate([kdw[p][:, HD:], eg[p] * qc[p] - qkw[p][:, HD:]], axis=0).astype(BF16)
           for p in probs}

    s = {(b, hh): s_refs[b][hh] for b, hh in bh}
    outs = {k: [] for k in bh}
    for c in range(n_ch):
        for b, hh in bh:
            p = (b, hh, c)
            ps = _dot(lhs[p], s[b, hh])
            outs[b, hh].append(ps[HD:, :] + qkw[p][:, :HD])
            s[b, hh] = jnp.exp(gl[p]) * s[b, hh] + (kdw[p][:, :HD] - ps[:HD, :])
    o_gs = []
    for b in seqs:
        heads = []
        for hh in range(HEADS):
            s_refs[b][hh] = s[b, hh]
            o = jnp.concatenate(outs[b, hh], axis=0)
            heads.append(_gdn_out_norm(o, gdng_ref[...], _silu(gzs[b][:, hh * HD:(hh + 1) * HD])))
        o_gs.append(jnp.concatenate(heads, axis=1))
    return o_gs


def _mixer_prompt_body(x_ref, cos_ref, sin_ref, rd_ref, ku_ref, ltri_ref,
                       pre_ref, post_ref, wmain_ref, wab_ref, wgates_ref, convw_ref, alog_ref, dtb_ref,
                       retg_ref, gdng_ref, wrb_ref, wgb_ref, wout_ref,
                       y_ref, sret_ref, sgdn_ref, conv_ref,
                       cbuf, *, ret_tile_decay):
    t = pl.program_id(1)
    nb = x_ref.shape[0]

    @pl.when(t == 0)
    def _():
        sret_ref[...] = jnp.zeros_like(sret_ref)
        sgdn_ref[...] = jnp.zeros_like(sgdn_ref)
        cbuf[:, 0:CONV_PAD, :] = jnp.zeros((nb, CONV_PAD, CONV_DIM), F32)

    cos = cos_ref[...]
    sin = sin_ref[...]
    ltri = ltri_ref[...]
    seqs = range(nb)
    rows = [slice(b * TM, (b + 1) * TM) for b in seqs]

    def stack(parts):
        return jnp.concatenate(list(parts), axis=0)

    def dot_all(lhs, w):
        return jnp.dot(lhs, w, preferred_element_type=F32)

    x_all = stack(x_ref[b] for b in seqs)
    h_all = _rms(x_all, pre_ref[...]).astype(BF16)
    pg_all = dot_all(h_all, wmain_ref[:, OFF_CONV:OFF_AB])
    ab_all = dot_all(h_all, wab_ref[...])
    cqs = [_short_conv_tile(pg_all[rows[b], 0:CONV_DIM], cbuf.at[b], convw_ref, conv_ref.at[b]) for b in seqs]
    pm_all = dot_all(h_all, wmain_ref[:, 0:OFF_CONV])
    o_rs = [_retention_tile(pm_all[rows[b]], cos, sin, rd_ref, ku_ref, retg_ref, sret_ref.at[b],
                            ret_tile_decay) for b in seqs]
    o_gs = _gdn_tiles(cqs, [ab_all[rows[b]] for b in seqs], [pg_all[rows[b], CONV_DIM:] for b in seqs],
                      ltri, alog_ref, dtb_ref, gdng_ref, [sgdn_ref.at[b] for b in seqs])
    gates = dot_all(h_all, wgates_ref[...])
    y_all = _merge(x_all, stack(o_rs), stack(o_gs), jax.nn.sigmoid(gates), wrb_ref, wgb_ref, wout_ref,
                   post_ref)
    for b in seqs:
        y_ref[b] = y_all[rows[b]]


def _ret_gammas():
    return 1.0 - 2.0 ** (-5.0 - np.arange(HEADS, dtype=np.float64))


def _rope_tables(pos):
    inv = ROPE_BASE ** (-np.arange(0, HD, 2, dtype=np.float64) / HD)
    ang = np.asarray(pos, np.float64)[:, None] * inv[None, :]
    cos = np.concatenate([np.cos(ang), np.cos(ang)], axis=1)
    sin = np.concatenate([-np.sin(ang), np.sin(ang)], axis=1)
    return jnp.asarray(cos, F32), jnp.asarray(sin, F32)


def _mixer_weight_specs():
    return [
        _resident((1, D_MODEL)), _resident((1, D_MODEL)),
        _resident((D_MODEL, OFF_AB)), _resident((D_MODEL, HD)), _resident((D_MODEL, 2 * D_MODEL)),
        _resident((CONV_W, CONV_DIM)), _resident((1, HD)), _resident((1, HD)),
        _resident((1, QK)), _resident((1, HD)),
        _resident((QK, D_MODEL)), _resident((QK, D_MODEL)), _resident((D_MODEL, D_MODEL)),
    ]


def _mixer_weight_args(w):
    return (w["mix_pre_g"], w["mix_post_g"], w["w_main"], w["w_ab"], w["w_gates"], w["conv_w"],
            w["a_log"], w["dt_bias"], w["ret_norm_g"], w["gdn_norm_g"], w["w_rb"], w["w_gb"], w["w_out"])


def _mixer_prompt(x, w):
    b, t, _ = x.shape
    assert t % TM == 0 and b % NB == 0
    cos, sin = _rope_tables(np.arange(t))
    gam = _ret_gammas()
    i1 = np.arange(1, TM + 1, dtype=np.float64)
    rd = np.broadcast_to((gam[:, None] ** i1[None, :])[:, :, None], (HEADS, TM, HD))
    ku = np.broadcast_to((HD ** -0.5 * gam[:, None] ** (-i1[None, :]))[:, :, None], (HEADS, TM, HD))
    tile_decay = tuple(float(v) for v in gam ** TM)
    r = np.arange(TM)
    ltri = ((r[:, None] >= r[None, :]) & (r[:, None] // CHUNK == r[None, :] // CHUNK))

    body = functools.partial(_mixer_prompt_body, ret_tile_decay=tile_decay)
    state_spec = pl.BlockSpec((NB, HEADS, HD, HD), lambda i, j: (i, 0, 0, 0))
    return pl.pallas_call(
        body,
        grid=(b // NB, t // TM),
        in_specs=[
            pl.BlockSpec((NB, TM, D_MODEL), lambda i, j: (i, j, 0)),
            pl.BlockSpec((TM, HD), lambda i, j: (j, 0)),
            pl.BlockSpec((TM, HD), lambda i, j: (j, 0)),
            _resident((HEADS, TM, HD)), _resident((HEADS, TM, HD)), _resident((TM, TM)),
        ] + _mixer_weight_specs(),
        out_specs=[
            pl.BlockSpec((NB, TM, D_MODEL), lambda i, j: (i, j, 0)),
            state_spec, state_spec,
            pl.BlockSpec((NB, CONV_W - 1, CONV_DIM), lambda i, j: (i, 0, 0)),
        ],
        out_shape=[
            jax.ShapeDtypeStruct((b, t, D_MODEL), F32),
            jax.ShapeDtypeStruct((b, HEADS, HD, HD), F32),
            jax.ShapeDtypeStruct((b, HEADS, HD, HD), F32),
            jax.ShapeDtypeStruct((b, CONV_W - 1, CONV_DIM), F32),
        ],
        scratch_shapes=[pltpu.VMEM((NB, CONV_PAD, CONV_DIM), F32)],
        compiler_params=pltpu.CompilerParams(dimension_semantics=("arbitrary", "arbitrary"),
                                             vmem_limit_bytes=VMEM_LIMIT),
        name="mixer_prompt",
    )(x, cos, sin, jnp.asarray(rd, F32), jnp.asarray(ku, F32), jnp.asarray(ltri, BF16),
      *_mixer_weight_args(w))


def _pick_rows(rows):
    ri = lax.broadcasted_iota(jnp.int32, rows[0].shape, 0)
    out = rows[0]
    for j in range(1, len(rows)):
        out = jnp.where(ri == j, rows[j], out)
    return out


def _mixer_sample_body(x_ref, cos_ref, sin_ref,
                       pre_ref, post_ref, wmain_ref, wab_ref, wgates_ref, convw_ref, alog_ref, dtb_ref,
                       retg_ref, gdng_ref, wrb_ref, wgb_ref, wout_ref,
                       sret_hbm, sgdn_hbm, conv_in,
                       y_ref, sret_out, sgdn_out, conv_out,
                       pm_s, gates_s, rq_s, rk_s, gq_s, gk_s, gv_s, eg_s, beta_s, or_s, og_s,
                       sret_buf, sgdn_buf, sems, *, ret_gamma):
    i = pl.program_id(0)
    n_steps = pl.num_programs(0)

    def state_copy(which, step):
        hbm, buf = ((sret_hbm, sret_buf), (sgdn_hbm, sgdn_buf))[which]
        slot = step % STATE_SLOTS
        return pltpu.make_async_copy(hbm.at[pl.ds(step * TB, TB)], buf.at[slot], sems.at[which, slot])

    @pl.when(i == 0)
    def _():
        for step in range(STATE_SLOTS - 1):
            state_copy(0, step).start(priority=0)
            state_copy(1, step).start(priority=1)

    @pl.when(i + (STATE_SLOTS - 1) < n_steps)
    def _():
        state_copy(0, i + (STATE_SLOTS - 1)).start(priority=0)
        state_copy(1, i + (STATE_SLOTS - 1)).start(priority=1)

    @pl.when(i == 0)
    def _():
        h = _rms(x_ref[...], pre_ref[...]).astype(BF16)
        pm = jnp.dot(h, wmain_ref[...], preferred_element_type=F32)
        ab = jnp.dot(h, wab_ref[...], preferred_element_type=F32)
        pm_s[...] = pm
        gates_s[...] = jnp.dot(h, wgates_ref[...], preferred_element_type=F32)
        cos = cos_ref[...]
        sin = sin_ref[...]
        cin = pm[:, OFF_CONV:OFF_CONV + CONV_DIM]
        acc = cin * convw_ref[CONV_W - 1:CONV_W, :]
        for r in range(CONV_W - 1):
            acc = acc + conv_in[r] * convw_ref[r:r + 1, :]
        for r in range(CONV_W - 2):
            conv_out[r] = conv_in[r + 1]
        conv_out[CONV_W - 2] = cin
        cq = _silu(acc)
        g_all, beta_all = _gdn_decay_beta(ab, alog_ref, dtb_ref)
        eg_s[...] = jnp.exp(g_all)
        beta_s[...] = beta_all
        for hh in range(HEADS):
            sl = slice(hh * HD, (hh + 1) * HD)
            rq_s[:, sl] = _rope(pm[:, OFF_RQ + hh * HD:OFF_RQ + (hh + 1) * HD], cos, sin)
            rk_s[:, sl] = _rope(pm[:, OFF_RK + hh * HD:OFF_RK + (hh + 1) * HD], cos, sin) * (HD ** -0.5)
            gq_s[:, sl] = _l2norm(cq[:, hh * HD:(hh + 1) * HD], HD ** -0.5)
            gk_s[:, sl] = _l2norm(cq[:, QK + hh * HD:QK + (hh + 1) * HD], 1.0)
        gv_s[...] = cq[:, 2 * QK:3 * QK]

    state_copy(0, i).wait()
    state_copy(1, i).wait()
    sret_in = sret_buf.at[i % STATE_SLOTS]
    sgdn_in = sgdn_buf.at[i % STATE_SLOTS]

    rows = pl.ds(pl.multiple_of(i * TB, TB), TB)
    eg_all = eg_s[rows, :]
    beta_all = beta_s[rows, :]
    for hh in range(HEADS):
        sl = slice(hh * HD, (hh + 1) * HD)
        q = rq_s[rows, sl]
        k_t = rk_s[rows, sl].T
        v = pm_s[rows, OFF_RV + hh * HD:OFF_RV + (hh + 1) * HD]
        o_rows = []
        for j in range(TB):
            s = ret_gamma[hh] * sret_in[j, hh] + k_t[:, j:j + 1] * v[j:j + 1, :]
            sret_out[j, hh] = s
            o_rows.append(_dot(q, s))
        or_s[rows, sl] = _pick_rows(o_rows)

        gq = gq_s[rows, sl]
        gk = gk_s[rows, sl]
        gk_t = gk.T
        gv = gv_s[rows, sl]
        o_rows = []
        for j in range(TB):
            s = sgdn_in[j, hh]
            eg = eg_all[j:j + 1, hh:hh + 1]
            beta = beta_all[j:j + 1, HEADS + hh:HEADS + hh + 1]
            ks = _dot(gk, s)[j:j + 1, :]
            u = beta * gv[j:j + 1, :] - (beta * eg) * ks
            s = eg * s + gk_t[:, j:j + 1] * u
            sgdn_out[j, hh] = s
            o_rows.append(_dot(gq, s))
        og_s[rows, sl] = _pick_rows(o_rows)

    @pl.when(i == pl.num_programs(0) - 1)
    def _():
        o_r = []
        o_g = []
        for hh in range(HEADS):
            sl = slice(hh * HD, (hh + 1) * HD)
            o_r.append(_ret_out_norm(or_s[:, sl], retg_ref[:, sl],
                                     _silu(pm_s[:, OFF_RG + hh * HD:OFF_RG + (hh + 1) * HD])))
            o_g.append(_gdn_out_norm(og_s[:, sl], gdng_ref[...],
                                     _silu(pm_s[:, OFF_GZ + hh * HD:OFF_GZ + (hh + 1) * HD])))
        y_ref[...] = _merge(x_ref[...], jnp.concatenate(o_r, axis=1), jnp.concatenate(o_g, axis=1),
                            jax.nn.sigmoid(gates_s[...]), wrb_ref, wgb_ref, wout_ref, post_ref)


def _mixer_sample(x, s_ret, s_gdn, s_conv, w, pos):
    n = x.shape[0]
    assert n % TB == 0 and n // TB >= STATE_SLOTS
    cos, sin = _rope_tables([pos])
    body = functools.partial(_mixer_sample_body, ret_gamma=tuple(float(v) for v in _ret_gammas()))
    state_spec = pl.BlockSpec((TB, HEADS, HD, HD), lambda i: (i, 0, 0, 0))
    state_hbm = pl.BlockSpec(memory_space=pl.ANY)
    conv_shape = (CONV_W - 1, n, CONV_DIM)
    return pl.pallas_call(
        body,
        grid=(n // TB,),
        in_specs=[_resident((n, D_MODEL)), _resident((1, HD)), _resident((1, HD))]
        + _mixer_weight_specs()
        + [state_hbm, state_hbm, _resident(conv_shape)],
        out_specs=[pl.BlockSpec((n, D_MODEL), lambda i: (0, 0)), state_spec, state_spec,
                   pl.BlockSpec(conv_shape, lambda i: (0, 0, 0))],
        out_shape=[
            jax.ShapeDtypeStruct((n, D_MODEL), F32),
            jax.ShapeDtypeStruct((n, HEADS, HD, HD), F32),
            jax.ShapeDtypeStruct((n, HEADS, HD, HD), F32),
            jax.ShapeDtypeStruct(conv_shape, F32),
        ],
        scratch_shapes=[pltpu.VMEM((n, OFF_AB), F32), pltpu.VMEM((n, 2 * D_MODEL), F32)]
        + [pltpu.VMEM((n, QK), F32) for _ in range(5)]
        + [pltpu.VMEM((n, HD), F32), pltpu.VMEM((n, HD), F32)]
        + [pltpu.VMEM((n, QK), F32), pltpu.VMEM((n, QK), F32)]
        + [pltpu.VMEM((STATE_SLOTS, TB, HEADS, HD, HD), F32), pltpu.VMEM((STATE_SLOTS, TB, HEADS, HD, HD), F32),
           pltpu.SemaphoreType.DMA((2, STATE_SLOTS))],
        compiler_params=pltpu.CompilerParams(dimension_semantics=("arbitrary",),
                                             vmem_limit_bytes=VMEM_LIMIT),
        name="mixer_sample",
    )(x, cos, sin, *_mixer_weight_args(w), s_ret, s_gdn, s_conv)


def _pad_lanes(v, n):
    return jnp.pad(v, ((0, 0), (0, n - v.shape[1])))


def kernel(x_prompt, x_sample, state_ret, state_gdn, state_conv, ffn1_pre_g, ffn1_post_g, ffn1_w_gate,
           ffn1_w_up, ffn1_w_down, mix_pre_g, mix_post_g, w_in, ret_norm_g, gdn_conv_w, gdn_a_log,
           gdn_dt_bias, gdn_norm_g, w_ret_branch, w_gdn_branch, w_out, ffn2_pre_g, ffn2_post_g,
           ffn2_w_gate, ffn2_w_up, ffn2_w_down):
    depth = w_in.shape[0]
    b, t, _ = x_prompt.shape
    n_s, t_s, _ = x_sample.shape
    assert t_s == 1
    yp = x_prompt.reshape(b * t, D_MODEL)
    ys = x_sample.reshape(n_s, D_MODEL)
    outs = [[] for _ in range(6)]
    for l in range(depth):
        row = lambda a: a[l][None, :]
        ys, wg1, wu1, wd1 = _ffn_stream(ys, row(ffn1_pre_g), row(ffn1_post_g),
                                        ffn1_w_gate[l], ffn1_w_up[l], ffn1_w_down[l])
        f1 = (row(ffn1_pre_g), row(ffn1_post_g), wg1, wu1, wd1)

        w_in_t = w_in[l].T
        casts = (
            (w_in_t, (0, OFF_AB)), (w_in_t, (OFF_GATES, 2 * D_MODEL)),
            (w_ret_branch[l], None), (w_gdn_branch[l], None), (w_out[l], None),
            (ffn2_w_gate[l], None), (ffn2_w_up[l], None), (ffn2_w_down[l], None),
        )
        yp, (w_main, w_gates, w_rb, w_gb, w_o, g2, u2, d2) = _ffn(yp, *f1, tm=FFN_TM, casts=casts)
        w = {
            "mix_pre_g": row(mix_pre_g), "mix_post_g": row(mix_post_g),
            "w_main": w_main,
            "w_ab": _pad_lanes(w_in[l, :, OFF_AB:OFF_GATES], HD).astype(BF16),
            "w_gates": w_gates,
            "conv_w": gdn_conv_w[l],
            "a_log": _pad_lanes(row(gdn_a_log), HD), "dt_bias": _pad_lanes(row(gdn_dt_bias), HD),
            "ret_norm_g": row(ret_norm_g), "gdn_norm_g": row(gdn_norm_g),
            "w_rb": w_rb, "w_gb": w_gb, "w_out": w_o,
        }
        f2 = (row(ffn2_pre_g), row(ffn2_post_g), g2, u2, d2)

        yp, r1, g1, c1 = _mixer_prompt(yp.reshape(b, t, D_MODEL), w)
        ys, r2, g2s, c2 = _mixer_sample(ys, state_ret[l], state_gdn[l],
                                        jnp.swapaxes(state_conv[l], 0, 1), w, PAST_LEN)
        c2 = jnp.swapaxes(c2, 0, 1)
        yp, ys, _ = _ffn(yp.reshape(b * t, D_MODEL), *f2, tm=FFN_TM, extra=ys)
        for lst, val in zip(outs, (r1, g1, c1, r2, g2s, c2)):
            lst.append(val)
    stacked = [v[0][None] if depth == 1 else jnp.stack(v) for v in outs]
    return (yp.reshape(b, t, D_MODEL), ys.reshape(n_s, t_s, D_MODEL), *stacked)
```
